```python
import jax, jax.numpy as jnp
from jax import lax
import numpy as np

D_MODEL = 2048
BATCH = 8
SEQ = 8192
DEPTH = 1

EPS = 1e-6
MEM_LEN = 256

CHUNK = 128
A_GROUP_DIM = 128
A_GROUPS = D_MODEL // A_GROUP_DIM
A_WIDTH = A_GROUPS * A_GROUP_DIM

QK_NOPE = 128
QK_ROPE = 64
V_DIM = 128
MLA_HEADS = D_MODEL // V_DIM
Q_LORA = 512
KV_LORA = 512
MLA_WIDTH = MLA_HEADS * V_DIM
QK_DIM = QK_NOPE + QK_ROPE
ROPE_THETA = 10000.0
Q_BLOCK = 128

MEM_HEADS = 4
MEM_HEAD_DIM = D_MODEL // MEM_HEADS
MEM_WIDTH = MEM_HEADS * MEM_HEAD_DIM

N_BRANCH = 3
BRANCH_WIDTH = D_MODEL

IN_SIZES = (A_WIDTH, A_WIDTH, A_WIDTH,
            Q_LORA, KV_LORA, QK_ROPE, MLA_WIDTH,
            MEM_WIDTH, MEM_WIDTH)
IN_TOTAL = int(sum(IN_SIZES))
IN_SPLITS = [int(o) for o in np.cumsum(IN_SIZES)[:-1]]

kernel_name = "hybrid_gmlp_mla_memory_gated"


def rmsnorm(x, g):
    xf = x.astype(jnp.float32)
    xf = xf * lax.rsqrt(jnp.mean(xf * xf, axis=-1, keepdims=True) + EPS)
    return xf.astype(x.dtype) * g


def layernorm(x, g, b):
    xf = x.astype(jnp.float32)
    mu = jnp.mean(xf, axis=-1, keepdims=True)
    var = jnp.mean(jnp.square(xf - mu), axis=-1, keepdims=True)
    return ((xf - mu) * lax.rsqrt(var + EPS)).astype(x.dtype) * g + b


def rope_tables(positions):
    inv_freq = 1.0 / (ROPE_THETA ** (jnp.arange(0, QK_ROPE, 2, dtype=jnp.float32) / QK_ROPE))
    ang = positions.astype(jnp.float32)[..., None] * inv_freq
    return jnp.cos(ang), jnp.sin(ang)


def apply_rope(t, cos, sin):
    t1, t2 = jnp.split(t, 2, axis=-1)
    cos = cos.astype(t.dtype)
    sin = sin.astype(t.dtype)
    return jnp.concatenate([t1 * cos - t2 * sin, t2 * cos + t1 * sin], axis=-1)


def chunked_spatial_gating(u_raw, v_raw, ln_g, ln_b, w_s, b_s):
    B, S, _ = u_raw.shape
    u = jax.nn.gelu(u_raw)
    v = layernorm(jax.nn.gelu(v_raw), ln_g, ln_b)
    vc = v.reshape(B, S // CHUNK, CHUNK, A_GROUPS, A_GROUP_DIM)
    causal = jnp.tril(jnp.ones((CHUNK, CHUNK), dtype=w_s.dtype))
    ws = w_s * causal[None]
    sv = jnp.einsum('gts,bcsgd->bctgd', ws, vc) + b_s.T[None, None, :, :, None]
    return u * sv.reshape(B, S, A_WIDTH)


def latent_attention(c_q, c_kv, k_rope, cos, sin, q_norm_g, w_uq, kv_norm_g, w_ukv):
    B, S, _ = c_q.shape
    q = (rmsnorm(c_q, q_norm_g) @ w_uq).reshape(B, S, MLA_HEADS, QK_DIM)
    q_nope, q_pe = jnp.split(q, [QK_NOPE], axis=-1)
    q_pe = apply_rope(q_pe, cos[:, :, None, :], sin[:, :, None, :])
    q = jnp.concatenate([q_nope, q_pe], axis=-1)

    kv = (rmsnorm(c_kv, kv_norm_g) @ w_ukv).reshape(B, S, MLA_HEADS, QK_NOPE + V_DIM)
    k_nope, v = jnp.split(kv, [QK_NOPE], axis=-1)
    k_pe = apply_rope(k_rope, cos, sin)
    k_pe = jnp.broadcast_to(k_pe[:, :, None, :], (B, S, MLA_HEADS, QK_ROPE))
    k = jnp.concatenate([k_nope, k_pe], axis=-1)

    qh = q.transpose(0, 2, 1, 3)
    kh = k.transpose(0, 2, 1, 3)
    vh = v.transpose(0, 2, 1, 3)
    n_blk = S // Q_BLOCK
    q_blocks = qh.reshape(B, MLA_HEADS, n_blk, Q_BLOCK, QK_DIM).transpose(2, 0, 1, 3, 4)
    scale = QK_DIM ** -0.5
    key_pos = jnp.arange(S)

    def one_block(args):
        qi, bi = args
        s = jnp.einsum('bhqd,bhkd->bhqk', qi, kh).astype(jnp.float32) * scale
        q_pos = bi * Q_BLOCK + jnp.arange(Q_BLOCK)
        mask = key_pos[None, :] <= q_pos[:, None]
        s = jnp.where(mask[None, None], s, -1e30)
        p = jax.nn.softmax(s, axis=-1).astype(vh.dtype)
        return jnp.einsum('bhqk,bhkd->bhqd', p, vh)

    o = lax.map(one_block, (q_blocks, jnp.arange(n_blk)))
    return o.transpose(1, 0, 3, 2, 4).reshape(B, S, MLA_WIDTH)


def memory_attention(q_m, mem, mem_norm_g, w_mem_kv):
    B, S, _ = q_m.shape
    kv = rmsnorm(mem, mem_norm_g) @ w_mem_kv
    k_m, v_m = jnp.split(kv.reshape(B, MEM_LEN, 2, MEM_HEADS, MEM_HEAD_DIM), 2, axis=2)
    k_m, v_m = k_m[:, :, 0], v_m[:, :, 0]
    q = q_m.reshape(B, S, MEM_HEADS, MEM_HEAD_DIM)
    s = jnp.einsum('bshd,bmhd->bhsm', q, k_m).astype(jnp.float32) * (MEM_HEAD_DIM ** -0.5)
    p = jax.nn.softmax(s, axis=-1).astype(v_m.dtype)
    return jnp.einsum('bhsm,bmhd->bshd', p, v_m).reshape(B, S, MEM_WIDTH)


def hybrid_layer(x, mem, cos, sin, g_pre, w_in, a_ln_g, a_ln_b, a_w_s, a_b_s,
                 q_norm_g, w_uq, kv_norm_g, w_ukv, mem_norm_g, w_mem_kv,
                 w_gate, b_gate, w_branch, w_out, g_post):
    B, S, D = x.shape
    h = rmsnorm(x, g_pre)
    proj = h @ w_in
    u, v, z_a, c_q, c_kv, k_rope, z_b, q_m, z_m = jnp.split(proj, IN_SPLITS, axis=-1)

    y_a = chunked_spatial_gating(u, v, a_ln_g, a_ln_b, a_w_s, a_b_s) * jax.nn.silu(z_a)
    y_b = latent_attention(c_q, c_kv, k_rope, cos, sin, q_norm_g, w_uq, kv_norm_g, w_ukv) * jax.nn.silu(z_b)
    y_m = memory_attention(q_m, mem, mem_norm_g, w_mem_kv) * jax.nn.silu(z_m)

    y = jnp.stack([y_a, y_b, y_m], axis=2)
    p = jnp.einsum('bsnc,ncd->bsnd', y, w_branch)
    gates = jax.nn.sigmoid(h @ w_gate + b_gate).reshape(B, S, N_BRANCH, D)
    merged = jnp.sum(gates * p, axis=2)
    out = merged @ w_out
    return x + rmsnorm(out, g_post)


def _fwd_setup_inputs(seed: int = 0) -> dict:
    key = jax.random.key(seed)
    ks = jax.random.split(key, 24)
    f32 = jnp.float32
    L, D = DEPTH, D_MODEL

    def w(k, shape, fan_in):
        return jax.random.normal(k, shape, f32) * (fan_in ** -0.5)

    def gain(k, shape):
        return 1.0 + 0.02 * jax.random.normal(k, shape, f32)

    def bias(k, shape):
        return 0.01 * jax.random.normal(k, shape, f32)

    return {
        "x": jax.random.normal(ks[0], (BATCH, SEQ, D), f32),
        "mem": jax.random.normal(ks[1], (BATCH, MEM_LEN, D), f32),
        "positions": jnp.broadcast_to(jnp.arange(SEQ, dtype=jnp.int32)[None], (BATCH, SEQ)),
        "g_pre": gain(ks[2], (L, D)),
        "w_in": w(ks[3], (L, D, IN_TOTAL), D),
        "a_ln_g": gain(ks[4], (L, A_WIDTH)),
        "a_ln_b": bias(ks[5], (L, A_WIDTH)),
        "a_w_s": w(ks[6], (L, A_GROUPS, CHUNK, CHUNK), CHUNK),
        "a_b_s": gain(ks[7], (L, A_GROUPS, CHUNK)),
        "q_norm_g": gain(ks[8], (L, Q_LORA)),
        "w_uq": w(ks[9], (L, Q_LORA, MLA_HEADS * QK_DIM), Q_LORA),
        "kv_norm_g": gain(ks[10], (L, KV_LORA)),
        "w_ukv": w(ks[11], (L, KV_LORA, MLA_HEADS * (QK_NOPE + V_DIM)), KV_LORA),
        "mem_norm_g": gain(ks[12], (L, D)),
        "w_mem_kv": w(ks[13], (L, D, 2 * MEM_WIDTH), D),
        "w_gate": w(ks[14], (L, D, N_BRANCH * D), D),
        "b_gate": bias(ks[15], (L, N_BRANCH * D)),
        "w_branch": w(ks[16], (L, N_BRANCH, BRANCH_WIDTH, D), BRANCH_WIDTH),
        "w_out": w(ks[17], (L, D, D), D),
        "g_post": gain(ks[18], (L, D)),
    }


def _fwd_reference(x, mem, positions, g_pre, w_in, a_ln_g, a_ln_b, a_w_s, a_b_s,
              q_norm_g, w_uq, kv_norm_g, w_ukv, mem_norm_g, w_mem_kv,
              w_gate, b_gate, w_branch, w_out, g_post):
    cos, sin = rope_tables(positions)
    for l in range(DEPTH):
        x = hybrid_layer(x, mem, cos, sin, g_pre[l], w_in[l], a_ln_g[l], a_ln_b[l],
                         a_w_s[l], a_b_s[l], q_norm_g[l], w_uq[l], kv_norm_g[l], w_ukv[l],
                         mem_norm_g[l], w_mem_kv[l], w_gate[l], b_gate[l], w_branch[l],
                         w_out[l], g_post[l])
    return x


import jax as _jax
import jax.numpy as _jnp

TWIN_FORMAT = 'train_step'
FWD_PARAMS = ['x', 'mem', 'positions', 'g_pre', 'w_in', 'a_ln_g', 'a_ln_b', 'a_w_s', 'a_b_s', 'q_norm_g', 'w_uq', 'kv_norm_g', 'w_ukv', 'mem_norm_g', 'w_mem_kv', 'w_gate', 'b_gate', 'w_branch', 'w_out', 'g_post']
TWIN_WEIGHTS = ['g_pre', 'w_in', 'a_ln_g', 'a_ln_b', 'a_w_s', 'a_b_s', 'q_norm_g', 'w_uq', 'kv_norm_g', 'w_ukv', 'mem_norm_g', 'w_mem_kv', 'w_gate', 'b_gate', 'w_branch', 'w_out', 'g_post']
TWIN_DIFF_INPUT = 'x'
TWIN_INPUTS = ['x', 'mem', 'positions', 'g_pre', 'w_in', 'a_ln_g', 'a_ln_b', 'a_w_s', 'a_b_s', 'q_norm_g', 'w_uq', 'kv_norm_g', 'w_ukv', 'mem_norm_g', 'w_mem_kv', 'w_gate', 'b_gate', 'w_branch', 'w_out', 'g_post', 'loss_target', 'm_g_pre', 'm_w_in', 'm_a_ln_g', 'm_a_ln_b', 'm_a_w_s', 'm_a_b_s', 'm_q_norm_g', 'm_w_uq', 'm_kv_norm_g', 'm_w_ukv', 'm_mem_norm_g', 'm_w_mem_kv', 'm_w_gate', 'm_b_gate', 'm_w_branch', 'm_w_out', 'm_g_post', 'v_g_pre', 'v_w_in', 'v_a_ln_g', 'v_a_ln_b', 'v_a_w_s', 'v_a_b_s', 'v_q_norm_g', 'v_w_uq', 'v_kv_norm_g', 'v_w_ukv', 'v_mem_norm_g', 'v_w_mem_kv', 'v_w_gate', 'v_b_gate', 'v_w_branch', 'v_w_out', 'v_g_post']
TWIN_OUTPUTS = ['loss', 'grad_x', 'grad_g_pre', 'grad_w_in', 'grad_a_ln_g', 'grad_a_ln_b', 'grad_a_w_s', 'grad_a_b_s', 'grad_q_norm_g', 'grad_w_uq', 'grad_kv_norm_g', 'grad_w_ukv', 'grad_mem_norm_g', 'grad_w_mem_kv', 'grad_w_gate', 'grad_b_gate', 'grad_w_branch', 'grad_w_out', 'grad_g_post', 'delta_g_pre', 'delta_w_in', 'delta_a_ln_g', 'delta_a_ln_b', 'delta_a_w_s', 'delta_a_b_s', 'delta_q_norm_g', 'delta_w_uq', 'delta_kv_norm_g', 'delta_w_ukv', 'delta_mem_norm_g', 'delta_w_mem_kv', 'delta_w_gate', 'delta_b_gate', 'delta_w_branch', 'delta_w_out', 'delta_g_post', 'new_m_g_pre', 'new_m_w_in', 'new_m_a_ln_g', 'new_m_a_ln_b', 'new_m_a_w_s', 'new_m_a_b_s', 'new_m_q_norm_g', 'new_m_w_uq', 'new_m_kv_norm_g', 'new_m_w_ukv', 'new_m_mem_norm_g', 'new_m_w_mem_kv', 'new_m_w_gate', 'new_m_b_gate', 'new_m_w_branch', 'new_m_w_out', 'new_m_g_post', 'new_v_g_pre', 'new_v_w_in', 'new_v_a_ln_g', 'new_v_a_ln_b', 'new_v_a_w_s', 'new_v_a_b_s', 'new_v_q_norm_g', 'new_v_w_uq', 'new_v_kv_norm_g', 'new_v_w_ukv', 'new_v_mem_norm_g', 'new_v_w_mem_kv', 'new_v_w_gate', 'new_v_b_gate', 'new_v_w_branch', 'new_v_w_out', 'new_v_g_post']
TWIN_LEAF_KINDS = {'loss': 'loss', 'grad_x': 'grad_x', 'grad_g_pre': 'grad_w', 'grad_w_in': 'grad_w', 'grad_a_ln_g': 'grad_w', 'grad_a_ln_b': 'grad_w', 'grad_a_w_s': 'grad_w', 'grad_a_b_s': 'grad_w', 'grad_q_norm_g': 'grad_w', 'grad_w_uq': 'grad_w', 'grad_kv_norm_g': 'grad_w', 'grad_w_ukv': 'grad_w', 'grad_mem_norm_g': 'grad_w', 'grad_w_mem_kv': 'grad_w', 'grad_w_gate': 'grad_w', 'grad_b_gate': 'grad_w', 'grad_w_branch': 'grad_w', 'grad_w_out': 'grad_w', 'grad_g_post': 'grad_w', 'delta_g_pre': 'delta_w', 'delta_w_in': 'delta_w', 'delta_a_ln_g': 'delta_w', 'delta_a_ln_b': 'delta_w', 'delta_a_w_s': 'delta_w', 'delta_a_b_s': 'delta_w', 'delta_q_norm_g': 'delta_w', 'delta_w_uq': 'delta_w', 'delta_kv_norm_g': 'delta_w', 'delta_w_ukv': 'delta_w', 'delta_mem_norm_g': 'delta_w', 'delta_w_mem_kv': 'delta_w', 'delta_w_gate': 'delta_w', 'delta_b_gate': 'delta_w', 'delta_w_branch': 'delta_w', 'delta_w_out': 'delta_w', 'delta_g_post': 'delta_w', 'new_m_g_pre': 'new_m', 'new_m_w_in': 'new_m', 'new_m_a_ln_g': 'new_m', 'new_m_a_ln_b': 'new_m', 'new_m_a_w_s': 'new_m', 'new_m_a_b_s': 'new_m', 'new_m_q_norm_g': 'new_m', 'new_m_w_uq': 'new_m', 'new_m_kv_norm_g': 'new_m', 'new_m_w_ukv': 'new_m', 'new_m_mem_norm_g': 'new_m', 'new_m_w_mem_kv': 'new_m', 'new_m_w_gate': 'new_m', 'new_m_b_gate': 'new_m', 'new_m_w_branch': 'new_m', 'new_m_w_out': 'new_m', 'new_m_g_post': 'new_m', 'new_v_g_pre': 'new_v', 'new_v_w_in': 'new_v', 'new_v_a_ln_g': 'new_v', 'new_v_a_ln_b': 'new_v', 'new_v_a_w_s': 'new_v', 'new_v_a_b_s': 'new_v', 'new_v_q_norm_g': 'new_v', 'new_v_w_uq': 'new_v', 'new_v_kv_norm_g': 'new_v', 'new_v_w_ukv': 'new_v', 'new_v_mem_norm_g': 'new_v', 'new_v_w_mem_kv': 'new_v', 'new_v_w_gate': 'new_v', 'new_v_b_gate': 'new_v', 'new_v_w_branch': 'new_v', 'new_v_w_out': 'new_v', 'new_v_g_post': 'new_v'}


def _forward(args):
    return _fwd_reference(*[args[k] for k in FWD_PARAMS])


def _output_shape():
    def fwd():
        inp = _fwd_setup_inputs(0)
        return _fwd_reference(*[inp[k] for k in FWD_PARAMS])
    out = _jax.eval_shape(fwd)
    return out.shape, out.dtype

N_MICROBATCH = 1
ADAM_LR = 0.001
ADAM_B1 = 0.9
ADAM_B2 = 0.999
ADAM_EPS = 1e-08
ADAM_WD = 0.01
ADAM_STEP = 10
PER_EXAMPLE_BATCH_AXIS = {'x': 0, 'mem': 0, 'positions': 0, 'loss_target': 0}
SHARED_INPUTS = []
_WEIGHT_DTYPES = {'g_pre': _jnp.float32, 'w_in': _jnp.float32, 'a_ln_g': _jnp.float32, 'a_ln_b': _jnp.float32, 'a_w_s': _jnp.float32, 'a_b_s': _jnp.float32, 'q_norm_g': _jnp.float32, 'w_uq': _jnp.float32, 'kv_norm_g': _jnp.float32, 'w_ukv': _jnp.float32, 'mem_norm_g': _jnp.float32, 'w_mem_kv': _jnp.float32, 'w_gate': _jnp.float32, 'b_gate': _jnp.float32, 'w_branch': _jnp.float32, 'w_out': _jnp.float32, 'g_post': _jnp.float32}
MOMENT_SCALE = {'g_pre': 3.115882e-01, 'w_in': 1.208544e-01, 'a_ln_g': 9.808793e-02, 'a_ln_b': 9.644392e-02, 'a_w_s': 9.938458e-02, 'a_b_s': 1.442057e-01, 'q_norm_g': 7.682139e-02, 'w_uq': 3.202050e-02, 'kv_norm_g': 1.200300e-01, 'w_ukv': 3.996234e-02, 'mem_norm_g': 3.584531e-02, 'w_mem_kv': 2.576807e-02, 'w_gate': 4.125095e-02, 'b_gate': 5.284964e-02, 'w_branch': 1.277058e-01, 'w_out': 2.221843e-01, 'g_post': 3.205064e+01}


def _to_microbatches(a, axis):
    t = _jnp.moveaxis(a, axis, 0)
    t = t.reshape((N_MICROBATCH, t.shape[0] // N_MICROBATCH) + t.shape[1:])
    return _jnp.moveaxis(t, 1, axis + 1)


def setup_inputs(seed: int = 0) -> dict:
    inp = _fwd_setup_inputs(seed)
    key = _jax.random.fold_in(_jax.random.key(seed), 7919)
    shape, _ = _output_shape()
    out = dict(inp)
    out["loss_target"] = _jax.random.normal(_jax.random.fold_in(key, 0), shape, _jnp.float32)
    for i, name in enumerate(TWIN_WEIGHTS):
        w = inp[name].astype(_jnp.float32)
        if MOMENT_SCALE is None:
            s = _jnp.sqrt(_jnp.mean(_jnp.square(w)) + 1e-30)
        else:
            s = MOMENT_SCALE[name]
        km, kv = _jax.random.split(_jax.random.fold_in(key, i + 1))
        out[name] = w
        out["m_" + name] = s * _jax.random.normal(km, w.shape, _jnp.float32)
        out["v_" + name] = (s * s) * _jax.random.uniform(kv, w.shape, _jnp.float32, 0.5, 1.5)
    if N_MICROBATCH > 1:
        for name, axis in PER_EXAMPLE_BATCH_AXIS.items():
            out[name] = _to_microbatches(out[name], axis)
    return {'x': out['x'], 'mem': out['mem'], 'positions': out['positions'], 'g_pre': out['g_pre'], 'w_in': out['w_in'], 'a_ln_g': out['a_ln_g'], 'a_ln_b': out['a_ln_b'], 'a_w_s': out['a_w_s'], 'a_b_s': out['a_b_s'], 'q_norm_g': out['q_norm_g'], 'w_uq': out['w_uq'], 'kv_norm_g': out['kv_norm_g'], 'w_ukv': out['w_ukv'], 'mem_norm_g': out['mem_norm_g'], 'w_mem_kv': out['w_mem_kv'], 'w_gate': out['w_gate'], 'b_gate': out['b_gate'], 'w_branch': out['w_branch'], 'w_out': out['w_out'], 'g_post': out['g_post'], 'loss_target': out['loss_target'], 'm_g_pre': out['m_g_pre'], 'm_w_in': out['m_w_in'], 'm_a_ln_g': out['m_a_ln_g'], 'm_a_ln_b': out['m_a_ln_b'], 'm_a_w_s': out['m_a_w_s'], 'm_a_b_s': out['m_a_b_s'], 'm_q_norm_g': out['m_q_norm_g'], 'm_w_uq': out['m_w_uq'], 'm_kv_norm_g': out['m_kv_norm_g'], 'm_w_ukv': out['m_w_ukv'], 'm_mem_norm_g': out['m_mem_norm_g'], 'm_w_mem_kv': out['m_w_mem_kv'], 'm_w_gate': out['m_w_gate'], 'm_b_gate': out['m_b_gate'], 'm_w_branch': out['m_w_branch'], 'm_w_out': out['m_w_out'], 'm_g_post': out['m_g_post'], 'v_g_pre': out['v_g_pre'], 'v_w_in': out['v_w_in'], 'v_a_ln_g': out['v_a_ln_g'], 'v_a_ln_b': out['v_a_ln_b'], 'v_a_w_s': out['v_a_w_s'], 'v_a_b_s': out['v_a_b_s'], 'v_q_norm_g': out['v_q_norm_g'], 'v_w_uq': out['v_w_uq'], 'v_kv_norm_g': out['v_kv_norm_g'], 'v_w_ukv': out['v_w_ukv'], 'v_mem_norm_g': out['v_mem_norm_g'], 'v_w_mem_kv': out['v_w_mem_kv'], 'v_w_gate': out['v_w_gate'], 'v_b_gate': out['v_b_gate'], 'v_w_branch': out['v_w_branch'], 'v_w_out': out['v_w_out'], 'v_g_post': out['v_g_post']}


def _loss(weights, diff, rest, loss_target):
    with _jax.named_scope("forward"):
        args = {**rest, TWIN_DIFF_INPUT: diff, **{k: w.astype(_WEIGHT_DTYPES[k]) for k, w in weights.items()}}
        y = _forward(args)
    with _jax.named_scope("loss_head"):
        err = _jnp.square(y.astype(_jnp.float32) - loss_target)
        return 0.5 * _jnp.sum(_jnp.mean(err, axis=-1)) if err.ndim else 0.5 * err


def _adamw(w, g, m, v):
    m = ADAM_B1 * m + (1.0 - ADAM_B1) * g
    v = ADAM_B2 * v + (1.0 - ADAM_B2) * _jnp.square(g)
    m_hat = m / (1.0 - ADAM_B1 ** ADAM_STEP)
    v_hat = v / (1.0 - ADAM_B2 ** ADAM_STEP)
    delta = -ADAM_LR * (m_hat / (_jnp.sqrt(v_hat) + ADAM_EPS) + ADAM_WD * w)
    return delta, m, v


def reference(x, mem, positions, g_pre, w_in, a_ln_g, a_ln_b, a_w_s, a_b_s, q_norm_g, w_uq, kv_norm_g, w_ukv, mem_norm_g, w_mem_kv, w_gate, b_gate, w_branch, w_out, g_post, loss_target, m_g_pre, m_w_in, m_a_ln_g, m_a_ln_b, m_a_w_s, m_a_b_s, m_q_norm_g, m_w_uq, m_kv_norm_g, m_w_ukv, m_mem_norm_g, m_w_mem_kv, m_w_gate, m_b_gate, m_w_branch, m_w_out, m_g_post, v_g_pre, v_w_in, v_a_ln_g, v_a_ln_b, v_a_w_s, v_a_b_s, v_q_norm_g, v_w_uq, v_kv_norm_g, v_w_ukv, v_mem_norm_g, v_w_mem_kv, v_w_gate, v_b_gate, v_w_branch, v_w_out, v_g_post):
    given = dict(x=x, mem=mem, positions=positions, g_pre=g_pre, w_in=w_in, a_ln_g=a_ln_g, a_ln_b=a_ln_b, a_w_s=a_w_s, a_b_s=a_b_s, q_norm_g=q_norm_g, w_uq=w_uq, kv_norm_g=kv_norm_g, w_ukv=w_ukv, mem_norm_g=mem_norm_g, w_mem_kv=w_mem_kv, w_gate=w_gate, b_gate=b_gate, w_branch=w_branch, w_out=w_out, g_post=g_post, loss_target=loss_target, m_g_pre=m_g_pre, m_w_in=m_w_in, m_a_ln_g=m_a_ln_g, m_a_ln_b=m_a_ln_b, m_a_w_s=m_a_w_s, m_a_b_s=m_a_b_s, m_q_norm_g=m_q_norm_g, m_w_uq=m_w_uq, m_kv_norm_g=m_kv_norm_g, m_w_ukv=m_w_ukv, m_mem_norm_g=m_mem_norm_g, m_w_mem_kv=m_w_mem_kv, m_w_gate=m_w_gate, m_b_gate=m_b_gate, m_w_branch=m_w_branch, m_w_out=m_w_out, m_g_post=m_g_post, v_g_pre=v_g_pre, v_w_in=v_w_in, v_a_ln_g=v_a_ln_g, v_a_ln_b=v_a_ln_b, v_a_w_s=v_a_w_s, v_a_b_s=v_a_b_s, v_q_norm_g=v_q_norm_g, v_w_uq=v_w_uq, v_kv_norm_g=v_kv_norm_g, v_w_ukv=v_w_ukv, v_mem_norm_g=v_mem_norm_g, v_w_mem_kv=v_w_mem_kv, v_w_gate=v_w_gate, v_b_gate=v_b_gate, v_w_branch=v_w_branch, v_w_out=v_w_out, v_g_post=v_g_post)
    weights = {n: given[n] for n in TWIN_WEIGHTS}
    shared = {n: given[n] for n in SHARED_INPUTS}
    per_example = {n: given[n] for n in ['x', 'mem', 'positions']}
    grad_fn = _jax.value_and_grad(_loss, argnums=(0, 1))

    def one_microbatch(ex, loss_target):
        ex = dict(ex)
        diff = ex.pop(TWIN_DIFF_INPUT)
        return grad_fn(weights, diff, {**shared, **ex}, loss_target)

    if N_MICROBATCH == 1:
        loss, (grad_w, grad_x) = one_microbatch(per_example, given["loss_target"])
    else:
        def body(carry, xs):
            loss_sum, grad_sum = carry
            l_k, (gw_k, gx_k) = one_microbatch(xs[0], xs[1])
            with _jax.named_scope("update"):
                return (loss_sum + l_k, _jax.tree.map(_jnp.add, grad_sum, gw_k)), gx_k

        init = (_jnp.zeros((), _jnp.float32), _jax.tree.map(_jnp.zeros_like, weights))
        (loss, grad_w), grad_x = _jax.lax.scan(body, init, (per_example, given["loss_target"]))
    with _jax.named_scope("update"):
        delta_w, new_m, new_v = {}, {}, {}
        for n in TWIN_WEIGHTS:
            delta_w[n], new_m[n], new_v[n] = _adamw(weights[n], grad_w[n], given["m_" + n], given["v_" + n])
    return (loss, grad_x, *[grad_w[n] for n in TWIN_WEIGHTS], *[delta_w[n] for n in TWIN_WEIGHTS],
            *[new_m[n] for n in TWIN_WEIGHTS], *[new_v[n] for n in TWIN_WEIGHTS])
```

```python
import functools
import math

import jax
import jax.numpy as jnp
import numpy as np
from jax import lax
from jax.experimental import pallas as pl
from jax.experimental.pallas import tpu as pltpu

F32 = jnp.float32
BF16 = jnp.bfloat16
MESH = pl.DeviceIdType.MESH
AXES = ("x", "y", "c")
N_DEV = 8

D_MODEL = 2048
EPS = 1e-6
CHUNK = 128
A_GROUPS = 16
MLA_HEADS = 16
QK_NOPE = 128
QK_ROPE = 64
QK_DIM = QK_NOPE + QK_ROPE
HEAD_PAD = 256
Q_LORA = 512
KV_LORA = 512
MEM_HEADS = 4
MEM_HEAD_DIM = 512
ROPE_THETA = 10000.0
MLA_SCALE = QK_DIM ** -0.5
MEM_SCALE = MEM_HEAD_DIM ** -0.5
NEG = -1e30

ADAM_LR = 0.001
ADAM_B1 = 0.9
ADAM_B2 = 0.999
ADAM_EPS = 1e-08
ADAM_WD = 0.01
ADAM_STEP = 10

BIG_W = 6 * D_MODEL
LAT_W = Q_LORA + KV_LORA + 128

VMEM_MIB = 1024 * 1024

ROW_BLK = 256
ATT_BLK = 512
GMLP_ROWS = 256
MEM_Q_BLK = 512


def _params(vmem_mib, **kw):
    return pltpu.CompilerParams(vmem_limit_bytes=int(vmem_mib * VMEM_MIB), **kw)


def _gelu(x):
    k = math.sqrt(2.0 / math.pi)
    t = jnp.tanh(k * (x + 0.044715 * (x * x * x)))
    return 0.5 * x * (1.0 + t)


def _gelu_and_grad(x):
    k = math.sqrt(2.0 / math.pi)
    x2 = x * x
    t = jnp.tanh(k * (x + 0.044715 * (x2 * x)))
    val = 0.5 * x * (1.0 + t)
    grad = 0.5 * (1.0 + t) + 0.5 * x * (1.0 - t * t) * (k * (1.0 + 3.0 * 0.044715 * x2))
    return val, grad


def _silu_and_grad(z):
    s = jax.nn.sigmoid(z)
    return z * s, s * (1.0 + z * (1.0 - s))


def _mm(a, b, *, name, tm, tn, tk, ta=False, tb=False, out_dtype=BF16, bias=None, act=None, add=None):
    m = a.shape[1] if ta else a.shape[0]
    k = a.shape[0] if ta else a.shape[1]
    n = b.shape[0] if tb else b.shape[1]
    assert k == (b.shape[1] if tb else b.shape[0])
    tm, tn, tk = min(tm, m), min(tn, n), min(tk, k)
    assert m % tm == 0 and n % tn == 0 and k % tk == 0, (name, m, n, k, tm, tn, tk)
    nk = k // tk
    a_spec = pl.BlockSpec((tk, tm), lambda i, j, kk: (kk, i)) if ta else pl.BlockSpec((tm, tk), lambda i, j, kk: (i, kk))
    b_spec = pl.BlockSpec((tn, tk), lambda i, j, kk: (j, kk)) if tb else pl.BlockSpec((tk, tn), lambda i, j, kk: (kk, j))
    dn = (((0 if ta else 1,), (1 if tb else 0,)), ((), ()))
    operands, in_specs = [a, b], [a_spec, b_spec]
    if bias is not None:
        operands.append(bias)
        in_specs.append(pl.BlockSpec((1, tn), lambda i, j, kk: (0, j)))
    if add is not None:
        operands.append(add)
        in_specs.append(pl.BlockSpec((tm, tn), lambda i, j, kk: (i, j)))

    def body(*refs):
        a_ref, b_ref = refs[0], refs[1]
        pos = 2
        bias_ref = add_ref = None
        if bias is not None:
            bias_ref = refs[pos]
            pos += 1
        if add is not None:
            add_ref = refs[pos]
            pos += 1
        o_ref = refs[pos]
        part = lax.dot_general(a_ref[...], b_ref[...], dn, preferred_element_type=F32)

        def finish(acc):
            if bias_ref is not None:
                acc = acc + bias_ref[...]
            if act == "sigmoid":
                acc = jax.nn.sigmoid(acc)
            if add_ref is not None:
                acc = acc + add_ref[...]
            o_ref[...] = acc.astype(o_ref.dtype)

        if nk == 1:
            finish(part)
        else:
            acc_ref = refs[pos + 1]
            kk = pl.program_id(2)

            @pl.when(kk == 0)
            def _():
                acc_ref[...] = part

            @pl.when(kk > 0)
            def _():
                acc_ref[...] += part

            @pl.when(kk == nk - 1)
            def _():
                finish(acc_ref[...])

    osz = jnp.dtype(out_dtype).itemsize
    est = 2 * 2 * (tm * tk + tk * tn) + 2 * osz * tm * tn + 8 * tm * tn + (2 * 4 * tm * tn if add is not None else 0)
    return pl.pallas_call(
        body,
        name=name,
        grid=(m // tm, n // tn, nk),
        in_specs=in_specs,
        out_specs=pl.BlockSpec((tm, tn), lambda i, j, kk: (i, j)),
        out_shape=jax.ShapeDtypeStruct((m, n), out_dtype),
        scratch_shapes=[pltpu.VMEM((tm, tn), F32)] if nk > 1 else [],
        compiler_params=_params(min(56, est / VMEM_MIB + 12)),
    )(*operands)


def _row_spec(tr, cols, col_blk=0):
    return pl.BlockSpec((tr, cols), lambda i: (i, col_blk))


def _full_spec(shape):
    nd = len(shape)
    return pl.BlockSpec(shape, lambda i: (0,) * nd)


def _pre_norm(x, g_pre):
    t, d = x.shape
    tr = min(ROW_BLK, t)

    def body(x_ref, g_ref, h_ref):
        xv = x_ref[...]
        r = lax.rsqrt(jnp.mean(xv * xv, axis=-1, keepdims=True) + EPS)
        h_ref[...] = ((xv * r) * g_ref[...]).astype(BF16)

    return pl.pallas_call(
        body, name="pre_norm", grid=(t // tr,),
        in_specs=[_row_spec(tr, d), _full_spec((1, d))],
        out_specs=_row_spec(tr, d),
        out_shape=jax.ShapeDtypeStruct((t, d), BF16),
        compiler_params=_params(32),
    )(x, g_pre)


def _rope_tables(pos_col, inv_freq_lanes):
    t = pos_col.shape[0]
    tr = min(ROW_BLK, t)

    def body(p_ref, f_ref, c_ref, sa_ref, sb_ref):
        ang = p_ref[...].astype(F32) * f_ref[...]
        lane = lax.broadcasted_iota(jnp.int32, ang.shape, 1)
        cos, sin = jnp.cos(ang), jnp.sin(ang)
        c_ref[...] = jnp.where(lane < QK_ROPE, cos, 0.0)
        sa_ref[...] = jnp.where(lane < QK_ROPE // 2, sin, 0.0)
        sb_ref[...] = jnp.where((lane >= QK_ROPE // 2) & (lane < QK_ROPE), sin, 0.0)

    tab = jax.ShapeDtypeStruct((t, 128), F32)
    return pl.pallas_call(
        body, name="rope_tables", grid=(t // tr,),
        in_specs=[_row_spec(tr, 1), _full_spec((1, 128))],
        out_specs=[_row_spec(tr, 128)] * 3,
        out_shape=[tab, tab, tab],
    )(pos_col, inv_freq_lanes)


def _rope_fwd(p, c, sa, sb):
    return p * c - pltpu.roll(p, 96, 1) * sa + pltpu.roll(p, 32, 1) * sb


def _rope_bwd(g, c, sa, sb):
    return g * c + pltpu.roll(g, 96, 1) * sa - pltpu.roll(g, 32, 1) * sb


def _rms(xv, g):
    r = lax.rsqrt(jnp.mean(xv * xv, axis=-1, keepdims=True) + EPS)
    return (xv * r) * g


def _rms_bwd(xv, g, dout):
    r = lax.rsqrt(jnp.mean(xv * xv, axis=-1, keepdims=True) + EPS)
    xn = xv * r
    dg = jnp.sum(dout * xn, axis=0, keepdims=True)
    dxn = dout * g
    dx = r * (dxn - xn * jnp.mean(dxn * xn, axis=-1, keepdims=True))
    return dx, dg


def _latent_norms(proj_lat, q_norm_g, kv_norm_g, c_tab, sa_tab, sb_tab):
    t = proj_lat.shape[0]
    tr = min(ROW_BLK, t)

    def body(cq_ref, ckv_ref, kr_ref, qg_ref, kg_ref, c_ref, sa_ref, sb_ref, cqn_ref, ckvn_ref, kpe_ref):
        cqn_ref[...] = _rms(cq_ref[...].astype(F32), qg_ref[...]).astype(BF16)
        ckvn_ref[...] = _rms(ckv_ref[...].astype(F32), kg_ref[...]).astype(BF16)
        kpe_ref[...] = _rope_fwd(kr_ref[...].astype(F32), c_ref[...], sa_ref[...], sb_ref[...]).astype(BF16)

    return pl.pallas_call(
        body, name="latent_norms", grid=(t // tr,),
        in_specs=[_row_spec(tr, Q_LORA, 0), _row_spec(tr, KV_LORA, 1), _row_spec(tr, 128, (Q_LORA + KV_LORA) // 128),
                  _full_spec((1, Q_LORA)), _full_spec((1, KV_LORA)),
                  _row_spec(tr, 128), _row_spec(tr, 128), _row_spec(tr, 128)],
        out_specs=[_row_spec(tr, Q_LORA), _row_spec(tr, KV_LORA), _row_spec(tr, 128)],
        out_shape=[jax.ShapeDtypeStruct((t, Q_LORA), BF16), jax.ShapeDtypeStruct((t, KV_LORA), BF16),
                   jax.ShapeDtypeStruct((t, 128), BF16)],
    )(proj_lat, proj_lat, proj_lat, q_norm_g, kv_norm_g, c_tab, sa_tab, sb_tab)


def _rope_q(q_raw, c_tab, sa_tab, sb_tab):
    t = q_raw.shape[0]
    tr = min(ROW_BLK, t)

    def body(q_ref, c_ref, sa_ref, sb_ref, o_ref):
        c, sa, sb = c_ref[...], sa_ref[...], sb_ref[...]
        for h in range(MLA_HEADS):
            o_ref[:, h * HEAD_PAD:h * HEAD_PAD + 128] = q_ref[:, h * HEAD_PAD:h * HEAD_PAD + 128].astype(BF16)
            pe = q_ref[:, h * HEAD_PAD + 128:(h + 1) * HEAD_PAD]
            o_ref[:, h * HEAD_PAD + 128:(h + 1) * HEAD_PAD] = _rope_fwd(pe, c, sa, sb).astype(BF16)

    w = MLA_HEADS * HEAD_PAD
    return pl.pallas_call(
        body, name="rope_q", grid=(t // tr,),
        in_specs=[_row_spec(tr, w), _row_spec(tr, 128), _row_spec(tr, 128), _row_spec(tr, 128)],
        out_specs=_row_spec(tr, w),
        out_shape=jax.ShapeDtypeStruct((t, w), BF16),
        compiler_params=_params(48),
    )(q_raw, c_tab, sa_tab, sb_tab)


def _rope_q_bwd(dq, dkpe_heads, c_tab, sa_tab, sb_tab):
    t = dq.shape[0]
    tr = min(ROW_BLK, t)

    def body(dq_ref, dkp_ref, c_ref, sa_ref, sb_ref, o_ref, dkr_ref):
        c, sa, sb = c_ref[...], sa_ref[...], sb_ref[...]
        for h in range(MLA_HEADS):
            o_ref[:, h * HEAD_PAD:h * HEAD_PAD + 128] = dq_ref[:, h * HEAD_PAD:h * HEAD_PAD + 128].astype(BF16)
            g = dq_ref[:, h * HEAD_PAD + 128:(h + 1) * HEAD_PAD]
            o_ref[:, h * HEAD_PAD + 128:(h + 1) * HEAD_PAD] = _rope_bwd(g, c, sa, sb).astype(BF16)
        tot = dkp_ref[0]
        for h in range(1, MLA_HEADS):
            tot = tot + dkp_ref[h]
        dkr_ref[...] = _rope_bwd(tot, c, sa, sb)

    w = MLA_HEADS * HEAD_PAD
    return pl.pallas_call(
        body, name="rope_q_bwd", grid=(t // tr,),
        in_specs=[_row_spec(tr, w), pl.BlockSpec((MLA_HEADS, tr, 128), lambda i: (0, i, 0)),
                  _row_spec(tr, 128), _row_spec(tr, 128), _row_spec(tr, 128)],
        out_specs=[_row_spec(tr, w), _row_spec(tr, 128)],
        out_shape=[jax.ShapeDtypeStruct((t, w), BF16), jax.ShapeDtypeStruct((t, 128), F32)],
        compiler_params=_params(48),
    )(dq, dkpe_heads, c_tab, sa_tab, sb_tab)


def _latent_norms_bwd(proj_lat, q_norm_g, kv_norm_g, dcqn, dckvn, dkr):
    t = proj_lat.shape[0]
    tr = min(ROW_BLK, t)

    def body(cq_ref, ckv_ref, qg_ref, kg_ref, dcqn_ref, dckvn_ref, dkr_ref, dl_ref, dqg_ref, dkg_ref):
        dcq, dqg = _rms_bwd(cq_ref[...].astype(F32), qg_ref[...], dcqn_ref[...])
        dckv, dkg = _rms_bwd(ckv_ref[...].astype(F32), kg_ref[...], dckvn_ref[...])
        dl_ref[:, 0:Q_LORA] = dcq.astype(BF16)
        dl_ref[:, Q_LORA:Q_LORA + KV_LORA] = dckv.astype(BF16)
        dl_ref[:, Q_LORA + KV_LORA:LAT_W] = dkr_ref[...].astype(BF16)

        @pl.when(pl.program_id(0) == 0)
        def _():
            dqg_ref[...] = jnp.zeros_like(dqg_ref)
            dkg_ref[...] = jnp.zeros_like(dkg_ref)

        dqg_ref[...] += dqg
        dkg_ref[...] += dkg

    return pl.pallas_call(
        body, name="latent_norms_bwd", grid=(t // tr,),
        in_specs=[_row_spec(tr, Q_LORA, 0), _row_spec(tr, KV_LORA, 1), _full_spec((1, Q_LORA)), _full_spec((1, KV_LORA)),
                  _row_spec(tr, Q_LORA), _row_spec(tr, KV_LORA), _row_spec(tr, 128)],
        out_specs=[_row_spec(tr, LAT_W), _full_spec((1, Q_LORA)), _full_spec((1, KV_LORA))],
        out_shape=[jax.ShapeDtypeStruct((t, LAT_W), BF16), jax.ShapeDtypeStruct((1, Q_LORA), F32),
                   jax.ShapeDtypeStruct((1, KV_LORA), F32)],
    )(proj_lat, proj_lat, q_norm_g, kv_norm_g, dcqn, dckvn, dkr)


def _mem_norm(mem, g):
    m, d = mem.shape

    def body(x_ref, g_ref, o_ref):
        o_ref[...] = _rms(x_ref[...], g_ref[...]).astype(BF16)

    return pl.pallas_call(
        body, name="mem_norm", grid=(1,),
        in_specs=[_full_spec((m, d)), _full_spec((1, d))],
        out_specs=_full_spec((m, d)),
        out_shape=jax.ShapeDtypeStruct((m, d), BF16),
    )(mem, g)


def _mem_norm_bwd(mem, g, dmemn):
    m, d = mem.shape

    def body(x_ref, g_ref, d_ref, dg_ref):
        _, dg = _rms_bwd(x_ref[...], g_ref[...], d_ref[...])
        dg_ref[...] = dg

    return pl.pallas_call(
        body, name="mem_norm_bwd", grid=(1,),
        in_specs=[_full_spec((m, d)), _full_spec((1, d)), _full_spec((m, d))],
        out_specs=_full_spec((1, d)),
        out_shape=jax.ShapeDtypeStruct((1, d), F32),
    )(mem, g, dmemn)


def _merge(gates, p_a, p_b, p_m):
    t, d = p_a.shape
    tr = min(ROW_BLK, t)

    def body(ga_ref, gb_ref, gm_ref, pa_ref, pb_ref, pm_ref, o_ref):
        acc = ga_ref[...].astype(F32) * pa_ref[...].astype(F32)
        acc = acc + gb_ref[...].astype(F32) * pb_ref[...].astype(F32)
        acc = acc + gm_ref[...].astype(F32) * pm_ref[...].astype(F32)
        o_ref[...] = acc.astype(BF16)

    return pl.pallas_call(
        body, name="merge", grid=(t // tr,),
        in_specs=[_row_spec(tr, d, 0), _row_spec(tr, d, 1), _row_spec(tr, d, 2),
                  _row_spec(tr, d), _row_spec(tr, d), _row_spec(tr, d)],
        out_specs=_row_spec(tr, d),
        out_shape=jax.ShapeDtypeStruct((t, d), BF16),
        compiler_params=_params(48),
    )(gates, gates, gates, p_a, p_b, p_m)


def _merge_bwd(dm, gates, p_a, p_b, p_m):
    t, d = dm.shape
    tr = min(ROW_BLK, t)

    def body(dm_ref, g_ref, pa_ref, pb_ref, pm_ref, dpa_ref, dpb_ref, dpm_ref, dgl_ref, db_ref):
        dmv = dm_ref[...].astype(F32)

        @pl.when(pl.program_id(0) == 0)
        def _():
            db_ref[...] = jnp.zeros_like(db_ref)

        for n, (p_ref, dp_ref) in enumerate(((pa_ref, dpa_ref), (pb_ref, dpb_ref), (pm_ref, dpm_ref))):
            g = g_ref[:, n * d:(n + 1) * d].astype(F32)
            dp_ref[...] = (dmv * g).astype(BF16)
            dgl = dmv * p_ref[...].astype(F32) * (g * (1.0 - g))
            dgl_ref[:, n * d:(n + 1) * d] = dgl.astype(BF16)
            db_ref[:, n * d:(n + 1) * d] += jnp.sum(dgl, axis=0, keepdims=True)

    act = jax.ShapeDtypeStruct((t, d), BF16)
    return pl.pallas_call(
        body, name="merge_bwd", grid=(t // tr,),
        in_specs=[_row_spec(tr, d), _row_spec(tr, 3 * d), _row_spec(tr, d), _row_spec(tr, d), _row_spec(tr, d)],
        out_specs=[_row_spec(tr, d), _row_spec(tr, d), _row_spec(tr, d), _row_spec(tr, 3 * d), _full_spec((1, 3 * d))],
        out_shape=[act, act, act, jax.ShapeDtypeStruct((t, 3 * d), BF16), jax.ShapeDtypeStruct((1, 3 * d), F32)],
        compiler_params=_params(56),
    )(dm, gates, p_a, p_b, p_m)


def _post_loss(out, x, tgt, g_post):
    t, d = out.shape
    tr = min(ROW_BLK, t)

    def body(o_ref, x_ref, t_ref, g_ref, do_ref, dy_ref, loss_ref, dg_ref):
        ov = o_ref[...]
        g = g_ref[...]
        r = lax.rsqrt(jnp.mean(ov * ov, axis=-1, keepdims=True) + EPS)
        on = ov * r
        err = (x_ref[...] + on * g) - t_ref[...]
        dy = err * (1.0 / d)
        dy_ref[...] = dy
        don = dy * g
        do_ref[...] = (r * (don - on * jnp.mean(don * on, axis=-1, keepdims=True))).astype(BF16)

        @pl.when(pl.program_id(0) == 0)
        def _():
            loss_ref[...] = jnp.zeros_like(loss_ref)
            dg_ref[...] = jnp.zeros_like(dg_ref)

        loss_ref[...] += 0.5 * jnp.sum(jnp.mean(err * err, axis=-1, keepdims=True))
        dg_ref[...] += jnp.sum(dy * on, axis=0, keepdims=True)

    return pl.pallas_call(
        body, name="post_loss", grid=(t // tr,),
        in_specs=[_row_spec(tr, d), _row_spec(tr, d), _row_spec(tr, d), _full_spec((1, d))],
        out_specs=[_row_spec(tr, d), _row_spec(tr, d), _full_spec((8, 128)), _full_spec((1, d))],
        out_shape=[jax.ShapeDtypeStruct((t, d), BF16), jax.ShapeDtypeStruct((t, d), F32),
                   jax.ShapeDtypeStruct((8, 128), F32), jax.ShapeDtypeStruct((1, d), F32)],
        compiler_params=_params(56),
    )(out, x, tgt, g_post)


def _pre_norm_bwd(x, g_pre, dh, dy):
    t, d = x.shape
    tr = min(ROW_BLK, t)

    def body(x_ref, g_ref, dh_ref, dy_ref, dx_ref, dg_ref):
        dx, dg = _rms_bwd(x_ref[...], g_ref[...], dh_ref[...])
        dx_ref[...] = dx + dy_ref[...]

        @pl.when(pl.program_id(0) == 0)
        def _():
            dg_ref[...] = jnp.zeros_like(dg_ref)

        dg_ref[...] += dg

    return pl.pallas_call(
        body, name="pre_norm_bwd", grid=(t // tr,),
        in_specs=[_row_spec(tr, d), _full_spec((1, d)), _row_spec(tr, d), _row_spec(tr, d)],
        out_specs=[_row_spec(tr, d), _full_spec((1, d))],
        out_shape=[jax.ShapeDtypeStruct((t, d), F32), jax.ShapeDtypeStruct((1, d), F32)],
        compiler_params=_params(56),
    )(x, g_pre, dh, dy)


def _causal_mask(n):
    row = lax.broadcasted_iota(jnp.int32, (n, n), 0)
    col = lax.broadcasted_iota(jnp.int32, (n, n), 1)
    return row >= col


def _layernorm_stats(vg):
    mu = jnp.mean(vg, axis=-1, keepdims=True)
    cen = vg - mu
    rstd = lax.rsqrt(jnp.mean(cen * cen, axis=-1, keepdims=True) + EPS)
    return cen * rstd, rstd


def _gmlp_fwd(proj_big, ln_g, ln_b, w_s, b_exp):
    t = proj_big.shape[0]
    rows = min(GMLP_ROWS, t)
    d = D_MODEL

    def body(u_ref, v_ref, z_ref, lg_ref, lb_ref, ws_ref, be_ref, y_ref, vn_scr):
        vhat, _ = _layernorm_stats(_gelu(v_ref[...].astype(F32)))
        vn_scr[...] = (vhat * lg_ref[...] + lb_ref[...]).astype(BF16)
        mask = _causal_mask(CHUNK)
        for g in range(A_GROUPS):
            cols = slice(g * 128, (g + 1) * 128)
            wsm = jnp.where(mask, ws_ref[g], 0.0).astype(BF16)
            for c in range(rows // CHUNK):
                rws = slice(c * CHUNK, (c + 1) * CHUNK)
                sv = jnp.dot(wsm, vn_scr[rws, cols], preferred_element_type=F32) + be_ref[g]
                zs, _ = _silu_and_grad(z_ref[rws, cols].astype(F32))
                y_ref[rws, cols] = (_gelu(u_ref[rws, cols].astype(F32)) * sv * zs).astype(BF16)

    return pl.pallas_call(
        body, name="gmlp_fwd", grid=(t // rows,),
        in_specs=[_row_spec(rows, d, 0), _row_spec(rows, d, 1), _row_spec(rows, d, 2),
                  _full_spec((1, d)), _full_spec((1, d)), _full_spec((A_GROUPS, CHUNK, CHUNK)),
                  _full_spec((A_GROUPS, CHUNK, 128))],
        out_specs=_row_spec(rows, d),
        out_shape=jax.ShapeDtypeStruct((t, d), BF16),
        scratch_shapes=[pltpu.VMEM((rows, d), BF16)],
        compiler_params=_params(40),
    )(proj_big, proj_big, proj_big, ln_g, ln_b, w_s, b_exp)


def _gmlp_bwd(proj_big, dya, ln_g, ln_b, w_s, b_exp):
    t = proj_big.shape[0]
    rows = min(GMLP_ROWS, t)
    d = D_MODEL
    nt = (((1,), (1,)), ((), ()))
    tn = (((0,), (0,)), ((), ()))

    def body(u_ref, v_ref, z_ref, dy_ref, lg_ref, lb_ref, ws_ref, be_ref,
             dp_ref, dws_ref, dbs_ref, dlg_ref, dlb_ref, vn_scr, dvn_scr):
        @pl.when(pl.program_id(0) == 0)
        def _():
            dws_ref[...] = jnp.zeros_like(dws_ref)
            dbs_ref[...] = jnp.zeros_like(dbs_ref)
            dlg_ref[...] = jnp.zeros_like(dlg_ref)
            dlb_ref[...] = jnp.zeros_like(dlb_ref)

        vg, vgrad = _gelu_and_grad(v_ref[...].astype(F32))
        vhat, rstd = _layernorm_stats(vg)
        vn_scr[...] = (vhat * lg_ref[...] + lb_ref[...]).astype(BF16)
        mask = _causal_mask(CHUNK)
        for g in range(A_GROUPS):
            cols = slice(g * 128, (g + 1) * 128)
            wsm = jnp.where(mask, ws_ref[g], 0.0).astype(BF16)
            dws = jnp.zeros((CHUNK, CHUNK), F32)
            dbs = jnp.zeros((CHUNK, 1), F32)
            for c in range(rows // CHUNK):
                rws = slice(c * CHUNK, (c + 1) * CHUNK)
                vn = vn_scr[rws, cols]
                sv = jnp.dot(wsm, vn, preferred_element_type=F32) + be_ref[g]
                ug, ugrad = _gelu_and_grad(u_ref[rws, cols].astype(F32))
                zs, zgrad = _silu_and_grad(z_ref[rws, cols].astype(F32))
                dya = dy_ref[rws, cols].astype(F32)
                dga = dya * zs
                dp_ref[rws, 2 * d + g * 128:2 * d + (g + 1) * 128] = (dya * (ug * sv) * zgrad).astype(BF16)
                dp_ref[rws, cols] = (dga * sv * ugrad).astype(BF16)
                dsv = dga * ug
                dsv16 = dsv.astype(BF16)
                dws = dws + lax.dot_general(dsv16, vn, nt, preferred_element_type=F32)
                dbs = dbs + jnp.sum(dsv, axis=-1, keepdims=True)
                dvn_scr[rws, cols] = lax.dot_general(wsm, dsv16, tn, preferred_element_type=F32)
            dws_ref[g] += jnp.where(mask, dws, 0.0)
            dbs_ref[g] += dbs
        dvn = dvn_scr[...]
        dlg_ref[...] += jnp.sum(dvn * vhat, axis=0, keepdims=True)
        dlb_ref[...] += jnp.sum(dvn, axis=0, keepdims=True)
        dvh = dvn * lg_ref[...]
        dvg = rstd * (dvh - jnp.mean(dvh, axis=-1, keepdims=True) - vhat * jnp.mean(dvh * vhat, axis=-1, keepdims=True))
        dp_ref[:, d:2 * d] = (dvg * vgrad).astype(BF16)

    return pl.pallas_call(
        body, name="gmlp_bwd", grid=(t // rows,),
        in_specs=[_row_spec(rows, d, 0), _row_spec(rows, d, 1), _row_spec(rows, d, 2), _row_spec(rows, d),
                  _full_spec((1, d)), _full_spec((1, d)), _full_spec((A_GROUPS, CHUNK, CHUNK)),
                  _full_spec((A_GROUPS, CHUNK, 128))],
        out_specs=[_row_spec(rows, 3 * d), _full_spec((A_GROUPS, CHUNK, CHUNK)), _full_spec((A_GROUPS, CHUNK, 1)),
                   _full_spec((1, d)), _full_spec((1, d))],
        out_shape=[jax.ShapeDtypeStruct((t, 3 * d), BF16), jax.ShapeDtypeStruct((A_GROUPS, CHUNK, CHUNK), F32),
                   jax.ShapeDtypeStruct((A_GROUPS, CHUNK, 1), F32), jax.ShapeDtypeStruct((1, d), F32),
                   jax.ShapeDtypeStruct((1, d), F32)],
        scratch_shapes=[pltpu.VMEM((rows, d), BF16), pltpu.VMEM((rows, d), F32)],
        compiler_params=_params(48),
    )(proj_big, proj_big, proj_big, dya, ln_g, ln_b, w_s, b_exp)


NT_DIMS = (((1,), (1,)), ((), ()))
TN_DIMS = (((0,), (0,)), ((), ()))


def _mla_fwd(q, kv, kpe, proj_big):
    t = q.shape[0]
    blk = min(ATT_BLK, t)
    nq = t // blk
    zb_blk0 = (3 * D_MODEL) // 128

    def body(q_ref, kv_ref, kp_ref, zb_ref, o_ref, yb_ref, lse_ref, m_scr, l_scr, acc_scr):
        i = pl.program_id(1)
        qv = q_ref[...]
        m_scr[...] = jnp.full_like(m_scr, NEG)
        l_scr[...] = jnp.zeros_like(l_scr)
        acc_scr[...] = jnp.zeros_like(acc_scr)

        def step(j, masked):
            ks = pl.ds(pl.multiple_of(j * blk, blk), blk)
            kc = jnp.concatenate([kv_ref[ks, 0:128], kp_ref[ks, :]], axis=1)
            s = lax.dot_general(qv, kc, NT_DIMS, preferred_element_type=F32) * MLA_SCALE
            if masked:
                s = jnp.where(_causal_mask(blk), s, NEG)
            m_prev = m_scr[...]
            m_new = jnp.maximum(m_prev, jnp.max(s, axis=-1, keepdims=True))
            alpha = jnp.exp(m_prev - m_new)
            p = jnp.exp(s - m_new)
            l_scr[...] = alpha * l_scr[...] + jnp.sum(p, axis=-1, keepdims=True)
            acc_scr[...] = alpha * acc_scr[...] + jnp.dot(p.astype(BF16), kv_ref[ks, 128:256], preferred_element_type=F32)
            m_scr[...] = m_new

        def loop_body(j, carry):
            step(j, False)
            return carry

        lax.fori_loop(0, i, loop_body, 0)
        step(i, True)
        l = l_scr[...]
        o = acc_scr[...] / l
        o_ref[...] = o.astype(BF16)
        zs, _ = _silu_and_grad(zb_ref[...].astype(F32))
        yb_ref[...] = (o * zs).astype(BF16)
        lse_ref[0] = jnp.broadcast_to(m_scr[...] + jnp.log(l), (blk, 128))

    act = jax.ShapeDtypeStruct((t, D_MODEL), BF16)
    return pl.pallas_call(
        body, name="mla_fwd", grid=(MLA_HEADS, nq),
        in_specs=[pl.BlockSpec((blk, HEAD_PAD), lambda h, i: (i, h)),
                  pl.BlockSpec((t, HEAD_PAD), lambda h, i: (0, h)),
                  pl.BlockSpec((t, 128), lambda h, i: (0, 0)),
                  pl.BlockSpec((blk, 128), lambda h, i: (i, zb_blk0 + h))],
        out_specs=[pl.BlockSpec((blk, 128), lambda h, i: (i, h)),
                   pl.BlockSpec((blk, 128), lambda h, i: (i, h)),
                   pl.BlockSpec((1, blk, 128), lambda h, i: (h, i, 0))],
        out_shape=[act, act, jax.ShapeDtypeStruct((MLA_HEADS, t, 128), F32)],
        scratch_shapes=[pltpu.VMEM((blk, 1), F32), pltpu.VMEM((blk, 1), F32), pltpu.VMEM((blk, 128), F32)],
        compiler_params=_params(48),
    )(q, kv, kpe, proj_big)


def _mla_gate_bwd(dyb, proj_big, o, lse):
    t, d = dyb.shape
    tr = min(ROW_BLK, t)

    def body(dy_ref, zb_ref, o_ref, lse_ref, do_ref, dz_ref, ld_ref):
        dy = dy_ref[...].astype(F32)
        ov = o_ref[...].astype(F32)
        zs, zgrad = _silu_and_grad(zb_ref[...].astype(F32))
        do16 = (dy * zs).astype(BF16)
        do_ref[...] = do16
        dz_ref[...] = (dy * ov * zgrad).astype(BF16)
        prod = do16.astype(F32) * ov
        lane = lax.broadcasted_iota(jnp.int32, (tr, 128), 1)
        for h in range(MLA_HEADS):
            delta = jnp.sum(prod[:, h * 128:(h + 1) * 128], axis=-1, keepdims=True)
            ld_ref[h] = jnp.where(lane == 0, lse_ref[h], jnp.broadcast_to(delta, (tr, 128)))

    act = jax.ShapeDtypeStruct((t, d), BF16)
    head_spec = pl.BlockSpec((MLA_HEADS, tr, 128), lambda i: (0, i, 0))
    return pl.pallas_call(
        body, name="mla_gate_bwd", grid=(t // tr,),
        in_specs=[_row_spec(tr, d), _row_spec(tr, d, 3), _row_spec(tr, d), head_spec],
        out_specs=[_row_spec(tr, d), _row_spec(tr, d), head_spec],
        out_shape=[act, act, jax.ShapeDtypeStruct((MLA_HEADS, t, 128), F32)],
        compiler_params=_params(56),
    )(dyb, proj_big, o, lse)


def _mla_bwd(q, kv, kpe, do, ld):
    t = q.shape[0]
    blk = min(ATT_BLK, t)
    nblk = t // blk

    def body(q_ref, do_ref, ld_ref, kv_ref, kp_ref, dq_ref, dkv_ref, dkp_ref, dk_scr, dv_scr):
        j = pl.program_id(1)
        kc = jnp.concatenate([kv_ref[:, 0:128], kp_ref[...]], axis=1)
        vv = kv_ref[:, 128:256]
        dk_scr[...] = jnp.zeros_like(dk_scr)
        dv_scr[...] = jnp.zeros_like(dv_scr)

        @pl.when(j == 0)
        def _():
            dq_ref[...] = jnp.zeros_like(dq_ref)

        def step(i, masked):
            qs = pl.ds(pl.multiple_of(i * blk, blk), blk)
            qv = q_ref[qs, :]
            dov = do_ref[qs, :]
            ldv = ld_ref[0, qs, :]
            lse = ldv[:, 0:1]
            delta = ldv[:, 1:2]
            s = lax.dot_general(qv, kc, NT_DIMS, preferred_element_type=F32) * MLA_SCALE
            if masked:
                s = jnp.where(_causal_mask(blk), s, NEG)
            p = jnp.exp(s - lse)
            dv_scr[...] += lax.dot_general(p.astype(BF16), dov, TN_DIMS, preferred_element_type=F32)
            dp = lax.dot_general(dov, vv, NT_DIMS, preferred_element_type=F32)
            ds = (p * (dp - delta) * MLA_SCALE).astype(BF16)
            dk_scr[...] += lax.dot_general(ds, qv, TN_DIMS, preferred_element_type=F32)
            dq_ref[qs, :] += jnp.dot(ds, kc, preferred_element_type=F32)

        step(j, True)

        def loop_body(i, carry):
            step(i, False)
            return carry

        lax.fori_loop(j + 1, nblk, loop_body, 0)
        dkv_ref[:, 0:128] = dk_scr[:, 0:128].astype(BF16)
        dkv_ref[:, 128:256] = dv_scr[...].astype(BF16)
        dkp_ref[0] = dk_scr[:, 128:256]

    return pl.pallas_call(
        body, name="mla_bwd", grid=(MLA_HEADS, nblk),
        in_specs=[pl.BlockSpec((t, HEAD_PAD), lambda h, j: (0, h)),
                  pl.BlockSpec((t, 128), lambda h, j: (0, h)),
                  pl.BlockSpec((1, t, 128), lambda h, j: (h, 0, 0)),
                  pl.BlockSpec((blk, HEAD_PAD), lambda h, j: (j, h)),
                  pl.BlockSpec((blk, 128), lambda h, j: (j, 0))],
        out_specs=[pl.BlockSpec((t, HEAD_PAD), lambda h, j: (0, h)),
                   pl.BlockSpec((blk, HEAD_PAD), lambda h, j: (j, h)),
                   pl.BlockSpec((1, blk, 128), lambda h, j: (h, j, 0))],
        out_shape=[jax.ShapeDtypeStruct((t, MLA_HEADS * HEAD_PAD), F32),
                   jax.ShapeDtypeStruct((t, 2 * D_MODEL), BF16),
                   jax.ShapeDtypeStruct((MLA_HEADS, t, 128), F32)],
        scratch_shapes=[pltpu.VMEM((blk, HEAD_PAD), F32), pltpu.VMEM((blk, 128), F32)],
        compiler_params=_params(58),
    )(q, do, ld, kv, kpe)


def _mem_attn_probs(qv, k_ref):
    s = lax.dot_general(qv, k_ref[...], NT_DIMS, preferred_element_type=F32) * MEM_SCALE
    e = jnp.exp(s - jnp.max(s, axis=-1, keepdims=True))
    return e / jnp.sum(e, axis=-1, keepdims=True)


def _mem_fwd(proj_big, kv_m):
    t = proj_big.shape[0]
    tq = min(MEM_Q_BLK, t)
    hd = MEM_HEAD_DIM
    q0, z0 = (4 * D_MODEL) // hd, (5 * D_MODEL) // hd

    def body(q_ref, z_ref, k_ref, v_ref, y_ref):
        p = _mem_attn_probs(q_ref[...], k_ref)
        o = jnp.dot(p.astype(BF16), v_ref[...], preferred_element_type=F32)
        zs, _ = _silu_and_grad(z_ref[...].astype(F32))
        y_ref[...] = (o * zs).astype(BF16)

    return pl.pallas_call(
        body, name="mem_fwd", grid=(t // tq, MEM_HEADS),
        in_specs=[pl.BlockSpec((tq, hd), lambda i, h: (i, q0 + h)), pl.BlockSpec((tq, hd), lambda i, h: (i, z0 + h)),
                  pl.BlockSpec((kv_m.shape[0], hd), lambda i, h: (0, h)),
                  pl.BlockSpec((kv_m.shape[0], hd), lambda i, h: (0, MEM_HEADS + h))],
        out_specs=pl.BlockSpec((tq, hd), lambda i, h: (i, h)),
        out_shape=jax.ShapeDtypeStruct((t, D_MODEL), BF16),
    )(proj_big, proj_big, kv_m, kv_m)


def _mem_bwd(proj_big, kv_m, dym):
    t = proj_big.shape[0]
    tq = min(MEM_Q_BLK, t)
    nq = t // tq
    hd = MEM_HEAD_DIM
    q0, z0 = (4 * D_MODEL) // hd, (5 * D_MODEL) // hd
    mlen = kv_m.shape[0]

    def body(q_ref, z_ref, k_ref, v_ref, dy_ref, dq_ref, dz_ref, dk_ref, dv_ref):
        @pl.when(pl.program_id(1) == 0)
        def _():
            dk_ref[...] = jnp.zeros_like(dk_ref)
            dv_ref[...] = jnp.zeros_like(dv_ref)

        qv = q_ref[...]
        p = _mem_attn_probs(qv, k_ref)
        p16 = p.astype(BF16)
        o = jnp.dot(p16, v_ref[...], preferred_element_type=F32)
        zs, zgrad = _silu_and_grad(z_ref[...].astype(F32))
        dy = dy_ref[...].astype(F32)
        dz_ref[...] = (dy * o * zgrad).astype(BF16)
        do = dy * zs
        do16 = do.astype(BF16)
        dv_ref[...] += lax.dot_general(p16, do16, TN_DIMS, preferred_element_type=F32)
        dp = lax.dot_general(do16, v_ref[...], NT_DIMS, preferred_element_type=F32)
        ds = (p * (dp - jnp.sum(dp * p, axis=-1, keepdims=True)) * MEM_SCALE).astype(BF16)
        dq_ref[...] = jnp.dot(ds, k_ref[...], preferred_element_type=F32).astype(BF16)
        dk_ref[...] += lax.dot_general(ds, qv, TN_DIMS, preferred_element_type=F32)

    return pl.pallas_call(
        body, name="mem_bwd", grid=(MEM_HEADS, nq),
        in_specs=[pl.BlockSpec((tq, hd), lambda h, i: (i, q0 + h)), pl.BlockSpec((tq, hd), lambda h, i: (i, z0 + h)),
                  pl.BlockSpec((mlen, hd), lambda h, i: (0, h)), pl.BlockSpec((mlen, hd), lambda h, i: (0, MEM_HEADS + h)),
                  pl.BlockSpec((tq, hd), lambda h, i: (i, h))],
        out_specs=[pl.BlockSpec((tq, hd), lambda h, i: (i, h)), pl.BlockSpec((tq, hd), lambda h, i: (i, h)),
                   pl.BlockSpec((mlen, hd), lambda h, i: (0, h)), pl.BlockSpec((mlen, hd), lambda h, i: (0, h))],
        out_shape=[jax.ShapeDtypeStruct((t, D_MODEL), BF16), jax.ShapeDtypeStruct((t, D_MODEL), BF16),
                   jax.ShapeDtypeStruct((mlen, D_MODEL), F32), jax.ShapeDtypeStruct((mlen, D_MODEL), F32)],
    )(proj_big, proj_big, kv_m, kv_m, dym)


HBM_SPEC = pl.BlockSpec(memory_space=pl.ANY)
N_PEERS = N_DEV - 1


def _dev_index(px, py, pc):
    return 4 * px + 2 * py + pc


def _all_gather(arrays, name):
    n = len(arrays)

    def body(*refs):
        ins, outs = refs[:n], refs[n:2 * n]
        send_sems, recv_sems, local_sems = refs[2 * n:]
        x, y, c = lax.axis_index("x"), lax.axis_index("y"), lax.axis_index("c")
        me, sibling = (x, y, c), (x, y, 1 - c)
        chips = [(1 - x, y), (x, 1 - y), (1 - x, 1 - y)]

        def copy(a, k, block, to, src=None):
            dst = outs[a].at[_dev_index(*block)]
            return pltpu.make_async_remote_copy(
                src_ref=dst if src is None else src, dst_ref=dst,
                send_sem=send_sems.at[a * N_PEERS + k], recv_sem=recv_sems.at[a * N_PEERS + k],
                device_id=to, device_id_type=MESH)

        mine = [pltpu.make_async_copy(ins[a], outs[a].at[_dev_index(*me)], local_sems.at[a]) for a in range(n)]
        for cp in mine:
            cp.start()
        first = []
        for a in range(n):
            first.append(copy(a, 0, me, sibling, src=ins[a]))
            first += [copy(a, 1 + j, me, (*chip, c), src=ins[a]) for j, chip in enumerate(chips)]
        for cp in first:
            cp.start()
        passed = []
        for j, chip in enumerate(chips):
            for a in range(n):
                copy(a, 1 + j, (*chip, c), me).wait_recv()
                fwd = copy(a, 4 + j, (*chip, c), sibling)
                fwd.start()
                passed.append(fwd)
        for a in range(n):
            copy(a, 0, sibling, me).wait_recv()
            for j, chip in enumerate(chips):
                copy(a, 4 + j, (*chip, 1 - c), me).wait_recv()
        for cp in first + passed:
            cp.wait_send()
        for cp in mine:
            cp.wait()

    return pl.pallas_call(
        body, name=name,
        in_specs=[HBM_SPEC] * n, out_specs=[HBM_SPEC] * n,
        out_shape=[jax.ShapeDtypeStruct((N_DEV,) + a.shape, a.dtype) for a in arrays],
        scratch_shapes=[pltpu.SemaphoreType.DMA((n * N_PEERS,)), pltpu.SemaphoreType.DMA((n * N_PEERS,)),
                        pltpu.SemaphoreType.DMA((n,))],
    )(*arrays)


def _all_to_all(arrays, name):
    n = len(arrays)

    def body(*refs):
        ins, outs = refs[:n], refs[n:2 * n]
        send_sems, recv_sems, local_sems = refs[2 * n:]
        x, y, c = lax.axis_index("x"), lax.axis_index("y"), lax.axis_index("c")
        my_idx = _dev_index(x, y, c)
        peers = []
        for k in range(1, N_DEV):
            dx, dy, dc = (k >> 2) & 1, (k >> 1) & 1, k & 1
            peers.append((1 - x if dx else x, 1 - y if dy else y, 1 - c if dc else c))

        def copy(a, k, peer):
            return pltpu.make_async_remote_copy(
                src_ref=ins[a].at[_dev_index(*peer)], dst_ref=outs[a].at[my_idx],
                send_sem=send_sems.at[a * N_PEERS + k], recv_sem=recv_sems.at[a * N_PEERS + k],
                device_id=peer, device_id_type=MESH)

        def landed(a, k, peer):
            slot = outs[a].at[_dev_index(*peer)]
            return pltpu.make_async_remote_copy(
                src_ref=slot, dst_ref=slot,
                send_sem=send_sems.at[a * N_PEERS + k], recv_sem=recv_sems.at[a * N_PEERS + k],
                device_id=peer, device_id_type=MESH)

        mine = [pltpu.make_async_copy(ins[a].at[my_idx], outs[a].at[my_idx], local_sems.at[a]) for a in range(n)]
        for cp in mine:
            cp.start()
        sends = [copy(a, k, peer) for a in range(n) for k, peer in enumerate(peers)]
        for cp in sends:
            cp.start()
        for a in range(n):
            for k, peer in enumerate(peers):
                landed(a, k, peer).wait_recv()
        for cp in sends:
            cp.wait_send()
        for cp in mine:
            cp.wait()

    return pl.pallas_call(
        body, name=name,
        in_specs=[HBM_SPEC] * n, out_specs=[HBM_SPEC] * n,
        out_shape=[jax.ShapeDtypeStruct(a.shape, a.dtype) for a in arrays],
        scratch_shapes=[pltpu.SemaphoreType.DMA((n * N_PEERS,)), pltpu.SemaphoreType.DMA((n * N_PEERS,)),
                        pltpu.SemaphoreType.DMA((n,))],
    )(*arrays)


def _adamw(w, g, m, v):
    m = ADAM_B1 * m + (1.0 - ADAM_B1) * g
    v = ADAM_B2 * v + (1.0 - ADAM_B2) * jnp.square(g)
    m_hat = m / (1.0 - ADAM_B1 ** ADAM_STEP)
    v_hat = v / (1.0 - ADAM_B2 ** ADAM_STEP)
    delta = -ADAM_LR * (m_hat / (jnp.sqrt(v_hat) + ADAM_EPS) + ADAM_WD * w)
    return delta, m, v


def _adam_sharded(parts, w, m, v, name):
    shape = w.shape
    cols = shape[-1]
    rows = int(np.prod(shape[:-1]))
    tr = min(128, rows)
    assert rows % tr == 0

    def body(p_ref, w_ref, m_ref, v_ref, g_ref, d_ref, nm_ref, nv_ref):
        g = p_ref[0].astype(F32)
        for e in range(1, N_DEV):
            g = g + p_ref[e].astype(F32)
        g_ref[...] = g
        d_ref[...], nm_ref[...], nv_ref[...] = _adamw(w_ref[...], g, m_ref[...], v_ref[...])

    spec = pl.BlockSpec((tr, cols), lambda i: (i, 0))
    flat = jax.ShapeDtypeStruct((rows, cols), F32)
    outs = pl.pallas_call(
        body, name=name, grid=(rows // tr,),
        in_specs=[pl.BlockSpec((N_DEV, tr, cols), lambda i: (0, i, 0)), spec, spec, spec],
        out_specs=[spec] * 4, out_shape=[flat] * 4,
        compiler_params=_params(40),
    )(parts.reshape(N_DEV, rows, cols), w.reshape(rows, cols), m.reshape(rows, cols), v.reshape(rows, cols))
    return [o.reshape(shape) for o in outs]


def _adam_replicated(parts, w, m, v):
    r = w.shape[0]

    def body(p_ref, w_ref, m_ref, v_ref, g_ref, d_ref, nm_ref, nv_ref):
        g = p_ref[0]
        for e in range(1, N_DEV):
            g = g + p_ref[e]
        g_ref[...] = g
        d_ref[...], nm_ref[...], nv_ref[...] = _adamw(w_ref[...], g, m_ref[...], v_ref[...])

    spec = _full_spec((r, 128))
    flat = jax.ShapeDtypeStruct((r, 128), F32)
    return pl.pallas_call(
        body, name="adam_replicated", grid=(1,),
        in_specs=[_full_spec((N_DEV, r, 128)), spec, spec, spec],
        out_specs=[spec] * 4, out_shape=[flat] * 4,
        compiler_params=_params(48),
    )(parts, w, m, v)


def _pack(arrays):
    parts = []
    for a in arrays:
        f = a.reshape(-1, 128)
        pad = -f.shape[0] % 8
        parts.append(jnp.pad(f, ((0, pad), (0, 0))) if pad else f)
    return jnp.concatenate(parts, axis=0)


def _unpack(packed, shapes):
    out, row = [], 0
    for shape in shapes:
        r = int(np.prod(shape)) // 128
        out.append(packed[row:row + r].reshape(shape))
        row += r + (-r % 8)
    return out


SHARDED = ("w_in", "w_uq", "w_ukv", "w_mem_kv", "w_gate", "w_branch", "w_out")
REPLICATED = ("g_pre", "a_ln_g", "a_ln_b", "a_w_s", "a_b_s", "q_norm_g", "kv_norm_g", "mem_norm_g", "b_gate", "g_post")
WEIGHT_ORDER = ("g_pre", "w_in", "a_ln_g", "a_ln_b", "a_w_s", "a_b_s", "q_norm_g", "w_uq", "kv_norm_g", "w_ukv",
                "mem_norm_g", "w_mem_kv", "w_gate", "b_gate", "w_branch", "w_out", "g_post")


def _unshard_cols(g):
    return g.transpose(1, 0, 2).reshape(g.shape[1], N_DEV * g.shape[2])


def _shard_cols(full):
    rows, n = full.shape
    return full.reshape(rows, N_DEV, n // N_DEV).transpose(1, 0, 2).astype(BF16)


def kernel(x, mem, positions, g_pre, w_in, a_ln_g, a_ln_b, a_w_s, a_b_s, q_norm_g, w_uq, kv_norm_g, w_ukv, mem_norm_g, w_mem_kv, w_gate, b_gate, w_branch, w_out, g_post, loss_target, m_g_pre, m_w_in, m_a_ln_g, m_a_ln_b, m_a_w_s, m_a_b_s, m_q_norm_g, m_w_uq, m_kv_norm_g, m_w_ukv, m_mem_norm_g, m_w_mem_kv, m_w_gate, m_b_gate, m_w_branch, m_w_out, m_g_post, v_g_pre, v_w_in, v_a_ln_g, v_a_ln_b, v_a_w_s, v_a_b_s, v_q_norm_g, v_w_uq, v_kv_norm_g, v_w_ukv, v_mem_norm_g, v_w_mem_kv, v_w_gate, v_b_gate, v_w_branch, v_w_out, v_g_post):
    weights = dict(g_pre=g_pre, w_in=w_in, a_ln_g=a_ln_g, a_ln_b=a_ln_b, a_w_s=a_w_s, a_b_s=a_b_s, q_norm_g=q_norm_g,
                   w_uq=w_uq, kv_norm_g=kv_norm_g, w_ukv=w_ukv, mem_norm_g=mem_norm_g, w_mem_kv=w_mem_kv,
                   w_gate=w_gate, b_gate=b_gate, w_branch=w_branch, w_out=w_out, g_post=g_post)
    mom1 = dict(g_pre=m_g_pre, w_in=m_w_in, a_ln_g=m_a_ln_g, a_ln_b=m_a_ln_b, a_w_s=m_a_w_s, a_b_s=m_a_b_s,
                q_norm_g=m_q_norm_g, w_uq=m_w_uq, kv_norm_g=m_kv_norm_g, w_ukv=m_w_ukv, mem_norm_g=m_mem_norm_g,
                w_mem_kv=m_w_mem_kv, w_gate=m_w_gate, b_gate=m_b_gate, w_branch=m_w_branch, w_out=m_w_out, g_post=m_g_post)
    mom2 = dict(g_pre=v_g_pre, w_in=v_w_in, a_ln_g=v_a_ln_g, a_ln_b=v_a_ln_b, a_w_s=v_a_w_s, a_b_s=v_a_b_s,
                q_norm_g=v_q_norm_g, w_uq=v_w_uq, kv_norm_g=v_kv_norm_g, w_ukv=v_w_ukv, mem_norm_g=v_mem_norm_g,
                w_mem_kv=v_w_mem_kv, w_gate=v_w_gate, b_gate=v_b_gate, w_branch=v_w_branch, w_out=v_w_out, g_post=v_g_post)
    d = D_MODEL
    t = x.shape[1]
    xs, tgt, mems = x[0], loss_target[0], mem[0]
    pos_col = positions.reshape(t, 1)

    g_in, g_uq, g_ukv, g_mem, g_gate, g_br, g_out = _all_gather(
        [weights[n][0].astype(BF16) for n in SHARDED], "gather_weights")
    w_in_full = _unshard_cols(g_in)
    lat0, lat1 = 3 * d, 3 * d + Q_LORA + KV_LORA + QK_ROPE
    w_big = jnp.concatenate([w_in_full[:, :lat0], w_in_full[:, lat1:]], axis=1)
    w_lat = jnp.concatenate([w_in_full[:, lat0:lat1], jnp.zeros((d, LAT_W - (lat1 - lat0)), BF16)], axis=1)
    w_uq_p = jnp.pad(_unshard_cols(g_uq).reshape(Q_LORA, MLA_HEADS, QK_DIM),
                     ((0, 0), (0, 0), (0, HEAD_PAD - QK_DIM))).reshape(Q_LORA, MLA_HEADS * HEAD_PAD)
    w_ukv_f = _unshard_cols(g_ukv)
    w_mem_f = _unshard_cols(g_mem)
    w_gate_f = _unshard_cols(g_gate)
    w_br_f = g_br.transpose(1, 0, 2, 3).reshape(3, d, d)
    w_out_f = g_out.reshape(d, d)

    inv_freq = 1.0 / (ROPE_THETA ** (jnp.arange(0, QK_ROPE, 2, dtype=F32) / QK_ROPE))
    inv_freq_lanes = jnp.concatenate([inv_freq, inv_freq, jnp.zeros((128 - QK_ROPE,), F32)]).reshape(1, 128)
    ws = a_w_s[0]
    b_exp = jnp.broadcast_to(a_b_s[0][:, :, None], (A_GROUPS, CHUNK, 128))

    h = _pre_norm(xs, g_pre)
    proj_big = _mm(h, w_big, name="proj_big", tm=1024, tn=1024, tk=2048)
    proj_lat = _mm(h, w_lat, name="proj_lat", tm=1024, tn=LAT_W, tk=2048)
    gates = _mm(h, w_gate_f, name="gates", tm=1024, tn=1024, tk=2048, bias=b_gate, act="sigmoid")
    c_tab, sa_tab, sb_tab = _rope_tables(pos_col, inv_freq_lanes)
    cqn, ckvn, kpe = _latent_norms(proj_lat, q_norm_g, kv_norm_g, c_tab, sa_tab, sb_tab)
    q_raw = _mm(cqn, w_uq_p, name="q_up", tm=1024, tn=1024, tk=512, out_dtype=F32)
    q = _rope_q(q_raw, c_tab, sa_tab, sb_tab)
    kv = _mm(ckvn, w_ukv_f, name="kv_up", tm=1024, tn=1024, tk=512)
    o_b, y_b, lse = _mla_fwd(q, kv, kpe, proj_big)
    memn = _mem_norm(mems, mem_norm_g)
    kv_m = _mm(memn, w_mem_f, name="mem_kv", tm=256, tn=1024, tk=2048)
    y_m = _mem_fwd(proj_big, kv_m)
    y_a = _gmlp_fwd(proj_big, a_ln_g, a_ln_b, ws, b_exp)
    ys = (y_a, y_b, y_m)
    ps = [_mm(ys[n], w_br_f[n], name=f"branch{n}", tm=1024, tn=1024, tk=2048) for n in range(3)]
    merged = _merge(gates, *ps)
    out = _mm(merged, w_out_f, name="out_proj", tm=1024, tn=1024, tk=2048, out_dtype=F32)
    d_out, dy, loss_blk, dg_post = _post_loss(out, xs, tgt, g_post)

    dmerged = _mm(d_out, w_out_f, name="d_merged", tb=True, tm=1024, tn=1024, tk=2048)
    dw_out = _mm(merged, d_out, name="dw_out", ta=True, tm=1024, tn=1024, tk=1024, out_dtype=F32)
    dp_a, dp_b, dp_m, dgl, db_gate = _merge_bwd(dmerged, gates, *ps)
    dps = (dp_a, dp_b, dp_m)
    dys = [_mm(dps[n], w_br_f[n], name=f"d_y{n}", tb=True, tm=1024, tn=1024, tk=2048) for n in range(3)]
    dw_br = [_mm(ys[n], dps[n], name=f"dw_branch{n}", ta=True, tm=1024, tn=1024, tk=1024, out_dtype=F32) for n in range(3)]
    dw_gate = _mm(h, dgl, name="dw_gate", ta=True, tm=1024, tn=1024, tk=1024, out_dtype=F32)

    d_abig, dws, dbs, dlng, dlnb = _gmlp_bwd(proj_big, dys[0], a_ln_g, a_ln_b, ws, b_exp)

    do_b, dz_b, ld = _mla_gate_bwd(dys[1], proj_big, o_b, lse)
    dq, dkv, dkpe_h = _mla_bwd(q, kv, kpe, do_b, ld)
    dq_raw, dkr = _rope_q_bwd(dq, dkpe_h, c_tab, sa_tab, sb_tab)
    dcqn = _mm(dq_raw, w_uq_p, name="d_cq", tb=True, tm=1024, tn=Q_LORA, tk=2048, out_dtype=F32)
    dw_uq_p = _mm(cqn, dq_raw, name="dw_uq", ta=True, tm=Q_LORA, tn=1024, tk=1024, out_dtype=F32)
    dckvn = _mm(dkv, w_ukv_f, name="d_ckv", tb=True, tm=1024, tn=KV_LORA, tk=2048, out_dtype=F32)
    dw_ukv = _mm(ckvn, dkv, name="dw_ukv", ta=True, tm=KV_LORA, tn=1024, tk=1024, out_dtype=F32)
    dproj_lat, dqg, dkg = _latent_norms_bwd(proj_lat, q_norm_g, kv_norm_g, dcqn, dckvn, dkr)

    dq_m, dz_m, dk_m, dv_m = _mem_bwd(proj_big, kv_m, dys[2])
    dkv_m = jnp.concatenate([dk_m, dv_m], axis=1).astype(BF16)
    dw_mem = _mm(memn, dkv_m, name="dw_mem", ta=True, tm=1024, tn=1024, tk=256, out_dtype=F32)
    dmemn = _mm(dkv_m, w_mem_f, name="d_memn", tb=True, tm=256, tn=1024, tk=2048, out_dtype=F32)
    dg_mem = _mem_norm_bwd(mems, mem_norm_g, dmemn)

    dproj_big = jnp.concatenate([d_abig, dz_b, dq_m, dz_m], axis=1)
    dw_big = _mm(h, dproj_big, name="dw_big", ta=True, tm=1024, tn=1024, tk=1024, out_dtype=F32)
    dw_lat = _mm(h, dproj_lat, name="dw_lat", ta=True, tm=1024, tn=LAT_W, tk=1024, out_dtype=F32)
    dh = _mm(dgl, w_gate_f, name="dh_gate", tb=True, tm=1024, tn=1024, tk=1024, out_dtype=F32)
    dh = _mm(dproj_lat, w_lat, name="dh_lat", tb=True, tm=1024, tn=1024, tk=LAT_W, out_dtype=F32, add=dh)
    dh = _mm(dproj_big, w_big, name="dh_big", tb=True, tm=1024, tn=1024, tk=1024, out_dtype=F32, add=dh)
    grad_x, dg_pre = _pre_norm_bwd(xs, g_pre, dh, dy)

    dw_in_full = jnp.concatenate([dw_big[:, :lat0], dw_lat[:, :lat1 - lat0], dw_big[:, lat0:]], axis=1)
    dw_uq_full = dw_uq_p.reshape(Q_LORA, MLA_HEADS, HEAD_PAD)[:, :, :QK_DIM].reshape(Q_LORA, MLA_HEADS * QK_DIM)
    dw_br_full = jnp.stack(dw_br).reshape(3, N_DEV, d // N_DEV, d).transpose(1, 0, 2, 3).astype(BF16)
    send = [_shard_cols(dw_in_full), _shard_cols(dw_uq_full), _shard_cols(dw_ukv), _shard_cols(dw_mem),
            _shard_cols(dw_gate), dw_br_full, dw_out.reshape(N_DEV, d // N_DEV, d).astype(BF16)]
    recv = _all_to_all(send, "scatter_grads")
    results = {}
    for n, parts in zip(SHARDED, recv):
        results[n] = [r[None] for r in _adam_sharded(parts, weights[n][0], mom1[n][0], mom2[n][0], "adam_" + n)]

    small = dict(g_pre=dg_pre, a_ln_g=dlng, a_ln_b=dlnb, a_w_s=dws, a_b_s=dbs, q_norm_g=dqg, kv_norm_g=dkg,
                 mem_norm_g=dg_mem, b_gate=db_gate, g_post=dg_post)
    (parts,) = _all_gather([_pack([small[n] for n in REPLICATED])], "gather_small_grads")
    packed = _adam_replicated(parts, _pack([weights[n] for n in REPLICATED]), _pack([mom1[n] for n in REPLICATED]),
                              _pack([mom2[n] for n in REPLICATED]))
    shapes = [weights[n].shape for n in REPLICATED]
    unpacked = [_unpack(p, shapes) for p in packed]
    for i, n in enumerate(REPLICATED):
        results[n] = [u[i] for u in unpacked]

    loss = lax.psum(loss_blk[0, 0], AXES)
    outs = [loss, grad_x[None]]
    for kind in range(4):
        outs += [results[n][kind] for n in WEIGHT_ORDER]
    return tuple(outs)
```

```python
import functools
import math

import jax
import jax.numpy as jnp
import numpy as np
from jax import lax
from jax.experimental import pallas as pl
from jax.experimental.pallas import tpu as pltpu

F32 = jnp.float32
BF16 = jnp.bfloat16
MESH = pl.DeviceIdType.MESH
AXES = ("x", "y", "c")
N_DEV = 8

D_MODEL = 2048
EPS = 1e-6
CHUNK = 128
A_GROUPS = 16
MLA_HEADS = 16
QK_NOPE = 128
QK_ROPE = 64
QK_DIM = QK_NOPE + QK_ROPE
HEAD_PAD = 256
Q_LORA = 512
KV_LORA = 512
MEM_HEADS = 4
MEM_HEAD_DIM = 512
ROPE_THETA = 10000.0
MLA_SCALE = QK_DIM ** -0.5
MEM_SCALE = MEM_HEAD_DIM ** -0.5
NEG = -1e30
LOG2E = 1.4426950408889634

ADAM_LR = 0.001
ADAM_B1 = 0.9
ADAM_B2 = 0.999
ADAM_EPS = 1e-08
ADAM_WD = 0.01
ADAM_STEP = 10

BIG_W = 6 * D_MODEL
LAT_W = Q_LORA + KV_LORA + 128

VMEM_MIB = 1024 * 1024

ROW_BLK = 256
ATT_BLK = 1024
GMLP_ROWS = 256
MEM_Q_BLK = 512


def _params(vmem_mib, **kw):
    return pltpu.CompilerParams(vmem_limit_bytes=int(vmem_mib * VMEM_MIB), **kw)


def _gelu(x):
    k = math.sqrt(2.0 / math.pi)
    t = jnp.tanh(k * (x + 0.044715 * (x * x * x)))
    return 0.5 * x * (1.0 + t)


def _gelu_and_grad(x):
    k = math.sqrt(2.0 / math.pi)
    x2 = x * x
    t = jnp.tanh(k * (x + 0.044715 * (x2 * x)))
    val = 0.5 * x * (1.0 + t)
    grad = 0.5 * (1.0 + t) + 0.5 * x * (1.0 - t * t) * (k * (1.0 + 3.0 * 0.044715 * x2))
    return val, grad


def _silu_and_grad(z):
    s = jax.nn.sigmoid(z)
    return z * s, s * (1.0 + z * (1.0 - s))


def _mm(a, b, *, name, tm, tn, tk, ta=False, tb=False, out_dtype=BF16, bias=None, act=None, add=None, carry=None):
    m = a.shape[1] if ta else a.shape[0]
    k = a.shape[0] if ta else a.shape[1]
    n = b.shape[0] if tb else b.shape[1]
    assert k == (b.shape[1] if tb else b.shape[0])
    tm, tn, tk = min(tm, m), min(tn, n), min(tk, k)
    assert m % tm == 0 and n % tn == 0 and k % tk == 0, (name, m, n, k, tm, tn, tk)
    nk = k // tk
    a_spec = pl.BlockSpec((tk, tm), lambda i, j, kk: (kk, i)) if ta else pl.BlockSpec((tm, tk), lambda i, j, kk: (i, kk))
    b_spec = pl.BlockSpec((tn, tk), lambda i, j, kk: (j, kk)) if tb else pl.BlockSpec((tk, tn), lambda i, j, kk: (kk, j))
    dn = (((0 if ta else 1,), (1 if tb else 0,)), ((), ()))
    operands, in_specs = [a, b], [a_spec, b_spec]
    if bias is not None:
        operands.append(bias)
        in_specs.append(pl.BlockSpec((1, tn), lambda i, j, kk: (0, j)))
    if add is not None:
        operands.append(add)
        in_specs.append(pl.BlockSpec((tm, tn), lambda i, j, kk: (i, j)))

    n_in = len(operands)
    n_carry = len(carry.arrays) if carry is not None else 0
    n_acc = 1 if nk > 1 else 0
    grid = (m // tm, n // tn, nk)

    def body(*refs):
        a_ref, b_ref = refs[0], refs[1]
        pos = 2
        bias_ref = add_ref = None
        if bias is not None:
            bias_ref = refs[pos]
            pos += 1
        if add is not None:
            add_ref = refs[pos]
            pos += 1
        o_ref = refs[n_in + n_carry]
        pos = n_in + n_carry
        if carry is not None:
            c_ins = refs[n_in:n_in + n_carry]
            c_outs = refs[n_in + n_carry + 1:n_in + 2 * n_carry + 1]
            c_sems = refs[n_in + 2 * n_carry + 1 + n_acc:]
            ids = [pl.program_id(ax) for ax in range(3)]

            @pl.when((ids[0] == 0) & (ids[1] == 0) & (ids[2] == 0))
            def _():
                carry.start(c_ins, c_outs, c_sems)

        part = lax.dot_general(a_ref[...], b_ref[...], dn, preferred_element_type=F32)

        def finish(acc):
            if bias_ref is not None:
                acc = acc + bias_ref[...]
            if act == "sigmoid":
                acc = jax.nn.sigmoid(acc)
            if add_ref is not None:
                acc = acc + add_ref[...]
            o_ref[...] = acc.astype(o_ref.dtype)

        if nk == 1:
            finish(part)
        else:
            acc_ref = refs[n_in + 2 * n_carry + 1]
            kk = pl.program_id(2)

            @pl.when(kk == 0)
            def _():
                acc_ref[...] = part

            @pl.when(kk > 0)
            def _():
                acc_ref[...] += part

            @pl.when(kk == nk - 1)
            def _():
                finish(acc_ref[...])

        if carry is not None:
            @pl.when((ids[0] == grid[0] - 1) & (ids[1] == grid[1] - 1) & (ids[2] == grid[2] - 1))
            def _():
                carry.finish(c_ins, c_outs, c_sems)

    osz = jnp.dtype(out_dtype).itemsize
    est = 2 * 2 * (tm * tk + tk * tn) + 2 * osz * tm * tn + 8 * tm * tn + (2 * 4 * tm * tn if add is not None else 0)
    main_spec = pl.BlockSpec((tm, tn), lambda i, j, kk: (i, j))
    main_shape = jax.ShapeDtypeStruct((m, n), out_dtype)
    scratch = [pltpu.VMEM((tm, tn), F32)] if nk > 1 else []
    if carry is None:
        return pl.pallas_call(
            body, name=name, grid=grid, in_specs=in_specs, out_specs=main_spec, out_shape=main_shape,
            scratch_shapes=scratch, compiler_params=_params(min(56, est / VMEM_MIB + 12)),
        )(*operands)
    return pl.pallas_call(
        body, name=name, grid=grid,
        in_specs=in_specs + [HBM_SPEC] * n_carry,
        out_specs=[main_spec] + [HBM_SPEC] * n_carry,
        out_shape=[main_shape] + carry.out_shapes,
        scratch_shapes=scratch + carry.sem_shapes,
        compiler_params=_params(min(56, est / VMEM_MIB + 12)),
    )(*operands, *carry.arrays)


def _row_spec(tr, cols, col_blk=0):
    return pl.BlockSpec((tr, cols), lambda i: (i, col_blk))


def _full_spec(shape):
    nd = len(shape)
    return pl.BlockSpec(shape, lambda i: (0,) * nd)


def _pre_norm(x, g_pre):
    t, d = x.shape
    tr = min(ROW_BLK, t)

    def body(x_ref, g_ref, h_ref):
        xv = x_ref[...]
        r = lax.rsqrt(jnp.mean(xv * xv, axis=-1, keepdims=True) + EPS)
        h_ref[...] = ((xv * r) * g_ref[...]).astype(BF16)

    return pl.pallas_call(
        body, name="pre_norm", grid=(t // tr,),
        in_specs=[_row_spec(tr, d), _full_spec((1, d))],
        out_specs=_row_spec(tr, d),
        out_shape=jax.ShapeDtypeStruct((t, d), BF16),
        compiler_params=_params(32),
    )(x, g_pre)


def _rope_tables(pos_col, inv_freq_lanes):
    t = pos_col.shape[0]
    tr = min(ROW_BLK, t)

    def body(p_ref, f_ref, c_ref, sa_ref, sb_ref):
        ang = p_ref[...].astype(F32) * f_ref[...]
        lane = lax.broadcasted_iota(jnp.int32, ang.shape, 1)
        cos, sin = jnp.cos(ang), jnp.sin(ang)
        c_ref[...] = jnp.where(lane < QK_ROPE, cos, 0.0)
        sa_ref[...] = jnp.where(lane < QK_ROPE // 2, sin, 0.0)
        sb_ref[...] = jnp.where((lane >= QK_ROPE // 2) & (lane < QK_ROPE), sin, 0.0)

    tab = jax.ShapeDtypeStruct((t, 128), F32)
    return pl.pallas_call(
        body, name="rope_tables", grid=(t // tr,),
        in_specs=[_row_spec(tr, 1), _full_spec((1, 128))],
        out_specs=[_row_spec(tr, 128)] * 3,
        out_shape=[tab, tab, tab],
    )(pos_col, inv_freq_lanes)


def _rope_fwd(p, c, sa, sb):
    return p * c - pltpu.roll(p, 96, 1) * sa + pltpu.roll(p, 32, 1) * sb


def _rope_bwd(g, c, sa, sb):
    return g * c + pltpu.roll(g, 96, 1) * sa - pltpu.roll(g, 32, 1) * sb


def _rms(xv, g):
    r = lax.rsqrt(jnp.mean(xv * xv, axis=-1, keepdims=True) + EPS)
    return (xv * r) * g


def _rms_bwd(xv, g, dout):
    r = lax.rsqrt(jnp.mean(xv * xv, axis=-1, keepdims=True) + EPS)
    xn = xv * r
    dg = jnp.sum(dout * xn, axis=0, keepdims=True)
    dxn = dout * g
    dx = r * (dxn - xn * jnp.mean(dxn * xn, axis=-1, keepdims=True))
    return dx, dg


def _latent_norms(proj_lat, q_norm_g, kv_norm_g, c_tab, sa_tab, sb_tab):
    t = proj_lat.shape[0]
    tr = min(ROW_BLK, t)

    def body(cq_ref, ckv_ref, kr_ref, qg_ref, kg_ref, c_ref, sa_ref, sb_ref, cqn_ref, ckvn_ref, kpe_ref):
        cqn_ref[...] = _rms(cq_ref[...].astype(F32), qg_ref[...]).astype(BF16)
        ckvn_ref[...] = _rms(ckv_ref[...].astype(F32), kg_ref[...]).astype(BF16)
        kpe_ref[...] = _rope_fwd(kr_ref[...].astype(F32), c_ref[...], sa_ref[...], sb_ref[...]).astype(BF16)

    return pl.pallas_call(
        body, name="latent_norms", grid=(t // tr,),
        in_specs=[_row_spec(tr, Q_LORA, 0), _row_spec(tr, KV_LORA, 1), _row_spec(tr, 128, (Q_LORA + KV_LORA) // 128),
                  _full_spec((1, Q_LORA)), _full_spec((1, KV_LORA)),
                  _row_spec(tr, 128), _row_spec(tr, 128), _row_spec(tr, 128)],
        out_specs=[_row_spec(tr, Q_LORA), _row_spec(tr, KV_LORA), _row_spec(tr, 128)],
        out_shape=[jax.ShapeDtypeStruct((t, Q_LORA), BF16), jax.ShapeDtypeStruct((t, KV_LORA), BF16),
                   jax.ShapeDtypeStruct((t, 128), BF16)],
    )(proj_lat, proj_lat, proj_lat, q_norm_g, kv_norm_g, c_tab, sa_tab, sb_tab)


def _rope_q(q_raw, c_tab, sa_tab, sb_tab):
    t = q_raw.shape[0]
    tr = min(ROW_BLK, t)

    def body(q_ref, c_ref, sa_ref, sb_ref, o_ref):
        c, sa, sb = c_ref[...], sa_ref[...], sb_ref[...]
        for h in range(MLA_HEADS):
            o_ref[:, h * HEAD_PAD:h * HEAD_PAD + 128] = q_ref[:, h * HEAD_PAD:h * HEAD_PAD + 128].astype(BF16)
            pe = q_ref[:, h * HEAD_PAD + 128:(h + 1) * HEAD_PAD]
            o_ref[:, h * HEAD_PAD + 128:(h + 1) * HEAD_PAD] = _rope_fwd(pe, c, sa, sb).astype(BF16)

    w = MLA_HEADS * HEAD_PAD
    return pl.pallas_call(
        body, name="rope_q", grid=(t // tr,),
        in_specs=[_row_spec(tr, w), _row_spec(tr, 128), _row_spec(tr, 128), _row_spec(tr, 128)],
        out_specs=_row_spec(tr, w),
        out_shape=jax.ShapeDtypeStruct((t, w), BF16),
        compiler_params=_params(48),
    )(q_raw, c_tab, sa_tab, sb_tab)


def _rope_q_bwd(dq, dkpe_heads, c_tab, sa_tab, sb_tab):
    t = dq.shape[0]
    tr = min(ROW_BLK, t)

    def body(dq_ref, dkp_ref, c_ref, sa_ref, sb_ref, o_ref, dkr_ref):
        c, sa, sb = c_ref[...], sa_ref[...], sb_ref[...]
        for h in range(MLA_HEADS):
            o_ref[:, h * HEAD_PAD:h * HEAD_PAD + 128] = (dq_ref[:, h * HEAD_PAD:h * HEAD_PAD + 128] * MLA_SCALE).astype(BF16)
            g = dq_ref[:, h * HEAD_PAD + 128:(h + 1) * HEAD_PAD] * MLA_SCALE
            o_ref[:, h * HEAD_PAD + 128:(h + 1) * HEAD_PAD] = _rope_bwd(g, c, sa, sb).astype(BF16)
        tot = dkp_ref[0]
        for h in range(1, MLA_HEADS):
            tot = tot + dkp_ref[h]
        dkr_ref[...] = _rope_bwd(tot, c, sa, sb)

    w = MLA_HEADS * HEAD_PAD
    return pl.pallas_call(
        body, name="rope_q_bwd", grid=(t // tr,),
        in_specs=[_row_spec(tr, w), pl.BlockSpec((MLA_HEADS, tr, 128), lambda i: (0, i, 0)),
                  _row_spec(tr, 128), _row_spec(tr, 128), _row_spec(tr, 128)],
        out_specs=[_row_spec(tr, w), _row_spec(tr, 128)],
        out_shape=[jax.ShapeDtypeStruct((t, w), BF16), jax.ShapeDtypeStruct((t, 128), F32)],
        compiler_params=_params(48),
    )(dq, dkpe_heads, c_tab, sa_tab, sb_tab)


def _latent_norms_bwd(proj_lat, q_norm_g, kv_norm_g, dcqn, dckvn, dkr):
    t = proj_lat.shape[0]
    tr = min(ROW_BLK, t)

    def body(cq_ref, ckv_ref, qg_ref, kg_ref, dcqn_ref, dckvn_ref, dkr_ref, dl_ref, dqg_ref, dkg_ref):
        dcq, dqg = _rms_bwd(cq_ref[...].astype(F32), qg_ref[...], dcqn_ref[...])
        dckv, dkg = _rms_bwd(ckv_ref[...].astype(F32), kg_ref[...], dckvn_ref[...])
        dl_ref[:, 0:Q_LORA] = dcq.astype(BF16)
        dl_ref[:, Q_LORA:Q_LORA + KV_LORA] = dckv.astype(BF16)
        dl_ref[:, Q_LORA + KV_LORA:LAT_W] = dkr_ref[...].astype(BF16)

        @pl.when(pl.program_id(0) == 0)
        def _():
            dqg_ref[...] = jnp.zeros_like(dqg_ref)
            dkg_ref[...] = jnp.zeros_like(dkg_ref)

        dqg_ref[...] += dqg
        dkg_ref[...] += dkg

    return pl.pallas_call(
        body, name="latent_norms_bwd", grid=(t // tr,),
        in_specs=[_row_spec(tr, Q_LORA, 0), _row_spec(tr, KV_LORA, 1), _full_spec((1, Q_LORA)), _full_spec((1, KV_LORA)),
                  _row_spec(tr, Q_LORA), _row_spec(tr, KV_LORA), _row_spec(tr, 128)],
        out_specs=[_row_spec(tr, LAT_W), _full_spec((1, Q_LORA)), _full_spec((1, KV_LORA))],
        out_shape=[jax.ShapeDtypeStruct((t, LAT_W), BF16), jax.ShapeDtypeStruct((1, Q_LORA), F32),
                   jax.ShapeDtypeStruct((1, KV_LORA), F32)],
    )(proj_lat, proj_lat, q_norm_g, kv_norm_g, dcqn, dckvn, dkr)


def _mem_norm(mem, g):
    m, d = mem.shape

    def body(x_ref, g_ref, o_ref):
        o_ref[...] = _rms(x_ref[...], g_ref[...]).astype(BF16)

    return pl.pallas_call(
        body, name="mem_norm", grid=(1,),
        in_specs=[_full_spec((m, d)), _full_spec((1, d))],
        out_specs=_full_spec((m, d)),
        out_shape=jax.ShapeDtypeStruct((m, d), BF16),
    )(mem, g)


def _mem_norm_bwd(mem, g, dmemn):
    m, d = mem.shape

    def body(x_ref, g_ref, d_ref, dg_ref):
        _, dg = _rms_bwd(x_ref[...], g_ref[...], d_ref[...])
        dg_ref[...] = dg

    return pl.pallas_call(
        body, name="mem_norm_bwd", grid=(1,),
        in_specs=[_full_spec((m, d)), _full_spec((1, d)), _full_spec((m, d))],
        out_specs=_full_spec((1, d)),
        out_shape=jax.ShapeDtypeStruct((1, d), F32),
    )(mem, g, dmemn)


def _merge(gates, p_a, p_b, p_m):
    t, d = p_a.shape
    tr = min(ROW_BLK, t)

    def body(ga_ref, gb_ref, gm_ref, pa_ref, pb_ref, pm_ref, o_ref):
        acc = ga_ref[...].astype(F32) * pa_ref[...].astype(F32)
        acc = acc + gb_ref[...].astype(F32) * pb_ref[...].astype(F32)
        acc = acc + gm_ref[...].astype(F32) * pm_ref[...].astype(F32)
        o_ref[...] = acc.astype(BF16)

    return pl.pallas_call(
        body, name="merge", grid=(t // tr,),
        in_specs=[_row_spec(tr, d, 0), _row_spec(tr, d, 1), _row_spec(tr, d, 2),
                  _row_spec(tr, d), _row_spec(tr, d), _row_spec(tr, d)],
        out_specs=_row_spec(tr, d),
        out_shape=jax.ShapeDtypeStruct((t, d), BF16),
        compiler_params=_params(48),
    )(gates, gates, gates, p_a, p_b, p_m)


def _merge_bwd(dm, gates, p_a, p_b, p_m):
    t, d = dm.shape
    tr = min(ROW_BLK, t)

    def body(dm_ref, g_ref, pa_ref, pb_ref, pm_ref, dpa_ref, dpb_ref, dpm_ref, dgl_ref, db_ref):
        dmv = dm_ref[...].astype(F32)

        @pl.when(pl.program_id(0) == 0)
        def _():
            db_ref[...] = jnp.zeros_like(db_ref)

        for n, (p_ref, dp_ref) in enumerate(((pa_ref, dpa_ref), (pb_ref, dpb_ref), (pm_ref, dpm_ref))):
            g = g_ref[:, n * d:(n + 1) * d].astype(F32)
            dp_ref[...] = (dmv * g).astype(BF16)
            dgl = dmv * p_ref[...].astype(F32) * (g * (1.0 - g))
            dgl_ref[:, n * d:(n + 1) * d] = dgl.astype(BF16)
            db_ref[:, n * d:(n + 1) * d] += jnp.sum(dgl, axis=0, keepdims=True)

    act = jax.ShapeDtypeStruct((t, d), BF16)
    return pl.pallas_call(
        body, name="merge_bwd", grid=(t // tr,),
        in_specs=[_row_spec(tr, d), _row_spec(tr, 3 * d), _row_spec(tr, d), _row_spec(tr, d), _row_spec(tr, d)],
        out_specs=[_row_spec(tr, d), _row_spec(tr, d), _row_spec(tr, d), _row_spec(tr, 3 * d), _full_spec((1, 3 * d))],
        out_shape=[act, act, act, jax.ShapeDtypeStruct((t, 3 * d), BF16), jax.ShapeDtypeStruct((1, 3 * d), F32)],
        compiler_params=_params(56),
    )(dm, gates, p_a, p_b, p_m)


def _post_loss(out, x, tgt, g_post):
    t, d = out.shape
    tr = min(ROW_BLK, t)

    def body(o_ref, x_ref, t_ref, g_ref, do_ref, dy_ref, loss_ref, dg_ref):
        ov = o_ref[...]
        g = g_ref[...]
        r = lax.rsqrt(jnp.mean(ov * ov, axis=-1, keepdims=True) + EPS)
        on = ov * r
        err = (x_ref[...] + on * g) - t_ref[...]
        dy = err * (1.0 / d)
        dy_ref[...] = dy
        don = dy * g
        do_ref[...] = (r * (don - on * jnp.mean(don * on, axis=-1, keepdims=True))).astype(BF16)

        @pl.when(pl.program_id(0) == 0)
        def _():
            loss_ref[...] = jnp.zeros_like(loss_ref)
            dg_ref[...] = jnp.zeros_like(dg_ref)

        loss_ref[...] += 0.5 * jnp.sum(jnp.mean(err * err, axis=-1, keepdims=True))
        dg_ref[...] += jnp.sum(dy * on, axis=0, keepdims=True)

    return pl.pallas_call(
        body, name="post_loss", grid=(t // tr,),
        in_specs=[_row_spec(tr, d), _row_spec(tr, d), _row_spec(tr, d), _full_spec((1, d))],
        out_specs=[_row_spec(tr, d), _row_spec(tr, d), _full_spec((8, 128)), _full_spec((1, d))],
        out_shape=[jax.ShapeDtypeStruct((t, d), BF16), jax.ShapeDtypeStruct((t, d), F32),
                   jax.ShapeDtypeStruct((8, 128), F32), jax.ShapeDtypeStruct((1, d), F32)],
        compiler_params=_params(56),
    )(out, x, tgt, g_post)


def _pre_norm_bwd(x, g_pre, dh, dy):
    t, d = x.shape
    tr = min(ROW_BLK, t)

    def body(x_ref, g_ref, dh_ref, dy_ref, dx_ref, dg_ref):
        dx, dg = _rms_bwd(x_ref[...], g_ref[...], dh_ref[...])
        dx_ref[...] = dx + dy_ref[...]

        @pl.when(pl.program_id(0) == 0)
        def _():
            dg_ref[...] = jnp.zeros_like(dg_ref)

        dg_ref[...] += dg

    return pl.pallas_call(
        body, name="pre_norm_bwd", grid=(t // tr,),
        in_specs=[_row_spec(tr, d), _full_spec((1, d)), _row_spec(tr, d), _row_spec(tr, d)],
        out_specs=[_row_spec(tr, d), _full_spec((1, d))],
        out_shape=[jax.ShapeDtypeStruct((t, d), F32), jax.ShapeDtypeStruct((1, d), F32)],
        compiler_params=_params(56),
    )(x, g_pre, dh, dy)


def _causal_mask(n):
    row = lax.broadcasted_iota(jnp.int32, (n, n), 0)
    col = lax.broadcasted_iota(jnp.int32, (n, n), 1)
    return row >= col


def _layernorm_stats(vg):
    mu = jnp.mean(vg, axis=-1, keepdims=True)
    cen = vg - mu
    rstd = lax.rsqrt(jnp.mean(cen * cen, axis=-1, keepdims=True) + EPS)
    return cen * rstd, rstd


def _gmlp_fwd(proj_big, ln_g, ln_b, w_s, b_exp):
    t = proj_big.shape[0]
    rows = min(GMLP_ROWS, t)
    d = D_MODEL

    def body(u_ref, v_ref, z_ref, lg_ref, lb_ref, ws_ref, be_ref, y_ref, vn_scr):
        vhat, _ = _layernorm_stats(_gelu(v_ref[...].astype(F32)))
        vn_scr[...] = (vhat * lg_ref[...] + lb_ref[...]).astype(BF16)
        mask = _causal_mask(CHUNK)
        for g in range(A_GROUPS):
            cols = slice(g * 128, (g + 1) * 128)
            wsm = jnp.where(mask, ws_ref[g], 0.0).astype(BF16)
            for c in range(rows // CHUNK):
                rws = slice(c * CHUNK, (c + 1) * CHUNK)
                sv = jnp.dot(wsm, vn_scr[rws, cols], preferred_element_type=F32) + be_ref[g]
                zs, _ = _silu_and_grad(z_ref[rws, cols].astype(F32))
                y_ref[rws, cols] = (_gelu(u_ref[rws, cols].astype(F32)) * sv * zs).astype(BF16)

    return pl.pallas_call(
        body, name="gmlp_fwd", grid=(t // rows,),
        in_specs=[_row_spec(rows, d, 0), _row_spec(rows, d, 1), _row_spec(rows, d, 2),
                  _full_spec((1, d)), _full_spec((1, d)), _full_spec((A_GROUPS, CHUNK, CHUNK)),
                  _full_spec((A_GROUPS, CHUNK, 128))],
        out_specs=_row_spec(rows, d),
        out_shape=jax.ShapeDtypeStruct((t, d), BF16),
        scratch_shapes=[pltpu.VMEM((rows, d), BF16)],
        compiler_params=_params(40),
    )(proj_big, proj_big, proj_big, ln_g, ln_b, w_s, b_exp)


def _gmlp_bwd(proj_big, dya, ln_g, ln_b, w_s, b_exp):
    t = proj_big.shape[0]
    rows = min(GMLP_ROWS, t)
    d = D_MODEL
    nt = (((1,), (1,)), ((), ()))
    tn = (((0,), (0,)), ((), ()))

    def body(u_ref, v_ref, z_ref, dy_ref, lg_ref, lb_ref, ws_ref, be_ref,
             dp_ref, dws_ref, dbs_ref, dlg_ref, dlb_ref, vn_scr, dvn_scr):
        @pl.when(pl.program_id(0) == 0)
        def _():
            dws_ref[...] = jnp.zeros_like(dws_ref)
            dbs_ref[...] = jnp.zeros_like(dbs_ref)
            dlg_ref[...] = jnp.zeros_like(dlg_ref)
            dlb_ref[...] = jnp.zeros_like(dlb_ref)

        vg, vgrad = _gelu_and_grad(v_ref[...].astype(F32))
        vhat, rstd = _layernorm_stats(vg)
        vn_scr[...] = (vhat * lg_ref[...] + lb_ref[...]).astype(BF16)
        mask = _causal_mask(CHUNK)
        for g in range(A_GROUPS):
            cols = slice(g * 128, (g + 1) * 128)
            wsm = jnp.where(mask, ws_ref[g], 0.0).astype(BF16)
            dws = jnp.zeros((CHUNK, CHUNK), F32)
            dbs = jnp.zeros((CHUNK, 1), F32)
            for c in range(rows // CHUNK):
                rws = slice(c * CHUNK, (c + 1) * CHUNK)
                vn = vn_scr[rws, cols]
                sv = jnp.dot(wsm, vn, preferred_element_type=F32) + be_ref[g]
                ug, ugrad = _gelu_and_grad(u_ref[rws, cols].astype(F32))
                zs, zgrad = _silu_and_grad(z_ref[rws, cols].astype(F32))
                dya = dy_ref[rws, cols].astype(F32)
                dga = dya * zs
                dp_ref[rws, 2 * d + g * 128:2 * d + (g + 1) * 128] = (dya * (ug * sv) * zgrad).astype(BF16)
                dp_ref[rws, cols] = (dga * sv * ugrad).astype(BF16)
                dsv = dga * ug
                dsv16 = dsv.astype(BF16)
                dws = dws + lax.dot_general(dsv16, vn, nt, preferred_element_type=F32)
                dbs = dbs + jnp.sum(dsv, axis=-1, keepdims=True)
                dvn_scr[rws, cols] = lax.dot_general(wsm, dsv16, tn, preferred_element_type=F32)
            dws_ref[g] += jnp.where(mask, dws, 0.0)
            dbs_ref[g] += dbs
        dvn = dvn_scr[...]
        dlg_ref[...] += jnp.sum(dvn * vhat, axis=0, keepdims=True)
        dlb_ref[...] += jnp.sum(dvn, axis=0, keepdims=True)
        dvh = dvn * lg_ref[...]
        dvg = rstd * (dvh - jnp.mean(dvh, axis=-1, keepdims=True) - vhat * jnp.mean(dvh * vhat, axis=-1, keepdims=True))
        dp_ref[:, d:2 * d] = (dvg * vgrad).astype(BF16)

    return pl.pallas_call(
        body, name="gmlp_bwd", grid=(t // rows,),
        in_specs=[_row_spec(rows, d, 0), _row_spec(rows, d, 1), _row_spec(rows, d, 2), _row_spec(rows, d),
                  _full_spec((1, d)), _full_spec((1, d)), _full_spec((A_GROUPS, CHUNK, CHUNK)),
                  _full_spec((A_GROUPS, CHUNK, 128))],
        out_specs=[_row_spec(rows, 3 * d), _full_spec((A_GROUPS, CHUNK, CHUNK)), _full_spec((A_GROUPS, CHUNK, 1)),
                   _full_spec((1, d)), _full_spec((1, d))],
        out_shape=[jax.ShapeDtypeStruct((t, 3 * d), BF16), jax.ShapeDtypeStruct((A_GROUPS, CHUNK, CHUNK), F32),
                   jax.ShapeDtypeStruct((A_GROUPS, CHUNK, 1), F32), jax.ShapeDtypeStruct((1, d), F32),
                   jax.ShapeDtypeStruct((1, d), F32)],
        scratch_shapes=[pltpu.VMEM((rows, d), BF16), pltpu.VMEM((rows, d), F32)],
        compiler_params=_params(48),
    )(proj_big, proj_big, proj_big, dya, ln_g, ln_b, w_s, b_exp)


NT_DIMS = (((1,), (1,)), ((), ()))
TN_DIMS = (((0,), (0,)), ((), ()))


def _mla_fwd(q, kv, kpe, proj_big):
    t = q.shape[0]
    blk = min(ATT_BLK, t)
    nq = t // blk
    zb_blk0 = (3 * D_MODEL) // 128

    nc = blk // 128

    def body(q_ref, kv_ref, kp_ref, zb_ref, o_ref, yb_ref, lse_ref, m_scr, acc_scr, p_scr):
        i = pl.program_id(1)
        qv = q_ref[...]
        m_scr[...] = jnp.full_like(m_scr, NEG)
        acc_scr[...] = jnp.zeros_like(acc_scr)
        ones = jnp.ones((blk, 128), BF16)

        def step(j, masked):
            ks = pl.ds(pl.multiple_of(j * blk, blk), blk)
            kc = jnp.concatenate([kv_ref[ks, 0:128], kp_ref[ks, :]], axis=1)
            tt = lax.dot_general(qv, kc, NT_DIMS, preferred_element_type=F32) * (MLA_SCALE * LOG2E)
            if masked:
                tt = jnp.where(_causal_mask(blk), tt, NEG)
            cm = tt[:, 0:128]
            for c in range(1, nc):
                cm = jnp.maximum(cm, tt[:, c * 128:(c + 1) * 128])
            m_prev = m_scr[...]
            m_new = jnp.maximum(m_prev, jnp.max(cm, axis=-1, keepdims=True))
            alpha = jnp.exp2(m_prev - m_new)
            m_scr[...] = m_new
            for c in range(nc):
                p_scr[:, c * 128:(c + 1) * 128] = jnp.exp2(tt[:, c * 128:(c + 1) * 128] - m_new).astype(BF16)
            vext = jnp.concatenate([kv_ref[ks, 128:256], ones], axis=1)
            pv = jnp.dot(p_scr[...], vext, preferred_element_type=F32)
            acc_scr[...] = jnp.concatenate([alpha, alpha], axis=1) * acc_scr[...] + pv

        def loop_body(j, carry):
            step(j, False)
            return carry

        lax.fori_loop(0, i, loop_body, 0)
        step(i, True)
        l = acc_scr[:, 128:256]
        o = acc_scr[:, 0:128] / l
        o_ref[...] = o.astype(BF16)
        zs, _ = _silu_and_grad(zb_ref[...].astype(F32))
        yb_ref[...] = (o * zs).astype(BF16)
        lse_ref[0] = m_scr[...] + jnp.log2(l)

    act = jax.ShapeDtypeStruct((t, D_MODEL), BF16)
    return pl.pallas_call(
        body, name="mla_fwd", grid=(MLA_HEADS, nq),
        in_specs=[pl.BlockSpec((blk, HEAD_PAD), lambda h, i: (i, h)),
                  pl.BlockSpec((t, HEAD_PAD), lambda h, i: (0, h)),
                  pl.BlockSpec((t, 128), lambda h, i: (0, 0)),
                  pl.BlockSpec((blk, 128), lambda h, i: (i, zb_blk0 + h))],
        out_specs=[pl.BlockSpec((blk, 128), lambda h, i: (i, h)),
                   pl.BlockSpec((blk, 128), lambda h, i: (i, h)),
                   pl.BlockSpec((1, blk, 128), lambda h, i: (h, i, 0))],
        out_shape=[act, act, jax.ShapeDtypeStruct((MLA_HEADS, t, 128), F32)],
        scratch_shapes=[pltpu.VMEM((blk, 128), F32), pltpu.VMEM((blk, HEAD_PAD), F32), pltpu.VMEM((blk, blk), BF16)],
        compiler_params=_params(56),
    )(q, kv, kpe, proj_big)


def _mla_gate_bwd(dyb, proj_big, o):
    t, d = dyb.shape
    tr = min(ROW_BLK, t)

    def body(dy_ref, zb_ref, o_ref, do_ref, dz_ref, dl_ref):
        dy = dy_ref[...].astype(F32)
        ov = o_ref[...].astype(F32)
        zs, zgrad = _silu_and_grad(zb_ref[...].astype(F32))
        do16 = (dy * zs).astype(BF16)
        do_ref[...] = do16
        dz_ref[...] = (dy * ov * zgrad).astype(BF16)
        prod = do16.astype(F32) * ov
        for h in range(MLA_HEADS):
            delta = jnp.sum(prod[:, h * 128:(h + 1) * 128], axis=-1, keepdims=True)
            dl_ref[h] = jnp.broadcast_to(delta, (tr, 128))

    act = jax.ShapeDtypeStruct((t, d), BF16)
    head_spec = pl.BlockSpec((MLA_HEADS, tr, 128), lambda i: (0, i, 0))
    return pl.pallas_call(
        body, name="mla_gate_bwd", grid=(t // tr,),
        in_specs=[_row_spec(tr, d), _row_spec(tr, d, 3), _row_spec(tr, d)],
        out_specs=[_row_spec(tr, d), _row_spec(tr, d), head_spec],
        out_shape=[act, act, jax.ShapeDtypeStruct((MLA_HEADS, t, 128), F32)],
        compiler_params=_params(56),
    )(dyb, proj_big, o)


def _mla_bwd(q, kv, kpe, do, lse, delta, carry):
    t = q.shape[0]
    blk = min(ATT_BLK, t)
    n = t // blk
    nc = blk // 128
    pairs = [(j, i) for j in range(n) for i in range(j, n)]
    j_tab = jnp.asarray([p[0] for p in pairs], jnp.int32)
    i_tab = jnp.asarray([p[1] for p in pairs], jnp.int32)
    n_carry = len(carry.arrays)

    def body(j_ref, i_ref, q_ref, do_ref, lse_ref, dl_ref, kv_ref, kp_ref, *rest):
        c_ins, rest = rest[:n_carry], rest[n_carry:]
        dq_ref, dkv_ref, dkp_ref = rest[:3]
        c_outs, rest = rest[3:3 + n_carry], rest[3 + n_carry:]
        dk_scr, dv_scr, p_scr, ds_scr = rest[:4]
        c_sems = rest[4:]
        head = pl.program_id(0)
        step = pl.program_id(1)
        j, i = j_ref[step], i_ref[step]

        @pl.when((head == 0) & (step == 0))
        def _():
            carry.start(c_ins, c_outs, c_sems)

        @pl.when(step == 0)
        def _():
            dq_ref[...] = jnp.zeros_like(dq_ref)

        @pl.when(i == j)
        def _():
            dk_scr[...] = jnp.zeros_like(dk_scr)
            dv_scr[...] = jnp.zeros_like(dv_scr)

        kc = jnp.concatenate([kv_ref[:, 0:128], kp_ref[...]], axis=1)
        qv, dov = q_ref[...], do_ref[...]
        tt = lax.dot_general(qv, kc, NT_DIMS, preferred_element_type=F32) * (MLA_SCALE * LOG2E)
        tt = jnp.where(_causal_mask(blk) | (i > j), tt, NEG)
        dp = lax.dot_general(dov, kv_ref[:, 128:256], NT_DIMS, preferred_element_type=F32)
        lse_v, dl_v = lse_ref[0], dl_ref[0]
        for c in range(nc):
            cols = slice(c * 128, (c + 1) * 128)
            p = jnp.exp2(tt[:, cols] - lse_v)
            p_scr[:, cols] = p.astype(BF16)
            ds_scr[:, cols] = (p * (dp[:, cols] - dl_v)).astype(BF16)
        dv_scr[...] += lax.dot_general(p_scr[...], dov, TN_DIMS, preferred_element_type=F32)
        dk_scr[...] += lax.dot_general(ds_scr[...], qv, TN_DIMS, preferred_element_type=F32)
        qs = pl.ds(pl.multiple_of(i * blk, blk), blk)
        dq_ref[qs, :] += jnp.dot(ds_scr[...], kc, preferred_element_type=F32)

        @pl.when(i == n - 1)
        def _():
            dkv_ref[:, 0:128] = (dk_scr[:, 0:128] * MLA_SCALE).astype(BF16)
            dkv_ref[:, 128:256] = dv_scr[...].astype(BF16)
            dkp_ref[0] = dk_scr[:, 128:256] * MLA_SCALE

        @pl.when((head == MLA_HEADS - 1) & (step == len(pairs) - 1))
        def _():
            carry.finish(c_ins, c_outs, c_sems)

    grid_spec = pltpu.PrefetchScalarGridSpec(
        num_scalar_prefetch=2, grid=(MLA_HEADS, len(pairs)),
        in_specs=[pl.BlockSpec((blk, HEAD_PAD), lambda h, s, jt, it: (it[s], h)),
                  pl.BlockSpec((blk, 128), lambda h, s, jt, it: (it[s], h)),
                  pl.BlockSpec((1, blk, 128), lambda h, s, jt, it: (h, it[s], 0)),
                  pl.BlockSpec((1, blk, 128), lambda h, s, jt, it: (h, it[s], 0)),
                  pl.BlockSpec((blk, HEAD_PAD), lambda h, s, jt, it: (jt[s], h)),
                  pl.BlockSpec((blk, 128), lambda h, s, jt, it: (jt[s], 0))] + [HBM_SPEC] * n_carry,
        out_specs=[pl.BlockSpec((t, HEAD_PAD), lambda h, s, jt, it: (0, h)),
                   pl.BlockSpec((blk, HEAD_PAD), lambda h, s, jt, it: (jt[s], h)),
                   pl.BlockSpec((1, blk, 128), lambda h, s, jt, it: (h, jt[s], 0))] + [HBM_SPEC] * n_carry,
        scratch_shapes=[pltpu.VMEM((blk, HEAD_PAD), F32), pltpu.VMEM((blk, 128), F32),
                        pltpu.VMEM((blk, blk), BF16), pltpu.VMEM((blk, blk), BF16)] + carry.sem_shapes,
    )
    return pl.pallas_call(
        body, name="mla_bwd", grid_spec=grid_spec,
        out_shape=[jax.ShapeDtypeStruct((t, MLA_HEADS * HEAD_PAD), F32),
                   jax.ShapeDtypeStruct((t, 2 * D_MODEL), BF16),
                   jax.ShapeDtypeStruct((MLA_HEADS, t, 128), F32)] + carry.out_shapes,
        compiler_params=_params(58),
    )(j_tab, i_tab, q, do, lse, delta, kv, kpe, *carry.arrays)


def _mem_attn_probs(qv, k_ref):
    s = lax.dot_general(qv, k_ref[...], NT_DIMS, preferred_element_type=F32) * MEM_SCALE
    e = jnp.exp(s - jnp.max(s, axis=-1, keepdims=True))
    return e / jnp.sum(e, axis=-1, keepdims=True)


def _mem_fwd(proj_big, kv_m):
    t = proj_big.shape[0]
    tq = min(MEM_Q_BLK, t)
    hd = MEM_HEAD_DIM
    q0, z0 = (4 * D_MODEL) // hd, (5 * D_MODEL) // hd

    def body(q_ref, z_ref, k_ref, v_ref, y_ref):
        p = _mem_attn_probs(q_ref[...], k_ref)
        o = jnp.dot(p.astype(BF16), v_ref[...], preferred_element_type=F32)
        zs, _ = _silu_and_grad(z_ref[...].astype(F32))
        y_ref[...] = (o * zs).astype(BF16)

    return pl.pallas_call(
        body, name="mem_fwd", grid=(t // tq, MEM_HEADS),
        in_specs=[pl.BlockSpec((tq, hd), lambda i, h: (i, q0 + h)), pl.BlockSpec((tq, hd), lambda i, h: (i, z0 + h)),
                  pl.BlockSpec((kv_m.shape[0], hd), lambda i, h: (0, h)),
                  pl.BlockSpec((kv_m.shape[0], hd), lambda i, h: (0, MEM_HEADS + h))],
        out_specs=pl.BlockSpec((tq, hd), lambda i, h: (i, h)),
        out_shape=jax.ShapeDtypeStruct((t, D_MODEL), BF16),
    )(proj_big, proj_big, kv_m, kv_m)


def _mem_bwd(proj_big, kv_m, dym):
    t = proj_big.shape[0]
    tq = min(MEM_Q_BLK, t)
    nq = t // tq
    hd = MEM_HEAD_DIM
    q0, z0 = (4 * D_MODEL) // hd, (5 * D_MODEL) // hd
    mlen = kv_m.shape[0]

    def body(q_ref, z_ref, k_ref, v_ref, dy_ref, dq_ref, dz_ref, dk_ref, dv_ref):
        @pl.when(pl.program_id(1) == 0)
        def _():
            dk_ref[...] = jnp.zeros_like(dk_ref)
            dv_ref[...] = jnp.zeros_like(dv_ref)

        qv = q_ref[...]
        p = _mem_attn_probs(qv, k_ref)
        p16 = p.astype(BF16)
        o = jnp.dot(p16, v_ref[...], preferred_element_type=F32)
        zs, zgrad = _silu_and_grad(z_ref[...].astype(F32))
        dy = dy_ref[...].astype(F32)
        dz_ref[...] = (dy * o * zgrad).astype(BF16)
        do = dy * zs
        do16 = do.astype(BF16)
        dv_ref[...] += lax.dot_general(p16, do16, TN_DIMS, preferred_element_type=F32)
        dp = lax.dot_general(do16, v_ref[...], NT_DIMS, preferred_element_type=F32)
        ds = (p * (dp - jnp.sum(dp * p, axis=-1, keepdims=True)) * MEM_SCALE).astype(BF16)
        dq_ref[...] = jnp.dot(ds, k_ref[...], preferred_element_type=F32).astype(BF16)
        dk_ref[...] += lax.dot_general(ds, qv, TN_DIMS, preferred_element_type=F32)

    return pl.pallas_call(
        body, name="mem_bwd", grid=(MEM_HEADS, nq),
        in_specs=[pl.BlockSpec((tq, hd), lambda h, i: (i, q0 + h)), pl.BlockSpec((tq, hd), lambda h, i: (i, z0 + h)),
                  pl.BlockSpec((mlen, hd), lambda h, i: (0, h)), pl.BlockSpec((mlen, hd), lambda h, i: (0, MEM_HEADS + h)),
                  pl.BlockSpec((tq, hd), lambda h, i: (i, h))],
        out_specs=[pl.BlockSpec((tq, hd), lambda h, i: (i, h)), pl.BlockSpec((tq, hd), lambda h, i: (i, h)),
                   pl.BlockSpec((mlen, hd), lambda h, i: (0, h)), pl.BlockSpec((mlen, hd), lambda h, i: (0, h))],
        out_shape=[jax.ShapeDtypeStruct((t, D_MODEL), BF16), jax.ShapeDtypeStruct((t, D_MODEL), BF16),
                   jax.ShapeDtypeStruct((mlen, D_MODEL), F32), jax.ShapeDtypeStruct((mlen, D_MODEL), F32)],
    )(proj_big, proj_big, kv_m, kv_m, dym)


HBM_SPEC = pl.BlockSpec(memory_space=pl.ANY)
N_PEERS = N_DEV - 1


def _dev_index(px, py, pc):
    return 4 * px + 2 * py + pc


class _Gather:
    def __init__(self, arrays):
        self.arrays = list(arrays)
        n = len(self.arrays)
        self.out_shapes = [jax.ShapeDtypeStruct((N_DEV,) + a.shape, a.dtype) for a in self.arrays]
        self.sem_shapes = [pltpu.SemaphoreType.DMA((n * N_PEERS,)), pltpu.SemaphoreType.DMA((n * N_PEERS,)),
                           pltpu.SemaphoreType.DMA((n,))]

    def _parts(self, ins, outs, sems):
        n = len(self.arrays)
        send_sems, recv_sems, local_sems = sems
        x, y, c = lax.axis_index("x"), lax.axis_index("y"), lax.axis_index("c")
        me, sibling = (x, y, c), (x, y, 1 - c)
        chips = [(1 - x, y), (x, 1 - y), (1 - x, 1 - y)]

        def copy(a, k, block, to, src=None):
            dst = outs[a].at[_dev_index(*block)]
            return pltpu.make_async_remote_copy(
                src_ref=dst if src is None else src, dst_ref=dst,
                send_sem=send_sems.at[a * N_PEERS + k], recv_sem=recv_sems.at[a * N_PEERS + k],
                device_id=to, device_id_type=MESH)

        mine = [pltpu.make_async_copy(ins[a], outs[a].at[_dev_index(*me)], local_sems.at[a]) for a in range(n)]
        first = []
        for a in range(n):
            first.append(copy(a, 0, me, sibling, src=ins[a]))
            first += [copy(a, 1 + j, me, (*chip, c), src=ins[a]) for j, chip in enumerate(chips)]
        return n, c, me, sibling, chips, copy, mine, first

    def start(self, ins, outs, sems):
        _, _, _, _, _, _, mine, first = self._parts(ins, outs, sems)
        for cp in mine + first:
            cp.start()

    def finish(self, ins, outs, sems):
        n, c, me, sibling, chips, copy, mine, first = self._parts(ins, outs, sems)
        passed = []
        for j, chip in enumerate(chips):
            for a in range(n):
                copy(a, 1 + j, (*chip, c), me).wait_recv()
                fwd = copy(a, 4 + j, (*chip, c), sibling)
                fwd.start()
                passed.append(fwd)
        for a in range(n):
            copy(a, 0, sibling, me).wait_recv()
            for j, chip in enumerate(chips):
                copy(a, 4 + j, (*chip, 1 - c), me).wait_recv()
        for cp in first + passed:
            cp.wait_send()
        for cp in mine:
            cp.wait()


class _AllToAll:
    def __init__(self, arrays):
        self.arrays = list(arrays)
        n = len(self.arrays)
        self.out_shapes = [jax.ShapeDtypeStruct(a.shape, a.dtype) for a in self.arrays]
        self.sem_shapes = [pltpu.SemaphoreType.DMA((n * N_PEERS,)), pltpu.SemaphoreType.DMA((n * N_PEERS,)),
                           pltpu.SemaphoreType.DMA((n,))]

    def _parts(self, ins, outs, sems):
        n = len(self.arrays)
        send_sems, recv_sems, local_sems = sems
        x, y, c = lax.axis_index("x"), lax.axis_index("y"), lax.axis_index("c")
        my_idx = _dev_index(x, y, c)
        peers = []
        for k in range(1, N_DEV):
            dx, dy, dc = (k >> 2) & 1, (k >> 1) & 1, k & 1
            peers.append((1 - x if dx else x, 1 - y if dy else y, 1 - c if dc else c))

        def copy(a, k, peer):
            return pltpu.make_async_remote_copy(
                src_ref=ins[a].at[_dev_index(*peer)], dst_ref=outs[a].at[my_idx],
                send_sem=send_sems.at[a * N_PEERS + k], recv_sem=recv_sems.at[a * N_PEERS + k],
                device_id=peer, device_id_type=MESH)

        def landed(a, k, peer):
            slot = outs[a].at[_dev_index(*peer)]
            return pltpu.make_async_remote_copy(
                src_ref=slot, dst_ref=slot,
                send_sem=send_sems.at[a * N_PEERS + k], recv_sem=recv_sems.at[a * N_PEERS + k],
                device_id=peer, device_id_type=MESH)

        mine = [pltpu.make_async_copy(ins[a].at[my_idx], outs[a].at[my_idx], local_sems.at[a]) for a in range(n)]
        sends = [copy(a, k, peer) for a in range(n) for k, peer in enumerate(peers)]
        return n, peers, landed, mine, sends

    def start(self, ins, outs, sems):
        _, _, _, mine, sends = self._parts(ins, outs, sems)
        for cp in mine + sends:
            cp.start()

    def finish(self, ins, outs, sems):
        n, peers, landed, mine, sends = self._parts(ins, outs, sems)
        for a in range(n):
            for k, peer in enumerate(peers):
                landed(a, k, peer).wait_recv()
        for cp in sends:
            cp.wait_send()
        for cp in mine:
            cp.wait()


def _exchange(plan, name):
    n = len(plan.arrays)

    def body(*refs):
        ins, outs, sems = refs[:n], refs[n:2 * n], refs[2 * n:]
        plan.start(ins, outs, sems)
        plan.finish(ins, outs, sems)

    return pl.pallas_call(
        body, name=name, in_specs=[HBM_SPEC] * n, out_specs=[HBM_SPEC] * n,
        out_shape=plan.out_shapes, scratch_shapes=plan.sem_shapes,
    )(*plan.arrays)


def _adamw(w, g, m, v):
    m = ADAM_B1 * m + (1.0 - ADAM_B1) * g
    v = ADAM_B2 * v + (1.0 - ADAM_B2) * jnp.square(g)
    m_hat = m / (1.0 - ADAM_B1 ** ADAM_STEP)
    v_hat = v / (1.0 - ADAM_B2 ** ADAM_STEP)
    delta = -ADAM_LR * (m_hat / (jnp.sqrt(v_hat) + ADAM_EPS) + ADAM_WD * w)
    return delta, m, v


def _adam_sharded(parts, w, m, v, name):
    shape = w.shape
    cols = shape[-1]
    rows = int(np.prod(shape[:-1]))
    tr = min(128, rows)
    assert rows % tr == 0

    def body(p_ref, w_ref, m_ref, v_ref, g_ref, d_ref, nm_ref, nv_ref):
        g = p_ref[0].astype(F32)
        for e in range(1, N_DEV):
            g = g + p_ref[e].astype(F32)
        g_ref[...] = g
        d_ref[...], nm_ref[...], nv_ref[...] = _adamw(w_ref[...], g, m_ref[...], v_ref[...])

    spec = pl.BlockSpec((tr, cols), lambda i: (i, 0))
    flat = jax.ShapeDtypeStruct((rows, cols), F32)
    outs = pl.pallas_call(
        body, name=name, grid=(rows // tr,),
        in_specs=[pl.BlockSpec((N_DEV, tr, cols), lambda i: (0, i, 0)), spec, spec, spec],
        out_specs=[spec] * 4, out_shape=[flat] * 4,
        compiler_params=_params(40),
    )(parts.reshape(N_DEV, rows, cols), w.reshape(rows, cols), m.reshape(rows, cols), v.reshape(rows, cols))
    return [o.reshape(shape) for o in outs]


def _adam_replicated(parts, w, m, v):
    r = w.shape[0]

    def body(p_ref, w_ref, m_ref, v_ref, g_ref, d_ref, nm_ref, nv_ref):
        g = p_ref[0]
        for e in range(1, N_DEV):
            g = g + p_ref[e]
        g_ref[...] = g
        d_ref[...], nm_ref[...], nv_ref[...] = _adamw(w_ref[...], g, m_ref[...], v_ref[...])

    spec = _full_spec((r, 128))
    flat = jax.ShapeDtypeStruct((r, 128), F32)
    return pl.pallas_call(
        body, name="adam_replicated", grid=(1,),
        in_specs=[_full_spec((N_DEV, r, 128)), spec, spec, spec],
        out_specs=[spec] * 4, out_shape=[flat] * 4,
        compiler_params=_params(48),
    )(parts, w, m, v)


def _pack(arrays):
    parts = []
    for a in arrays:
        f = a.reshape(-1, 128)
        pad = -f.shape[0] % 8
        parts.append(jnp.pad(f, ((0, pad), (0, 0))) if pad else f)
    return jnp.concatenate(parts, axis=0)


def _unpack(packed, shapes):
    out, row = [], 0
    for shape in shapes:
        r = int(np.prod(shape)) // 128
        out.append(packed[row:row + r].reshape(shape))
        row += r + (-r % 8)
    return out


SHARDED = ("w_in", "w_uq", "w_ukv", "w_mem_kv", "w_gate", "w_branch", "w_out")
REPLICATED = ("g_pre", "a_ln_g", "a_ln_b", "a_w_s", "a_b_s", "q_norm_g", "kv_norm_g", "mem_norm_g", "b_gate", "g_post")
WEIGHT_ORDER = ("g_pre", "w_in", "a_ln_g", "a_ln_b", "a_w_s", "a_b_s", "q_norm_g", "w_uq", "kv_norm_g", "w_ukv",
                "mem_norm_g", "w_mem_kv", "w_gate", "b_gate", "w_branch", "w_out", "g_post")


def _unshard_cols(g):
    return g.transpose(1, 0, 2).reshape(g.shape[1], N_DEV * g.shape[2])


def _shard_cols(full):
    rows, n = full.shape
    return full.reshape(rows, N_DEV, n // N_DEV).transpose(1, 0, 2).astype(BF16)


def kernel(x, mem, positions, g_pre, w_in, a_ln_g, a_ln_b, a_w_s, a_b_s, q_norm_g, w_uq, kv_norm_g, w_ukv, mem_norm_g, w_mem_kv, w_gate, b_gate, w_branch, w_out, g_post, loss_target, m_g_pre, m_w_in, m_a_ln_g, m_a_ln_b, m_a_w_s, m_a_b_s, m_q_norm_g, m_w_uq, m_kv_norm_g, m_w_ukv, m_mem_norm_g, m_w_mem_kv, m_w_gate, m_b_gate, m_w_branch, m_w_out, m_g_post, v_g_pre, v_w_in, v_a_ln_g, v_a_ln_b, v_a_w_s, v_a_b_s, v_q_norm_g, v_w_uq, v_kv_norm_g, v_w_ukv, v_mem_norm_g, v_w_mem_kv, v_w_gate, v_b_gate, v_w_branch, v_w_out, v_g_post):
    weights = dict(g_pre=g_pre, w_in=w_in, a_ln_g=a_ln_g, a_ln_b=a_ln_b, a_w_s=a_w_s, a_b_s=a_b_s, q_norm_g=q_norm_g,
                   w_uq=w_uq, kv_norm_g=kv_norm_g, w_ukv=w_ukv, mem_norm_g=mem_norm_g, w_mem_kv=w_mem_kv,
                   w_gate=w_gate, b_gate=b_gate, w_branch=w_branch, w_out=w_out, g_post=g_post)
    mom1 = dict(g_pre=m_g_pre, w_in=m_w_in, a_ln_g=m_a_ln_g, a_ln_b=m_a_ln_b, a_w_s=m_a_w_s, a_b_s=m_a_b_s,
                q_norm_g=m_q_norm_g, w_uq=m_w_uq, kv_norm_g=m_kv_norm_g, w_ukv=m_w_ukv, mem_norm_g=m_mem_norm_g,
                w_mem_kv=m_w_mem_kv, w_gate=m_w_gate, b_gate=m_b_gate, w_branch=m_w_branch, w_out=m_w_out, g_post=m_g_post)
    mom2 = dict(g_pre=v_g_pre, w_in=v_w_in, a_ln_g=v_a_ln_g, a_ln_b=v_a_ln_b, a_w_s=v_a_w_s, a_b_s=v_a_b_s,
                q_norm_g=v_q_norm_g, w_uq=v_w_uq, kv_norm_g=v_kv_norm_g, w_ukv=v_w_ukv, mem_norm_g=v_mem_norm_g,
                w_mem_kv=v_w_mem_kv, w_gate=v_w_gate, b_gate=v_b_gate, w_branch=v_w_branch, w_out=v_w_out, g_post=v_g_post)
    d = D_MODEL
    t = x.shape[1]
    xs, tgt, mems = x[0], loss_target[0], mem[0]
    pos_col = positions.reshape(t, 1)

    shard16 = {n: weights[n][0].astype(BF16) for n in SHARDED}
    g_in, g_uq, g_ukv = _exchange(_Gather([shard16[n] for n in ("w_in", "w_uq", "w_ukv")]), "gather_weights_in")
    w_in_full = _unshard_cols(g_in)
    lat0, lat1 = 3 * d, 3 * d + Q_LORA + KV_LORA + QK_ROPE
    w_big = jnp.concatenate([w_in_full[:, :lat0], w_in_full[:, lat1:]], axis=1)
    w_lat = jnp.concatenate([w_in_full[:, lat0:lat1], jnp.zeros((d, LAT_W - (lat1 - lat0)), BF16)], axis=1)
    w_uq_p = jnp.pad(_unshard_cols(g_uq).reshape(Q_LORA, MLA_HEADS, QK_DIM),
                     ((0, 0), (0, 0), (0, HEAD_PAD - QK_DIM))).reshape(Q_LORA, MLA_HEADS * HEAD_PAD)
    w_ukv_f = _unshard_cols(g_ukv)

    inv_freq = 1.0 / (ROPE_THETA ** (jnp.arange(0, QK_ROPE, 2, dtype=F32) / QK_ROPE))
    inv_freq_lanes = jnp.concatenate([inv_freq, inv_freq, jnp.zeros((128 - QK_ROPE,), F32)]).reshape(1, 128)
    ws = a_w_s[0]
    b_exp = jnp.broadcast_to(a_b_s[0][:, :, None], (A_GROUPS, CHUNK, 128))

    h = _pre_norm(xs, g_pre)
    proj_big, g_mem, g_gate, g_br, g_out = _mm(
        h, w_big, name="proj_big", tm=1024, tn=1024, tk=2048,
        carry=_Gather([shard16[n] for n in ("w_mem_kv", "w_gate", "w_branch", "w_out")]))
    w_mem_f = _unshard_cols(g_mem)
    w_gate_f = _unshard_cols(g_gate)
    w_br_f = g_br.transpose(1, 0, 2, 3).reshape(3, d, d)
    w_out_f = g_out.reshape(d, d)
    proj_lat = _mm(h, w_lat, name="proj_lat", tm=1024, tn=LAT_W, tk=2048)
    gates = _mm(h, w_gate_f, name="gates", tm=1024, tn=1024, tk=2048, bias=b_gate, act="sigmoid")
    c_tab, sa_tab, sb_tab = _rope_tables(pos_col, inv_freq_lanes)
    cqn, ckvn, kpe = _latent_norms(proj_lat, q_norm_g, kv_norm_g, c_tab, sa_tab, sb_tab)
    q_raw = _mm(cqn, w_uq_p, name="q_up", tm=1024, tn=1024, tk=512, out_dtype=F32)
    q = _rope_q(q_raw, c_tab, sa_tab, sb_tab)
    kv = _mm(ckvn, w_ukv_f, name="kv_up", tm=1024, tn=1024, tk=512)
    o_b, y_b, lse = _mla_fwd(q, kv, kpe, proj_big)
    memn = _mem_norm(mems, mem_norm_g)
    kv_m = _mm(memn, w_mem_f, name="mem_kv", tm=256, tn=1024, tk=2048)
    y_m = _mem_fwd(proj_big, kv_m)
    y_a = _gmlp_fwd(proj_big, a_ln_g, a_ln_b, ws, b_exp)
    ys = (y_a, y_b, y_m)
    ps = [_mm(ys[n], w_br_f[n], name=f"branch{n}", tm=1024, tn=1024, tk=2048) for n in range(3)]
    merged = _merge(gates, *ps)
    out = _mm(merged, w_out_f, name="out_proj", tm=1024, tn=1024, tk=2048, out_dtype=F32)
    d_out, dy, loss_blk, dg_post = _post_loss(out, xs, tgt, g_post)

    dmerged = _mm(d_out, w_out_f, name="d_merged", tb=True, tm=1024, tn=1024, tk=2048)
    dw_out = _mm(merged, d_out, name="dw_out", ta=True, tm=1024, tn=1024, tk=1024, out_dtype=F32)
    dp_a, dp_b, dp_m, dgl, db_gate = _merge_bwd(dmerged, gates, *ps)
    dps = (dp_a, dp_b, dp_m)
    dys = [_mm(dps[n], w_br_f[n], name=f"d_y{n}", tb=True, tm=1024, tn=1024, tk=2048) for n in range(3)]
    dw_br = [_mm(ys[n], dps[n], name=f"dw_branch{n}", ta=True, tm=1024, tn=1024, tk=1024, out_dtype=F32) for n in range(3)]
    dw_gate = _mm(h, dgl, name="dw_gate", ta=True, tm=1024, tn=1024, tk=1024, out_dtype=F32)

    d_abig, dws, dbs, dlng, dlnb = _gmlp_bwd(proj_big, dys[0], a_ln_g, a_ln_b, ws, b_exp)

    recv = {}
    send = [_shard_cols(dw_gate), jnp.stack(dw_br).reshape(3, N_DEV, d // N_DEV, d).transpose(1, 0, 2, 3).astype(BF16),
            dw_out.reshape(N_DEV, d // N_DEV, d).astype(BF16)]
    do_b, dz_b, delta = _mla_gate_bwd(dys[1], proj_big, o_b)
    dq, dkv, dkpe_h, recv["w_gate"], recv["w_branch"], recv["w_out"] = _mla_bwd(
        q, kv, kpe, do_b, lse, delta, _AllToAll(send))
    dq_raw, dkr = _rope_q_bwd(dq, dkpe_h, c_tab, sa_tab, sb_tab)
    dcqn = _mm(dq_raw, w_uq_p, name="d_cq", tb=True, tm=1024, tn=Q_LORA, tk=2048, out_dtype=F32)
    dw_uq_p = _mm(cqn, dq_raw, name="dw_uq", ta=True, tm=Q_LORA, tn=1024, tk=1024, out_dtype=F32)
    dckvn = _mm(dkv, w_ukv_f, name="d_ckv", tb=True, tm=1024, tn=KV_LORA, tk=2048, out_dtype=F32)
    dw_ukv = _mm(ckvn, dkv, name="dw_ukv", ta=True, tm=KV_LORA, tn=1024, tk=1024, out_dtype=F32)
    dproj_lat, dqg, dkg = _latent_norms_bwd(proj_lat, q_norm_g, kv_norm_g, dcqn, dckvn, dkr)

    dq_m, dz_m, dk_m, dv_m = _mem_bwd(proj_big, kv_m, dys[2])
    dkv_m = jnp.concatenate([dk_m, dv_m], axis=1).astype(BF16)
    dw_mem = _mm(memn, dkv_m, name="dw_mem", ta=True, tm=1024, tn=1024, tk=256, out_dtype=F32)
    dmemn = _mm(dkv_m, w_mem_f, name="d_memn", tb=True, tm=256, tn=1024, tk=2048, out_dtype=F32)
    dg_mem = _mem_norm_bwd(mems, mem_norm_g, dmemn)

    dw_uq_full = dw_uq_p.reshape(Q_LORA, MLA_HEADS, HEAD_PAD)[:, :, :QK_DIM].reshape(Q_LORA, MLA_HEADS * QK_DIM)
    dh, recv["w_uq"], recv["w_ukv"], recv["w_mem_kv"] = _mm(
        dgl, w_gate_f, name="dh_gate", tb=True, tm=1024, tn=1024, tk=1024, out_dtype=F32,
        carry=_AllToAll([_shard_cols(dw_uq_full), _shard_cols(dw_ukv), _shard_cols(dw_mem)]))
    dproj_big = jnp.concatenate([d_abig, dz_b, dq_m, dz_m], axis=1)
    dw_big = _mm(h, dproj_big, name="dw_big", ta=True, tm=1024, tn=1024, tk=1024, out_dtype=F32)
    dw_lat = _mm(h, dproj_lat, name="dw_lat", ta=True, tm=1024, tn=LAT_W, tk=1024, out_dtype=F32)
    dw_in_full = jnp.concatenate([dw_big[:, :lat0], dw_lat[:, :lat1 - lat0], dw_big[:, lat0:]], axis=1)
    dh = _mm(dproj_lat, w_lat, name="dh_lat", tb=True, tm=1024, tn=1024, tk=LAT_W, out_dtype=F32, add=dh)
    dh, recv["w_in"] = _mm(dproj_big, w_big, name="dh_big", tb=True, tm=1024, tn=1024, tk=1024, out_dtype=F32, add=dh,
                           carry=_AllToAll([_shard_cols(dw_in_full)]))
    grad_x, dg_pre = _pre_norm_bwd(xs, g_pre, dh, dy)

    results = {}
    for n in SHARDED:
        results[n] = [r[None] for r in _adam_sharded(recv[n], weights[n][0], mom1[n][0], mom2[n][0], "adam_" + n)]

    small = dict(g_pre=dg_pre, a_ln_g=dlng, a_ln_b=dlnb, a_w_s=dws, a_b_s=dbs, q_norm_g=dqg, kv_norm_g=dkg,
                 mem_norm_g=dg_mem, b_gate=db_gate, g_post=dg_post)
    (parts,) = _exchange(_Gather([_pack([small[n] for n in REPLICATED])]), "gather_small_grads")
    packed = _adam_replicated(parts, _pack([weights[n] for n in REPLICATED]), _pack([mom1[n] for n in REPLICATED]),
                              _pack([mom2[n] for n in REPLICATED]))
    shapes = [weights[n].shape for n in REPLICATED]
    unpacked = [_unpack(p, shapes) for p in packed]
    for i, n in enumerate(REPLICATED):
        results[n] = [u[i] for u in unpacked]

    loss = lax.psum(loss_blk[0, 0], AXES)
    outs = [loss, grad_x[None]]
    for kind in range(4):
        outs += [results[n][kind] for n in WEIGHT_ORDER]
    return tuple(outs)
```

```python
import functools
import math

import jax
import jax.numpy as jnp
import numpy as np
from jax import lax
from jax.experimental import pallas as pl
from jax.experimental.pallas import tpu as pltpu

F32 = jnp.float32
BF16 = jnp.bfloat16
MESH = pl.DeviceIdType.MESH
AXES = ("x", "y", "c")
N_DEV = 8

D_MODEL = 2048
EPS = 1e-6
CHUNK = 128
A_GROUPS = 16
MLA_HEADS = 16
QK_NOPE = 128
QK_ROPE = 64
QK_DIM = QK_NOPE + QK_ROPE
HEAD_PAD = 256
Q_LORA = 512
KV_LORA = 512
MEM_HEADS = 4
MEM_HEAD_DIM = 512
ROPE_THETA = 10000.0
MLA_SCALE = QK_DIM ** -0.5
MEM_SCALE = MEM_HEAD_DIM ** -0.5
NEG = -1e30
LOG2E = 1.4426950408889634

ADAM_LR = 0.001
ADAM_B1 = 0.9
ADAM_B2 = 0.999
ADAM_EPS = 1e-08
ADAM_WD = 0.01
ADAM_STEP = 10

BIG_W = 6 * D_MODEL
LAT_W = Q_LORA + KV_LORA + 128

VMEM_MIB = 1024 * 1024

ROW_BLK = 256
ATT_BLK = 1024
ATT_SUB = 256
GMLP_ROWS = 256
MEM_Q_BLK = 512


def _params(vmem_mib, **kw):
    return pltpu.CompilerParams(vmem_limit_bytes=int(vmem_mib * VMEM_MIB), **kw)


def _gelu(x):
    k = math.sqrt(2.0 / math.pi)
    t = jnp.tanh(k * (x + 0.044715 * (x * x * x)))
    return 0.5 * x * (1.0 + t)


def _gelu_and_grad(x):
    k = math.sqrt(2.0 / math.pi)
    x2 = x * x
    t = jnp.tanh(k * (x + 0.044715 * (x2 * x)))
    val = 0.5 * x * (1.0 + t)
    grad = 0.5 * (1.0 + t) + 0.5 * x * (1.0 - t * t) * (k * (1.0 + 3.0 * 0.044715 * x2))
    return val, grad


def _silu_and_grad(z):
    s = jax.nn.sigmoid(z)
    return z * s, s * (1.0 + z * (1.0 - s))


def _mm(a, b, *, name, tm, tn, tk, ta=False, tb=False, out_dtype=BF16, bias=None, act=None, add=None, carry=None):
    m = a.shape[1] if ta else a.shape[0]
    k = a.shape[0] if ta else a.shape[1]
    n = b.shape[0] if tb else b.shape[1]
    assert k == (b.shape[1] if tb else b.shape[0])
    tm, tn, tk = min(tm, m), min(tn, n), min(tk, k)
    assert m % tm == 0 and n % tn == 0 and k % tk == 0, (name, m, n, k, tm, tn, tk)
    nk = k // tk
    a_spec = pl.BlockSpec((tk, tm), lambda i, j, kk: (kk, i)) if ta else pl.BlockSpec((tm, tk), lambda i, j, kk: (i, kk))
    b_spec = pl.BlockSpec((tn, tk), lambda i, j, kk: (j, kk)) if tb else pl.BlockSpec((tk, tn), lambda i, j, kk: (kk, j))
    dn = (((0 if ta else 1,), (1 if tb else 0,)), ((), ()))
    operands, in_specs = [a, b], [a_spec, b_spec]
    if bias is not None:
        operands.append(bias)
        in_specs.append(pl.BlockSpec((1, tn), lambda i, j, kk: (0, j)))
    if add is not None:
        operands.append(add)
        in_specs.append(pl.BlockSpec((tm, tn), lambda i, j, kk: (i, j)))

    n_in = len(operands)
    n_carry = len(carry.arrays) if carry is not None else 0
    n_acc = 1 if nk > 1 else 0
    grid = (m // tm, n // tn, nk)

    def body(*refs):
        a_ref, b_ref = refs[0], refs[1]
        pos = 2
        bias_ref = add_ref = None
        if bias is not None:
            bias_ref = refs[pos]
            pos += 1
        if add is not None:
            add_ref = refs[pos]
            pos += 1
        o_ref = refs[n_in + n_carry]
        pos = n_in + n_carry
        if carry is not None:
            c_ins = refs[n_in:n_in + n_carry]
            c_outs = refs[n_in + n_carry + 1:n_in + 2 * n_carry + 1]
            c_sems = refs[n_in + 2 * n_carry + 1 + n_acc:]
            ids = [pl.program_id(ax) for ax in range(3)]

            @pl.when((ids[0] == 0) & (ids[1] == 0) & (ids[2] == 0))
            def _():
                carry.start(c_ins, c_outs, c_sems)

        part = lax.dot_general(a_ref[...], b_ref[...], dn, preferred_element_type=F32)

        def finish(acc):
            if bias_ref is not None:
                acc = acc + bias_ref[...]
            if act == "sigmoid":
                acc = jax.nn.sigmoid(acc)
            if add_ref is not None:
                acc = acc + add_ref[...]
            o_ref[...] = acc.astype(o_ref.dtype)

        if nk == 1:
            finish(part)
        else:
            acc_ref = refs[n_in + 2 * n_carry + 1]
            kk = pl.program_id(2)

            @pl.when(kk == 0)
            def _():
                acc_ref[...] = part

            @pl.when(kk > 0)
            def _():
                acc_ref[...] += part

            @pl.when(kk == nk - 1)
            def _():
                finish(acc_ref[...])

        if carry is not None:
            @pl.when((ids[0] == grid[0] - 1) & (ids[1] == grid[1] - 1) & (ids[2] == grid[2] - 1))
            def _():
                carry.finish(c_ins, c_outs, c_sems)

    osz = jnp.dtype(out_dtype).itemsize
    est = 2 * 2 * (tm * tk + tk * tn) + 2 * osz * tm * tn + 8 * tm * tn + (2 * 4 * tm * tn if add is not None else 0)
    main_spec = pl.BlockSpec((tm, tn), lambda i, j, kk: (i, j))
    main_shape = jax.ShapeDtypeStruct((m, n), out_dtype)
    scratch = [pltpu.VMEM((tm, tn), F32)] if nk > 1 else []
    if carry is None:
        return pl.pallas_call(
            body, name=name, grid=grid, in_specs=in_specs, out_specs=main_spec, out_shape=main_shape,
            scratch_shapes=scratch, compiler_params=_params(min(56, est / VMEM_MIB + 12)),
        )(*operands)
    return pl.pallas_call(
        body, name=name, grid=grid,
        in_specs=in_specs + [HBM_SPEC] * n_carry,
        out_specs=[main_spec] + [HBM_SPEC] * n_carry,
        out_shape=[main_shape] + carry.out_shapes,
        scratch_shapes=scratch + carry.sem_shapes,
        compiler_params=_params(min(56, est / VMEM_MIB + 12)),
    )(*operands, *carry.arrays)


def _row_spec(tr, cols, col_blk=0):
    return pl.BlockSpec((tr, cols), lambda i: (i, col_blk))


def _full_spec(shape):
    nd = len(shape)
    return pl.BlockSpec(shape, lambda i: (0,) * nd)


def _pre_norm(x, g_pre):
    t, d = x.shape
    tr = min(ROW_BLK, t)

    def body(x_ref, g_ref, h_ref):
        xv = x_ref[...]
        r = lax.rsqrt(jnp.mean(xv * xv, axis=-1, keepdims=True) + EPS)
        h_ref[...] = ((xv * r) * g_ref[...]).astype(BF16)

    return pl.pallas_call(
        body, name="pre_norm", grid=(t // tr,),
        in_specs=[_row_spec(tr, d), _full_spec((1, d))],
        out_specs=_row_spec(tr, d),
        out_shape=jax.ShapeDtypeStruct((t, d), BF16),
        compiler_params=_params(32),
    )(x, g_pre)


def _rope_tables(pos_col, inv_freq_lanes):
    t = pos_col.shape[0]
    tr = min(ROW_BLK, t)

    def body(p_ref, f_ref, c_ref, sa_ref, sb_ref):
        ang = p_ref[...].astype(F32) * f_ref[...]
        lane = lax.broadcasted_iota(jnp.int32, ang.shape, 1)
        cos, sin = jnp.cos(ang), jnp.sin(ang)
        c_ref[...] = jnp.where(lane < QK_ROPE, cos, 0.0)
        sa_ref[...] = jnp.where(lane < QK_ROPE // 2, sin, 0.0)
        sb_ref[...] = jnp.where((lane >= QK_ROPE // 2) & (lane < QK_ROPE), sin, 0.0)

    tab = jax.ShapeDtypeStruct((t, 128), F32)
    return pl.pallas_call(
        body, name="rope_tables", grid=(t // tr,),
        in_specs=[_row_spec(tr, 1), _full_spec((1, 128))],
        out_specs=[_row_spec(tr, 128)] * 3,
        out_shape=[tab, tab, tab],
    )(pos_col, inv_freq_lanes)


def _rope_fwd(p, c, sa, sb):
    return p * c - pltpu.roll(p, 96, 1) * sa + pltpu.roll(p, 32, 1) * sb


def _rope_bwd(g, c, sa, sb):
    return g * c + pltpu.roll(g, 96, 1) * sa - pltpu.roll(g, 32, 1) * sb


def _rms(xv, g):
    r = lax.rsqrt(jnp.mean(xv * xv, axis=-1, keepdims=True) + EPS)
    return (xv * r) * g


def _rms_bwd(xv, g, dout):
    r = lax.rsqrt(jnp.mean(xv * xv, axis=-1, keepdims=True) + EPS)
    xn = xv * r
    dg = jnp.sum(dout * xn, axis=0, keepdims=True)
    dxn = dout * g
    dx = r * (dxn - xn * jnp.mean(dxn * xn, axis=-1, keepdims=True))
    return dx, dg


def _latent_norms(proj_lat, q_norm_g, kv_norm_g, c_tab, sa_tab, sb_tab):
    t = proj_lat.shape[0]
    tr = min(ROW_BLK, t)

    def body(cq_ref, ckv_ref, kr_ref, qg_ref, kg_ref, c_ref, sa_ref, sb_ref, cqn_ref, ckvn_ref, kpe_ref):
        cqn_ref[...] = _rms(cq_ref[...].astype(F32), qg_ref[...]).astype(BF16)
        ckvn_ref[...] = _rms(ckv_ref[...].astype(F32), kg_ref[...]).astype(BF16)
        kpe_ref[...] = _rope_fwd(kr_ref[...].astype(F32), c_ref[...], sa_ref[...], sb_ref[...]).astype(BF16)

    return pl.pallas_call(
        body, name="latent_norms", grid=(t // tr,),
        in_specs=[_row_spec(tr, Q_LORA, 0), _row_spec(tr, KV_LORA, 1), _row_spec(tr, 128, (Q_LORA + KV_LORA) // 128),
                  _full_spec((1, Q_LORA)), _full_spec((1, KV_LORA)),
                  _row_spec(tr, 128), _row_spec(tr, 128), _row_spec(tr, 128)],
        out_specs=[_row_spec(tr, Q_LORA), _row_spec(tr, KV_LORA), _row_spec(tr, 128)],
        out_shape=[jax.ShapeDtypeStruct((t, Q_LORA), BF16), jax.ShapeDtypeStruct((t, KV_LORA), BF16),
                   jax.ShapeDtypeStruct((t, 128), BF16)],
    )(proj_lat, proj_lat, proj_lat, q_norm_g, kv_norm_g, c_tab, sa_tab, sb_tab)


def _rope_q(q_raw, c_tab, sa_tab, sb_tab):
    t = q_raw.shape[0]
    tr = min(ROW_BLK, t)
    qs = MLA_SCALE * LOG2E

    def body(q_ref, c_ref, sa_ref, sb_ref, o_ref):
        c, sa, sb = c_ref[...], sa_ref[...], sb_ref[...]
        for h in range(MLA_HEADS):
            o_ref[:, h * HEAD_PAD:h * HEAD_PAD + 128] = (q_ref[:, h * HEAD_PAD:h * HEAD_PAD + 128] * qs).astype(BF16)
            pe = q_ref[:, h * HEAD_PAD + 128:(h + 1) * HEAD_PAD]
            o_ref[:, h * HEAD_PAD + 128:(h + 1) * HEAD_PAD] = (_rope_fwd(pe, c, sa, sb) * qs).astype(BF16)

    w = MLA_HEADS * HEAD_PAD
    return pl.pallas_call(
        body, name="rope_q", grid=(t // tr,),
        in_specs=[_row_spec(tr, w), _row_spec(tr, 128), _row_spec(tr, 128), _row_spec(tr, 128)],
        out_specs=_row_spec(tr, w),
        out_shape=jax.ShapeDtypeStruct((t, w), BF16),
        compiler_params=_params(48),
    )(q_raw, c_tab, sa_tab, sb_tab)


def _rope_q_bwd(dq, dkpe_heads, c_tab, sa_tab, sb_tab):
    t = dq.shape[0]
    tr = min(ROW_BLK, t)

    def body(dq_ref, dkp_ref, c_ref, sa_ref, sb_ref, o_ref, dkr_ref):
        c, sa, sb = c_ref[...], sa_ref[...], sb_ref[...]
        for h in range(MLA_HEADS):
            o_ref[:, h * HEAD_PAD:h * HEAD_PAD + 128] = (dq_ref[:, h * HEAD_PAD:h * HEAD_PAD + 128] * MLA_SCALE).astype(BF16)
            g = dq_ref[:, h * HEAD_PAD + 128:(h + 1) * HEAD_PAD] * MLA_SCALE
            o_ref[:, h * HEAD_PAD + 128:(h + 1) * HEAD_PAD] = _rope_bwd(g, c, sa, sb).astype(BF16)
        tot = dkp_ref[0]
        for h in range(1, MLA_HEADS):
            tot = tot + dkp_ref[h]
        dkr_ref[...] = _rope_bwd(tot, c, sa, sb)

    w = MLA_HEADS * HEAD_PAD
    return pl.pallas_call(
        body, name="rope_q_bwd", grid=(t // tr,),
        in_specs=[_row_spec(tr, w), pl.BlockSpec((MLA_HEADS, tr, 128), lambda i: (0, i, 0)),
                  _row_spec(tr, 128), _row_spec(tr, 128), _row_spec(tr, 128)],
        out_specs=[_row_spec(tr, w), _row_spec(tr, 128)],
        out_shape=[jax.ShapeDtypeStruct((t, w), BF16), jax.ShapeDtypeStruct((t, 128), F32)],
        compiler_params=_params(48),
    )(dq, dkpe_heads, c_tab, sa_tab, sb_tab)


def _latent_norms_bwd(proj_lat, q_norm_g, kv_norm_g, dcqn, dckvn, dkr):
    t = proj_lat.shape[0]
    tr = min(ROW_BLK, t)

    def body(cq_ref, ckv_ref, qg_ref, kg_ref, dcqn_ref, dckvn_ref, dkr_ref, dl_ref, dqg_ref, dkg_ref):
        dcq, dqg = _rms_bwd(cq_ref[...].astype(F32), qg_ref[...], dcqn_ref[...])
        dckv, dkg = _rms_bwd(ckv_ref[...].astype(F32), kg_ref[...], dckvn_ref[...])
        dl_ref[:, 0:Q_LORA] = dcq.astype(BF16)
        dl_ref[:, Q_LORA:Q_LORA + KV_LORA] = dckv.astype(BF16)
        dl_ref[:, Q_LORA + KV_LORA:LAT_W] = dkr_ref[...].astype(BF16)

        @pl.when(pl.program_id(0) == 0)
        def _():
            dqg_ref[...] = jnp.zeros_like(dqg_ref)
            dkg_ref[...] = jnp.zeros_like(dkg_ref)

        dqg_ref[...] += dqg
        dkg_ref[...] += dkg

    return pl.pallas_call(
        body, name="latent_norms_bwd", grid=(t // tr,),
        in_specs=[_row_spec(tr, Q_LORA, 0), _row_spec(tr, KV_LORA, 1), _full_spec((1, Q_LORA)), _full_spec((1, KV_LORA)),
                  _row_spec(tr, Q_LORA), _row_spec(tr, KV_LORA), _row_spec(tr, 128)],
        out_specs=[_row_spec(tr, LAT_W), _full_spec((1, Q_LORA)), _full_spec((1, KV_LORA))],
        out_shape=[jax.ShapeDtypeStruct((t, LAT_W), BF16), jax.ShapeDtypeStruct((1, Q_LORA), F32),
                   jax.ShapeDtypeStruct((1, KV_LORA), F32)],
    )(proj_lat, proj_lat, q_norm_g, kv_norm_g, dcqn, dckvn, dkr)


def _mem_norm(mem, g):
    m, d = mem.shape

    def body(x_ref, g_ref, o_ref):
        o_ref[...] = _rms(x_ref[...], g_ref[...]).astype(BF16)

    return pl.pallas_call(
        body, name="mem_norm", grid=(1,),
        in_specs=[_full_spec((m, d)), _full_spec((1, d))],
        out_specs=_full_spec((m, d)),
        out_shape=jax.ShapeDtypeStruct((m, d), BF16),
    )(mem, g)


def _mem_norm_bwd(mem, g, dmemn):
    m, d = mem.shape

    def body(x_ref, g_ref, d_ref, dg_ref):
        _, dg = _rms_bwd(x_ref[...], g_ref[...], d_ref[...])
        dg_ref[...] = dg

    return pl.pallas_call(
        body, name="mem_norm_bwd", grid=(1,),
        in_specs=[_full_spec((m, d)), _full_spec((1, d)), _full_spec((m, d))],
        out_specs=_full_spec((1, d)),
        out_shape=jax.ShapeDtypeStruct((1, d), F32),
    )(mem, g, dmemn)


def _merge(gates, p_a, p_b, p_m):
    t, d = p_a.shape
    tr = min(ROW_BLK, t)

    def body(ga_ref, gb_ref, gm_ref, pa_ref, pb_ref, pm_ref, o_ref):
        acc = ga_ref[...].astype(F32) * pa_ref[...].astype(F32)
        acc = acc + gb_ref[...].astype(F32) * pb_ref[...].astype(F32)
        acc = acc + gm_ref[...].astype(F32) * pm_ref[...].astype(F32)
        o_ref[...] = acc.astype(BF16)

    return pl.pallas_call(
        body, name="merge", grid=(t // tr,),
        in_specs=[_row_spec(tr, d, 0), _row_spec(tr, d, 1), _row_spec(tr, d, 2),
                  _row_spec(tr, d), _row_spec(tr, d), _row_spec(tr, d)],
        out_specs=_row_spec(tr, d),
        out_shape=jax.ShapeDtypeStruct((t, d), BF16),
        compiler_params=_params(48),
    )(gates, gates, gates, p_a, p_b, p_m)


def _merge_bwd(dm, gates, p_a, p_b, p_m):
    t, d = dm.shape
    tr = min(ROW_BLK, t)

    def body(dm_ref, g_ref, pa_ref, pb_ref, pm_ref, dpa_ref, dpb_ref, dpm_ref, dgl_ref, db_ref):
        dmv = dm_ref[...].astype(F32)

        @pl.when(pl.program_id(0) == 0)
        def _():
            db_ref[...] = jnp.zeros_like(db_ref)

        for n, (p_ref, dp_ref) in enumerate(((pa_ref, dpa_ref), (pb_ref, dpb_ref), (pm_ref, dpm_ref))):
            g = g_ref[:, n * d:(n + 1) * d].astype(F32)
            dp_ref[...] = (dmv * g).astype(BF16)
            dgl = dmv * p_ref[...].astype(F32) * (g * (1.0 - g))
            dgl_ref[:, n * d:(n + 1) * d] = dgl.astype(BF16)
            db_ref[:, n * d:(n + 1) * d] += jnp.sum(dgl, axis=0, keepdims=True)

    act = jax.ShapeDtypeStruct((t, d), BF16)
    return pl.pallas_call(
        body, name="merge_bwd", grid=(t // tr,),
        in_specs=[_row_spec(tr, d), _row_spec(tr, 3 * d), _row_spec(tr, d), _row_spec(tr, d), _row_spec(tr, d)],
        out_specs=[_row_spec(tr, d), _row_spec(tr, d), _row_spec(tr, d), _row_spec(tr, 3 * d), _full_spec((1, 3 * d))],
        out_shape=[act, act, act, jax.ShapeDtypeStruct((t, 3 * d), BF16), jax.ShapeDtypeStruct((1, 3 * d), F32)],
        compiler_params=_params(56),
    )(dm, gates, p_a, p_b, p_m)


def _post_loss(out, x, tgt, g_post):
    t, d = out.shape
    tr = min(ROW_BLK, t)

    def body(o_ref, x_ref, t_ref, g_ref, do_ref, dy_ref, loss_ref, dg_ref):
        ov = o_ref[...]
        g = g_ref[...]
        r = lax.rsqrt(jnp.mean(ov * ov, axis=-1, keepdims=True) + EPS)
        on = ov * r
        err = (x_ref[...] + on * g) - t_ref[...]
        dy = err * (1.0 / d)
        dy_ref[...] = dy
        don = dy * g
        do_ref[...] = (r * (don - on * jnp.mean(don * on, axis=-1, keepdims=True))).astype(BF16)

        @pl.when(pl.program_id(0) == 0)
        def _():
            loss_ref[...] = jnp.zeros_like(loss_ref)
            dg_ref[...] = jnp.zeros_like(dg_ref)

        loss_ref[...] += 0.5 * jnp.sum(jnp.mean(err * err, axis=-1, keepdims=True))
        dg_ref[...] += jnp.sum(dy * on, axis=0, keepdims=True)

    return pl.pallas_call(
        body, name="post_loss", grid=(t // tr,),
        in_specs=[_row_spec(tr, d), _row_spec(tr, d), _row_spec(tr, d), _full_spec((1, d))],
        out_specs=[_row_spec(tr, d), _row_spec(tr, d), _full_spec((8, 128)), _full_spec((1, d))],
        out_shape=[jax.ShapeDtypeStruct((t, d), BF16), jax.ShapeDtypeStruct((t, d), F32),
                   jax.ShapeDtypeStruct((8, 128), F32), jax.ShapeDtypeStruct((1, d), F32)],
        compiler_params=_params(56),
    )(out, x, tgt, g_post)


def _pre_norm_bwd(x, g_pre, dh, dy):
    t, d = x.shape
    tr = min(ROW_BLK, t)

    def body(x_ref, g_ref, dh_ref, dy_ref, dx_ref, dg_ref):
        dx, dg = _rms_bwd(x_ref[...], g_ref[...], dh_ref[...])
        dx_ref[...] = dx + dy_ref[...]

        @pl.when(pl.program_id(0) == 0)
        def _():
            dg_ref[...] = jnp.zeros_like(dg_ref)

        dg_ref[...] += dg

    return pl.pallas_call(
        body, name="pre_norm_bwd", grid=(t // tr,),
        in_specs=[_row_spec(tr, d), _full_spec((1, d)), _row_spec(tr, d), _row_spec(tr, d)],
        out_specs=[_row_spec(tr, d), _full_spec((1, d))],
        out_shape=[jax.ShapeDtypeStruct((t, d), F32), jax.ShapeDtypeStruct((1, d), F32)],
        compiler_params=_params(56),
    )(x, g_pre, dh, dy)


def _causal_mask(n):
    row = lax.broadcasted_iota(jnp.int32, (n, n), 0)
    col = lax.broadcasted_iota(jnp.int32, (n, n), 1)
    return row >= col


def _layernorm_stats(vg):
    mu = jnp.mean(vg, axis=-1, keepdims=True)
    cen = vg - mu
    rstd = lax.rsqrt(jnp.mean(cen * cen, axis=-1, keepdims=True) + EPS)
    return cen * rstd, rstd


def _gmlp_fwd(proj_big, ln_g, ln_b, w_s, b_exp):
    t = proj_big.shape[0]
    rows = min(GMLP_ROWS, t)
    d = D_MODEL

    def body(u_ref, v_ref, z_ref, lg_ref, lb_ref, ws_ref, be_ref, y_ref, vn_scr):
        vhat, _ = _layernorm_stats(_gelu(v_ref[...].astype(F32)))
        vn_scr[...] = (vhat * lg_ref[...] + lb_ref[...]).astype(BF16)
        mask = _causal_mask(CHUNK)
        for g in range(A_GROUPS):
            cols = slice(g * 128, (g + 1) * 128)
            wsm = jnp.where(mask, ws_ref[g], 0.0).astype(BF16)
            for c in range(rows // CHUNK):
                rws = slice(c * CHUNK, (c + 1) * CHUNK)
                sv = jnp.dot(wsm, vn_scr[rws, cols], preferred_element_type=F32) + be_ref[g]
                zs, _ = _silu_and_grad(z_ref[rws, cols].astype(F32))
                y_ref[rws, cols] = (_gelu(u_ref[rws, cols].astype(F32)) * sv * zs).astype(BF16)

    return pl.pallas_call(
        body, name="gmlp_fwd", grid=(t // rows,),
        in_specs=[_row_spec(rows, d, 0), _row_spec(rows, d, 1), _row_spec(rows, d, 2),
                  _full_spec((1, d)), _full_spec((1, d)), _full_spec((A_GROUPS, CHUNK, CHUNK)),
                  _full_spec((A_GROUPS, CHUNK, 128))],
        out_specs=_row_spec(rows, d),
        out_shape=jax.ShapeDtypeStruct((t, d), BF16),
        scratch_shapes=[pltpu.VMEM((rows, d), BF16)],
        compiler_params=_params(40),
    )(proj_big, proj_big, proj_big, ln_g, ln_b, w_s, b_exp)


def _gmlp_bwd(proj_big, dya, ln_g, ln_b, w_s, b_exp):
    t = proj_big.shape[0]
    rows = min(GMLP_ROWS, t)
    d = D_MODEL
    nt = (((1,), (1,)), ((), ()))
    tn = (((0,), (0,)), ((), ()))

    def body(u_ref, v_ref, z_ref, dy_ref, lg_ref, lb_ref, ws_ref, be_ref,
             dp_ref, dws_ref, dbs_ref, dlg_ref, dlb_ref, vn_scr, dvn_scr):
        @pl.when(pl.program_id(0) == 0)
        def _():
            dws_ref[...] = jnp.zeros_like(dws_ref)
            dbs_ref[...] = jnp.zeros_like(dbs_ref)
            dlg_ref[...] = jnp.zeros_like(dlg_ref)
            dlb_ref[...] = jnp.zeros_like(dlb_ref)

        vg, vgrad = _gelu_and_grad(v_ref[...].astype(F32))
        vhat, rstd = _layernorm_stats(vg)
        vn_scr[...] = (vhat * lg_ref[...] + lb_ref[...]).astype(BF16)
        mask = _causal_mask(CHUNK)
        for g in range(A_GROUPS):
            cols = slice(g * 128, (g + 1) * 128)
            wsm = jnp.where(mask, ws_ref[g], 0.0).astype(BF16)
            dws = jnp.zeros((CHUNK, CHUNK), F32)
            dbs = jnp.zeros((CHUNK, 1), F32)
            for c in range(rows // CHUNK):
                rws = slice(c * CHUNK, (c + 1) * CHUNK)
                vn = vn_scr[rws, cols]
                sv = jnp.dot(wsm, vn, preferred_element_type=F32) + be_ref[g]
                ug, ugrad = _gelu_and_grad(u_ref[rws, cols].astype(F32))
                zs, zgrad = _silu_and_grad(z_ref[rws, cols].astype(F32))
                dya = dy_ref[rws, cols].astype(F32)
                dga = dya * zs
                dp_ref[rws, 2 * d + g * 128:2 * d + (g + 1) * 128] = (dya * (ug * sv) * zgrad).astype(BF16)
                dp_ref[rws, cols] = (dga * sv * ugrad).astype(BF16)
                dsv = dga * ug
                dsv16 = dsv.astype(BF16)
                dws = dws + lax.dot_general(dsv16, vn, nt, preferred_element_type=F32)
                dbs = dbs + jnp.sum(dsv, axis=-1, keepdims=True)
                dvn_scr[rws, cols] = lax.dot_general(wsm, dsv16, tn, preferred_element_type=F32)
            dws_ref[g] += jnp.where(mask, dws, 0.0)
            dbs_ref[g] += dbs
        dvn = dvn_scr[...]
        dlg_ref[...] += jnp.sum(dvn * vhat, axis=0, keepdims=True)
        dlb_ref[...] += jnp.sum(dvn, axis=0, keepdims=True)
        dvh = dvn * lg_ref[...]
        dvg = rstd * (dvh - jnp.mean(dvh, axis=-1, keepdims=True) - vhat * jnp.mean(dvh * vhat, axis=-1, keepdims=True))
        dp_ref[:, d:2 * d] = (dvg * vgrad).astype(BF16)

    return pl.pallas_call(
        body, name="gmlp_bwd", grid=(t // rows,),
        in_specs=[_row_spec(rows, d, 0), _row_spec(rows, d, 1), _row_spec(rows, d, 2), _row_spec(rows, d),
                  _full_spec((1, d)), _full_spec((1, d)), _full_spec((A_GROUPS, CHUNK, CHUNK)),
                  _full_spec((A_GROUPS, CHUNK, 128))],
        out_specs=[_row_spec(rows, 3 * d), _full_spec((A_GROUPS, CHUNK, CHUNK)), _full_spec((A_GROUPS, CHUNK, 1)),
                   _full_spec((1, d)), _full_spec((1, d))],
        out_shape=[jax.ShapeDtypeStruct((t, 6 * d), BF16), jax.ShapeDtypeStruct((A_GROUPS, CHUNK, CHUNK), F32),
                   jax.ShapeDtypeStruct((A_GROUPS, CHUNK, 1), F32), jax.ShapeDtypeStruct((1, d), F32),
                   jax.ShapeDtypeStruct((1, d), F32)],
        scratch_shapes=[pltpu.VMEM((rows, d), BF16), pltpu.VMEM((rows, d), F32)],
        compiler_params=_params(48),
    )(proj_big, proj_big, proj_big, dya, ln_g, ln_b, w_s, b_exp)


NT_DIMS = (((1,), (1,)), ((), ()))
TN_DIMS = (((0,), (0,)), ((), ()))


def _mla_fwd(q, kv, kpe, proj_big):
    t = q.shape[0]
    blk = min(ATT_BLK, t)
    nq = t // blk
    zb_blk0 = (3 * D_MODEL) // 128

    nc = blk // 128

    sub = min(ATT_SUB, blk)

    def body(q_ref, kv_ref, kp_ref, zb_ref, o_ref, yb_ref, lse_ref, m_scr, acc_scr):
        i = pl.program_id(1)
        qv = q_ref[...]
        m_scr[...] = jnp.full_like(m_scr, NEG)
        acc_scr[...] = jnp.zeros_like(acc_scr)
        ones = jnp.ones((blk, 128), BF16)

        def step(j, masked):
            ks = pl.ds(pl.multiple_of(j * blk, blk), blk)
            kc = jnp.concatenate([kv_ref[ks, 0:128], kp_ref[ks, :]], axis=1)
            vext = jnp.concatenate([kv_ref[ks, 128:256], ones], axis=1)
            for r in range(blk // sub):
                rows = slice(r * sub, (r + 1) * sub)
                tt = lax.dot_general(qv[rows], kc, NT_DIMS, preferred_element_type=F32)
                if masked:
                    row = lax.broadcasted_iota(jnp.int32, (sub, blk), 0) + r * sub
                    col = lax.broadcasted_iota(jnp.int32, (sub, blk), 1)
                    tt = jnp.where(row >= col, tt, NEG)
                cm = tt[:, 0:128]
                for c in range(1, nc):
                    cm = jnp.maximum(cm, tt[:, c * 128:(c + 1) * 128])
                m_prev = m_scr[rows, :]
                m_new = jnp.maximum(m_prev, jnp.max(cm, axis=-1, keepdims=True))
                alpha = jnp.exp2(m_prev - m_new)
                m_scr[rows, :] = m_new
                p = jnp.concatenate([jnp.exp2(tt[:, c * 128:(c + 1) * 128] - m_new).astype(BF16) for c in range(nc)], axis=1)
                pv = jnp.dot(p, vext, preferred_element_type=F32)
                acc_scr[rows, :] = jnp.concatenate([alpha, alpha], axis=1) * acc_scr[rows, :] + pv

        def loop_body(j, carry):
            step(j, False)
            return carry

        lax.fori_loop(0, i, loop_body, 0)
        step(i, True)
        l = acc_scr[:, 128:256]
        o = acc_scr[:, 0:128] / l
        o_ref[...] = o.astype(BF16)
        zs, _ = _silu_and_grad(zb_ref[...].astype(F32))
        yb_ref[...] = (o * zs).astype(BF16)
        lse_ref[0] = m_scr[...] + jnp.log2(l)

    act = jax.ShapeDtypeStruct((t, D_MODEL), BF16)
    return pl.pallas_call(
        body, name="mla_fwd", grid=(MLA_HEADS, nq),
        in_specs=[pl.BlockSpec((blk, HEAD_PAD), lambda h, i: (i, h)),
                  pl.BlockSpec((t, HEAD_PAD), lambda h, i: (0, h)),
                  pl.BlockSpec((t, 128), lambda h, i: (0, 0)),
                  pl.BlockSpec((blk, 128), lambda h, i: (i, zb_blk0 + h))],
        out_specs=[pl.BlockSpec((blk, 128), lambda h, i: (i, h)),
                   pl.BlockSpec((blk, 128), lambda h, i: (i, h)),
                   pl.BlockSpec((1, blk, 128), lambda h, i: (h, i, 0))],
        out_shape=[act, act, jax.ShapeDtypeStruct((MLA_HEADS, t, 128), F32)],
        scratch_shapes=[pltpu.VMEM((blk, 128), F32), pltpu.VMEM((blk, HEAD_PAD), F32)],
        compiler_params=_params(56),
    )(q, kv, kpe, proj_big)


def _mla_gate_bwd(dyb, proj_big, o, dproj):
    t, d = dyb.shape
    tr = min(ROW_BLK, t)

    def body(dy_ref, zb_ref, o_ref, buf_ref, do_ref, dz_ref, dl_ref):
        del buf_ref
        dy = dy_ref[...].astype(F32)
        ov = o_ref[...].astype(F32)
        zs, zgrad = _silu_and_grad(zb_ref[...].astype(F32))
        do16 = (dy * zs).astype(BF16)
        do_ref[...] = do16
        dz_ref[...] = (dy * ov * zgrad).astype(BF16)
        prod = do16.astype(F32) * ov
        for h in range(MLA_HEADS):
            delta = jnp.sum(prod[:, h * 128:(h + 1) * 128], axis=-1, keepdims=True)
            dl_ref[h] = jnp.broadcast_to(delta, (tr, 128))

    act = jax.ShapeDtypeStruct((t, d), BF16)
    head_spec = pl.BlockSpec((MLA_HEADS, tr, 128), lambda i: (0, i, 0))
    return pl.pallas_call(
        body, name="mla_gate_bwd", grid=(t // tr,),
        in_specs=[_row_spec(tr, d), _row_spec(tr, d, 3), _row_spec(tr, d), HBM_SPEC],
        out_specs=[_row_spec(tr, d), _row_spec(tr, d, 3), head_spec],
        out_shape=[act, jax.ShapeDtypeStruct((t, 6 * d), BF16), jax.ShapeDtypeStruct((MLA_HEADS, t, 128), F32)],
        input_output_aliases={3: 1},
        compiler_params=_params(56),
    )(dyb, proj_big, o, dproj)


def _mla_bwd(q, kv, kpe, do, lse, delta, carry):
    t = q.shape[0]
    blk = min(ATT_BLK, t)
    n = t // blk
    nc = blk // 128
    pairs = [(j, i) for j in range(n) for i in range(j, n)]
    j_tab = jnp.asarray([p[0] for p in pairs], jnp.int32)
    i_tab = jnp.asarray([p[1] for p in pairs], jnp.int32)
    n_carry = len(carry.arrays)
    sub = min(ATT_SUB, blk)

    def body(j_ref, i_ref, q_ref, do_ref, lse_ref, dl_ref, kv_ref, kp_ref, *rest):
        c_ins, rest = rest[:n_carry], rest[n_carry:]
        dq_ref, dkv_ref, dkp_ref = rest[:3]
        c_outs, rest = rest[3:3 + n_carry], rest[3 + n_carry:]
        dk_scr, dv_scr = rest[:2]
        c_sems = rest[2:]
        head = pl.program_id(0)
        step = pl.program_id(1)
        j, i = j_ref[step], i_ref[step]

        @pl.when((head == 0) & (step == 0))
        def _():
            carry.start(c_ins, c_outs, c_sems)

        @pl.when(step == 0)
        def _():
            dq_ref[...] = jnp.zeros_like(dq_ref)

        @pl.when(i == j)
        def _():
            dk_scr[...] = jnp.zeros_like(dk_scr)
            dv_scr[...] = jnp.zeros_like(dv_scr)

        kc = jnp.concatenate([kv_ref[:, 0:128], kp_ref[...]], axis=1)
        vv = kv_ref[:, 128:256]
        off_diag = i > j
        for r in range(blk // sub):
            rows = slice(r * sub, (r + 1) * sub)
            qv, dov = q_ref[rows, :], do_ref[rows, :]
            tt = lax.dot_general(qv, kc, NT_DIMS, preferred_element_type=F32)
            row = lax.broadcasted_iota(jnp.int32, (sub, blk), 0) + r * sub
            col = lax.broadcasted_iota(jnp.int32, (sub, blk), 1)
            tt = jnp.where((row >= col) | off_diag, tt, NEG)
            dp = lax.dot_general(dov, vv, NT_DIMS, preferred_element_type=F32)
            lse_v, dl_v = lse_ref[0, rows, :], dl_ref[0, rows, :]
            ps, dss = [], []
            for c in range(nc):
                cols = slice(c * 128, (c + 1) * 128)
                p = jnp.exp2(tt[:, cols] - lse_v)
                ps.append(p.astype(BF16))
                dss.append((p * (dp[:, cols] - dl_v)).astype(BF16))
            p16 = jnp.concatenate(ps, axis=1)
            ds16 = jnp.concatenate(dss, axis=1)
            dv_scr[...] += lax.dot_general(p16, dov, TN_DIMS, preferred_element_type=F32)
            dk_scr[...] += lax.dot_general(ds16, qv, TN_DIMS, preferred_element_type=F32)
            qs = pl.ds(pl.multiple_of(i * blk + r * sub, sub), sub)
            dq_ref[qs, :] += jnp.dot(ds16, kc, preferred_element_type=F32)

        @pl.when(i == n - 1)
        def _():
            dkv_ref[:, 0:128] = (dk_scr[:, 0:128] * (1.0 / LOG2E)).astype(BF16)
            dkv_ref[:, 128:256] = dv_scr[...].astype(BF16)
            dkp_ref[0] = dk_scr[:, 128:256] * (1.0 / LOG2E)

        @pl.when((head == MLA_HEADS - 1) & (step == len(pairs) - 1))
        def _():
            carry.finish(c_ins, c_outs, c_sems)

    grid_spec = pltpu.PrefetchScalarGridSpec(
        num_scalar_prefetch=2, grid=(MLA_HEADS, len(pairs)),
        in_specs=[pl.BlockSpec((blk, HEAD_PAD), lambda h, s, jt, it: (it[s], h)),
                  pl.BlockSpec((blk, 128), lambda h, s, jt, it: (it[s], h)),
                  pl.BlockSpec((1, blk, 128), lambda h, s, jt, it: (h, it[s], 0)),
                  pl.BlockSpec((1, blk, 128), lambda h, s, jt, it: (h, it[s], 0)),
                  pl.BlockSpec((blk, HEAD_PAD), lambda h, s, jt, it: (jt[s], h)),
                  pl.BlockSpec((blk, 128), lambda h, s, jt, it: (jt[s], 0))] + [HBM_SPEC] * n_carry,
        out_specs=[pl.BlockSpec((t, HEAD_PAD), lambda h, s, jt, it: (0, h)),
                   pl.BlockSpec((blk, HEAD_PAD), lambda h, s, jt, it: (jt[s], h)),
                   pl.BlockSpec((1, blk, 128), lambda h, s, jt, it: (h, jt[s], 0))] + [HBM_SPEC] * n_carry,
        scratch_shapes=[pltpu.VMEM((blk, HEAD_PAD), F32), pltpu.VMEM((blk, 128), F32)] + carry.sem_shapes,
    )
    return pl.pallas_call(
        body, name="mla_bwd", grid_spec=grid_spec,
        out_shape=[jax.ShapeDtypeStruct((t, MLA_HEADS * HEAD_PAD), F32),
                   jax.ShapeDtypeStruct((t, 2 * D_MODEL), BF16),
                   jax.ShapeDtypeStruct((MLA_HEADS, t, 128), F32)] + carry.out_shapes,
        compiler_params=_params(58),
    )(j_tab, i_tab, q, do, lse, delta, kv, kpe, *carry.arrays)


def _mem_attn_probs(qv, k_ref):
    s = lax.dot_general(qv, k_ref[...], NT_DIMS, preferred_element_type=F32) * MEM_SCALE
    e = jnp.exp(s - jnp.max(s, axis=-1, keepdims=True))
    return e / jnp.sum(e, axis=-1, keepdims=True)


def _mem_fwd(proj_big, kv_m):
    t = proj_big.shape[0]
    tq = min(MEM_Q_BLK, t)
    hd = MEM_HEAD_DIM
    q0, z0 = (4 * D_MODEL) // hd, (5 * D_MODEL) // hd

    def body(q_ref, z_ref, k_ref, v_ref, y_ref):
        p = _mem_attn_probs(q_ref[...], k_ref)
        o = jnp.dot(p.astype(BF16), v_ref[...], preferred_element_type=F32)
        zs, _ = _silu_and_grad(z_ref[...].astype(F32))
        y_ref[...] = (o * zs).astype(BF16)

    return pl.pallas_call(
        body, name="mem_fwd", grid=(t // tq, MEM_HEADS),
        in_specs=[pl.BlockSpec((tq, hd), lambda i, h: (i, q0 + h)), pl.BlockSpec((tq, hd), lambda i, h: (i, z0 + h)),
                  pl.BlockSpec((kv_m.shape[0], hd), lambda i, h: (0, h)),
                  pl.BlockSpec((kv_m.shape[0], hd), lambda i, h: (0, MEM_HEADS + h))],
        out_specs=pl.BlockSpec((tq, hd), lambda i, h: (i, h)),
        out_shape=jax.ShapeDtypeStruct((t, D_MODEL), BF16),
    )(proj_big, proj_big, kv_m, kv_m)


def _mem_bwd(proj_big, kv_m, dym, dproj):
    t = proj_big.shape[0]
    tq = min(ROW_BLK, t)
    hd = MEM_HEAD_DIM
    d = D_MODEL
    mlen = kv_m.shape[0]

    def body(q_ref, z_ref, kv_ref, dy_ref, buf_ref, dqz_ref, dkv_ref):
        del buf_ref

        @pl.when(pl.program_id(0) == 0)
        def _():
            dkv_ref[...] = jnp.zeros_like(dkv_ref)

        for h in range(MEM_HEADS):
            cols = slice(h * hd, (h + 1) * hd)
            k_ref, v_ref = kv_ref.at[:, cols], kv_ref.at[:, d + h * hd:d + (h + 1) * hd]
            qv = q_ref[:, cols]
            p = _mem_attn_probs(qv, k_ref)
            p16 = p.astype(BF16)
            o = jnp.dot(p16, v_ref[...], preferred_element_type=F32)
            zs, zgrad = _silu_and_grad(z_ref[:, cols].astype(F32))
            dy = dy_ref[:, cols].astype(F32)
            dqz_ref[:, d + h * hd:d + (h + 1) * hd] = (dy * o * zgrad).astype(BF16)
            do16 = (dy * zs).astype(BF16)
            dkv_ref[:, d + h * hd:d + (h + 1) * hd] += lax.dot_general(p16, do16, TN_DIMS, preferred_element_type=F32)
            dp = lax.dot_general(do16, v_ref[...], NT_DIMS, preferred_element_type=F32)
            ds = (p * (dp - jnp.sum(dp * p, axis=-1, keepdims=True)) * MEM_SCALE).astype(BF16)
            dqz_ref[:, cols] = jnp.dot(ds, k_ref[...], preferred_element_type=F32).astype(BF16)
            dkv_ref[:, cols] += lax.dot_general(ds, qv, TN_DIMS, preferred_element_type=F32)

    return pl.pallas_call(
        body, name="mem_bwd", grid=(t // tq,),
        in_specs=[_row_spec(tq, d, 4), _row_spec(tq, d, 5), _full_spec((mlen, 2 * d)), _row_spec(tq, d), HBM_SPEC],
        out_specs=[_row_spec(tq, 2 * d, 2), _full_spec((mlen, 2 * d))],
        out_shape=[jax.ShapeDtypeStruct((t, 6 * d), BF16), jax.ShapeDtypeStruct((mlen, 2 * d), F32)],
        input_output_aliases={4: 0},
        compiler_params=_params(40),
    )(proj_big, proj_big, kv_m, dym, dproj)


HBM_SPEC = pl.BlockSpec(memory_space=pl.ANY)
N_PEERS = N_DEV - 1


def _dev_index(px, py, pc):
    return 4 * px + 2 * py + pc


class _Gather:
    def __init__(self, arrays):
        self.arrays = list(arrays)
        n = len(self.arrays)
        self.out_shapes = [jax.ShapeDtypeStruct((N_DEV,) + a.shape, a.dtype) for a in self.arrays]
        self.sem_shapes = [pltpu.SemaphoreType.DMA((n * N_PEERS,)), pltpu.SemaphoreType.DMA((n * N_PEERS,)),
                           pltpu.SemaphoreType.DMA((n,))]

    def _parts(self, ins, outs, sems):
        n = len(self.arrays)
        send_sems, recv_sems, local_sems = sems
        x, y, c = lax.axis_index("x"), lax.axis_index("y"), lax.axis_index("c")
        me, sibling = (x, y, c), (x, y, 1 - c)
        chips = [(1 - x, y), (x, 1 - y), (1 - x, 1 - y)]

        def copy(a, k, block, to, src=None):
            dst = outs[a].at[_dev_index(*block)]
            return pltpu.make_async_remote_copy(
                src_ref=dst if src is None else src, dst_ref=dst,
                send_sem=send_sems.at[a * N_PEERS + k], recv_sem=recv_sems.at[a * N_PEERS + k],
                device_id=to, device_id_type=MESH)

        mine = [pltpu.make_async_copy(ins[a], outs[a].at[_dev_index(*me)], local_sems.at[a]) for a in range(n)]
        first = []
        for a in range(n):
            first.append(copy(a, 0, me, sibling, src=ins[a]))
            first += [copy(a, 1 + j, me, (*chip, c), src=ins[a]) for j, chip in enumerate(chips)]
        return n, c, me, sibling, chips, copy, mine, first

    def start(self, ins, outs, sems):
        _, _, _, _, _, _, mine, first = self._parts(ins, outs, sems)
        for cp in mine + first:
            cp.start()

    def finish(self, ins, outs, sems):
        n, c, me, sibling, chips, copy, mine, first = self._parts(ins, outs, sems)
        passed = []
        for j, chip in enumerate(chips):
            for a in range(n):
                copy(a, 1 + j, (*chip, c), me).wait_recv()
                fwd = copy(a, 4 + j, (*chip, c), sibling)
                fwd.start()
                passed.append(fwd)
        for a in range(n):
            copy(a, 0, sibling, me).wait_recv()
            for j, chip in enumerate(chips):
                copy(a, 4 + j, (*chip, 1 - c), me).wait_recv()
        for cp in first + passed:
            cp.wait_send()
        for cp in mine:
            cp.wait()


class _AllToAll:
    def __init__(self, arrays):
        self.arrays = list(arrays)
        n = len(self.arrays)
        self.out_shapes = [jax.ShapeDtypeStruct(a.shape, a.dtype) for a in self.arrays]
        self.sem_shapes = [pltpu.SemaphoreType.DMA((n * N_PEERS,)), pltpu.SemaphoreType.DMA((n * N_PEERS,)),
                           pltpu.SemaphoreType.DMA((n,))]

    def _parts(self, ins, outs, sems):
        n = len(self.arrays)
        send_sems, recv_sems, local_sems = sems
        x, y, c = lax.axis_index("x"), lax.axis_index("y"), lax.axis_index("c")
        my_idx = _dev_index(x, y, c)
        peers = []
        for k in range(1, N_DEV):
            dx, dy, dc = (k >> 2) & 1, (k >> 1) & 1, k & 1
            peers.append((1 - x if dx else x, 1 - y if dy else y, 1 - c if dc else c))

        def copy(a, k, peer):
            return pltpu.make_async_remote_copy(
                src_ref=ins[a].at[_dev_index(*peer)], dst_ref=outs[a].at[my_idx],
                send_sem=send_sems.at[a * N_PEERS + k], recv_sem=recv_sems.at[a * N_PEERS + k],
                device_id=peer, device_id_type=MESH)

        def landed(a, k, peer):
            slot = outs[a].at[_dev_index(*peer)]
            return pltpu.make_async_remote_copy(
                src_ref=slot, dst_ref=slot,
                send_sem=send_sems.at[a * N_PEERS + k], recv_sem=recv_sems.at[a * N_PEERS + k],
                device_id=peer, device_id_type=MESH)

        mine = [pltpu.make_async_copy(ins[a].at[my_idx], outs[a].at[my_idx], local_sems.at[a]) for a in range(n)]
        sends = [copy(a, k, peer) for a in range(n) for k, peer in enumerate(peers)]
        return n, peers, landed, mine, sends

    def start(self, ins, outs, sems):
        _, _, _, mine, sends = self._parts(ins, outs, sems)
        for cp in mine + sends:
            cp.start()

    def finish(self, ins, outs, sems):
        n, peers, landed, mine, sends = self._parts(ins, outs, sems)
        for a in range(n):
            for k, peer in enumerate(peers):
                landed(a, k, peer).wait_recv()
        for cp in sends:
            cp.wait_send()
        for cp in mine:
            cp.wait()


def _exchange(plan, name):
    n = len(plan.arrays)

    def body(*refs):
        ins, outs, sems = refs[:n], refs[n:2 * n], refs[2 * n:]
        plan.start(ins, outs, sems)
        plan.finish(ins, outs, sems)

    return pl.pallas_call(
        body, name=name, in_specs=[HBM_SPEC] * n, out_specs=[HBM_SPEC] * n,
        out_shape=plan.out_shapes, scratch_shapes=plan.sem_shapes,
    )(*plan.arrays)


def _adamw(w, g, m, v):
    m = ADAM_B1 * m + (1.0 - ADAM_B1) * g
    v = ADAM_B2 * v + (1.0 - ADAM_B2) * jnp.square(g)
    m_hat = m / (1.0 - ADAM_B1 ** ADAM_STEP)
    v_hat = v / (1.0 - ADAM_B2 ** ADAM_STEP)
    delta = -ADAM_LR * (m_hat / (jnp.sqrt(v_hat) + ADAM_EPS) + ADAM_WD * w)
    return delta, m, v


def _adam_sharded(parts, w, m, v, name):
    shape = w.shape
    cols = shape[-1]
    rows = int(np.prod(shape[:-1]))
    tr = min(128, rows)
    assert rows % tr == 0

    def body(p_ref, w_ref, m_ref, v_ref, g_ref, d_ref, nm_ref, nv_ref):
        g = p_ref[0].astype(F32)
        for e in range(1, N_DEV):
            g = g + p_ref[e].astype(F32)
        g_ref[...] = g
        d_ref[...], nm_ref[...], nv_ref[...] = _adamw(w_ref[...], g, m_ref[...], v_ref[...])

    spec = pl.BlockSpec((tr, cols), lambda i: (i, 0))
    flat = jax.ShapeDtypeStruct((rows, cols), F32)
    outs = pl.pallas_call(
        body, name=name, grid=(rows // tr,),
        in_specs=[pl.BlockSpec((N_DEV, tr, cols), lambda i: (0, i, 0)), spec, spec, spec],
        out_specs=[spec] * 4, out_shape=[flat] * 4,
        compiler_params=_params(40),
    )(parts.reshape(N_DEV, rows, cols), w.reshape(rows, cols), m.reshape(rows, cols), v.reshape(rows, cols))
    return [o.reshape(shape) for o in outs]


def _adam_replicated(parts, w, m, v):
    r = w.shape[0]

    def body(p_ref, w_ref, m_ref, v_ref, g_ref, d_ref, nm_ref, nv_ref):
        g = p_ref[0]
        for e in range(1, N_DEV):
            g = g + p_ref[e]
        g_ref[...] = g
        d_ref[...], nm_ref[...], nv_ref[...] = _adamw(w_ref[...], g, m_ref[...], v_ref[...])

    spec = _full_spec((r, 128))
    flat = jax.ShapeDtypeStruct((r, 128), F32)
    return pl.pallas_call(
        body, name="adam_replicated", grid=(1,),
        in_specs=[_full_spec((N_DEV, r, 128)), spec, spec, spec],
        out_specs=[spec] * 4, out_shape=[flat] * 4,
        compiler_params=_params(48),
    )(parts, w, m, v)


def _pack(arrays):
    parts = []
    for a in arrays:
        f = a.reshape(-1, 128)
        pad = -f.shape[0] % 8
        parts.append(jnp.pad(f, ((0, pad), (0, 0))) if pad else f)
    return jnp.concatenate(parts, axis=0)


def _unpack(packed, shapes):
    out, row = [], 0
    for shape in shapes:
        r = int(np.prod(shape)) // 128
        out.append(packed[row:row + r].reshape(shape))
        row += r + (-r % 8)
    return out


SHARDED = ("w_in", "w_uq", "w_ukv", "w_mem_kv", "w_gate", "w_branch", "w_out")
REPLICATED = ("g_pre", "a_ln_g", "a_ln_b", "a_w_s", "a_b_s", "q_norm_g", "kv_norm_g", "mem_norm_g", "b_gate", "g_post")
WEIGHT_ORDER = ("g_pre", "w_in", "a_ln_g", "a_ln_b", "a_w_s", "a_b_s", "q_norm_g", "w_uq", "kv_norm_g", "w_ukv",
                "mem_norm_g", "w_mem_kv", "w_gate", "b_gate", "w_branch", "w_out", "g_post")


def _unshard_cols(g):
    return g.transpose(1, 0, 2).reshape(g.shape[1], N_DEV * g.shape[2])


def _shard_cols(full):
    rows, n = full.shape
    return full.reshape(rows, N_DEV, n // N_DEV).transpose(1, 0, 2).astype(BF16)


def kernel(x, mem, positions, g_pre, w_in, a_ln_g, a_ln_b, a_w_s, a_b_s, q_norm_g, w_uq, kv_norm_g, w_ukv, mem_norm_g, w_mem_kv, w_gate, b_gate, w_branch, w_out, g_post, loss_target, m_g_pre, m_w_in, m_a_ln_g, m_a_ln_b, m_a_w_s, m_a_b_s, m_q_norm_g, m_w_uq, m_kv_norm_g, m_w_ukv, m_mem_norm_g, m_w_mem_kv, m_w_gate, m_b_gate, m_w_branch, m_w_out, m_g_post, v_g_pre, v_w_in, v_a_ln_g, v_a_ln_b, v_a_w_s, v_a_b_s, v_q_norm_g, v_w_uq, v_kv_norm_g, v_w_ukv, v_mem_norm_g, v_w_mem_kv, v_w_gate, v_b_gate, v_w_branch, v_w_out, v_g_post):
    weights = dict(g_pre=g_pre, w_in=w_in, a_ln_g=a_ln_g, a_ln_b=a_ln_b, a_w_s=a_w_s, a_b_s=a_b_s, q_norm_g=q_norm_g,
                   w_uq=w_uq, kv_norm_g=kv_norm_g, w_ukv=w_ukv, mem_norm_g=mem_norm_g, w_mem_kv=w_mem_kv,
                   w_gate=w_gate, b_gate=b_gate, w_branch=w_branch, w_out=w_out, g_post=g_post)
    mom1 = dict(g_pre=m_g_pre, w_in=m_w_in, a_ln_g=m_a_ln_g, a_ln_b=m_a_ln_b, a_w_s=m_a_w_s, a_b_s=m_a_b_s,
                q_norm_g=m_q_norm_g, w_uq=m_w_uq, kv_norm_g=m_kv_norm_g, w_ukv=m_w_ukv, mem_norm_g=m_mem_norm_g,
                w_mem_kv=m_w_mem_kv, w_gate=m_w_gate, b_gate=m_b_gate, w_branch=m_w_branch, w_out=m_w_out, g_post=m_g_post)
    mom2 = dict(g_pre=v_g_pre, w_in=v_w_in, a_ln_g=v_a_ln_g, a_ln_b=v_a_ln_b, a_w_s=v_a_w_s, a_b_s=v_a_b_s,
                q_norm_g=v_q_norm_g, w_uq=v_w_uq, kv_norm_g=v_kv_norm_g, w_ukv=v_w_ukv, mem_norm_g=v_mem_norm_g,
                w_mem_kv=v_w_mem_kv, w_gate=v_w_gate, b_gate=v_b_gate, w_branch=v_w_branch, w_out=v_w_out, g_post=v_g_post)
    d = D_MODEL
    t = x.shape[1]
    xs, tgt, mems = x[0], loss_target[0], mem[0]
    pos_col = positions.reshape(t, 1)

    shard16 = {n: weights[n][0].astype(BF16) for n in SHARDED}
    g_in, g_uq, g_ukv = _exchange(_Gather([shard16[n] for n in ("w_in", "w_uq", "w_ukv")]), "gather_weights_in")
    w_in_full = _unshard_cols(g_in)
    lat0, lat1 = 3 * d, 3 * d + Q_LORA + KV_LORA + QK_ROPE
    w_big = jnp.concatenate([w_in_full[:, :lat0], w_in_full[:, lat1:]], axis=1)
    w_lat = jnp.concatenate([w_in_full[:, lat0:lat1], jnp.zeros((d, LAT_W - (lat1 - lat0)), BF16)], axis=1)
    w_uq_p = jnp.pad(_unshard_cols(g_uq).reshape(Q_LORA, MLA_HEADS, QK_DIM),
                     ((0, 0), (0, 0), (0, HEAD_PAD - QK_DIM))).reshape(Q_LORA, MLA_HEADS * HEAD_PAD)
    w_ukv_f = _unshard_cols(g_ukv)

    inv_freq = 1.0 / (ROPE_THETA ** (jnp.arange(0, QK_ROPE, 2, dtype=F32) / QK_ROPE))
    inv_freq_lanes = jnp.concatenate([inv_freq, inv_freq, jnp.zeros((128 - QK_ROPE,), F32)]).reshape(1, 128)
    ws = a_w_s[0]
    b_exp = jnp.broadcast_to(a_b_s[0][:, :, None], (A_GROUPS, CHUNK, 128))

    h = _pre_norm(xs, g_pre)
    proj_big, g_mem, g_gate, g_br, g_out = _mm(
        h, w_big, name="proj_big", tm=1024, tn=1024, tk=2048,
        carry=_Gather([shard16[n] for n in ("w_mem_kv", "w_gate", "w_branch", "w_out")]))
    w_mem_f = _unshard_cols(g_mem)
    w_gate_f = _unshard_cols(g_gate)
    w_br_f = g_br.transpose(1, 0, 2, 3).reshape(3, d, d)
    w_out_f = g_out.reshape(d, d)
    proj_lat = _mm(h, w_lat, name="proj_lat", tm=1024, tn=LAT_W, tk=2048)
    gates = _mm(h, w_gate_f, name="gates", tm=1024, tn=1024, tk=2048, bias=b_gate, act="sigmoid")
    c_tab, sa_tab, sb_tab = _rope_tables(pos_col, inv_freq_lanes)
    cqn, ckvn, kpe = _latent_norms(proj_lat, q_norm_g, kv_norm_g, c_tab, sa_tab, sb_tab)
    q_raw = _mm(cqn, w_uq_p, name="q_up", tm=1024, tn=1024, tk=512, out_dtype=F32)
    q = _rope_q(q_raw, c_tab, sa_tab, sb_tab)
    kv = _mm(ckvn, w_ukv_f, name="kv_up", tm=1024, tn=1024, tk=512)
    o_b, y_b, lse = _mla_fwd(q, kv, kpe, proj_big)
    memn = _mem_norm(mems, mem_norm_g)
    kv_m = _mm(memn, w_mem_f, name="mem_kv", tm=256, tn=1024, tk=2048)
    y_m = _mem_fwd(proj_big, kv_m)
    y_a = _gmlp_fwd(proj_big, a_ln_g, a_ln_b, ws, b_exp)
    ys = (y_a, y_b, y_m)
    ps = [_mm(ys[n], w_br_f[n], name=f"branch{n}", tm=1024, tn=1024, tk=2048) for n in range(3)]
    merged = _merge(gates, *ps)
    out = _mm(merged, w_out_f, name="out_proj", tm=1024, tn=1024, tk=2048, out_dtype=F32)
    d_out, dy, loss_blk, dg_post = _post_loss(out, xs, tgt, g_post)

    dmerged = _mm(d_out, w_out_f, name="d_merged", tb=True, tm=1024, tn=1024, tk=2048)
    dw_out = _mm(merged, d_out, name="dw_out", ta=True, tm=1024, tn=1024, tk=2048)
    dp_a, dp_b, dp_m, dgl, db_gate = _merge_bwd(dmerged, gates, *ps)
    dps = (dp_a, dp_b, dp_m)
    dys = [_mm(dps[n], w_br_f[n], name=f"d_y{n}", tb=True, tm=1024, tn=1024, tk=2048) for n in range(3)]
    dw_br = [_mm(ys[n], dps[n], name=f"dw_branch{n}", ta=True, tm=1024, tn=1024, tk=2048) for n in range(3)]
    dw_gate = _mm(h, dgl, name="dw_gate", ta=True, tm=1024, tn=1024, tk=2048)

    dproj_big, dws, dbs, dlng, dlnb = _gmlp_bwd(proj_big, dys[0], a_ln_g, a_ln_b, ws, b_exp)

    recv = {}
    send = [_shard_cols(dw_gate), jnp.stack(dw_br).reshape(3, N_DEV, d // N_DEV, d).transpose(1, 0, 2, 3),
            dw_out.reshape(N_DEV, d // N_DEV, d)]
    do_b, dproj_big, delta = _mla_gate_bwd(dys[1], proj_big, o_b, dproj_big)
    dq, dkv, dkpe_h, recv["w_gate"], recv["w_branch"], recv["w_out"] = _mla_bwd(
        q, kv, kpe, do_b, lse, delta, _AllToAll(send))
    dq_raw, dkr = _rope_q_bwd(dq, dkpe_h, c_tab, sa_tab, sb_tab)
    dcqn = _mm(dq_raw, w_uq_p, name="d_cq", tb=True, tm=1024, tn=Q_LORA, tk=2048, out_dtype=F32)
    dw_uq_p = _mm(cqn, dq_raw, name="dw_uq", ta=True, tm=Q_LORA, tn=1024, tk=2048)
    dckvn = _mm(dkv, w_ukv_f, name="d_ckv", tb=True, tm=1024, tn=KV_LORA, tk=2048, out_dtype=F32)
    dw_ukv = _mm(ckvn, dkv, name="dw_ukv", ta=True, tm=KV_LORA, tn=1024, tk=2048)
    dproj_lat, dqg, dkg = _latent_norms_bwd(proj_lat, q_norm_g, kv_norm_g, dcqn, dckvn, dkr)

    dproj_big, dkv_m32 = _mem_bwd(proj_big, kv_m, dys[2], dproj_big)
    dkv_m = dkv_m32.astype(BF16)
    dw_mem = _mm(memn, dkv_m, name="dw_mem", ta=True, tm=1024, tn=1024, tk=256)
    dmemn = _mm(dkv_m, w_mem_f, name="d_memn", tb=True, tm=256, tn=1024, tk=2048, out_dtype=F32)
    dg_mem = _mem_norm_bwd(mems, mem_norm_g, dmemn)

    dw_uq_full = dw_uq_p.reshape(Q_LORA, MLA_HEADS, HEAD_PAD)[:, :, :QK_DIM].reshape(Q_LORA, MLA_HEADS * QK_DIM)
    dh, recv["w_uq"], recv["w_ukv"], recv["w_mem_kv"] = _mm(
        dgl, w_gate_f, name="dh_gate", tb=True, tm=1024, tn=1024, tk=2048, out_dtype=F32,
        carry=_AllToAll([_shard_cols(dw_uq_full), _shard_cols(dw_ukv), _shard_cols(dw_mem)]))
    dw_big = _mm(h, dproj_big, name="dw_big", ta=True, tm=1024, tn=1024, tk=2048)
    dw_lat = _mm(h, dproj_lat, name="dw_lat", ta=True, tm=1024, tn=LAT_W, tk=2048)
    dw_in_full = jnp.concatenate([dw_big[:, :lat0], dw_lat[:, :lat1 - lat0], dw_big[:, lat0:]], axis=1)
    dh = _mm(dproj_lat, w_lat, name="dh_lat", tb=True, tm=1024, tn=1024, tk=LAT_W, out_dtype=F32, add=dh)
    dh, recv["w_in"] = _mm(dproj_big, w_big, name="dh_big", tb=True, tm=1024, tn=1024, tk=2048, out_dtype=F32, add=dh,
                           carry=_AllToAll([_shard_cols(dw_in_full)]))
    grad_x, dg_pre = _pre_norm_bwd(xs, g_pre, dh, dy)

    results = {}
    for n in SHARDED:
        results[n] = [r[None] for r in _adam_sharded(recv[n], weights[n][0], mom1[n][0], mom2[n][0], "adam_" + n)]

    small = dict(g_pre=dg_pre, a_ln_g=dlng, a_ln_b=dlnb, a_w_s=dws, a_b_s=dbs, q_norm_g=dqg, kv_norm_g=dkg,
                 mem_norm_g=dg_mem, b_gate=db_gate, g_post=dg_post)
    (parts,) = _exchange(_Gather([_pack([small[n] for n in REPLICATED])]), "gather_small_grads")
    packed = _adam_replicated(parts, _pack([weights[n] for n in REPLICATED]), _pack([mom1[n] for n in REPLICATED]),
                              _pack([mom2[n] for n in REPLICATED]))
    shapes = [weights[n].shape for n in REPLICATED]
    unpacked = [_unpack(p, shapes) for p in packed]
    for i, n in enumerate(REPLICATED):
        results[n] = [u[i] for u in unpacked]

    loss = lax.psum(loss_blk[0, 0], AXES)
    outs = [loss, grad_x[None]]
    for kind in range(4):
        outs += [results[n][kind] for n in WEIGHT_ORDER]
    return tuple(outs)
```

```python
import functools
import math

import jax
import jax.numpy as jnp
import numpy as np
from jax import lax
from jax.experimental import pallas as pl
from jax.experimental.pallas import tpu as pltpu

F32 = jnp.float32
BF16 = jnp.bfloat16
MESH = pl.DeviceIdType.MESH
AXES = ("x", "y", "c")
N_DEV = 8

D_MODEL = 2048
EPS = 1e-6
CHUNK = 128
A_GROUPS = 16
MLA_HEADS = 16
QK_NOPE = 128
QK_ROPE = 64
QK_DIM = QK_NOPE + QK_ROPE
HEAD_PAD = 256
Q_LORA = 512
KV_LORA = 512
MEM_HEADS = 4
MEM_HEAD_DIM = 512
ROPE_THETA = 10000.0
MLA_SCALE = QK_DIM ** -0.5
MEM_SCALE = MEM_HEAD_DIM ** -0.5
NEG = -1e30
LOG2E = 1.4426950408889634

ADAM_LR = 0.001
ADAM_B1 = 0.9
ADAM_B2 = 0.999
ADAM_EPS = 1e-08
ADAM_WD = 0.01
ADAM_STEP = 10

BIG_W = 6 * D_MODEL
LAT_W = Q_LORA + KV_LORA + 128

VMEM_MIB = 1024 * 1024

ROW_BLK = 256
ATT_BLK = 1024
ATT_SUB = 256
GMLP_ROWS = 256
MEM_Q_BLK = 512


def _params(vmem_mib, **kw):
    return pltpu.CompilerParams(vmem_limit_bytes=int(vmem_mib * VMEM_MIB), **kw)


def _gelu(x):
    k = math.sqrt(2.0 / math.pi)
    t = jnp.tanh(k * (x + 0.044715 * (x * x * x)))
    return 0.5 * x * (1.0 + t)


def _gelu_and_grad(x):
    k = math.sqrt(2.0 / math.pi)
    x2 = x * x
    t = jnp.tanh(k * (x + 0.044715 * (x2 * x)))
    val = 0.5 * x * (1.0 + t)
    grad = 0.5 * (1.0 + t) + 0.5 * x * (1.0 - t * t) * (k * (1.0 + 3.0 * 0.044715 * x2))
    return val, grad


def _silu_and_grad(z):
    s = jax.nn.sigmoid(z)
    return z * s, s * (1.0 + z * (1.0 - s))


def _mm(a, b, *, name, tm, tn, tk, ta=False, tb=False, out_dtype=BF16, bias=None, act=None, add=None, carry=None):
    m = a.shape[1] if ta else a.shape[0]
    k = a.shape[0] if ta else a.shape[1]
    n = b.shape[0] if tb else b.shape[1]
    assert k == (b.shape[1] if tb else b.shape[0])
    tm, tn, tk = min(tm, m), min(tn, n), min(tk, k)
    assert m % tm == 0 and n % tn == 0 and k % tk == 0, (name, m, n, k, tm, tn, tk)
    nk = k // tk
    a_spec = pl.BlockSpec((tk, tm), lambda i, j, kk: (kk, i)) if ta else pl.BlockSpec((tm, tk), lambda i, j, kk: (i, kk))
    b_spec = pl.BlockSpec((tn, tk), lambda i, j, kk: (j, kk)) if tb else pl.BlockSpec((tk, tn), lambda i, j, kk: (kk, j))
    dn = (((0 if ta else 1,), (1 if tb else 0,)), ((), ()))
    operands, in_specs = [a, b], [a_spec, b_spec]
    if bias is not None:
        operands.append(bias)
        in_specs.append(pl.BlockSpec((1, tn), lambda i, j, kk: (0, j)))
    if add is not None:
        operands.append(add)
        in_specs.append(pl.BlockSpec((tm, tn), lambda i, j, kk: (i, j)))

    n_in = len(operands)
    n_carry = len(carry.arrays) if carry is not None else 0
    n_acc = 1 if nk > 1 else 0
    grid = (m // tm, n // tn, nk)

    def body(*refs):
        a_ref, b_ref = refs[0], refs[1]
        pos = 2
        bias_ref = add_ref = None
        if bias is not None:
            bias_ref = refs[pos]
            pos += 1
        if add is not None:
            add_ref = refs[pos]
            pos += 1
        o_ref = refs[n_in + n_carry]
        pos = n_in + n_carry
        if carry is not None:
            c_ins = refs[n_in:n_in + n_carry]
            c_outs = refs[n_in + n_carry + 1:n_in + 2 * n_carry + 1]
            c_sems = refs[n_in + 2 * n_carry + 1 + n_acc:]
            ids = [pl.program_id(ax) for ax in range(3)]

            @pl.when((ids[0] == 0) & (ids[1] == 0) & (ids[2] == 0))
            def _():
                carry.start(c_ins, c_outs, c_sems)

        part = lax.dot_general(a_ref[...], b_ref[...], dn, preferred_element_type=F32)

        def finish(acc):
            if bias_ref is not None:
                acc = acc + bias_ref[...]
            if act == "sigmoid":
                acc = jax.nn.sigmoid(acc)
            if add_ref is not None:
                acc = acc + add_ref[...]
            o_ref[...] = acc.astype(o_ref.dtype)

        if nk == 1:
            finish(part)
        else:
            acc_ref = refs[n_in + 2 * n_carry + 1]
            kk = pl.program_id(2)

            @pl.when(kk == 0)
            def _():
                acc_ref[...] = part

            @pl.when(kk > 0)
            def _():
                acc_ref[...] += part

            @pl.when(kk == nk - 1)
            def _():
                finish(acc_ref[...])

        if carry is not None:
            @pl.when((ids[0] == grid[0] - 1) & (ids[1] == grid[1] - 1) & (ids[2] == grid[2] - 1))
            def _():
                carry.finish(c_ins, c_outs, c_sems)

    osz = jnp.dtype(out_dtype).itemsize
    est = 2 * 2 * (tm * tk + tk * tn) + 2 * osz * tm * tn + 8 * tm * tn + (2 * 4 * tm * tn if add is not None else 0)
    main_spec = pl.BlockSpec((tm, tn), lambda i, j, kk: (i, j))
    main_shape = jax.ShapeDtypeStruct((m, n), out_dtype)
    scratch = [pltpu.VMEM((tm, tn), F32)] if nk > 1 else []
    if carry is None:
        return pl.pallas_call(
            body, name=name, grid=grid, in_specs=in_specs, out_specs=main_spec, out_shape=main_shape,
            scratch_shapes=scratch, compiler_params=_params(min(56, est / VMEM_MIB + 12)),
        )(*operands)
    return pl.pallas_call(
        body, name=name, grid=grid,
        in_specs=in_specs + [HBM_SPEC] * n_carry,
        out_specs=[main_spec] + [HBM_SPEC] * n_carry,
        out_shape=[main_shape] + carry.out_shapes,
        scratch_shapes=scratch + carry.sem_shapes,
        compiler_params=_params(min(56, est / VMEM_MIB + 12)),
    )(*operands, *carry.arrays)


def _row_spec(tr, cols, col_blk=0):
    return pl.BlockSpec((tr, cols), lambda i: (i, col_blk))


def _full_spec(shape):
    nd = len(shape)
    return pl.BlockSpec(shape, lambda i: (0,) * nd)


def _pre_norm(x, g_pre):
    t, d = x.shape
    tr = min(ROW_BLK, t)

    def body(x_ref, g_ref, h_ref):
        xv = x_ref[...]
        r = lax.rsqrt(jnp.mean(xv * xv, axis=-1, keepdims=True) + EPS)
        h_ref[...] = ((xv * r) * g_ref[...]).astype(BF16)

    return pl.pallas_call(
        body, name="pre_norm", grid=(t // tr,),
        in_specs=[_row_spec(tr, d), _full_spec((1, d))],
        out_specs=_row_spec(tr, d),
        out_shape=jax.ShapeDtypeStruct((t, d), BF16),
        compiler_params=_params(32),
    )(x, g_pre)


def _rope_tables(pos_col, inv_freq_lanes):
    t = pos_col.shape[0]
    tr = min(ROW_BLK, t)

    def body(p_ref, f_ref, c_ref, sa_ref, sb_ref):
        ang = p_ref[...].astype(F32) * f_ref[...]
        lane = lax.broadcasted_iota(jnp.int32, ang.shape, 1)
        cos, sin = jnp.cos(ang), jnp.sin(ang)
        c_ref[...] = jnp.where(lane < QK_ROPE, cos, 0.0)
        sa_ref[...] = jnp.where(lane < QK_ROPE // 2, sin, 0.0)
        sb_ref[...] = jnp.where((lane >= QK_ROPE // 2) & (lane < QK_ROPE), sin, 0.0)

    tab = jax.ShapeDtypeStruct((t, 128), F32)
    return pl.pallas_call(
        body, name="rope_tables", grid=(t // tr,),
        in_specs=[_row_spec(tr, 1), _full_spec((1, 128))],
        out_specs=[_row_spec(tr, 128)] * 3,
        out_shape=[tab, tab, tab],
    )(pos_col, inv_freq_lanes)


def _rope_fwd(p, c, sa, sb):
    return p * c - pltpu.roll(p, 96, 1) * sa + pltpu.roll(p, 32, 1) * sb


def _rope_bwd(g, c, sa, sb):
    return g * c + pltpu.roll(g, 96, 1) * sa - pltpu.roll(g, 32, 1) * sb


def _rms(xv, g):
    r = lax.rsqrt(jnp.mean(xv * xv, axis=-1, keepdims=True) + EPS)
    return (xv * r) * g


def _rms_bwd(xv, g, dout):
    r = lax.rsqrt(jnp.mean(xv * xv, axis=-1, keepdims=True) + EPS)
    xn = xv * r
    dg = jnp.sum(dout * xn, axis=0, keepdims=True)
    dxn = dout * g
    dx = r * (dxn - xn * jnp.mean(dxn * xn, axis=-1, keepdims=True))
    return dx, dg


def _latent_norms(proj_lat, q_norm_g, kv_norm_g, c_tab, sa_tab, sb_tab):
    t = proj_lat.shape[0]
    tr = min(ROW_BLK, t)

    def body(cq_ref, ckv_ref, kr_ref, qg_ref, kg_ref, c_ref, sa_ref, sb_ref, cqn_ref, ckvn_ref, kpe_ref):
        cqn_ref[...] = _rms(cq_ref[...].astype(F32), qg_ref[...]).astype(BF16)
        ckvn_ref[...] = _rms(ckv_ref[...].astype(F32), kg_ref[...]).astype(BF16)
        kpe_ref[...] = _rope_fwd(kr_ref[...].astype(F32), c_ref[...], sa_ref[...], sb_ref[...]).astype(BF16)

    return pl.pallas_call(
        body, name="latent_norms", grid=(t // tr,),
        in_specs=[_row_spec(tr, Q_LORA, 0), _row_spec(tr, KV_LORA, 1), _row_spec(tr, 128, (Q_LORA + KV_LORA) // 128),
                  _full_spec((1, Q_LORA)), _full_spec((1, KV_LORA)),
                  _row_spec(tr, 128), _row_spec(tr, 128), _row_spec(tr, 128)],
        out_specs=[_row_spec(tr, Q_LORA), _row_spec(tr, KV_LORA), _row_spec(tr, 128)],
        out_shape=[jax.ShapeDtypeStruct((t, Q_LORA), BF16), jax.ShapeDtypeStruct((t, KV_LORA), BF16),
                   jax.ShapeDtypeStruct((t, 128), BF16)],
    )(proj_lat, proj_lat, proj_lat, q_norm_g, kv_norm_g, c_tab, sa_tab, sb_tab)


def _rope_q(q_raw, c_tab, sa_tab, sb_tab):
    t = q_raw.shape[0]
    tr = min(ROW_BLK, t)
    qs = MLA_SCALE * LOG2E

    def body(q_ref, c_ref, sa_ref, sb_ref, o_ref):
        c, sa, sb = c_ref[...], sa_ref[...], sb_ref[...]
        for h in range(MLA_HEADS):
            o_ref[:, h * HEAD_PAD:h * HEAD_PAD + 128] = (q_ref[:, h * HEAD_PAD:h * HEAD_PAD + 128] * qs).astype(BF16)
            pe = q_ref[:, h * HEAD_PAD + 128:(h + 1) * HEAD_PAD]
            o_ref[:, h * HEAD_PAD + 128:(h + 1) * HEAD_PAD] = (_rope_fwd(pe, c, sa, sb) * qs).astype(BF16)

    w = MLA_HEADS * HEAD_PAD
    return pl.pallas_call(
        body, name="rope_q", grid=(t // tr,),
        in_specs=[_row_spec(tr, w), _row_spec(tr, 128), _row_spec(tr, 128), _row_spec(tr, 128)],
        out_specs=_row_spec(tr, w),
        out_shape=jax.ShapeDtypeStruct((t, w), BF16),
        compiler_params=_params(48),
    )(q_raw, c_tab, sa_tab, sb_tab)


def _rope_q_bwd(dq, dkpe_heads, c_tab, sa_tab, sb_tab):
    t = dq.shape[0]
    tr = min(ROW_BLK, t)

    def body(dq_ref, dkp_ref, c_ref, sa_ref, sb_ref, o_ref, dkr_ref):
        c, sa, sb = c_ref[...], sa_ref[...], sb_ref[...]
        for h in range(MLA_HEADS):
            o_ref[:, h * HEAD_PAD:h * HEAD_PAD + 128] = (dq_ref[:, h * HEAD_PAD:h * HEAD_PAD + 128] * MLA_SCALE).astype(BF16)
            g = dq_ref[:, h * HEAD_PAD + 128:(h + 1) * HEAD_PAD] * MLA_SCALE
            o_ref[:, h * HEAD_PAD + 128:(h + 1) * HEAD_PAD] = _rope_bwd(g, c, sa, sb).astype(BF16)
        tot = dkp_ref[0]
        for h in range(1, MLA_HEADS):
            tot = tot + dkp_ref[h]
        dkr_ref[...] = _rope_bwd(tot, c, sa, sb)

    w = MLA_HEADS * HEAD_PAD
    return pl.pallas_call(
        body, name="rope_q_bwd", grid=(t // tr,),
        in_specs=[_row_spec(tr, w), pl.BlockSpec((MLA_HEADS, tr, 128), lambda i: (0, i, 0)),
                  _row_spec(tr, 128), _row_spec(tr, 128), _row_spec(tr, 128)],
        out_specs=[_row_spec(tr, w), _row_spec(tr, 128)],
        out_shape=[jax.ShapeDtypeStruct((t, w), BF16), jax.ShapeDtypeStruct((t, 128), F32)],
        compiler_params=_params(48),
    )(dq, dkpe_heads, c_tab, sa_tab, sb_tab)


def _latent_norms_bwd(proj_lat, q_norm_g, kv_norm_g, dcqn, dckvn, dkr):
    t = proj_lat.shape[0]
    tr = min(ROW_BLK, t)

    def body(cq_ref, ckv_ref, qg_ref, kg_ref, dcqn_ref, dckvn_ref, dkr_ref, dl_ref, dqg_ref, dkg_ref):
        dcq, dqg = _rms_bwd(cq_ref[...].astype(F32), qg_ref[...], dcqn_ref[...])
        dckv, dkg = _rms_bwd(ckv_ref[...].astype(F32), kg_ref[...], dckvn_ref[...])
        dl_ref[:, 0:Q_LORA] = dcq.astype(BF16)
        dl_ref[:, Q_LORA:Q_LORA + KV_LORA] = dckv.astype(BF16)
        dl_ref[:, Q_LORA + KV_LORA:LAT_W] = dkr_ref[...].astype(BF16)

        @pl.when(pl.program_id(0) == 0)
        def _():
            dqg_ref[...] = jnp.zeros_like(dqg_ref)
            dkg_ref[...] = jnp.zeros_like(dkg_ref)

        dqg_ref[...] += dqg
        dkg_ref[...] += dkg

    return pl.pallas_call(
        body, name="latent_norms_bwd", grid=(t // tr,),
        in_specs=[_row_spec(tr, Q_LORA, 0), _row_spec(tr, KV_LORA, 1), _full_spec((1, Q_LORA)), _full_spec((1, KV_LORA)),
                  _row_spec(tr, Q_LORA), _row_spec(tr, KV_LORA), _row_spec(tr, 128)],
        out_specs=[_row_spec(tr, LAT_W), _full_spec((1, Q_LORA)), _full_spec((1, KV_LORA))],
        out_shape=[jax.ShapeDtypeStruct((t, LAT_W), BF16), jax.ShapeDtypeStruct((1, Q_LORA), F32),
                   jax.ShapeDtypeStruct((1, KV_LORA), F32)],
    )(proj_lat, proj_lat, q_norm_g, kv_norm_g, dcqn, dckvn, dkr)


def _mem_norm(mem, g):
    m, d = mem.shape

    def body(x_ref, g_ref, o_ref):
        o_ref[...] = _rms(x_ref[...], g_ref[...]).astype(BF16)

    return pl.pallas_call(
        body, name="mem_norm", grid=(1,),
        in_specs=[_full_spec((m, d)), _full_spec((1, d))],
        out_specs=_full_spec((m, d)),
        out_shape=jax.ShapeDtypeStruct((m, d), BF16),
    )(mem, g)


def _mem_norm_bwd(mem, g, dmemn):
    m, d = mem.shape

    def body(x_ref, g_ref, d_ref, dg_ref):
        _, dg = _rms_bwd(x_ref[...], g_ref[...], d_ref[...])
        dg_ref[...] = dg

    return pl.pallas_call(
        body, name="mem_norm_bwd", grid=(1,),
        in_specs=[_full_spec((m, d)), _full_spec((1, d)), _full_spec((m, d))],
        out_specs=_full_spec((1, d)),
        out_shape=jax.ShapeDtypeStruct((1, d), F32),
    )(mem, g, dmemn)


def _merge(gates, p_a, p_b, p_m):
    t, d = p_a.shape
    tr = min(ROW_BLK, t)

    def body(ga_ref, gb_ref, gm_ref, pa_ref, pb_ref, pm_ref, o_ref):
        acc = ga_ref[...].astype(F32) * pa_ref[...].astype(F32)
        acc = acc + gb_ref[...].astype(F32) * pb_ref[...].astype(F32)
        acc = acc + gm_ref[...].astype(F32) * pm_ref[...].astype(F32)
        o_ref[...] = acc.astype(BF16)

    return pl.pallas_call(
        body, name="merge", grid=(t // tr,),
        in_specs=[_row_spec(tr, d, 0), _row_spec(tr, d, 1), _row_spec(tr, d, 2),
                  _row_spec(tr, d), _row_spec(tr, d), _row_spec(tr, d)],
        out_specs=_row_spec(tr, d),
        out_shape=jax.ShapeDtypeStruct((t, d), BF16),
        compiler_params=_params(48),
    )(gates, gates, gates, p_a, p_b, p_m)


def _merge_bwd(dm, gates, p_a, p_b, p_m):
    t, d = dm.shape
    tr = min(ROW_BLK, t)

    def body(dm_ref, g_ref, pa_ref, pb_ref, pm_ref, dpa_ref, dpb_ref, dpm_ref, dgl_ref, db_ref):
        dmv = dm_ref[...].astype(F32)

        @pl.when(pl.program_id(0) == 0)
        def _():
            db_ref[...] = jnp.zeros_like(db_ref)

        for n, (p_ref, dp_ref) in enumerate(((pa_ref, dpa_ref), (pb_ref, dpb_ref), (pm_ref, dpm_ref))):
            g = g_ref[:, n * d:(n + 1) * d].astype(F32)
            dp_ref[...] = (dmv * g).astype(BF16)
            dgl = dmv * p_ref[...].astype(F32) * (g * (1.0 - g))
            dgl_ref[:, n * d:(n + 1) * d] = dgl.astype(BF16)
            db_ref[:, n * d:(n + 1) * d] += jnp.sum(dgl, axis=0, keepdims=True)

    act = jax.ShapeDtypeStruct((t, d), BF16)
    return pl.pallas_call(
        body, name="merge_bwd", grid=(t // tr,),
        in_specs=[_row_spec(tr, d), _row_spec(tr, 3 * d), _row_spec(tr, d), _row_spec(tr, d), _row_spec(tr, d)],
        out_specs=[_row_spec(tr, d), _row_spec(tr, d), _row_spec(tr, d), _row_spec(tr, 3 * d), _full_spec((1, 3 * d))],
        out_shape=[act, act, act, jax.ShapeDtypeStruct((t, 3 * d), BF16), jax.ShapeDtypeStruct((1, 3 * d), F32)],
        compiler_params=_params(56),
    )(dm, gates, p_a, p_b, p_m)


def _post_loss(out, x, tgt, g_post):
    t, d = out.shape
    tr = min(ROW_BLK, t)

    def body(o_ref, x_ref, t_ref, g_ref, do_ref, dy_ref, loss_ref, dg_ref):
        ov = o_ref[...]
        g = g_ref[...]
        r = lax.rsqrt(jnp.mean(ov * ov, axis=-1, keepdims=True) + EPS)
        on = ov * r
        err = (x_ref[...] + on * g) - t_ref[...]
        dy = err * (1.0 / d)
        dy_ref[...] = dy
        don = dy * g
        do_ref[...] = (r * (don - on * jnp.mean(don * on, axis=-1, keepdims=True))).astype(BF16)

        @pl.when(pl.program_id(0) == 0)
        def _():
            loss_ref[...] = jnp.zeros_like(loss_ref)
            dg_ref[...] = jnp.zeros_like(dg_ref)

        loss_ref[...] += 0.5 * jnp.sum(jnp.mean(err * err, axis=-1, keepdims=True))
        dg_ref[...] += jnp.sum(dy * on, axis=0, keepdims=True)

    return pl.pallas_call(
        body, name="post_loss", grid=(t // tr,),
        in_specs=[_row_spec(tr, d), _row_spec(tr, d), _row_spec(tr, d), _full_spec((1, d))],
        out_specs=[_row_spec(tr, d), _row_spec(tr, d), _full_spec((8, 128)), _full_spec((1, d))],
        out_shape=[jax.ShapeDtypeStruct((t, d), BF16), jax.ShapeDtypeStruct((t, d), F32),
                   jax.ShapeDtypeStruct((8, 128), F32), jax.ShapeDtypeStruct((1, d), F32)],
        compiler_params=_params(56),
    )(out, x, tgt, g_post)


def _pre_norm_bwd(x, g_pre, dh, dy):
    t, d = x.shape
    tr = min(ROW_BLK, t)

    def body(x_ref, g_ref, dh_ref, dy_ref, dx_ref, dg_ref):
        dx, dg = _rms_bwd(x_ref[...], g_ref[...], dh_ref[...])
        dx_ref[...] = dx + dy_ref[...]

        @pl.when(pl.program_id(0) == 0)
        def _():
            dg_ref[...] = jnp.zeros_like(dg_ref)

        dg_ref[...] += dg

    return pl.pallas_call(
        body, name="pre_norm_bwd", grid=(t // tr,),
        in_specs=[_row_spec(tr, d), _full_spec((1, d)), _row_spec(tr, d), _row_spec(tr, d)],
        out_specs=[_row_spec(tr, d), _full_spec((1, d))],
        out_shape=[jax.ShapeDtypeStruct((t, d), F32), jax.ShapeDtypeStruct((1, d), F32)],
        compiler_params=_params(56),
    )(x, g_pre, dh, dy)


def _causal_mask(n):
    row = lax.broadcasted_iota(jnp.int32, (n, n), 0)
    col = lax.broadcasted_iota(jnp.int32, (n, n), 1)
    return row >= col


def _layernorm_stats(vg):
    mu = jnp.mean(vg, axis=-1, keepdims=True)
    cen = vg - mu
    rstd = lax.rsqrt(jnp.mean(cen * cen, axis=-1, keepdims=True) + EPS)
    return cen * rstd, rstd


def _gmlp_fwd(proj_big, ln_g, ln_b, w_s, b_exp):
    t = proj_big.shape[0]
    rows = min(GMLP_ROWS, t)
    d = D_MODEL

    def body(u_ref, v_ref, z_ref, lg_ref, lb_ref, ws_ref, be_ref, y_ref, vn_scr):
        vhat, _ = _layernorm_stats(_gelu(v_ref[...].astype(F32)))
        vn_scr[...] = (vhat * lg_ref[...] + lb_ref[...]).astype(BF16)
        mask = _causal_mask(CHUNK)
        for g in range(A_GROUPS):
            cols = slice(g * 128, (g + 1) * 128)
            wsm = jnp.where(mask, ws_ref[g], 0.0).astype(BF16)
            for c in range(rows // CHUNK):
                rws = slice(c * CHUNK, (c + 1) * CHUNK)
                sv = jnp.dot(wsm, vn_scr[rws, cols], preferred_element_type=F32) + be_ref[g]
                zs, _ = _silu_and_grad(z_ref[rws, cols].astype(F32))
                y_ref[rws, cols] = (_gelu(u_ref[rws, cols].astype(F32)) * sv * zs).astype(BF16)

    return pl.pallas_call(
        body, name="gmlp_fwd", grid=(t // rows,),
        in_specs=[_row_spec(rows, d, 0), _row_spec(rows, d, 1), _row_spec(rows, d, 2),
                  _full_spec((1, d)), _full_spec((1, d)), _full_spec((A_GROUPS, CHUNK, CHUNK)),
                  _full_spec((A_GROUPS, CHUNK, 128))],
        out_specs=_row_spec(rows, d),
        out_shape=jax.ShapeDtypeStruct((t, d), BF16),
        scratch_shapes=[pltpu.VMEM((rows, d), BF16)],
        compiler_params=_params(40),
    )(proj_big, proj_big, proj_big, ln_g, ln_b, w_s, b_exp)


def _gmlp_bwd(proj_big, dya, ln_g, ln_b, w_s, b_exp):
    t = proj_big.shape[0]
    rows = min(GMLP_ROWS, t)
    d = D_MODEL
    nt = (((1,), (1,)), ((), ()))
    tn = (((0,), (0,)), ((), ()))

    def body(u_ref, v_ref, z_ref, dy_ref, lg_ref, lb_ref, ws_ref, be_ref,
             dp_ref, dws_ref, dbs_ref, dlg_ref, dlb_ref, vn_scr, dvn_scr):
        @pl.when(pl.program_id(0) == 0)
        def _():
            dws_ref[...] = jnp.zeros_like(dws_ref)
            dbs_ref[...] = jnp.zeros_like(dbs_ref)
            dlg_ref[...] = jnp.zeros_like(dlg_ref)
            dlb_ref[...] = jnp.zeros_like(dlb_ref)

        vg, vgrad = _gelu_and_grad(v_ref[...].astype(F32))
        vhat, rstd = _layernorm_stats(vg)
        vn_scr[...] = (vhat * lg_ref[...] + lb_ref[...]).astype(BF16)
        mask = _causal_mask(CHUNK)
        for g in range(A_GROUPS):
            cols = slice(g * 128, (g + 1) * 128)
            wsm = jnp.where(mask, ws_ref[g], 0.0).astype(BF16)
            dws = jnp.zeros((CHUNK, CHUNK), F32)
            dbs = jnp.zeros((CHUNK, 1), F32)
            for c in range(rows // CHUNK):
                rws = slice(c * CHUNK, (c + 1) * CHUNK)
                vn = vn_scr[rws, cols]
                sv = jnp.dot(wsm, vn, preferred_element_type=F32) + be_ref[g]
                ug, ugrad = _gelu_and_grad(u_ref[rws, cols].astype(F32))
                zs, zgrad = _silu_and_grad(z_ref[rws, cols].astype(F32))
                dya = dy_ref[rws, cols].astype(F32)
                dga = dya * zs
                dp_ref[rws, 2 * d + g * 128:2 * d + (g + 1) * 128] = (dya * (ug * sv) * zgrad).astype(BF16)
                dp_ref[rws, cols] = (dga * sv * ugrad).astype(BF16)
                dsv = dga * ug
                dsv16 = dsv.astype(BF16)
                dws = dws + lax.dot_general(dsv16, vn, nt, preferred_element_type=F32)
                dbs = dbs + jnp.sum(dsv, axis=-1, keepdims=True)
                dvn_scr[rws, cols] = lax.dot_general(wsm, dsv16, tn, preferred_element_type=F32)
            dws_ref[g] += jnp.where(mask, dws, 0.0)
            dbs_ref[g] += dbs
        dvn = dvn_scr[...]
        dlg_ref[...] += jnp.sum(dvn * vhat, axis=0, keepdims=True)
        dlb_ref[...] += jnp.sum(dvn, axis=0, keepdims=True)
        dvh = dvn * lg_ref[...]
        dvg = rstd * (dvh - jnp.mean(dvh, axis=-1, keepdims=True) - vhat * jnp.mean(dvh * vhat, axis=-1, keepdims=True))
        dp_ref[:, d:2 * d] = (dvg * vgrad).astype(BF16)

    return pl.pallas_call(
        body, name="gmlp_bwd", grid=(t // rows,),
        in_specs=[_row_spec(rows, d, 0), _row_spec(rows, d, 1), _row_spec(rows, d, 2), _row_spec(rows, d),
                  _full_spec((1, d)), _full_spec((1, d)), _full_spec((A_GROUPS, CHUNK, CHUNK)),
                  _full_spec((A_GROUPS, CHUNK, 128))],
        out_specs=[_row_spec(rows, 3 * d), _full_spec((A_GROUPS, CHUNK, CHUNK)), _full_spec((A_GROUPS, CHUNK, 1)),
                   _full_spec((1, d)), _full_spec((1, d))],
        out_shape=[jax.ShapeDtypeStruct((t, 6 * d), BF16), jax.ShapeDtypeStruct((A_GROUPS, CHUNK, CHUNK), F32),
                   jax.ShapeDtypeStruct((A_GROUPS, CHUNK, 1), F32), jax.ShapeDtypeStruct((1, d), F32),
                   jax.ShapeDtypeStruct((1, d), F32)],
        scratch_shapes=[pltpu.VMEM((rows, d), BF16), pltpu.VMEM((rows, d), F32)],
        compiler_params=_params(48),
    )(proj_big, proj_big, proj_big, dya, ln_g, ln_b, w_s, b_exp)


NT_DIMS = (((1,), (1,)), ((), ()))
TN_DIMS = (((0,), (0,)), ((), ()))


def _mla_fwd(q, kv, kpe, proj_big):
    t = q.shape[0]
    blk = min(ATT_BLK, t)
    nq = t // blk
    zb_blk0 = (3 * D_MODEL) // 128

    nc = blk // 128

    sub = min(ATT_SUB, blk)

    def body(q_ref, kv_ref, kp_ref, zb_ref, o_ref, yb_ref, lse_ref, m_scr, acc_scr):
        i = pl.program_id(1)
        qv = q_ref[...]
        m_scr[...] = jnp.full_like(m_scr, NEG)
        acc_scr[...] = jnp.zeros_like(acc_scr)
        ones = jnp.ones((blk, 128), BF16)

        def step(j, masked):
            ks = pl.ds(pl.multiple_of(j * blk, blk), blk)
            kc = jnp.concatenate([kv_ref[ks, 0:128], kp_ref[ks, :]], axis=1)
            vext = jnp.concatenate([kv_ref[ks, 128:256], ones], axis=1)
            for r in range(blk // sub):
                rows = slice(r * sub, (r + 1) * sub)
                kw = (r + 1) * sub if masked else blk
                tt = lax.dot_general(qv[rows], kc[:kw], NT_DIMS, preferred_element_type=F32)
                if masked:
                    row = lax.broadcasted_iota(jnp.int32, (sub, kw), 0) + r * sub
                    col = lax.broadcasted_iota(jnp.int32, (sub, kw), 1)
                    tt = jnp.where(row >= col, tt, NEG)
                cm = tt[:, 0:128]
                for c in range(1, kw // 128):
                    cm = jnp.maximum(cm, tt[:, c * 128:(c + 1) * 128])
                m_prev = m_scr[rows, :]
                m_new = jnp.maximum(m_prev, jnp.max(cm, axis=-1, keepdims=True))
                alpha = jnp.exp2(m_prev - m_new)
                m_scr[rows, :] = m_new
                p = jnp.concatenate([jnp.exp2(tt[:, c * 128:(c + 1) * 128] - m_new).astype(BF16)
                                     for c in range(kw // 128)], axis=1)
                pv = jnp.dot(p, vext[:kw], preferred_element_type=F32)
                acc_scr[rows, :] = jnp.concatenate([alpha, alpha], axis=1) * acc_scr[rows, :] + pv

        def loop_body(j, carry):
            step(j, False)
            return carry

        lax.fori_loop(0, i, loop_body, 0)
        step(i, True)
        l = acc_scr[:, 128:256]
        o = acc_scr[:, 0:128] / l
        o_ref[...] = o.astype(BF16)
        zs, _ = _silu_and_grad(zb_ref[...].astype(F32))
        yb_ref[...] = (o * zs).astype(BF16)
        lse_ref[0] = m_scr[...] + jnp.log2(l)

    act = jax.ShapeDtypeStruct((t, D_MODEL), BF16)
    return pl.pallas_call(
        body, name="mla_fwd", grid=(MLA_HEADS, nq),
        in_specs=[pl.BlockSpec((blk, HEAD_PAD), lambda h, i: (i, h)),
                  pl.BlockSpec((t, HEAD_PAD), lambda h, i: (0, h)),
                  pl.BlockSpec((t, 128), lambda h, i: (0, 0)),
                  pl.BlockSpec((blk, 128), lambda h, i: (i, zb_blk0 + h))],
        out_specs=[pl.BlockSpec((blk, 128), lambda h, i: (i, h)),
                   pl.BlockSpec((blk, 128), lambda h, i: (i, h)),
                   pl.BlockSpec((1, blk, 128), lambda h, i: (h, i, 0))],
        out_shape=[act, act, jax.ShapeDtypeStruct((MLA_HEADS, t, 128), F32)],
        scratch_shapes=[pltpu.VMEM((blk, 128), F32), pltpu.VMEM((blk, HEAD_PAD), F32)],
        compiler_params=_params(56),
    )(q, kv, kpe, proj_big)


def _mla_gate_bwd(dyb, proj_big, o, dproj):
    t, d = dyb.shape
    tr = min(ROW_BLK, t)

    def body(dy_ref, zb_ref, o_ref, buf_ref, do_ref, dz_ref, dl_ref):
        del buf_ref
        dy = dy_ref[...].astype(F32)
        ov = o_ref[...].astype(F32)
        zs, zgrad = _silu_and_grad(zb_ref[...].astype(F32))
        do16 = (dy * zs).astype(BF16)
        do_ref[...] = do16
        dz_ref[...] = (dy * ov * zgrad).astype(BF16)
        prod = do16.astype(F32) * ov
        for h in range(MLA_HEADS):
            delta = jnp.sum(prod[:, h * 128:(h + 1) * 128], axis=-1, keepdims=True)
            dl_ref[h] = jnp.broadcast_to(delta, (tr, 128))

    act = jax.ShapeDtypeStruct((t, d), BF16)
    head_spec = pl.BlockSpec((MLA_HEADS, tr, 128), lambda i: (0, i, 0))
    return pl.pallas_call(
        body, name="mla_gate_bwd", grid=(t // tr,),
        in_specs=[_row_spec(tr, d), _row_spec(tr, d, 3), _row_spec(tr, d), HBM_SPEC],
        out_specs=[_row_spec(tr, d), _row_spec(tr, d, 3), head_spec],
        out_shape=[act, jax.ShapeDtypeStruct((t, 6 * d), BF16), jax.ShapeDtypeStruct((MLA_HEADS, t, 128), F32)],
        input_output_aliases={3: 1},
        compiler_params=_params(56),
    )(dyb, proj_big, o, dproj)


def _mla_bwd(q, kv, kpe, do, lse, delta, carry):
    t = q.shape[0]
    blk = min(ATT_BLK, t)
    n = t // blk
    nc = blk // 128
    pairs = [(j, i) for j in range(n) for i in range(j, n)]
    j_tab = jnp.asarray([p[0] for p in pairs], jnp.int32)
    i_tab = jnp.asarray([p[1] for p in pairs], jnp.int32)
    n_carry = len(carry.arrays)
    sub = min(ATT_SUB, blk)

    def body(j_ref, i_ref, q_ref, do_ref, lse_ref, dl_ref, kv_ref, kp_ref, *rest):
        c_ins, rest = rest[:n_carry], rest[n_carry:]
        dq_ref, dkv_ref, dkp_ref = rest[:3]
        c_outs, rest = rest[3:3 + n_carry], rest[3 + n_carry:]
        dk_scr, dv_scr = rest[:2]
        c_sems = rest[2:]
        head = pl.program_id(0)
        step = pl.program_id(1)
        j, i = j_ref[step], i_ref[step]

        @pl.when((head == 0) & (step == 0))
        def _():
            carry.start(c_ins, c_outs, c_sems)

        @pl.when(step == 0)
        def _():
            dq_ref[...] = jnp.zeros_like(dq_ref)

        @pl.when(i == j)
        def _():
            dk_scr[...] = jnp.zeros_like(dk_scr)
            dv_scr[...] = jnp.zeros_like(dv_scr)

        kc = jnp.concatenate([kv_ref[:, 0:128], kp_ref[...]], axis=1)
        vv = kv_ref[:, 128:256]

        def tile(diag):
            for r in range(blk // sub):
                rows = slice(r * sub, (r + 1) * sub)
                kw = (r + 1) * sub if diag else blk
                qv, dov = q_ref[rows, :], do_ref[rows, :]
                tt = lax.dot_general(qv, kc[:kw], NT_DIMS, preferred_element_type=F32)
                if diag:
                    row = lax.broadcasted_iota(jnp.int32, (sub, kw), 0) + r * sub
                    col = lax.broadcasted_iota(jnp.int32, (sub, kw), 1)
                    tt = jnp.where(row >= col, tt, NEG)
                dp = lax.dot_general(dov, vv[:kw], NT_DIMS, preferred_element_type=F32)
                lse_v, dl_v = lse_ref[0, rows, :], dl_ref[0, rows, :]
                ps, dss = [], []
                for c in range(kw // 128):
                    cols = slice(c * 128, (c + 1) * 128)
                    p = jnp.exp2(tt[:, cols] - lse_v)
                    ps.append(p.astype(BF16))
                    dss.append((p * (dp[:, cols] - dl_v)).astype(BF16))
                p16 = jnp.concatenate(ps, axis=1)
                ds16 = jnp.concatenate(dss, axis=1)
                dv_scr[0:kw, :] += lax.dot_general(p16, dov, TN_DIMS, preferred_element_type=F32)
                dk_scr[0:kw, :] += lax.dot_general(ds16, qv, TN_DIMS, preferred_element_type=F32)
                qs = pl.ds(pl.multiple_of(i * blk + r * sub, sub), sub)
                dq_ref[qs, :] += jnp.dot(ds16, kc[:kw], preferred_element_type=F32)

        @pl.when(i == j)
        def _():
            tile(True)

        @pl.when(i > j)
        def _():
            tile(False)

        @pl.when(i == n - 1)
        def _():
            dkv_ref[:, 0:128] = (dk_scr[:, 0:128] * (1.0 / LOG2E)).astype(BF16)
            dkv_ref[:, 128:256] = dv_scr[...].astype(BF16)
            dkp_ref[0] = dk_scr[:, 128:256] * (1.0 / LOG2E)

        @pl.when((head == MLA_HEADS - 1) & (step == len(pairs) - 1))
        def _():
            carry.finish(c_ins, c_outs, c_sems)

    grid_spec = pltpu.PrefetchScalarGridSpec(
        num_scalar_prefetch=2, grid=(MLA_HEADS, len(pairs)),
        in_specs=[pl.BlockSpec((blk, HEAD_PAD), lambda h, s, jt, it: (it[s], h)),
                  pl.BlockSpec((blk, 128), lambda h, s, jt, it: (it[s], h)),
                  pl.BlockSpec((1, blk, 128), lambda h, s, jt, it: (h, it[s], 0)),
                  pl.BlockSpec((1, blk, 128), lambda h, s, jt, it: (h, it[s], 0)),
                  pl.BlockSpec((blk, HEAD_PAD), lambda h, s, jt, it: (jt[s], h)),
                  pl.BlockSpec((blk, 128), lambda h, s, jt, it: (jt[s], 0))] + [HBM_SPEC] * n_carry,
        out_specs=[pl.BlockSpec((t, HEAD_PAD), lambda h, s, jt, it: (0, h)),
                   pl.BlockSpec((blk, HEAD_PAD), lambda h, s, jt, it: (jt[s], h)),
                   pl.BlockSpec((1, blk, 128), lambda h, s, jt, it: (h, jt[s], 0))] + [HBM_SPEC] * n_carry,
        scratch_shapes=[pltpu.VMEM((blk, HEAD_PAD), F32), pltpu.VMEM((blk, 128), F32)] + carry.sem_shapes,
    )
    return pl.pallas_call(
        body, name="mla_bwd", grid_spec=grid_spec,
        out_shape=[jax.ShapeDtypeStruct((t, MLA_HEADS * HEAD_PAD), F32),
                   jax.ShapeDtypeStruct((t, 2 * D_MODEL), BF16),
                   jax.ShapeDtypeStruct((MLA_HEADS, t, 128), F32)] + carry.out_shapes,
        compiler_params=_params(58),
    )(j_tab, i_tab, q, do, lse, delta, kv, kpe, *carry.arrays)


def _mem_attn_probs(qv, k_ref):
    s = lax.dot_general(qv, k_ref[...], NT_DIMS, preferred_element_type=F32) * MEM_SCALE
    e = jnp.exp(s - jnp.max(s, axis=-1, keepdims=True))
    return e / jnp.sum(e, axis=-1, keepdims=True)


def _mem_fwd(proj_big, kv_m):
    t = proj_big.shape[0]
    tq = min(MEM_Q_BLK, t)
    hd = MEM_HEAD_DIM
    q0, z0 = (4 * D_MODEL) // hd, (5 * D_MODEL) // hd

    def body(q_ref, z_ref, k_ref, v_ref, y_ref):
        p = _mem_attn_probs(q_ref[...], k_ref)
        o = jnp.dot(p.astype(BF16), v_ref[...], preferred_element_type=F32)
        zs, _ = _silu_and_grad(z_ref[...].astype(F32))
        y_ref[...] = (o * zs).astype(BF16)

    return pl.pallas_call(
        body, name="mem_fwd", grid=(t // tq, MEM_HEADS),
        in_specs=[pl.BlockSpec((tq, hd), lambda i, h: (i, q0 + h)), pl.BlockSpec((tq, hd), lambda i, h: (i, z0 + h)),
                  pl.BlockSpec((kv_m.shape[0], hd), lambda i, h: (0, h)),
                  pl.BlockSpec((kv_m.shape[0], hd), lambda i, h: (0, MEM_HEADS + h))],
        out_specs=pl.BlockSpec((tq, hd), lambda i, h: (i, h)),
        out_shape=jax.ShapeDtypeStruct((t, D_MODEL), BF16),
    )(proj_big, proj_big, kv_m, kv_m)


def _mem_bwd(proj_big, kv_m, dym, dproj):
    t = proj_big.shape[0]
    tq = min(ROW_BLK, t)
    hd = MEM_HEAD_DIM
    d = D_MODEL
    mlen = kv_m.shape[0]

    def body(q_ref, z_ref, kv_ref, dy_ref, buf_ref, dqz_ref, dkv_ref):
        del buf_ref

        @pl.when(pl.program_id(0) == 0)
        def _():
            dkv_ref[...] = jnp.zeros_like(dkv_ref)

        for h in range(MEM_HEADS):
            cols = slice(h * hd, (h + 1) * hd)
            k_ref, v_ref = kv_ref.at[:, cols], kv_ref.at[:, d + h * hd:d + (h + 1) * hd]
            qv = q_ref[:, cols]
            p = _mem_attn_probs(qv, k_ref)
            p16 = p.astype(BF16)
            o = jnp.dot(p16, v_ref[...], preferred_element_type=F32)
            zs, zgrad = _silu_and_grad(z_ref[:, cols].astype(F32))
            dy = dy_ref[:, cols].astype(F32)
            dqz_ref[:, d + h * hd:d + (h + 1) * hd] = (dy * o * zgrad).astype(BF16)
            do16 = (dy * zs).astype(BF16)
            dkv_ref[:, d + h * hd:d + (h + 1) * hd] += lax.dot_general(p16, do16, TN_DIMS, preferred_element_type=F32)
            dp = lax.dot_general(do16, v_ref[...], NT_DIMS, preferred_element_type=F32)
            ds = (p * (dp - jnp.sum(dp * p, axis=-1, keepdims=True)) * MEM_SCALE).astype(BF16)
            dqz_ref[:, cols] = jnp.dot(ds, k_ref[...], preferred_element_type=F32).astype(BF16)
            dkv_ref[:, cols] += lax.dot_general(ds, qv, TN_DIMS, preferred_element_type=F32)

    return pl.pallas_call(
        body, name="mem_bwd", grid=(t // tq,),
        in_specs=[_row_spec(tq, d, 4), _row_spec(tq, d, 5), _full_spec((mlen, 2 * d)), _row_spec(tq, d), HBM_SPEC],
        out_specs=[_row_spec(tq, 2 * d, 2), _full_spec((mlen, 2 * d))],
        out_shape=[jax.ShapeDtypeStruct((t, 6 * d), BF16), jax.ShapeDtypeStruct((mlen, 2 * d), F32)],
        input_output_aliases={4: 0},
        compiler_params=_params(40),
    )(proj_big, proj_big, kv_m, dym, dproj)


HBM_SPEC = pl.BlockSpec(memory_space=pl.ANY)
N_PEERS = N_DEV - 1


def _dev_index(px, py, pc):
    return 4 * px + 2 * py + pc


class _Gather:
    def __init__(self, arrays):
        self.arrays = list(arrays)
        n = len(self.arrays)
        self.out_shapes = [jax.ShapeDtypeStruct((N_DEV,) + a.shape, a.dtype) for a in self.arrays]
        self.sem_shapes = [pltpu.SemaphoreType.DMA((n * N_PEERS,)), pltpu.SemaphoreType.DMA((n * N_PEERS,)),
                           pltpu.SemaphoreType.DMA((n,))]

    def _parts(self, ins, outs, sems):
        n = len(self.arrays)
        send_sems, recv_sems, local_sems = sems
        x, y, c = lax.axis_index("x"), lax.axis_index("y"), lax.axis_index("c")
        me, sibling = (x, y, c), (x, y, 1 - c)
        chips = [(1 - x, y), (x, 1 - y), (1 - x, 1 - y)]

        def copy(a, k, block, to, src=None):
            dst = outs[a].at[_dev_index(*block)]
            return pltpu.make_async_remote_copy(
                src_ref=dst if src is None else src, dst_ref=dst,
                send_sem=send_sems.at[a * N_PEERS + k], recv_sem=recv_sems.at[a * N_PEERS + k],
                device_id=to, device_id_type=MESH)

        mine = [pltpu.make_async_copy(ins[a], outs[a].at[_dev_index(*me)], local_sems.at[a]) for a in range(n)]
        first = []
        for a in range(n):
            first.append(copy(a, 0, me, sibling, src=ins[a]))
            first += [copy(a, 1 + j, me, (*chip, c), src=ins[a]) for j, chip in enumerate(chips)]
        return n, c, me, sibling, chips, copy, mine, first

    def start(self, ins, outs, sems):
        _, _, _, _, _, _, mine, first = self._parts(ins, outs, sems)
        for cp in mine + first:
            cp.start()

    def finish(self, ins, outs, sems):
        n, c, me, sibling, chips, copy, mine, first = self._parts(ins, outs, sems)
        passed = []
        for j, chip in enumerate(chips):
            for a in range(n):
                copy(a, 1 + j, (*chip, c), me).wait_recv()
                fwd = copy(a, 4 + j, (*chip, c), sibling)
                fwd.start()
                passed.append(fwd)
        for a in range(n):
            copy(a, 0, sibling, me).wait_recv()
            for j, chip in enumerate(chips):
                copy(a, 4 + j, (*chip, 1 - c), me).wait_recv()
        for cp in first + passed:
            cp.wait_send()
        for cp in mine:
            cp.wait()


class _AllToAll:
    def __init__(self, arrays):
        self.arrays = list(arrays)
        n = len(self.arrays)
        self.out_shapes = [jax.ShapeDtypeStruct(a.shape, a.dtype) for a in self.arrays]
        self.sem_shapes = [pltpu.SemaphoreType.DMA((n * N_PEERS,)), pltpu.SemaphoreType.DMA((n * N_PEERS,)),
                           pltpu.SemaphoreType.DMA((n,))]

    def _parts(self, ins, outs, sems):
        n = len(self.arrays)
        send_sems, recv_sems, local_sems = sems
        x, y, c = lax.axis_index("x"), lax.axis_index("y"), lax.axis_index("c")
        my_idx = _dev_index(x, y, c)
        peers = []
        for k in range(1, N_DEV):
            dx, dy, dc = (k >> 2) & 1, (k >> 1) & 1, k & 1
            peers.append((1 - x if dx else x, 1 - y if dy else y, 1 - c if dc else c))

        def copy(a, k, peer):
            return pltpu.make_async_remote_copy(
                src_ref=ins[a].at[_dev_index(*peer)], dst_ref=outs[a].at[my_idx],
                send_sem=send_sems.at[a * N_PEERS + k], recv_sem=recv_sems.at[a * N_PEERS + k],
                device_id=peer, device_id_type=MESH)

        def landed(a, k, peer):
            slot = outs[a].at[_dev_index(*peer)]
            return pltpu.make_async_remote_copy(
                src_ref=slot, dst_ref=slot,
                send_sem=send_sems.at[a * N_PEERS + k], recv_sem=recv_sems.at[a * N_PEERS + k],
                device_id=peer, device_id_type=MESH)

        mine = [pltpu.make_async_copy(ins[a].at[my_idx], outs[a].at[my_idx], local_sems.at[a]) for a in range(n)]
        sends = [copy(a, k, peer) for a in range(n) for k, peer in enumerate(peers)]
        return n, peers, landed, mine, sends

    def start(self, ins, outs, sems):
        _, _, _, mine, sends = self._parts(ins, outs, sems)
        for cp in mine + sends:
            cp.start()

    def finish(self, ins, outs, sems):
        n, peers, landed, mine, sends = self._parts(ins, outs, sems)
        for a in range(n):
            for k, peer in enumerate(peers):
                landed(a, k, peer).wait_recv()
        for cp in sends:
            cp.wait_send()
        for cp in mine:
            cp.wait()


def _exchange(plan, name):
    n = len(plan.arrays)

    def body(*refs):
        ins, outs, sems = refs[:n], refs[n:2 * n], refs[2 * n:]
        plan.start(ins, outs, sems)
        plan.finish(ins, outs, sems)

    return pl.pallas_call(
        body, name=name, in_specs=[HBM_SPEC] * n, out_specs=[HBM_SPEC] * n,
        out_shape=plan.out_shapes, scratch_shapes=plan.sem_shapes,
    )(*plan.arrays)


def _adamw(w, g, m, v):
    m = ADAM_B1 * m + (1.0 - ADAM_B1) * g
    v = ADAM_B2 * v + (1.0 - ADAM_B2) * jnp.square(g)
    m_hat = m / (1.0 - ADAM_B1 ** ADAM_STEP)
    v_hat = v / (1.0 - ADAM_B2 ** ADAM_STEP)
    delta = -ADAM_LR * (m_hat / (jnp.sqrt(v_hat) + ADAM_EPS) + ADAM_WD * w)
    return delta, m, v


def _adam_sharded(parts, w, m, v, name):
    shape = w.shape
    cols = shape[-1]
    rows = int(np.prod(shape[:-1]))
    tr = min(128, rows)
    assert rows % tr == 0

    def body(p_ref, w_ref, m_ref, v_ref, g_ref, d_ref, nm_ref, nv_ref):
        g = p_ref[0].astype(F32)
        for e in range(1, N_DEV):
            g = g + p_ref[e].astype(F32)
        g_ref[...] = g
        d_ref[...], nm_ref[...], nv_ref[...] = _adamw(w_ref[...], g, m_ref[...], v_ref[...])

    spec = pl.BlockSpec((tr, cols), lambda i: (i, 0))
    flat = jax.ShapeDtypeStruct((rows, cols), F32)
    outs = pl.pallas_call(
        body, name=name, grid=(rows // tr,),
        in_specs=[pl.BlockSpec((N_DEV, tr, cols), lambda i: (0, i, 0)), spec, spec, spec],
        out_specs=[spec] * 4, out_shape=[flat] * 4,
        compiler_params=_params(40),
    )(parts.reshape(N_DEV, rows, cols), w.reshape(rows, cols), m.reshape(rows, cols), v.reshape(rows, cols))
    return [o.reshape(shape) for o in outs]


def _adam_replicated(parts, w, m, v):
    r = w.shape[0]

    def body(p_ref, w_ref, m_ref, v_ref, g_ref, d_ref, nm_ref, nv_ref):
        g = p_ref[0]
        for e in range(1, N_DEV):
            g = g + p_ref[e]
        g_ref[...] = g
        d_ref[...], nm_ref[...], nv_ref[...] = _adamw(w_ref[...], g, m_ref[...], v_ref[...])

    spec = _full_spec((r, 128))
    flat = jax.ShapeDtypeStruct((r, 128), F32)
    return pl.pallas_call(
        body, name="adam_replicated", grid=(1,),
        in_specs=[_full_spec((N_DEV, r, 128)), spec, spec, spec],
        out_specs=[spec] * 4, out_shape=[flat] * 4,
        compiler_params=_params(48),
    )(parts, w, m, v)


def _pack(arrays):
    parts = []
    for a in arrays:
        f = a.reshape(-1, 128)
        pad = -f.shape[0] % 8
        parts.append(jnp.pad(f, ((0, pad), (0, 0))) if pad else f)
    return jnp.concatenate(parts, axis=0)


def _unpack(packed, shapes):
    out, row = [], 0
    for shape in shapes:
        r = int(np.prod(shape)) // 128
        out.append(packed[row:row + r].reshape(shape))
        row += r + (-r % 8)
    return out


SHARDED = ("w_in", "w_uq", "w_ukv", "w_mem_kv", "w_gate", "w_branch", "w_out")
REPLICATED = ("g_pre", "a_ln_g", "a_ln_b", "a_w_s", "a_b_s", "q_norm_g", "kv_norm_g", "mem_norm_g", "b_gate", "g_post")
WEIGHT_ORDER = ("g_pre", "w_in", "a_ln_g", "a_ln_b", "a_w_s", "a_b_s", "q_norm_g", "w_uq", "kv_norm_g", "w_ukv",
                "mem_norm_g", "w_mem_kv", "w_gate", "b_gate", "w_branch", "w_out", "g_post")


def _unshard_cols(g):
    return g.transpose(1, 0, 2).reshape(g.shape[1], N_DEV * g.shape[2])


def _shard_cols(full):
    rows, n = full.shape
    return full.reshape(rows, N_DEV, n // N_DEV).transpose(1, 0, 2).astype(BF16)


def kernel(x, mem, positions, g_pre, w_in, a_ln_g, a_ln_b, a_w_s, a_b_s, q_norm_g, w_uq, kv_norm_g, w_ukv, mem_norm_g, w_mem_kv, w_gate, b_gate, w_branch, w_out, g_post, loss_target, m_g_pre, m_w_in, m_a_ln_g, m_a_ln_b, m_a_w_s, m_a_b_s, m_q_norm_g, m_w_uq, m_kv_norm_g, m_w_ukv, m_mem_norm_g, m_w_mem_kv, m_w_gate, m_b_gate, m_w_branch, m_w_out, m_g_post, v_g_pre, v_w_in, v_a_ln_g, v_a_ln_b, v_a_w_s, v_a_b_s, v_q_norm_g, v_w_uq, v_kv_norm_g, v_w_ukv, v_mem_norm_g, v_w_mem_kv, v_w_gate, v_b_gate, v_w_branch, v_w_out, v_g_post):
    weights = dict(g_pre=g_pre, w_in=w_in, a_ln_g=a_ln_g, a_ln_b=a_ln_b, a_w_s=a_w_s, a_b_s=a_b_s, q_norm_g=q_norm_g,
                   w_uq=w_uq, kv_norm_g=kv_norm_g, w_ukv=w_ukv, mem_norm_g=mem_norm_g, w_mem_kv=w_mem_kv,
                   w_gate=w_gate, b_gate=b_gate, w_branch=w_branch, w_out=w_out, g_post=g_post)
    mom1 = dict(g_pre=m_g_pre, w_in=m_w_in, a_ln_g=m_a_ln_g, a_ln_b=m_a_ln_b, a_w_s=m_a_w_s, a_b_s=m_a_b_s,
                q_norm_g=m_q_norm_g, w_uq=m_w_uq, kv_norm_g=m_kv_norm_g, w_ukv=m_w_ukv, mem_norm_g=m_mem_norm_g,
                w_mem_kv=m_w_mem_kv, w_gate=m_w_gate, b_gate=m_b_gate, w_branch=m_w_branch, w_out=m_w_out, g_post=m_g_post)
    mom2 = dict(g_pre=v_g_pre, w_in=v_w_in, a_ln_g=v_a_ln_g, a_ln_b=v_a_ln_b, a_w_s=v_a_w_s, a_b_s=v_a_b_s,
                q_norm_g=v_q_norm_g, w_uq=v_w_uq, kv_norm_g=v_kv_norm_g, w_ukv=v_w_ukv, mem_norm_g=v_mem_norm_g,
                w_mem_kv=v_w_mem_kv, w_gate=v_w_gate, b_gate=v_b_gate, w_branch=v_w_branch, w_out=v_w_out, g_post=v_g_post)
    d = D_MODEL
    t = x.shape[1]
    xs, tgt, mems = x[0], loss_target[0], mem[0]
    pos_col = positions.reshape(t, 1)

    shard16 = {n: weights[n][0].astype(BF16) for n in SHARDED}
    g_in, g_uq, g_ukv = _exchange(_Gather([shard16[n] for n in ("w_in", "w_uq", "w_ukv")]), "gather_weights_in")
    w_in_full = _unshard_cols(g_in)
    lat0, lat1 = 3 * d, 3 * d + Q_LORA + KV_LORA + QK_ROPE
    w_big = jnp.concatenate([w_in_full[:, :lat0], w_in_full[:, lat1:]], axis=1)
    w_lat = jnp.concatenate([w_in_full[:, lat0:lat1], jnp.zeros((d, LAT_W - (lat1 - lat0)), BF16)], axis=1)
    w_uq_p = jnp.pad(_unshard_cols(g_uq).reshape(Q_LORA, MLA_HEADS, QK_DIM),
                     ((0, 0), (0, 0), (0, HEAD_PAD - QK_DIM))).reshape(Q_LORA, MLA_HEADS * HEAD_PAD)
    w_ukv_f = _unshard_cols(g_ukv)

    inv_freq = 1.0 / (ROPE_THETA ** (jnp.arange(0, QK_ROPE, 2, dtype=F32) / QK_ROPE))
    inv_freq_lanes = jnp.concatenate([inv_freq, inv_freq, jnp.zeros((128 - QK_ROPE,), F32)]).reshape(1, 128)
    ws = a_w_s[0]
    b_exp = jnp.broadcast_to(a_b_s[0][:, :, None], (A_GROUPS, CHUNK, 128))

    h = _pre_norm(xs, g_pre)
    proj_big, g_mem, g_gate, g_br, g_out = _mm(
        h, w_big, name="proj_big", tm=1024, tn=1024, tk=2048,
        carry=_Gather([shard16[n] for n in ("w_mem_kv", "w_gate", "w_branch", "w_out")]))
    w_mem_f = _unshard_cols(g_mem)
    w_gate_f = _unshard_cols(g_gate)
    w_br_f = g_br.transpose(1, 0, 2, 3).reshape(3, d, d)
    w_out_f = g_out.reshape(d, d)
    proj_lat = _mm(h, w_lat, name="proj_lat", tm=1024, tn=LAT_W, tk=2048)
    gates = _mm(h, w_gate_f, name="gates", tm=1024, tn=1024, tk=2048, bias=b_gate, act="sigmoid")
    c_tab, sa_tab, sb_tab = _rope_tables(pos_col, inv_freq_lanes)
    cqn, ckvn, kpe = _latent_norms(proj_lat, q_norm_g, kv_norm_g, c_tab, sa_tab, sb_tab)
    q_raw = _mm(cqn, w_uq_p, name="q_up", tm=1024, tn=1024, tk=512, out_dtype=F32)
    q = _rope_q(q_raw, c_tab, sa_tab, sb_tab)
    kv = _mm(ckvn, w_ukv_f, name="kv_up", tm=1024, tn=1024, tk=512)
    o_b, y_b, lse = _mla_fwd(q, kv, kpe, proj_big)
    memn = _mem_norm(mems, mem_norm_g)
    kv_m = _mm(memn, w_mem_f, name="mem_kv", tm=256, tn=1024, tk=2048)
    y_m = _mem_fwd(proj_big, kv_m)
    y_a = _gmlp_fwd(proj_big, a_ln_g, a_ln_b, ws, b_exp)
    ys = (y_a, y_b, y_m)
    ps = [_mm(ys[n], w_br_f[n], name=f"branch{n}", tm=1024, tn=1024, tk=2048) for n in range(3)]
    merged = _merge(gates, *ps)
    out = _mm(merged, w_out_f, name="out_proj", tm=1024, tn=1024, tk=2048, out_dtype=F32)
    d_out, dy, loss_blk, dg_post = _post_loss(out, xs, tgt, g_post)

    dmerged = _mm(d_out, w_out_f, name="d_merged", tb=True, tm=1024, tn=1024, tk=2048)
    dw_out = _mm(merged, d_out, name="dw_out", ta=True, tm=1024, tn=1024, tk=2048)
    dp_a, dp_b, dp_m, dgl, db_gate = _merge_bwd(dmerged, gates, *ps)
    dps = (dp_a, dp_b, dp_m)
    dys = [_mm(dps[n], w_br_f[n], name=f"d_y{n}", tb=True, tm=1024, tn=1024, tk=2048) for n in range(3)]
    dw_br = [_mm(ys[n], dps[n], name=f"dw_branch{n}", ta=True, tm=1024, tn=1024, tk=2048) for n in range(3)]
    dw_gate = _mm(h, dgl, name="dw_gate", ta=True, tm=1024, tn=1024, tk=2048)

    dproj_big, dws, dbs, dlng, dlnb = _gmlp_bwd(proj_big, dys[0], a_ln_g, a_ln_b, ws, b_exp)

    recv = {}
    send = [_shard_cols(dw_gate), jnp.stack(dw_br).reshape(3, N_DEV, d // N_DEV, d).transpose(1, 0, 2, 3),
            dw_out.reshape(N_DEV, d // N_DEV, d)]
    do_b, dproj_big, delta = _mla_gate_bwd(dys[1], proj_big, o_b, dproj_big)
    dq, dkv, dkpe_h, recv["w_gate"], recv["w_branch"], recv["w_out"] = _mla_bwd(
        q, kv, kpe, do_b, lse, delta, _AllToAll(send))
    dq_raw, dkr = _rope_q_bwd(dq, dkpe_h, c_tab, sa_tab, sb_tab)
    dcqn = _mm(dq_raw, w_uq_p, name="d_cq", tb=True, tm=1024, tn=Q_LORA, tk=2048, out_dtype=F32)
    dw_uq_p = _mm(cqn, dq_raw, name="dw_uq", ta=True, tm=Q_LORA, tn=1024, tk=2048)
    dckvn = _mm(dkv, w_ukv_f, name="d_ckv", tb=True, tm=1024, tn=KV_LORA, tk=2048, out_dtype=F32)
    dw_ukv = _mm(ckvn, dkv, name="dw_ukv", ta=True, tm=KV_LORA, tn=1024, tk=2048)
    dproj_lat, dqg, dkg = _latent_norms_bwd(proj_lat, q_norm_g, kv_norm_g, dcqn, dckvn, dkr)

    dproj_big, dkv_m32 = _mem_bwd(proj_big, kv_m, dys[2], dproj_big)
    dkv_m = dkv_m32.astype(BF16)
    dw_mem = _mm(memn, dkv_m, name="dw_mem", ta=True, tm=1024, tn=1024, tk=256)
    dmemn = _mm(dkv_m, w_mem_f, name="d_memn", tb=True, tm=256, tn=1024, tk=2048, out_dtype=F32)
    dg_mem = _mem_norm_bwd(mems, mem_norm_g, dmemn)

    dw_uq_full = dw_uq_p.reshape(Q_LORA, MLA_HEADS, HEAD_PAD)[:, :, :QK_DIM].reshape(Q_LORA, MLA_HEADS * QK_DIM)
    dh, recv["w_uq"], recv["w_ukv"], recv["w_mem_kv"] = _mm(
        dgl, w_gate_f, name="dh_gate", tb=True, tm=1024, tn=1024, tk=2048, out_dtype=F32,
        carry=_AllToAll([_shard_cols(dw_uq_full), _shard_cols(dw_ukv), _shard_cols(dw_mem)]))
    dw_big = _mm(h, dproj_big, name="dw_big", ta=True, tm=1024, tn=1024, tk=2048)
    dw_lat = _mm(h, dproj_lat, name="dw_lat", ta=True, tm=1024, tn=LAT_W, tk=2048)
    dw_in_full = jnp.concatenate([dw_big[:, :lat0], dw_lat[:, :lat1 - lat0], dw_big[:, lat0:]], axis=1)
    dh = _mm(dproj_lat, w_lat, name="dh_lat", tb=True, tm=1024, tn=1024, tk=LAT_W, out_dtype=F32, add=dh)
    dh, recv["w_in"] = _mm(dproj_big, w_big, name="dh_big", tb=True, tm=1024, tn=1024, tk=2048, out_dtype=F32, add=dh,
                           carry=_AllToAll([_shard_cols(dw_in_full)]))
    grad_x, dg_pre = _pre_norm_bwd(xs, g_pre, dh, dy)

    results = {}
    for n in SHARDED:
        results[n] = [r[None] for r in _adam_sharded(recv[n], weights[n][0], mom1[n][0], mom2[n][0], "adam_" + n)]

    small = dict(g_pre=dg_pre, a_ln_g=dlng, a_ln_b=dlnb, a_w_s=dws, a_b_s=dbs, q_norm_g=dqg, kv_norm_g=dkg,
                 mem_norm_g=dg_mem, b_gate=db_gate, g_post=dg_post)
    (parts,) = _exchange(_Gather([_pack([small[n] for n in REPLICATED])]), "gather_small_grads")
    packed = _adam_replicated(parts, _pack([weights[n] for n in REPLICATED]), _pack([mom1[n] for n in REPLICATED]),
                              _pack([mom2[n] for n in REPLICATED]))
    shapes = [weights[n].shape for n in REPLICATED]
    unpacked = [_unpack(p, shapes) for p in packed]
    for i, n in enumerate(REPLICATED):
        results[n] = [u[i] for u in unpacked]

    loss = lax.psum(loss_blk[0, 0], AXES)
    outs = [loss, grad_x[None]]
    for kind in range(4):
        outs += [results[n][kind] for n in WEIGHT_ORDER]
    return tuple(outs)
```

```python
import functools
import math

import jax
import jax.numpy as jnp
import numpy as np
from jax import lax
from jax.experimental import pallas as pl
from jax.experimental.pallas import tpu as pltpu

F32 = jnp.float32
BF16 = jnp.bfloat16
MESH = pl.DeviceIdType.MESH
AXES = ("x", "y", "c")
N_DEV = 8

D_MODEL = 2048
EPS = 1e-6
CHUNK = 128
A_GROUPS = 16
MLA_HEADS = 16
QK_NOPE = 128
QK_ROPE = 64
QK_DIM = QK_NOPE + QK_ROPE
HEAD_PAD = 256
Q_LORA = 512
KV_LORA = 512
MEM_HEADS = 4
MEM_HEAD_DIM = 512
ROPE_THETA = 10000.0
MLA_SCALE = QK_DIM ** -0.5
MEM_SCALE = MEM_HEAD_DIM ** -0.5
NEG = -1e30
LOG2E = 1.4426950408889634

ADAM_LR = 0.001
ADAM_B1 = 0.9
ADAM_B2 = 0.999
ADAM_EPS = 1e-08
ADAM_WD = 0.01
ADAM_STEP = 10

BIG_W = 6 * D_MODEL
LAT_W = Q_LORA + KV_LORA + 128

VMEM_MIB = 1024 * 1024

ROW_BLK = 256
ATT_BLK = 1024
ATT_SUB = 256
GMLP_ROWS = 256
MEM_Q_BLK = 512


def _params(vmem_mib, **kw):
    return pltpu.CompilerParams(vmem_limit_bytes=int(vmem_mib * VMEM_MIB), **kw)


def _gelu(x):
    k = math.sqrt(2.0 / math.pi)
    t = jnp.tanh(k * (x + 0.044715 * (x * x * x)))
    return 0.5 * x * (1.0 + t)


def _gelu_and_grad(x):
    k = math.sqrt(2.0 / math.pi)
    x2 = x * x
    t = jnp.tanh(k * (x + 0.044715 * (x2 * x)))
    val = 0.5 * x * (1.0 + t)
    grad = 0.5 * (1.0 + t) + 0.5 * x * (1.0 - t * t) * (k * (1.0 + 3.0 * 0.044715 * x2))
    return val, grad


def _silu_and_grad(z):
    s = jax.nn.sigmoid(z)
    return z * s, s * (1.0 + z * (1.0 - s))


def _mm(a, b, *, name, tm, tn, tk, ta=False, tb=False, out_dtype=BF16, bias=None, act=None, add=None, carry=None):
    m = a.shape[1] if ta else a.shape[0]
    k = a.shape[0] if ta else a.shape[1]
    n = b.shape[0] if tb else b.shape[1]
    assert k == (b.shape[1] if tb else b.shape[0])
    tm, tn, tk = min(tm, m), min(tn, n), min(tk, k)
    assert m % tm == 0 and n % tn == 0 and k % tk == 0, (name, m, n, k, tm, tn, tk)
    nk = k // tk
    a_spec = pl.BlockSpec((tk, tm), lambda i, j, kk: (kk, i)) if ta else pl.BlockSpec((tm, tk), lambda i, j, kk: (i, kk))
    b_spec = pl.BlockSpec((tn, tk), lambda i, j, kk: (j, kk)) if tb else pl.BlockSpec((tk, tn), lambda i, j, kk: (kk, j))
    dn = (((0 if ta else 1,), (1 if tb else 0,)), ((), ()))
    operands, in_specs = [a, b], [a_spec, b_spec]
    if bias is not None:
        operands.append(bias)
        in_specs.append(pl.BlockSpec((1, tn), lambda i, j, kk: (0, j)))
    if add is not None:
        operands.append(add)
        in_specs.append(pl.BlockSpec((tm, tn), lambda i, j, kk: (i, j)))

    n_in = len(operands)
    n_carry = len(carry.arrays) if carry is not None else 0
    n_acc = 1 if nk > 1 else 0
    grid = (m // tm, n // tn, nk)

    def body(*refs):
        a_ref, b_ref = refs[0], refs[1]
        pos = 2
        bias_ref = add_ref = None
        if bias is not None:
            bias_ref = refs[pos]
            pos += 1
        if add is not None:
            add_ref = refs[pos]
            pos += 1
        o_ref = refs[n_in + n_carry]
        pos = n_in + n_carry
        if carry is not None:
            c_ins = refs[n_in:n_in + n_carry]
            c_outs = refs[n_in + n_carry + 1:n_in + 2 * n_carry + 1]
            c_sems = refs[n_in + 2 * n_carry + 1 + n_acc:]
            ids = [pl.program_id(ax) for ax in range(3)]

            @pl.when((ids[0] == 0) & (ids[1] == 0) & (ids[2] == 0))
            def _():
                carry.start(c_ins, c_outs, c_sems)

        part = lax.dot_general(a_ref[...], b_ref[...], dn, preferred_element_type=F32)

        def finish(acc):
            if bias_ref is not None:
                acc = acc + bias_ref[...]
            if act == "sigmoid":
                acc = jax.nn.sigmoid(acc)
            if add_ref is not None:
                acc = acc + add_ref[...]
            o_ref[...] = acc.astype(o_ref.dtype)

        if nk == 1:
            finish(part)
        else:
            acc_ref = refs[n_in + 2 * n_carry + 1]
            kk = pl.program_id(2)

            @pl.when(kk == 0)
            def _():
                acc_ref[...] = part

            @pl.when(kk > 0)
            def _():
                acc_ref[...] += part

            @pl.when(kk == nk - 1)
            def _():
                finish(acc_ref[...])

        if carry is not None:
            @pl.when((ids[0] == grid[0] - 1) & (ids[1] == grid[1] - 1) & (ids[2] == grid[2] - 1))
            def _():
                carry.finish(c_ins, c_outs, c_sems)

    osz = jnp.dtype(out_dtype).itemsize
    est = 2 * 2 * (tm * tk + tk * tn) + 2 * osz * tm * tn + 8 * tm * tn + (2 * 4 * tm * tn if add is not None else 0)
    main_spec = pl.BlockSpec((tm, tn), lambda i, j, kk: (i, j))
    main_shape = jax.ShapeDtypeStruct((m, n), out_dtype)
    scratch = [pltpu.VMEM((tm, tn), F32)] if nk > 1 else []
    if carry is None:
        return pl.pallas_call(
            body, name=name, grid=grid, in_specs=in_specs, out_specs=main_spec, out_shape=main_shape,
            scratch_shapes=scratch, compiler_params=_params(min(56, est / VMEM_MIB + 12)),
        )(*operands)
    return pl.pallas_call(
        body, name=name, grid=grid,
        in_specs=in_specs + [HBM_SPEC] * n_carry,
        out_specs=[main_spec] + [HBM_SPEC] * n_carry,
        out_shape=[main_shape] + carry.out_shapes,
        scratch_shapes=scratch + carry.sem_shapes,
        compiler_params=_params(min(56, est / VMEM_MIB + 12)),
    )(*operands, *carry.arrays)


def _row_spec(tr, cols, col_blk=0):
    return pl.BlockSpec((tr, cols), lambda i: (i, col_blk))


def _full_spec(shape):
    nd = len(shape)
    return pl.BlockSpec(shape, lambda i: (0,) * nd)


def _pre_norm(x, g_pre, carry):
    t, d = x.shape
    tr = min(ROW_BLK, t)
    n_carry = len(carry.arrays)
    steps = t // tr

    def body(x_ref, g_ref, *rest):
        c_ins, h_ref = rest[:n_carry], rest[n_carry]
        c_outs, c_sems = rest[n_carry + 1:2 * n_carry + 1], rest[2 * n_carry + 1:]

        @pl.when(pl.program_id(0) == 0)
        def _():
            carry.start(c_ins, c_outs, c_sems)

        xv = x_ref[...]
        r = lax.rsqrt(jnp.mean(xv * xv, axis=-1, keepdims=True) + EPS)
        h_ref[...] = ((xv * r) * g_ref[...]).astype(BF16)

        @pl.when(pl.program_id(0) == steps - 1)
        def _():
            carry.finish(c_ins, c_outs, c_sems)

    return pl.pallas_call(
        body, name="pre_norm", grid=(steps,),
        in_specs=[_row_spec(tr, d), _full_spec((1, d))] + [HBM_SPEC] * n_carry,
        out_specs=[_row_spec(tr, d)] + [HBM_SPEC] * n_carry,
        out_shape=[jax.ShapeDtypeStruct((t, d), BF16)] + carry.out_shapes,
        scratch_shapes=carry.sem_shapes,
        compiler_params=_params(32),
    )(x, g_pre, *carry.arrays)


def _rope_tables(pos_col, inv_freq_lanes):
    t = pos_col.shape[0]
    tr = min(ROW_BLK, t)

    def body(p_ref, f_ref, c_ref, sa_ref, sb_ref):
        ang = p_ref[...].astype(F32) * f_ref[...]
        lane = lax.broadcasted_iota(jnp.int32, ang.shape, 1)
        cos, sin = jnp.cos(ang), jnp.sin(ang)
        c_ref[...] = jnp.where(lane < QK_ROPE, cos, 0.0)
        sa_ref[...] = jnp.where(lane < QK_ROPE // 2, sin, 0.0)
        sb_ref[...] = jnp.where((lane >= QK_ROPE // 2) & (lane < QK_ROPE), sin, 0.0)

    tab = jax.ShapeDtypeStruct((t, 128), F32)
    return pl.pallas_call(
        body, name="rope_tables", grid=(t // tr,),
        in_specs=[_row_spec(tr, 1), _full_spec((1, 128))],
        out_specs=[_row_spec(tr, 128)] * 3,
        out_shape=[tab, tab, tab],
    )(pos_col, inv_freq_lanes)


def _rope_fwd(p, c, sa, sb):
    return p * c - pltpu.roll(p, 96, 1) * sa + pltpu.roll(p, 32, 1) * sb


def _rope_bwd(g, c, sa, sb):
    return g * c + pltpu.roll(g, 96, 1) * sa - pltpu.roll(g, 32, 1) * sb


def _rms(xv, g):
    r = lax.rsqrt(jnp.mean(xv * xv, axis=-1, keepdims=True) + EPS)
    return (xv * r) * g


def _rms_bwd(xv, g, dout):
    r = lax.rsqrt(jnp.mean(xv * xv, axis=-1, keepdims=True) + EPS)
    xn = xv * r
    dg = jnp.sum(dout * xn, axis=0, keepdims=True)
    dxn = dout * g
    dx = r * (dxn - xn * jnp.mean(dxn * xn, axis=-1, keepdims=True))
    return dx, dg


def _latent_norms(proj_lat, q_norm_g, kv_norm_g, c_tab, sa_tab, sb_tab):
    t = proj_lat.shape[0]
    tr = min(ROW_BLK, t)

    def body(cq_ref, ckv_ref, kr_ref, qg_ref, kg_ref, c_ref, sa_ref, sb_ref, cqn_ref, ckvn_ref, kpe_ref):
        cqn_ref[...] = _rms(cq_ref[...].astype(F32), qg_ref[...]).astype(BF16)
        ckvn_ref[...] = _rms(ckv_ref[...].astype(F32), kg_ref[...]).astype(BF16)
        kpe_ref[...] = _rope_fwd(kr_ref[...].astype(F32), c_ref[...], sa_ref[...], sb_ref[...]).astype(BF16)

    return pl.pallas_call(
        body, name="latent_norms", grid=(t // tr,),
        in_specs=[_row_spec(tr, Q_LORA, 0), _row_spec(tr, KV_LORA, 1), _row_spec(tr, 128, (Q_LORA + KV_LORA) // 128),
                  _full_spec((1, Q_LORA)), _full_spec((1, KV_LORA)),
                  _row_spec(tr, 128), _row_spec(tr, 128), _row_spec(tr, 128)],
        out_specs=[_row_spec(tr, Q_LORA), _row_spec(tr, KV_LORA), _row_spec(tr, 128)],
        out_shape=[jax.ShapeDtypeStruct((t, Q_LORA), BF16), jax.ShapeDtypeStruct((t, KV_LORA), BF16),
                   jax.ShapeDtypeStruct((t, 128), BF16)],
    )(proj_lat, proj_lat, proj_lat, q_norm_g, kv_norm_g, c_tab, sa_tab, sb_tab)


def _rope_q(q_raw, c_tab, sa_tab, sb_tab):
    t = q_raw.shape[0]
    tr = min(ROW_BLK, t)
    qs = MLA_SCALE * LOG2E

    def body(q_ref, c_ref, sa_ref, sb_ref, o_ref):
        c, sa, sb = c_ref[...], sa_ref[...], sb_ref[...]
        for h in range(MLA_HEADS):
            o_ref[:, h * HEAD_PAD:h * HEAD_PAD + 128] = (q_ref[:, h * HEAD_PAD:h * HEAD_PAD + 128] * qs).astype(BF16)
            pe = q_ref[:, h * HEAD_PAD + 128:(h + 1) * HEAD_PAD]
            o_ref[:, h * HEAD_PAD + 128:(h + 1) * HEAD_PAD] = (_rope_fwd(pe, c, sa, sb) * qs).astype(BF16)

    w = MLA_HEADS * HEAD_PAD
    return pl.pallas_call(
        body, name="rope_q", grid=(t // tr,),
        in_specs=[_row_spec(tr, w), _row_spec(tr, 128), _row_spec(tr, 128), _row_spec(tr, 128)],
        out_specs=_row_spec(tr, w),
        out_shape=jax.ShapeDtypeStruct((t, w), BF16),
        compiler_params=_params(48),
    )(q_raw, c_tab, sa_tab, sb_tab)


def _rope_q_bwd(dq, dkpe, c_tab, sa_tab, sb_tab):
    t = dq.shape[0]
    tr = min(ROW_BLK, t)

    def body(dq_ref, dkp_ref, c_ref, sa_ref, sb_ref, o_ref, dkr_ref):
        c, sa, sb = c_ref[...], sa_ref[...], sb_ref[...]
        for h in range(MLA_HEADS):
            o_ref[:, h * HEAD_PAD:h * HEAD_PAD + 128] = (dq_ref[:, h * HEAD_PAD:h * HEAD_PAD + 128] * MLA_SCALE).astype(BF16)
            g = dq_ref[:, h * HEAD_PAD + 128:(h + 1) * HEAD_PAD] * MLA_SCALE
            o_ref[:, h * HEAD_PAD + 128:(h + 1) * HEAD_PAD] = _rope_bwd(g, c, sa, sb).astype(BF16)
        dkr_ref[...] = _rope_bwd(dkp_ref[...], c, sa, sb)

    w = MLA_HEADS * HEAD_PAD
    return pl.pallas_call(
        body, name="rope_q_bwd", grid=(t // tr,),
        in_specs=[_row_spec(tr, w), _row_spec(tr, 128),
                  _row_spec(tr, 128), _row_spec(tr, 128), _row_spec(tr, 128)],
        out_specs=[_row_spec(tr, w), _row_spec(tr, 128)],
        out_shape=[jax.ShapeDtypeStruct((t, w), BF16), jax.ShapeDtypeStruct((t, 128), F32)],
        compiler_params=_params(48),
    )(dq, dkpe, c_tab, sa_tab, sb_tab)


def _latent_norms_bwd(proj_lat, q_norm_g, kv_norm_g, dcqn, dckvn, dkr):
    t = proj_lat.shape[0]
    tr = min(ROW_BLK, t)

    def body(cq_ref, ckv_ref, qg_ref, kg_ref, dcqn_ref, dckvn_ref, dkr_ref, dl_ref, dqg_ref, dkg_ref):
        dcq, dqg = _rms_bwd(cq_ref[...].astype(F32), qg_ref[...], dcqn_ref[...])
        dckv, dkg = _rms_bwd(ckv_ref[...].astype(F32), kg_ref[...], dckvn_ref[...])
        dl_ref[:, 0:Q_LORA] = dcq.astype(BF16)
        dl_ref[:, Q_LORA:Q_LORA + KV_LORA] = dckv.astype(BF16)
        dl_ref[:, Q_LORA + KV_LORA:LAT_W] = dkr_ref[...].astype(BF16)

        @pl.when(pl.program_id(0) == 0)
        def _():
            dqg_ref[...] = jnp.zeros_like(dqg_ref)
            dkg_ref[...] = jnp.zeros_like(dkg_ref)

        dqg_ref[...] += dqg
        dkg_ref[...] += dkg

    return pl.pallas_call(
        body, name="latent_norms_bwd", grid=(t // tr,),
        in_specs=[_row_spec(tr, Q_LORA, 0), _row_spec(tr, KV_LORA, 1), _full_spec((1, Q_LORA)), _full_spec((1, KV_LORA)),
                  _row_spec(tr, Q_LORA), _row_spec(tr, KV_LORA), _row_spec(tr, 128)],
        out_specs=[_row_spec(tr, LAT_W), _full_spec((1, Q_LORA)), _full_spec((1, KV_LORA))],
        out_shape=[jax.ShapeDtypeStruct((t, LAT_W), BF16), jax.ShapeDtypeStruct((1, Q_LORA), F32),
                   jax.ShapeDtypeStruct((1, KV_LORA), F32)],
    )(proj_lat, proj_lat, q_norm_g, kv_norm_g, dcqn, dckvn, dkr)


def _mem_norm(mem, g):
    m, d = mem.shape

    def body(x_ref, g_ref, o_ref):
        o_ref[...] = _rms(x_ref[...], g_ref[...]).astype(BF16)

    return pl.pallas_call(
        body, name="mem_norm", grid=(1,),
        in_specs=[_full_spec((m, d)), _full_spec((1, d))],
        out_specs=_full_spec((m, d)),
        out_shape=jax.ShapeDtypeStruct((m, d), BF16),
    )(mem, g)


def _mem_norm_bwd(mem, g, dmemn):
    m, d = mem.shape

    def body(x_ref, g_ref, d_ref, dg_ref):
        _, dg = _rms_bwd(x_ref[...], g_ref[...], d_ref[...])
        dg_ref[...] = dg

    return pl.pallas_call(
        body, name="mem_norm_bwd", grid=(1,),
        in_specs=[_full_spec((m, d)), _full_spec((1, d)), _full_spec((m, d))],
        out_specs=_full_spec((1, d)),
        out_shape=jax.ShapeDtypeStruct((1, d), F32),
    )(mem, g, dmemn)


def _merge(gates, p_a, p_b, p_m):
    t, d = p_a.shape
    tr = min(ROW_BLK, t)

    def body(ga_ref, gb_ref, gm_ref, pa_ref, pb_ref, pm_ref, o_ref):
        acc = ga_ref[...].astype(F32) * pa_ref[...].astype(F32)
        acc = acc + gb_ref[...].astype(F32) * pb_ref[...].astype(F32)
        acc = acc + gm_ref[...].astype(F32) * pm_ref[...].astype(F32)
        o_ref[...] = acc.astype(BF16)

    return pl.pallas_call(
        body, name="merge", grid=(t // tr,),
        in_specs=[_row_spec(tr, d, 0), _row_spec(tr, d, 1), _row_spec(tr, d, 2),
                  _row_spec(tr, d), _row_spec(tr, d), _row_spec(tr, d)],
        out_specs=_row_spec(tr, d),
        out_shape=jax.ShapeDtypeStruct((t, d), BF16),
        compiler_params=_params(48),
    )(gates, gates, gates, p_a, p_b, p_m)


def _merge_bwd(dm, gates, p_a, p_b, p_m):
    t, d = dm.shape
    tr = min(ROW_BLK, t)

    def body(dm_ref, g_ref, pa_ref, pb_ref, pm_ref, dpa_ref, dpb_ref, dpm_ref, dgl_ref, db_ref):
        dmv = dm_ref[...].astype(F32)

        @pl.when(pl.program_id(0) == 0)
        def _():
            db_ref[...] = jnp.zeros_like(db_ref)

        for n, (p_ref, dp_ref) in enumerate(((pa_ref, dpa_ref), (pb_ref, dpb_ref), (pm_ref, dpm_ref))):
            g = g_ref[:, n * d:(n + 1) * d].astype(F32)
            dp_ref[...] = (dmv * g).astype(BF16)
            dgl = dmv * p_ref[...].astype(F32) * (g * (1.0 - g))
            dgl_ref[:, n * d:(n + 1) * d] = dgl.astype(BF16)
            db_ref[:, n * d:(n + 1) * d] += jnp.sum(dgl, axis=0, keepdims=True)

    act = jax.ShapeDtypeStruct((t, d), BF16)
    return pl.pallas_call(
        body, name="merge_bwd", grid=(t // tr,),
        in_specs=[_row_spec(tr, d), _row_spec(tr, 3 * d), _row_spec(tr, d), _row_spec(tr, d), _row_spec(tr, d)],
        out_specs=[_row_spec(tr, d), _row_spec(tr, d), _row_spec(tr, d), _row_spec(tr, 3 * d), _full_spec((1, 3 * d))],
        out_shape=[act, act, act, jax.ShapeDtypeStruct((t, 3 * d), BF16), jax.ShapeDtypeStruct((1, 3 * d), F32)],
        compiler_params=_params(56),
    )(dm, gates, p_a, p_b, p_m)


def _post_loss(out, x, tgt, g_post):
    t, d = out.shape
    tr = min(ROW_BLK, t)

    def body(o_ref, x_ref, t_ref, g_ref, do_ref, dy_ref, loss_ref, dg_ref):
        ov = o_ref[...]
        g = g_ref[...]
        r = lax.rsqrt(jnp.mean(ov * ov, axis=-1, keepdims=True) + EPS)
        on = ov * r
        err = (x_ref[...] + on * g) - t_ref[...]
        dy = err * (1.0 / d)
        dy_ref[...] = dy
        don = dy * g
        do_ref[...] = (r * (don - on * jnp.mean(don * on, axis=-1, keepdims=True))).astype(BF16)

        @pl.when(pl.program_id(0) == 0)
        def _():
            loss_ref[...] = jnp.zeros_like(loss_ref)
            dg_ref[...] = jnp.zeros_like(dg_ref)

        loss_ref[...] += 0.5 * jnp.sum(jnp.mean(err * err, axis=-1, keepdims=True))
        dg_ref[...] += jnp.sum(dy * on, axis=0, keepdims=True)

    return pl.pallas_call(
        body, name="post_loss", grid=(t // tr,),
        in_specs=[_row_spec(tr, d), _row_spec(tr, d), _row_spec(tr, d), _full_spec((1, d))],
        out_specs=[_row_spec(tr, d), _row_spec(tr, d), _full_spec((8, 128)), _full_spec((1, d))],
        out_shape=[jax.ShapeDtypeStruct((t, d), BF16), jax.ShapeDtypeStruct((t, d), F32),
                   jax.ShapeDtypeStruct((8, 128), F32), jax.ShapeDtypeStruct((1, d), F32)],
        compiler_params=_params(56),
    )(out, x, tgt, g_post)


def _pre_norm_bwd(x, g_pre, dh, dy):
    t, d = x.shape
    tr = min(ROW_BLK, t)

    def body(x_ref, g_ref, dh_ref, dy_ref, dx_ref, dg_ref):
        dx, dg = _rms_bwd(x_ref[...], g_ref[...], dh_ref[...])
        dx_ref[...] = dx + dy_ref[...]

        @pl.when(pl.program_id(0) == 0)
        def _():
            dg_ref[...] = jnp.zeros_like(dg_ref)

        dg_ref[...] += dg

    return pl.pallas_call(
        body, name="pre_norm_bwd", grid=(t // tr,),
        in_specs=[_row_spec(tr, d), _full_spec((1, d)), _row_spec(tr, d), _row_spec(tr, d)],
        out_specs=[_row_spec(tr, d), _full_spec((1, d))],
        out_shape=[jax.ShapeDtypeStruct((t, d), F32), jax.ShapeDtypeStruct((1, d), F32)],
        compiler_params=_params(56),
    )(x, g_pre, dh, dy)


def _causal_mask(n):
    row = lax.broadcasted_iota(jnp.int32, (n, n), 0)
    col = lax.broadcasted_iota(jnp.int32, (n, n), 1)
    return row >= col


def _layernorm_stats(vg):
    mu = jnp.mean(vg, axis=-1, keepdims=True)
    cen = vg - mu
    rstd = lax.rsqrt(jnp.mean(cen * cen, axis=-1, keepdims=True) + EPS)
    return cen * rstd, rstd


def _gmlp_fwd(proj_big, ln_g, ln_b, w_s, b_exp):
    t = proj_big.shape[0]
    rows = min(GMLP_ROWS, t)
    d = D_MODEL

    def body(u_ref, v_ref, z_ref, lg_ref, lb_ref, ws_ref, be_ref, y_ref, vn_scr):
        vhat, _ = _layernorm_stats(_gelu(v_ref[...].astype(F32)))
        vn_scr[...] = (vhat * lg_ref[...] + lb_ref[...]).astype(BF16)
        mask = _causal_mask(CHUNK)
        for g in range(A_GROUPS):
            cols = slice(g * 128, (g + 1) * 128)
            wsm = jnp.where(mask, ws_ref[g], 0.0).astype(BF16)
            for c in range(rows // CHUNK):
                rws = slice(c * CHUNK, (c + 1) * CHUNK)
                sv = jnp.dot(wsm, vn_scr[rws, cols], preferred_element_type=F32) + be_ref[g]
                zs, _ = _silu_and_grad(z_ref[rws, cols].astype(F32))
                y_ref[rws, cols] = (_gelu(u_ref[rws, cols].astype(F32)) * sv * zs).astype(BF16)

    return pl.pallas_call(
        body, name="gmlp_fwd", grid=(t // rows,),
        in_specs=[_row_spec(rows, d, 0), _row_spec(rows, d, 1), _row_spec(rows, d, 2),
                  _full_spec((1, d)), _full_spec((1, d)), _full_spec((A_GROUPS, CHUNK, CHUNK)),
                  _full_spec((A_GROUPS, CHUNK, 128))],
        out_specs=_row_spec(rows, d),
        out_shape=jax.ShapeDtypeStruct((t, d), BF16),
        scratch_shapes=[pltpu.VMEM((rows, d), BF16)],
        compiler_params=_params(40),
    )(proj_big, proj_big, proj_big, ln_g, ln_b, w_s, b_exp)


def _gmlp_bwd(proj_big, dya, ln_g, ln_b, w_s, b_exp):
    t = proj_big.shape[0]
    rows = min(GMLP_ROWS, t)
    d = D_MODEL
    nt = (((1,), (1,)), ((), ()))
    tn = (((0,), (0,)), ((), ()))

    def body(u_ref, v_ref, z_ref, dy_ref, lg_ref, lb_ref, ws_ref, be_ref,
             dp_ref, dws_ref, dbs_ref, dlg_ref, dlb_ref, vn_scr, dvn_scr):
        @pl.when(pl.program_id(0) == 0)
        def _():
            dws_ref[...] = jnp.zeros_like(dws_ref)
            dbs_ref[...] = jnp.zeros_like(dbs_ref)
            dlg_ref[...] = jnp.zeros_like(dlg_ref)
            dlb_ref[...] = jnp.zeros_like(dlb_ref)

        vg, vgrad = _gelu_and_grad(v_ref[...].astype(F32))
        vhat, rstd = _layernorm_stats(vg)
        vn_scr[...] = (vhat * lg_ref[...] + lb_ref[...]).astype(BF16)
        mask = _causal_mask(CHUNK)
        for g in range(A_GROUPS):
            cols = slice(g * 128, (g + 1) * 128)
            wsm = jnp.where(mask, ws_ref[g], 0.0).astype(BF16)
            dws = jnp.zeros((CHUNK, CHUNK), F32)
            dbs = jnp.zeros((CHUNK, 1), F32)
            for c in range(rows // CHUNK):
                rws = slice(c * CHUNK, (c + 1) * CHUNK)
                vn = vn_scr[rws, cols]
                sv = jnp.dot(wsm, vn, preferred_element_type=F32) + be_ref[g]
                ug, ugrad = _gelu_and_grad(u_ref[rws, cols].astype(F32))
                zs, zgrad = _silu_and_grad(z_ref[rws, cols].astype(F32))
                dya = dy_ref[rws, cols].astype(F32)
                dga = dya * zs
                dp_ref[rws, 2 * d + g * 128:2 * d + (g + 1) * 128] = (dya * (ug * sv) * zgrad).astype(BF16)
                dp_ref[rws, cols] = (dga * sv * ugrad).astype(BF16)
                dsv = dga * ug
                dsv16 = dsv.astype(BF16)
                dws = dws + lax.dot_general(dsv16, vn, nt, preferred_element_type=F32)
                dbs = dbs + jnp.sum(dsv, axis=-1, keepdims=True)
                dvn_scr[rws, cols] = lax.dot_general(wsm, dsv16, tn, preferred_element_type=F32)
            dws_ref[g] += jnp.where(mask, dws, 0.0)
            dbs_ref[g] += dbs
        dvn = dvn_scr[...]
        dlg_ref[...] += jnp.sum(dvn * vhat, axis=0, keepdims=True)
        dlb_ref[...] += jnp.sum(dvn, axis=0, keepdims=True)
        dvh = dvn * lg_ref[...]
        dvg = rstd * (dvh - jnp.mean(dvh, axis=-1, keepdims=True) - vhat * jnp.mean(dvh * vhat, axis=-1, keepdims=True))
        dp_ref[:, d:2 * d] = (dvg * vgrad).astype(BF16)

    return pl.pallas_call(
        body, name="gmlp_bwd", grid=(t // rows,),
        in_specs=[_row_spec(rows, d, 0), _row_spec(rows, d, 1), _row_spec(rows, d, 2), _row_spec(rows, d),
                  _full_spec((1, d)), _full_spec((1, d)), _full_spec((A_GROUPS, CHUNK, CHUNK)),
                  _full_spec((A_GROUPS, CHUNK, 128))],
        out_specs=[_row_spec(rows, 3 * d), _full_spec((A_GROUPS, CHUNK, CHUNK)), _full_spec((A_GROUPS, CHUNK, 1)),
                   _full_spec((1, d)), _full_spec((1, d))],
        out_shape=[jax.ShapeDtypeStruct((t, 6 * d), BF16), jax.ShapeDtypeStruct((A_GROUPS, CHUNK, CHUNK), F32),
                   jax.ShapeDtypeStruct((A_GROUPS, CHUNK, 1), F32), jax.ShapeDtypeStruct((1, d), F32),
                   jax.ShapeDtypeStruct((1, d), F32)],
        scratch_shapes=[pltpu.VMEM((rows, d), BF16), pltpu.VMEM((rows, d), F32)],
        compiler_params=_params(48),
    )(proj_big, proj_big, proj_big, dya, ln_g, ln_b, w_s, b_exp)


NT_DIMS = (((1,), (1,)), ((), ()))
TN_DIMS = (((0,), (0,)), ((), ()))


def _mla_fwd(q, kv, kpe, proj_big):
    t = q.shape[0]
    blk = min(ATT_BLK, t)
    nq = t // blk
    zb_blk0 = (3 * D_MODEL) // 128

    nc = blk // 128

    sub = min(ATT_SUB, blk)

    def body(q_ref, kv_ref, kp_ref, zb_ref, o_ref, yb_ref, lse_ref, m_scr, acc_scr):
        i = pl.program_id(1)
        qv = q_ref[...]
        m_scr[...] = jnp.full_like(m_scr, NEG)
        acc_scr[...] = jnp.zeros_like(acc_scr)
        ones = jnp.ones((blk, 128), BF16)

        def step(j, masked):
            ks = pl.ds(pl.multiple_of(j * blk, blk), blk)
            kc = jnp.concatenate([kv_ref[ks, 0:128], kp_ref[ks, :]], axis=1)
            vext = jnp.concatenate([kv_ref[ks, 128:256], ones], axis=1)
            for r in range(blk // sub):
                rows = slice(r * sub, (r + 1) * sub)
                kw = (r + 1) * sub if masked else blk
                tt = lax.dot_general(qv[rows], kc[:kw], NT_DIMS, preferred_element_type=F32)
                if masked:
                    row = lax.broadcasted_iota(jnp.int32, (sub, kw), 0) + r * sub
                    col = lax.broadcasted_iota(jnp.int32, (sub, kw), 1)
                    tt = jnp.where(row >= col, tt, NEG)
                cm = tt[:, 0:128]
                for c in range(1, kw // 128):
                    cm = jnp.maximum(cm, tt[:, c * 128:(c + 1) * 128])
                m_prev = m_scr[rows, :]
                m_new = jnp.maximum(m_prev, jnp.max(cm, axis=-1, keepdims=True))
                alpha = jnp.exp2(m_prev - m_new)
                m_scr[rows, :] = m_new
                p = jnp.concatenate([jnp.exp2(tt[:, c * 128:(c + 1) * 128] - m_new).astype(BF16)
                                     for c in range(kw // 128)], axis=1)
                pv = jnp.dot(p, vext[:kw], preferred_element_type=F32)
                acc_scr[rows, :] = jnp.concatenate([alpha, alpha], axis=1) * acc_scr[rows, :] + pv

        def loop_body(j, carry):
            step(j, False)
            return carry

        lax.fori_loop(0, i, loop_body, 0)
        step(i, True)
        l = acc_scr[:, 128:256]
        o = acc_scr[:, 0:128] / l
        o_ref[...] = o.astype(BF16)
        zs, _ = _silu_and_grad(zb_ref[...].astype(F32))
        yb_ref[...] = (o * zs).astype(BF16)
        lse_ref[0] = m_scr[...] + jnp.log2(l)

    act = jax.ShapeDtypeStruct((t, D_MODEL), BF16)
    return pl.pallas_call(
        body, name="mla_fwd", grid=(MLA_HEADS, nq),
        in_specs=[pl.BlockSpec((blk, HEAD_PAD), lambda h, i: (i, h)),
                  pl.BlockSpec((t, HEAD_PAD), lambda h, i: (0, h)),
                  pl.BlockSpec((t, 128), lambda h, i: (0, 0)),
                  pl.BlockSpec((blk, 128), lambda h, i: (i, zb_blk0 + h))],
        out_specs=[pl.BlockSpec((blk, 128), lambda h, i: (i, h)),
                   pl.BlockSpec((blk, 128), lambda h, i: (i, h)),
                   pl.BlockSpec((1, blk, 128), lambda h, i: (h, i, 0))],
        out_shape=[act, act, jax.ShapeDtypeStruct((MLA_HEADS, t, 128), F32)],
        scratch_shapes=[pltpu.VMEM((blk, 128), F32), pltpu.VMEM((blk, HEAD_PAD), F32)],
        compiler_params=_params(56),
    )(q, kv, kpe, proj_big)


def _mla_gate_bwd(dyb, proj_big, o, dproj):
    t, d = dyb.shape
    tr = min(ROW_BLK, t)

    def body(dy_ref, zb_ref, o_ref, buf_ref, do_ref, dz_ref, dl_ref):
        del buf_ref
        dy = dy_ref[...].astype(F32)
        ov = o_ref[...].astype(F32)
        zs, zgrad = _silu_and_grad(zb_ref[...].astype(F32))
        do16 = (dy * zs).astype(BF16)
        do_ref[...] = do16
        dz_ref[...] = (dy * ov * zgrad).astype(BF16)
        prod = do16.astype(F32) * ov
        for h in range(MLA_HEADS):
            delta = jnp.sum(prod[:, h * 128:(h + 1) * 128], axis=-1, keepdims=True)
            dl_ref[h] = jnp.broadcast_to(delta, (tr, 128))

    act = jax.ShapeDtypeStruct((t, d), BF16)
    head_spec = pl.BlockSpec((MLA_HEADS, tr, 128), lambda i: (0, i, 0))
    return pl.pallas_call(
        body, name="mla_gate_bwd", grid=(t // tr,),
        in_specs=[_row_spec(tr, d), _row_spec(tr, d, 3), _row_spec(tr, d), HBM_SPEC],
        out_specs=[_row_spec(tr, d), _row_spec(tr, d, 3), head_spec],
        out_shape=[act, jax.ShapeDtypeStruct((t, 6 * d), BF16), jax.ShapeDtypeStruct((MLA_HEADS, t, 128), F32)],
        input_output_aliases={3: 1},
        compiler_params=_params(56),
    )(dyb, proj_big, o, dproj)


def _mla_bwd(q, kv, kpe, do, lse, delta, carry):
    t = q.shape[0]
    blk = min(ATT_BLK, t)
    n = t // blk
    nc = blk // 128
    pairs = [(j, i) for j in range(n) for i in range(j, n)]
    j_tab = jnp.asarray([p[0] for p in pairs], jnp.int32)
    i_tab = jnp.asarray([p[1] for p in pairs], jnp.int32)
    n_carry = len(carry.arrays)
    sub = min(ATT_SUB, blk)

    def body(j_ref, i_ref, q_ref, do_ref, lse_ref, dl_ref, kv_ref, kp_ref, *rest):
        c_ins, rest = rest[:n_carry], rest[n_carry:]
        dq_ref, dkv_ref, dkp_ref = rest[:3]
        c_outs, rest = rest[3:3 + n_carry], rest[3 + n_carry:]
        dk_scr, dv_scr = rest[:2]
        c_sems = rest[2:]
        head = pl.program_id(0)
        step = pl.program_id(1)
        j, i = j_ref[step], i_ref[step]

        @pl.when((head == 0) & (step == 0))
        def _():
            carry.start(c_ins, c_outs, c_sems)

        @pl.when(step == 0)
        def _():
            dq_ref[...] = jnp.zeros_like(dq_ref)

        @pl.when(i == j)
        def _():
            dk_scr[...] = jnp.zeros_like(dk_scr)
            dv_scr[...] = jnp.zeros_like(dv_scr)

        kc = jnp.concatenate([kv_ref[:, 0:128], kp_ref[...]], axis=1)
        vv = kv_ref[:, 128:256]

        def tile(diag):
            for r in range(blk // sub):
                rows = slice(r * sub, (r + 1) * sub)
                kw = (r + 1) * sub if diag else blk
                qv, dov = q_ref[rows, :], do_ref[rows, :]
                tt = lax.dot_general(qv, kc[:kw], NT_DIMS, preferred_element_type=F32)
                if diag:
                    row = lax.broadcasted_iota(jnp.int32, (sub, kw), 0) + r * sub
                    col = lax.broadcasted_iota(jnp.int32, (sub, kw), 1)
                    tt = jnp.where(row >= col, tt, NEG)
                dp = lax.dot_general(dov, vv[:kw], NT_DIMS, preferred_element_type=F32)
                lse_v, dl_v = lse_ref[0, rows, :], dl_ref[0, rows, :]
                ps, dss = [], []
                for c in range(kw // 128):
                    cols = slice(c * 128, (c + 1) * 128)
                    p = jnp.exp2(tt[:, cols] - lse_v)
                    ps.append(p.astype(BF16))
                    dss.append((p * (dp[:, cols] - dl_v)).astype(BF16))
                p16 = jnp.concatenate(ps, axis=1)
                ds16 = jnp.concatenate(dss, axis=1)
                dv_scr[0:kw, :] += lax.dot_general(p16, dov, TN_DIMS, preferred_element_type=F32)
                dk_scr[0:kw, :] += lax.dot_general(ds16, qv, TN_DIMS, preferred_element_type=F32)
                qs = pl.ds(pl.multiple_of(i * blk + r * sub, sub), sub)
                dq_ref[qs, :] += jnp.dot(ds16, kc[:kw], preferred_element_type=F32)

        @pl.when(i == j)
        def _():
            tile(True)

        @pl.when(i > j)
        def _():
            tile(False)

        @pl.when(i == n - 1)
        def _():
            dkv_ref[:, 0:128] = (dk_scr[:, 0:128] * (1.0 / LOG2E)).astype(BF16)
            dkv_ref[:, 128:256] = dv_scr[...].astype(BF16)
            ks = pl.ds(pl.multiple_of(j * blk, blk), blk)
            dkp = dk_scr[:, 128:256] * (1.0 / LOG2E)

            @pl.when(head == 0)
            def _():
                dkp_ref[ks, :] = dkp

            @pl.when(head > 0)
            def _():
                dkp_ref[ks, :] += dkp

        @pl.when((head == MLA_HEADS - 1) & (step == len(pairs) - 1))
        def _():
            carry.finish(c_ins, c_outs, c_sems)

    grid_spec = pltpu.PrefetchScalarGridSpec(
        num_scalar_prefetch=2, grid=(MLA_HEADS, len(pairs)),
        in_specs=[pl.BlockSpec((blk, HEAD_PAD), lambda h, s, jt, it: (it[s], h)),
                  pl.BlockSpec((blk, 128), lambda h, s, jt, it: (it[s], h)),
                  pl.BlockSpec((1, blk, 128), lambda h, s, jt, it: (h, it[s], 0)),
                  pl.BlockSpec((1, blk, 128), lambda h, s, jt, it: (h, it[s], 0)),
                  pl.BlockSpec((blk, HEAD_PAD), lambda h, s, jt, it: (jt[s], h)),
                  pl.BlockSpec((blk, 128), lambda h, s, jt, it: (jt[s], 0))] + [HBM_SPEC] * n_carry,
        out_specs=[pl.BlockSpec((t, HEAD_PAD), lambda h, s, jt, it: (0, h)),
                   pl.BlockSpec((blk, HEAD_PAD), lambda h, s, jt, it: (jt[s], h)),
                   pl.BlockSpec((t, 128), lambda h, s, jt, it: (0, 0))] + [HBM_SPEC] * n_carry,
        scratch_shapes=[pltpu.VMEM((blk, HEAD_PAD), F32), pltpu.VMEM((blk, 128), F32)] + carry.sem_shapes,
    )
    return pl.pallas_call(
        body, name="mla_bwd", grid_spec=grid_spec,
        out_shape=[jax.ShapeDtypeStruct((t, MLA_HEADS * HEAD_PAD), F32),
                   jax.ShapeDtypeStruct((t, 2 * D_MODEL), BF16),
                   jax.ShapeDtypeStruct((t, 128), F32)] + carry.out_shapes,
        compiler_params=_params(58),
    )(j_tab, i_tab, q, do, lse, delta, kv, kpe, *carry.arrays)


def _mem_attn_probs(qv, k_ref):
    s = lax.dot_general(qv, k_ref[...], NT_DIMS, preferred_element_type=F32) * MEM_SCALE
    e = jnp.exp(s - jnp.max(s, axis=-1, keepdims=True))
    return e / jnp.sum(e, axis=-1, keepdims=True)


def _mem_fwd(proj_big, kv_m):
    t = proj_big.shape[0]
    tq = min(MEM_Q_BLK, t)
    hd = MEM_HEAD_DIM
    q0, z0 = (4 * D_MODEL) // hd, (5 * D_MODEL) // hd

    def body(q_ref, z_ref, k_ref, v_ref, y_ref):
        p = _mem_attn_probs(q_ref[...], k_ref)
        o = jnp.dot(p.astype(BF16), v_ref[...], preferred_element_type=F32)
        zs, _ = _silu_and_grad(z_ref[...].astype(F32))
        y_ref[...] = (o * zs).astype(BF16)

    return pl.pallas_call(
        body, name="mem_fwd", grid=(t // tq, MEM_HEADS),
        in_specs=[pl.BlockSpec((tq, hd), lambda i, h: (i, q0 + h)), pl.BlockSpec((tq, hd), lambda i, h: (i, z0 + h)),
                  pl.BlockSpec((kv_m.shape[0], hd), lambda i, h: (0, h)),
                  pl.BlockSpec((kv_m.shape[0], hd), lambda i, h: (0, MEM_HEADS + h))],
        out_specs=pl.BlockSpec((tq, hd), lambda i, h: (i, h)),
        out_shape=jax.ShapeDtypeStruct((t, D_MODEL), BF16),
    )(proj_big, proj_big, kv_m, kv_m)


def _mem_bwd(proj_big, kv_m, dym, dproj):
    t = proj_big.shape[0]
    tq = min(ROW_BLK, t)
    hd = MEM_HEAD_DIM
    d = D_MODEL
    mlen = kv_m.shape[0]

    def body(q_ref, z_ref, kv_ref, dy_ref, buf_ref, dqz_ref, dkv_ref):
        del buf_ref

        @pl.when(pl.program_id(0) == 0)
        def _():
            dkv_ref[...] = jnp.zeros_like(dkv_ref)

        for h in range(MEM_HEADS):
            cols = slice(h * hd, (h + 1) * hd)
            k_ref, v_ref = kv_ref.at[:, cols], kv_ref.at[:, d + h * hd:d + (h + 1) * hd]
            qv = q_ref[:, cols]
            p = _mem_attn_probs(qv, k_ref)
            p16 = p.astype(BF16)
            o = jnp.dot(p16, v_ref[...], preferred_element_type=F32)
            zs, zgrad = _silu_and_grad(z_ref[:, cols].astype(F32))
            dy = dy_ref[:, cols].astype(F32)
            dqz_ref[:, d + h * hd:d + (h + 1) * hd] = (dy * o * zgrad).astype(BF16)
            do16 = (dy * zs).astype(BF16)
            dkv_ref[:, d + h * hd:d + (h + 1) * hd] += lax.dot_general(p16, do16, TN_DIMS, preferred_element_type=F32)
            dp = lax.dot_general(do16, v_ref[...], NT_DIMS, preferred_element_type=F32)
            ds = (p * (dp - jnp.sum(dp * p, axis=-1, keepdims=True)) * MEM_SCALE).astype(BF16)
            dqz_ref[:, cols] = jnp.dot(ds, k_ref[...], preferred_element_type=F32).astype(BF16)
            dkv_ref[:, cols] += lax.dot_general(ds, qv, TN_DIMS, preferred_element_type=F32)

    return pl.pallas_call(
        body, name="mem_bwd", grid=(t // tq,),
        in_specs=[_row_spec(tq, d, 4), _row_spec(tq, d, 5), _full_spec((mlen, 2 * d)), _row_spec(tq, d), HBM_SPEC],
        out_specs=[_row_spec(tq, 2 * d, 2), _full_spec((mlen, 2 * d))],
        out_shape=[jax.ShapeDtypeStruct((t, 6 * d), BF16), jax.ShapeDtypeStruct((mlen, 2 * d), F32)],
        input_output_aliases={4: 0},
        compiler_params=_params(40),
    )(proj_big, proj_big, kv_m, dym, dproj)


HBM_SPEC = pl.BlockSpec(memory_space=pl.ANY)
N_PEERS = N_DEV - 1


def _dev_index(px, py, pc):
    return 4 * px + 2 * py + pc


class _Gather:
    def __init__(self, arrays):
        self.arrays = list(arrays)
        n = len(self.arrays)
        self.out_shapes = [jax.ShapeDtypeStruct((N_DEV,) + a.shape, a.dtype) for a in self.arrays]
        self.sem_shapes = [pltpu.SemaphoreType.DMA((n * N_PEERS,)), pltpu.SemaphoreType.DMA((n * N_PEERS,)),
                           pltpu.SemaphoreType.DMA((n,))]

    def _parts(self, ins, outs, sems):
        n = len(self.arrays)
        send_sems, recv_sems, local_sems = sems
        x, y, c = lax.axis_index("x"), lax.axis_index("y"), lax.axis_index("c")
        me, sibling = (x, y, c), (x, y, 1 - c)
        chips = [(1 - x, y), (x, 1 - y), (1 - x, 1 - y)]

        def copy(a, k, block, to, src=None):
            dst = outs[a].at[_dev_index(*block)]
            return pltpu.make_async_remote_copy(
                src_ref=dst if src is None else src, dst_ref=dst,
                send_sem=send_sems.at[a * N_PEERS + k], recv_sem=recv_sems.at[a * N_PEERS + k],
                device_id=to, device_id_type=MESH)

        mine = [pltpu.make_async_copy(ins[a], outs[a].at[_dev_index(*me)], local_sems.at[a]) for a in range(n)]
        first = []
        for a in range(n):
            first.append(copy(a, 0, me, sibling, src=ins[a]))
            first += [copy(a, 1 + j, me, (*chip, c), src=ins[a]) for j, chip in enumerate(chips)]
        return n, c, me, sibling, chips, copy, mine, first

    def start(self, ins, outs, sems):
        _, _, _, _, _, _, mine, first = self._parts(ins, outs, sems)
        for cp in mine + first:
            cp.start()

    def finish(self, ins, outs, sems):
        n, c, me, sibling, chips, copy, mine, first = self._parts(ins, outs, sems)
        passed = []
        for j, chip in enumerate(chips):
            for a in range(n):
                copy(a, 1 + j, (*chip, c), me).wait_recv()
                fwd = copy(a, 4 + j, (*chip, c), sibling)
                fwd.start()
                passed.append(fwd)
        for a in range(n):
            copy(a, 0, sibling, me).wait_recv()
            for j, chip in enumerate(chips):
                copy(a, 4 + j, (*chip, 1 - c), me).wait_recv()
        for cp in first + passed:
            cp.wait_send()
        for cp in mine:
            cp.wait()


class _AllToAll:
    def __init__(self, arrays):
        self.arrays = list(arrays)
        n = len(self.arrays)
        self.out_shapes = [jax.ShapeDtypeStruct(a.shape, a.dtype) for a in self.arrays]
        self.sem_shapes = [pltpu.SemaphoreType.DMA((n * N_PEERS,)), pltpu.SemaphoreType.DMA((n * N_PEERS,)),
                           pltpu.SemaphoreType.DMA((n,))]

    def _parts(self, ins, outs, sems):
        n = len(self.arrays)
        send_sems, recv_sems, local_sems = sems
        x, y, c = lax.axis_index("x"), lax.axis_index("y"), lax.axis_index("c")
        my_idx = _dev_index(x, y, c)
        peers = []
        for k in range(1, N_DEV):
            dx, dy, dc = (k >> 2) & 1, (k >> 1) & 1, k & 1
            peers.append((1 - x if dx else x, 1 - y if dy else y, 1 - c if dc else c))

        def copy(a, k, peer):
            return pltpu.make_async_remote_copy(
                src_ref=ins[a].at[_dev_index(*peer)], dst_ref=outs[a].at[my_idx],
                send_sem=send_sems.at[a * N_PEERS + k], recv_sem=recv_sems.at[a * N_PEERS + k],
                device_id=peer, device_id_type=MESH)

        def landed(a, k, peer):
            slot = outs[a].at[_dev_index(*peer)]
            return pltpu.make_async_remote_copy(
                src_ref=slot, dst_ref=slot,
                send_sem=send_sems.at[a * N_PEERS + k], recv_sem=recv_sems.at[a * N_PEERS + k],
                device_id=peer, device_id_type=MESH)

        mine = [pltpu.make_async_copy(ins[a].at[my_idx], outs[a].at[my_idx], local_sems.at[a]) for a in range(n)]
        sends = [copy(a, k, peer) for a in range(n) for k, peer in enumerate(peers)]
        return n, peers, landed, mine, sends

    def start(self, ins, outs, sems):
        _, _, _, mine, sends = self._parts(ins, outs, sems)
        for cp in mine + sends:
            cp.start()

    def finish(self, ins, outs, sems):
        n, peers, landed, mine, sends = self._parts(ins, outs, sems)
        for a in range(n):
            for k, peer in enumerate(peers):
                landed(a, k, peer).wait_recv()
        for cp in sends:
            cp.wait_send()
        for cp in mine:
            cp.wait()


def _exchange(plan, name):
    n = len(plan.arrays)

    def body(*refs):
        ins, outs, sems = refs[:n], refs[n:2 * n], refs[2 * n:]
        plan.start(ins, outs, sems)
        plan.finish(ins, outs, sems)

    return pl.pallas_call(
        body, name=name, in_specs=[HBM_SPEC] * n, out_specs=[HBM_SPEC] * n,
        out_shape=plan.out_shapes, scratch_shapes=plan.sem_shapes,
    )(*plan.arrays)


def _adamw(w, g, m, v):
    m = ADAM_B1 * m + (1.0 - ADAM_B1) * g
    v = ADAM_B2 * v + (1.0 - ADAM_B2) * jnp.square(g)
    m_hat = m / (1.0 - ADAM_B1 ** ADAM_STEP)
    v_hat = v / (1.0 - ADAM_B2 ** ADAM_STEP)
    delta = -ADAM_LR * (m_hat / (jnp.sqrt(v_hat) + ADAM_EPS) + ADAM_WD * w)
    return delta, m, v


def _adam_sharded(parts_list, w, m, v, name):
    shape = w.shape
    cols = shape[-1]
    rows = int(np.prod(shape[:-1]))
    tr = min(128, rows)
    parts_list = [p.reshape(N_DEV, -1, cols) for p in parts_list]
    bounds = np.cumsum([0] + [p.shape[1] // tr for p in parts_list])
    assert rows % tr == 0 and all(p.shape[1] % tr == 0 for p in parts_list) and bounds[-1] == rows // tr
    n_parts = len(parts_list)

    def body(*refs):
        p_refs = refs[:n_parts]
        w_ref, m_ref, v_ref, g_ref, d_ref, nm_ref, nv_ref = refs[n_parts:]
        i = pl.program_id(0)
        for k, p_ref in enumerate(p_refs):
            @pl.when((i >= bounds[k]) & (i < bounds[k + 1]))
            def _():
                g = p_ref[0].astype(F32)
                for e in range(1, N_DEV):
                    g = g + p_ref[e].astype(F32)
                g_ref[...] = g
                d_ref[...], nm_ref[...], nv_ref[...] = _adamw(w_ref[...], g, m_ref[...], v_ref[...])

    def part_spec(k):
        lo, hi = int(bounds[k]), int(bounds[k + 1])
        return pl.BlockSpec((N_DEV, tr, cols), lambda i: (0, jnp.clip(i, lo, hi - 1) - lo, 0))

    spec = pl.BlockSpec((tr, cols), lambda i: (i, 0))
    flat = jax.ShapeDtypeStruct((rows, cols), F32)
    outs = pl.pallas_call(
        body, name=name, grid=(rows // tr,),
        in_specs=[part_spec(k) for k in range(n_parts)] + [spec, spec, spec],
        out_specs=[spec] * 4, out_shape=[flat] * 4,
        compiler_params=_params(40),
    )(*parts_list, w.reshape(rows, cols), m.reshape(rows, cols), v.reshape(rows, cols))
    return [o.reshape(shape) for o in outs]


def _adam_replicated(parts, w, m, v):
    r = w.shape[0]

    def body(p_ref, w_ref, m_ref, v_ref, g_ref, d_ref, nm_ref, nv_ref):
        g = p_ref[0]
        for e in range(1, N_DEV):
            g = g + p_ref[e]
        g_ref[...] = g
        d_ref[...], nm_ref[...], nv_ref[...] = _adamw(w_ref[...], g, m_ref[...], v_ref[...])

    spec = _full_spec((r, 128))
    flat = jax.ShapeDtypeStruct((r, 128), F32)
    return pl.pallas_call(
        body, name="adam_replicated", grid=(1,),
        in_specs=[_full_spec((N_DEV, r, 128)), spec, spec, spec],
        out_specs=[spec] * 4, out_shape=[flat] * 4,
        compiler_params=_params(48),
    )(parts, w, m, v)


def _pack(arrays):
    parts = []
    for a in arrays:
        f = a.reshape(-1, 128)
        pad = -f.shape[0] % 8
        parts.append(jnp.pad(f, ((0, pad), (0, 0))) if pad else f)
    return jnp.concatenate(parts, axis=0)


def _unpack(packed, shapes):
    out, row = [], 0
    for shape in shapes:
        r = int(np.prod(shape)) // 128
        out.append(packed[row:row + r].reshape(shape))
        row += r + (-r % 8)
    return out


SHARDED = ("w_in", "w_uq", "w_ukv", "w_mem_kv", "w_gate", "w_branch", "w_out")
REPLICATED = ("g_pre", "a_ln_g", "a_ln_b", "a_w_s", "a_b_s", "q_norm_g", "kv_norm_g", "mem_norm_g", "b_gate", "g_post")
WEIGHT_ORDER = ("g_pre", "w_in", "a_ln_g", "a_ln_b", "a_w_s", "a_b_s", "q_norm_g", "w_uq", "kv_norm_g", "w_ukv",
                "mem_norm_g", "w_mem_kv", "w_gate", "b_gate", "w_branch", "w_out", "g_post")


def _unshard_cols(g):
    return g.transpose(1, 0, 2).reshape(g.shape[1], N_DEV * g.shape[2])


def _shard_cols(full):
    rows, n = full.shape
    return full.reshape(rows, N_DEV, n // N_DEV).transpose(1, 0, 2).astype(BF16)


def kernel(x, mem, positions, g_pre, w_in, a_ln_g, a_ln_b, a_w_s, a_b_s, q_norm_g, w_uq, kv_norm_g, w_ukv, mem_norm_g, w_mem_kv, w_gate, b_gate, w_branch, w_out, g_post, loss_target, m_g_pre, m_w_in, m_a_ln_g, m_a_ln_b, m_a_w_s, m_a_b_s, m_q_norm_g, m_w_uq, m_kv_norm_g, m_w_ukv, m_mem_norm_g, m_w_mem_kv, m_w_gate, m_b_gate, m_w_branch, m_w_out, m_g_post, v_g_pre, v_w_in, v_a_ln_g, v_a_ln_b, v_a_w_s, v_a_b_s, v_q_norm_g, v_w_uq, v_kv_norm_g, v_w_ukv, v_mem_norm_g, v_w_mem_kv, v_w_gate, v_b_gate, v_w_branch, v_w_out, v_g_post):
    weights = dict(g_pre=g_pre, w_in=w_in, a_ln_g=a_ln_g, a_ln_b=a_ln_b, a_w_s=a_w_s, a_b_s=a_b_s, q_norm_g=q_norm_g,
                   w_uq=w_uq, kv_norm_g=kv_norm_g, w_ukv=w_ukv, mem_norm_g=mem_norm_g, w_mem_kv=w_mem_kv,
                   w_gate=w_gate, b_gate=b_gate, w_branch=w_branch, w_out=w_out, g_post=g_post)
    mom1 = dict(g_pre=m_g_pre, w_in=m_w_in, a_ln_g=m_a_ln_g, a_ln_b=m_a_ln_b, a_w_s=m_a_w_s, a_b_s=m_a_b_s,
                q_norm_g=m_q_norm_g, w_uq=m_w_uq, kv_norm_g=m_kv_norm_g, w_ukv=m_w_ukv, mem_norm_g=m_mem_norm_g,
                w_mem_kv=m_w_mem_kv, w_gate=m_w_gate, b_gate=m_b_gate, w_branch=m_w_branch, w_out=m_w_out, g_post=m_g_post)
    mom2 = dict(g_pre=v_g_pre, w_in=v_w_in, a_ln_g=v_a_ln_g, a_ln_b=v_a_ln_b, a_w_s=v_a_w_s, a_b_s=v_a_b_s,
                q_norm_g=v_q_norm_g, w_uq=v_w_uq, kv_norm_g=v_kv_norm_g, w_ukv=v_w_ukv, mem_norm_g=v_mem_norm_g,
                w_mem_kv=v_w_mem_kv, w_gate=v_w_gate, b_gate=v_b_gate, w_branch=v_w_branch, w_out=v_w_out, g_post=v_g_post)
    d = D_MODEL
    t = x.shape[1]
    xs, tgt, mems = x[0], loss_target[0], mem[0]
    pos_col = positions.reshape(t, 1)

    shard16 = {n: weights[n][0].astype(BF16) for n in SHARDED}
    h, g_in = _pre_norm(xs, g_pre, _Gather([shard16["w_in"]]))
    w_in_full = _unshard_cols(g_in)
    lat0, lat1 = 3 * d, 3 * d + Q_LORA + KV_LORA + QK_ROPE
    w_big = jnp.concatenate([w_in_full[:, :lat0], w_in_full[:, lat1:]], axis=1)
    w_lat = jnp.concatenate([w_in_full[:, lat0:lat1], jnp.zeros((d, LAT_W - (lat1 - lat0)), BF16)], axis=1)

    inv_freq = 1.0 / (ROPE_THETA ** (jnp.arange(0, QK_ROPE, 2, dtype=F32) / QK_ROPE))
    inv_freq_lanes = jnp.concatenate([inv_freq, inv_freq, jnp.zeros((128 - QK_ROPE,), F32)]).reshape(1, 128)
    ws = a_w_s[0]
    b_exp = jnp.broadcast_to(a_b_s[0][:, :, None], (A_GROUPS, CHUNK, 128))

    proj_big, g_uq, g_ukv, g_gate, g_mem = _mm(
        h, w_big, name="proj_big", tm=1024, tn=1024, tk=2048,
        carry=_Gather([shard16[n] for n in ("w_uq", "w_ukv", "w_gate", "w_mem_kv")]))
    w_uq_p = jnp.pad(_unshard_cols(g_uq).reshape(Q_LORA, MLA_HEADS, QK_DIM),
                     ((0, 0), (0, 0), (0, HEAD_PAD - QK_DIM))).reshape(Q_LORA, MLA_HEADS * HEAD_PAD)
    w_ukv_f = _unshard_cols(g_ukv)
    w_mem_f = _unshard_cols(g_mem)
    w_gate_f = _unshard_cols(g_gate)
    proj_lat = _mm(h, w_lat, name="proj_lat", tm=1024, tn=LAT_W, tk=2048)
    gates, g_br, g_out = _mm(h, w_gate_f, name="gates", tm=1024, tn=1024, tk=2048, bias=b_gate, act="sigmoid",
                             carry=_Gather([shard16["w_branch"], shard16["w_out"]]))
    w_br_f = g_br.transpose(1, 0, 2, 3).reshape(3, d, d)
    w_out_f = g_out.reshape(d, d)
    c_tab, sa_tab, sb_tab = _rope_tables(pos_col, inv_freq_lanes)
    cqn, ckvn, kpe = _latent_norms(proj_lat, q_norm_g, kv_norm_g, c_tab, sa_tab, sb_tab)
    q_raw = _mm(cqn, w_uq_p, name="q_up", tm=1024, tn=1024, tk=512, out_dtype=F32)
    q = _rope_q(q_raw, c_tab, sa_tab, sb_tab)
    kv = _mm(ckvn, w_ukv_f, name="kv_up", tm=1024, tn=1024, tk=512)
    o_b, y_b, lse = _mla_fwd(q, kv, kpe, proj_big)
    memn = _mem_norm(mems, mem_norm_g)
    kv_m = _mm(memn, w_mem_f, name="mem_kv", tm=256, tn=1024, tk=2048)
    y_m = _mem_fwd(proj_big, kv_m)
    y_a = _gmlp_fwd(proj_big, a_ln_g, a_ln_b, ws, b_exp)
    ys = (y_a, y_b, y_m)
    ps = [_mm(ys[n], w_br_f[n], name=f"branch{n}", tm=1024, tn=1024, tk=2048) for n in range(3)]
    merged = _merge(gates, *ps)
    out = _mm(merged, w_out_f, name="out_proj", tm=1024, tn=1024, tk=2048, out_dtype=F32)
    d_out, dy, loss_blk, dg_post = _post_loss(out, xs, tgt, g_post)

    dmerged = _mm(d_out, w_out_f, name="d_merged", tb=True, tm=1024, tn=1024, tk=2048)
    dw_out = _mm(merged, d_out, name="dw_out", ta=True, tm=1024, tn=1024, tk=2048)
    dp_a, dp_b, dp_m, dgl, db_gate = _merge_bwd(dmerged, gates, *ps)
    dps = (dp_a, dp_b, dp_m)
    dys = [_mm(dps[n], w_br_f[n], name=f"d_y{n}", tb=True, tm=1024, tn=1024, tk=2048) for n in range(3)]
    dw_br = [_mm(ys[n], dps[n], name=f"dw_branch{n}", ta=True, tm=1024, tn=1024, tk=2048) for n in range(3)]
    dw_gate = _mm(h, dgl, name="dw_gate", ta=True, tm=1024, tn=1024, tk=2048)

    dproj_big, dws, dbs, dlng, dlnb = _gmlp_bwd(proj_big, dys[0], a_ln_g, a_ln_b, ws, b_exp)

    recv = {}
    send = [_shard_cols(dw_gate), jnp.stack(dw_br).reshape(3, N_DEV, d // N_DEV, d).transpose(1, 0, 2, 3),
            dw_out.reshape(N_DEV, d // N_DEV, d)]
    do_b, dproj_big, delta = _mla_gate_bwd(dys[1], proj_big, o_b, dproj_big)
    dq, dkv, dkpe, r_gate, r_br, r_out = _mla_bwd(q, kv, kpe, do_b, lse, delta, _AllToAll(send))
    recv["w_gate"], recv["w_branch"], recv["w_out"] = [r_gate], [r_br], [r_out]
    dq_raw, dkr = _rope_q_bwd(dq, dkpe, c_tab, sa_tab, sb_tab)
    dcqn = _mm(dq_raw, w_uq_p, name="d_cq", tb=True, tm=1024, tn=Q_LORA, tk=2048, out_dtype=F32)
    dw_uq_p = _mm(cqn, dq_raw, name="dw_uq", ta=True, tm=Q_LORA, tn=1024, tk=2048)
    dckvn = _mm(dkv, w_ukv_f, name="d_ckv", tb=True, tm=1024, tn=KV_LORA, tk=2048, out_dtype=F32)
    dw_ukv = _mm(ckvn, dkv, name="dw_ukv", ta=True, tm=KV_LORA, tn=1024, tk=2048)
    dproj_lat, dqg, dkg = _latent_norms_bwd(proj_lat, q_norm_g, kv_norm_g, dcqn, dckvn, dkr)

    dproj_big, dkv_m32 = _mem_bwd(proj_big, kv_m, dys[2], dproj_big)
    dkv_m = dkv_m32.astype(BF16)
    dw_mem = _mm(memn, dkv_m, name="dw_mem", ta=True, tm=1024, tn=1024, tk=256)

    dw_uq_full = dw_uq_p.reshape(Q_LORA, MLA_HEADS, HEAD_PAD)[:, :, :QK_DIM].reshape(Q_LORA, MLA_HEADS * QK_DIM)
    dw_big, r_uq, r_ukv, r_mem = _mm(
        h, dproj_big, name="dw_big", ta=True, tm=1024, tn=1024, tk=2048,
        carry=_AllToAll([_shard_cols(dw_uq_full), _shard_cols(dw_ukv), _shard_cols(dw_mem)]))
    recv["w_uq"], recv["w_ukv"], recv["w_mem_kv"] = [r_uq], [r_ukv], [r_mem]
    dw_lat = _mm(h, dproj_lat, name="dw_lat", ta=True, tm=1024, tn=LAT_W, tk=2048)
    dw_in_full = jnp.concatenate([dw_big[:, :lat0], dw_lat[:, :lat1 - lat0], dw_big[:, lat0:]], axis=1)
    send_in = _shard_cols(dw_in_full)
    dmemn = _mm(dkv_m, w_mem_f, name="d_memn", tb=True, tm=256, tn=1024, tk=2048, out_dtype=F32)
    dg_mem = _mem_norm_bwd(mems, mem_norm_g, dmemn)
    dh, r_in0 = _mm(dgl, w_gate_f, name="dh_gate", tb=True, tm=1024, tn=1024, tk=2048, out_dtype=F32,
                    carry=_AllToAll([send_in[:, :d // 2]]))
    dh = _mm(dproj_lat, w_lat, name="dh_lat", tb=True, tm=1024, tn=1024, tk=LAT_W, out_dtype=F32, add=dh)
    dh, r_in1 = _mm(dproj_big, w_big, name="dh_big", tb=True, tm=1024, tn=1024, tk=2048, out_dtype=F32, add=dh,
                    carry=_AllToAll([send_in[:, d // 2:]]))
    recv["w_in"] = [r_in0, r_in1]
    grad_x, dg_pre = _pre_norm_bwd(xs, g_pre, dh, dy)

    results = {}
    for n in SHARDED:
        results[n] = [r[None] for r in _adam_sharded(recv[n], weights[n][0], mom1[n][0], mom2[n][0], "adam_" + n)]

    small = dict(g_pre=dg_pre, a_ln_g=dlng, a_ln_b=dlnb, a_w_s=dws, a_b_s=dbs, q_norm_g=dqg, kv_norm_g=dkg,
                 mem_norm_g=dg_mem, b_gate=db_gate, g_post=dg_post)
    (parts,) = _exchange(_Gather([_pack([small[n] for n in REPLICATED])]), "gather_small_grads")
    packed = _adam_replicated(parts, _pack([weights[n] for n in REPLICATED]), _pack([mom1[n] for n in REPLICATED]),
                              _pack([mom2[n] for n in REPLICATED]))
    shapes = [weights[n].shape for n in REPLICATED]
    unpacked = [_unpack(p, shapes) for p in packed]
    for i, n in enumerate(REPLICATED):
        results[n] = [u[i] for u in unpacked]

    loss = lax.psum(loss_blk[0, 0], AXES)
    outs = [loss, grad_x[None]]
    for kind in range(4):
        outs += [results[n][kind] for n in WEIGHT_ORDER]
    return tuple(outs)
```

```python
import functools
import math

import jax
import jax.numpy as jnp
import numpy as np
from jax import lax
from jax.experimental import pallas as pl
from jax.experimental.pallas import tpu as pltpu

F32 = jnp.float32
BF16 = jnp.bfloat16
MESH = pl.DeviceIdType.MESH
AXES = ("x", "y", "c")
N_DEV = 8

D_MODEL = 2048
EPS = 1e-6
CHUNK = 128
A_GROUPS = 16
MLA_HEADS = 16
QK_NOPE = 128
QK_ROPE = 64
QK_DIM = QK_NOPE + QK_ROPE
HEAD_PAD = 256
Q_LORA = 512
KV_LORA = 512
MEM_HEADS = 4
MEM_HEAD_DIM = 512
ROPE_THETA = 10000.0
MLA_SCALE = QK_DIM ** -0.5
MEM_SCALE = MEM_HEAD_DIM ** -0.5
NEG = -1e30
LOG2E = 1.4426950408889634

ADAM_LR = 0.001
ADAM_B1 = 0.9
ADAM_B2 = 0.999
ADAM_EPS = 1e-08
ADAM_WD = 0.01
ADAM_STEP = 10

BIG_W = 6 * D_MODEL
LAT_W = Q_LORA + KV_LORA + 128

VMEM_MIB = 1024 * 1024

ROW_BLK = 256
ATT_BLK = 2048
ATT_BLK_FWD = 2048
ATT_SUB = 256
GMLP_ROWS = 256
MEM_Q_BLK = 512


def _params(vmem_mib, **kw):
    return pltpu.CompilerParams(vmem_limit_bytes=int(vmem_mib * VMEM_MIB), **kw)


def _gelu(x):
    k = math.sqrt(2.0 / math.pi)
    t = jnp.tanh(k * (x + 0.044715 * (x * x * x)))
    return 0.5 * x * (1.0 + t)


def _gelu_and_grad(x):
    k = math.sqrt(2.0 / math.pi)
    x2 = x * x
    t = jnp.tanh(k * (x + 0.044715 * (x2 * x)))
    val = 0.5 * x * (1.0 + t)
    grad = 0.5 * (1.0 + t) + 0.5 * x * (1.0 - t * t) * (k * (1.0 + 3.0 * 0.044715 * x2))
    return val, grad


def _silu_and_grad(z):
    s = jax.nn.sigmoid(z)
    return z * s, s * (1.0 + z * (1.0 - s))


def _mm(a, b, *, name, tm, tn, tk, ta=False, tb=False, out_dtype=BF16, bias=None, act=None, add=None, carry=None,
        post=None, post_rows=()):
    m = a.shape[1] if ta else a.shape[0]
    k = a.shape[0] if ta else a.shape[1]
    n = b.shape[0] if tb else b.shape[1]
    assert k == (b.shape[1] if tb else b.shape[0])
    tm, tn, tk = min(tm, m), min(tn, n), min(tk, k)
    assert m % tm == 0 and n % tn == 0 and k % tk == 0, (name, m, n, k, tm, tn, tk)
    nk = k // tk
    a_spec = pl.BlockSpec((tk, tm), lambda i, j, kk: (kk, i)) if ta else pl.BlockSpec((tm, tk), lambda i, j, kk: (i, kk))
    b_spec = pl.BlockSpec((tn, tk), lambda i, j, kk: (j, kk)) if tb else pl.BlockSpec((tk, tn), lambda i, j, kk: (kk, j))
    dn = (((0 if ta else 1,), (1 if tb else 0,)), ((), ()))
    operands, in_specs = [a, b], [a_spec, b_spec]
    if bias is not None:
        operands.append(bias)
        in_specs.append(pl.BlockSpec((1, tn), lambda i, j, kk: (0, j)))
    if add is not None:
        operands.append(add)
        in_specs.append(pl.BlockSpec((tm, tn), lambda i, j, kk: (i, j)))
    n_fixed = len(operands)
    for r in post_rows:
        operands.append(r)
        in_specs.append(pl.BlockSpec((tm, r.shape[1]), lambda i, j, kk: (i, 0)))

    n_in = len(operands)
    n_carry = len(carry.arrays) if carry is not None else 0
    n_acc = 1 if nk > 1 else 0
    grid = (m // tm, n // tn, nk)

    def body(*refs):
        a_ref, b_ref = refs[0], refs[1]
        pos = 2
        bias_ref = add_ref = None
        if bias is not None:
            bias_ref = refs[pos]
            pos += 1
        if add is not None:
            add_ref = refs[pos]
            pos += 1
        o_ref = refs[n_in + n_carry]
        pos = n_in + n_carry
        if carry is not None:
            c_ins = refs[n_in:n_in + n_carry]
            c_outs = refs[n_in + n_carry + 1:n_in + 2 * n_carry + 1]
            c_sems = refs[n_in + 2 * n_carry + 1 + n_acc:]
            ids = [pl.program_id(ax) for ax in range(3)]

            @pl.when((ids[0] == 0) & (ids[1] == 0) & (ids[2] == 0))
            def _():
                carry.start(c_ins, c_outs, c_sems)

        part = lax.dot_general(a_ref[...], b_ref[...], dn, preferred_element_type=F32)

        def finish(acc):
            if bias_ref is not None:
                acc = acc + bias_ref[...]
            if act == "sigmoid":
                acc = jax.nn.sigmoid(acc)
            if add_ref is not None:
                acc = acc + add_ref[...]
            if post is not None:
                acc = post(acc, *[r[...] for r in refs[n_fixed:n_in]])
            o_ref[...] = acc.astype(o_ref.dtype)

        if nk == 1:
            finish(part)
        else:
            acc_ref = refs[n_in + 2 * n_carry + 1]
            kk = pl.program_id(2)

            @pl.when(kk == 0)
            def _():
                acc_ref[...] = part

            @pl.when(kk > 0)
            def _():
                acc_ref[...] += part

            @pl.when(kk == nk - 1)
            def _():
                finish(acc_ref[...])

        if carry is not None:
            @pl.when((ids[0] == grid[0] - 1) & (ids[1] == grid[1] - 1) & (ids[2] == grid[2] - 1))
            def _():
                carry.finish(c_ins, c_outs, c_sems)

    osz = jnp.dtype(out_dtype).itemsize
    est = 2 * 2 * (tm * tk + tk * tn) + 2 * osz * tm * tn + 8 * tm * tn + (2 * 4 * tm * tn if add is not None else 0)
    main_spec = pl.BlockSpec((tm, tn), lambda i, j, kk: (i, j))
    main_shape = jax.ShapeDtypeStruct((m, n), out_dtype)
    scratch = [pltpu.VMEM((tm, tn), F32)] if nk > 1 else []
    if carry is None:
        return pl.pallas_call(
            body, name=name, grid=grid, in_specs=in_specs, out_specs=main_spec, out_shape=main_shape,
            scratch_shapes=scratch, compiler_params=_params(min(56, est / VMEM_MIB + 12)),
        )(*operands)
    return pl.pallas_call(
        body, name=name, grid=grid,
        in_specs=in_specs + [HBM_SPEC] * n_carry,
        out_specs=[main_spec] + [HBM_SPEC] * n_carry,
        out_shape=[main_shape] + carry.out_shapes,
        scratch_shapes=scratch + carry.sem_shapes,
        compiler_params=_params(min(56, est / VMEM_MIB + 12)),
    )(*operands, *carry.arrays)


def _row_spec(tr, cols, col_blk=0):
    return pl.BlockSpec((tr, cols), lambda i: (i, col_blk))


def _full_spec(shape):
    nd = len(shape)
    return pl.BlockSpec(shape, lambda i: (0,) * nd)


def _pre_norm(x, g_pre, carry):
    t, d = x.shape
    tr = min(ROW_BLK, t)
    n_carry = len(carry.arrays)
    steps = t // tr

    def body(x_ref, g_ref, *rest):
        c_ins, h_ref = rest[:n_carry], rest[n_carry]
        c_outs, c_sems = rest[n_carry + 1:2 * n_carry + 1], rest[2 * n_carry + 1:]

        @pl.when(pl.program_id(0) == 0)
        def _():
            carry.start(c_ins, c_outs, c_sems)

        xv = x_ref[...]
        r = lax.rsqrt(jnp.mean(xv * xv, axis=-1, keepdims=True) + EPS)
        h_ref[...] = ((xv * r) * g_ref[...]).astype(BF16)

        @pl.when(pl.program_id(0) == steps - 1)
        def _():
            carry.finish(c_ins, c_outs, c_sems)

    return pl.pallas_call(
        body, name="pre_norm", grid=(steps,),
        in_specs=[_row_spec(tr, d), _full_spec((1, d))] + [HBM_SPEC] * n_carry,
        out_specs=[_row_spec(tr, d)] + [HBM_SPEC] * n_carry,
        out_shape=[jax.ShapeDtypeStruct((t, d), BF16)] + carry.out_shapes,
        scratch_shapes=carry.sem_shapes,
        compiler_params=_params(32),
    )(x, g_pre, *carry.arrays)


def _rope_tables(pos_col, inv_freq_lanes):
    t = pos_col.shape[0]
    tr = min(ROW_BLK, t)

    def body(p_ref, f_ref, c_ref, sa_ref, sb_ref):
        ang = p_ref[...].astype(F32) * f_ref[...]
        lane = lax.broadcasted_iota(jnp.int32, ang.shape, 1)
        cos, sin = jnp.cos(ang), jnp.sin(ang)
        c_ref[...] = jnp.where(lane < QK_ROPE, cos, 0.0)
        sa_ref[...] = jnp.where(lane < QK_ROPE // 2, sin, 0.0)
        sb_ref[...] = jnp.where((lane >= QK_ROPE // 2) & (lane < QK_ROPE), sin, 0.0)

    tab = jax.ShapeDtypeStruct((t, 128), F32)
    return pl.pallas_call(
        body, name="rope_tables", grid=(t // tr,),
        in_specs=[_row_spec(tr, 1), _full_spec((1, 128))],
        out_specs=[_row_spec(tr, 128)] * 3,
        out_shape=[tab, tab, tab],
    )(pos_col, inv_freq_lanes)


def _rope_fwd(p, c, sa, sb):
    return p * c - pltpu.roll(p, 96, 1) * sa + pltpu.roll(p, 32, 1) * sb


def _rope_bwd(g, c, sa, sb):
    return g * c + pltpu.roll(g, 96, 1) * sa - pltpu.roll(g, 32, 1) * sb


def _rms(xv, g):
    r = lax.rsqrt(jnp.mean(xv * xv, axis=-1, keepdims=True) + EPS)
    return (xv * r) * g


def _rms_bwd(xv, g, dout):
    r = lax.rsqrt(jnp.mean(xv * xv, axis=-1, keepdims=True) + EPS)
    xn = xv * r
    dg = jnp.sum(dout * xn, axis=0, keepdims=True)
    dxn = dout * g
    dx = r * (dxn - xn * jnp.mean(dxn * xn, axis=-1, keepdims=True))
    return dx, dg


def _latent_norms(proj_lat, q_norm_g, kv_norm_g, c_tab, sa_tab, sb_tab):
    t = proj_lat.shape[0]
    tr = min(ROW_BLK, t)

    def body(cq_ref, ckv_ref, kr_ref, qg_ref, kg_ref, c_ref, sa_ref, sb_ref, cqn_ref, ckvn_ref, kpe_ref):
        cqn_ref[...] = _rms(cq_ref[...].astype(F32), qg_ref[...]).astype(BF16)
        ckvn_ref[...] = _rms(ckv_ref[...].astype(F32), kg_ref[...]).astype(BF16)
        kpe_ref[...] = _rope_fwd(kr_ref[...].astype(F32), c_ref[...], sa_ref[...], sb_ref[...]).astype(BF16)

    return pl.pallas_call(
        body, name="latent_norms", grid=(t // tr,),
        in_specs=[_row_spec(tr, Q_LORA, 0), _row_spec(tr, KV_LORA, 1), _row_spec(tr, 128, (Q_LORA + KV_LORA) // 128),
                  _full_spec((1, Q_LORA)), _full_spec((1, KV_LORA)),
                  _row_spec(tr, 128), _row_spec(tr, 128), _row_spec(tr, 128)],
        out_specs=[_row_spec(tr, Q_LORA), _row_spec(tr, KV_LORA), _row_spec(tr, 128)],
        out_shape=[jax.ShapeDtypeStruct((t, Q_LORA), BF16), jax.ShapeDtypeStruct((t, KV_LORA), BF16),
                   jax.ShapeDtypeStruct((t, 128), BF16)],
    )(proj_lat, proj_lat, proj_lat, q_norm_g, kv_norm_g, c_tab, sa_tab, sb_tab)


def _rope_q_tile(acc, c, sa, sb):
    qs = MLA_SCALE * LOG2E
    parts = []
    for h in range(acc.shape[1] // HEAD_PAD):
        parts.append(acc[:, h * HEAD_PAD:h * HEAD_PAD + 128] * qs)
        parts.append(_rope_fwd(acc[:, h * HEAD_PAD + 128:(h + 1) * HEAD_PAD], c, sa, sb) * qs)
    return jnp.concatenate(parts, axis=1)


def _rope_q_bwd(dq, dkpe, c_tab, sa_tab, sb_tab):
    t = dq.shape[0]
    tr = min(ROW_BLK, t)

    def body(dq_ref, dkp_ref, c_ref, sa_ref, sb_ref, o_ref, dkr_ref):
        c, sa, sb = c_ref[...], sa_ref[...], sb_ref[...]
        for h in range(MLA_HEADS):
            o_ref[:, h * HEAD_PAD:h * HEAD_PAD + 128] = (dq_ref[:, h * HEAD_PAD:h * HEAD_PAD + 128] * MLA_SCALE).astype(BF16)
            g = dq_ref[:, h * HEAD_PAD + 128:(h + 1) * HEAD_PAD] * MLA_SCALE
            o_ref[:, h * HEAD_PAD + 128:(h + 1) * HEAD_PAD] = _rope_bwd(g, c, sa, sb).astype(BF16)
        dkr_ref[...] = _rope_bwd(dkp_ref[...], c, sa, sb)

    w = MLA_HEADS * HEAD_PAD
    return pl.pallas_call(
        body, name="rope_q_bwd", grid=(t // tr,),
        in_specs=[_row_spec(tr, w), _row_spec(tr, 128),
                  _row_spec(tr, 128), _row_spec(tr, 128), _row_spec(tr, 128)],
        out_specs=[_row_spec(tr, w), _row_spec(tr, 128)],
        out_shape=[jax.ShapeDtypeStruct((t, w), BF16), jax.ShapeDtypeStruct((t, 128), F32)],
        compiler_params=_params(48),
    )(dq, dkpe, c_tab, sa_tab, sb_tab)


def _latent_norms_bwd(proj_lat, q_norm_g, kv_norm_g, dcqn, dckvn, dkr):
    t = proj_lat.shape[0]
    tr = min(ROW_BLK, t)

    def body(cq_ref, ckv_ref, qg_ref, kg_ref, dcqn_ref, dckvn_ref, dkr_ref, dl_ref, dqg_ref, dkg_ref):
        dcq, dqg = _rms_bwd(cq_ref[...].astype(F32), qg_ref[...], dcqn_ref[...])
        dckv, dkg = _rms_bwd(ckv_ref[...].astype(F32), kg_ref[...], dckvn_ref[...])
        dl_ref[:, 0:Q_LORA] = dcq.astype(BF16)
        dl_ref[:, Q_LORA:Q_LORA + KV_LORA] = dckv.astype(BF16)
        dl_ref[:, Q_LORA + KV_LORA:LAT_W] = dkr_ref[...].astype(BF16)

        @pl.when(pl.program_id(0) == 0)
        def _():
            dqg_ref[...] = jnp.zeros_like(dqg_ref)
            dkg_ref[...] = jnp.zeros_like(dkg_ref)

        dqg_ref[...] += dqg
        dkg_ref[...] += dkg

    return pl.pallas_call(
        body, name="latent_norms_bwd", grid=(t // tr,),
        in_specs=[_row_spec(tr, Q_LORA, 0), _row_spec(tr, KV_LORA, 1), _full_spec((1, Q_LORA)), _full_spec((1, KV_LORA)),
                  _row_spec(tr, Q_LORA), _row_spec(tr, KV_LORA), _row_spec(tr, 128)],
        out_specs=[_row_spec(tr, LAT_W), _full_spec((1, Q_LORA)), _full_spec((1, KV_LORA))],
        out_shape=[jax.ShapeDtypeStruct((t, LAT_W), BF16), jax.ShapeDtypeStruct((1, Q_LORA), F32),
                   jax.ShapeDtypeStruct((1, KV_LORA), F32)],
    )(proj_lat, proj_lat, q_norm_g, kv_norm_g, dcqn, dckvn, dkr)


def _mem_norm(mem, g):
    m, d = mem.shape

    def body(x_ref, g_ref, o_ref):
        o_ref[...] = _rms(x_ref[...], g_ref[...]).astype(BF16)

    return pl.pallas_call(
        body, name="mem_norm", grid=(1,),
        in_specs=[_full_spec((m, d)), _full_spec((1, d))],
        out_specs=_full_spec((m, d)),
        out_shape=jax.ShapeDtypeStruct((m, d), BF16),
    )(mem, g)


def _mem_norm_bwd(mem, g, dmemn):
    m, d = mem.shape

    def body(x_ref, g_ref, d_ref, dg_ref):
        _, dg = _rms_bwd(x_ref[...], g_ref[...], d_ref[...])
        dg_ref[...] = dg

    return pl.pallas_call(
        body, name="mem_norm_bwd", grid=(1,),
        in_specs=[_full_spec((m, d)), _full_spec((1, d)), _full_spec((m, d))],
        out_specs=_full_spec((1, d)),
        out_shape=jax.ShapeDtypeStruct((1, d), F32),
    )(mem, g, dmemn)


def _merge(gates, p_a, p_b, p_m):
    t, d = p_a.shape
    tr = min(ROW_BLK, t)

    def body(ga_ref, gb_ref, gm_ref, pa_ref, pb_ref, pm_ref, o_ref):
        acc = ga_ref[...].astype(F32) * pa_ref[...].astype(F32)
        acc = acc + gb_ref[...].astype(F32) * pb_ref[...].astype(F32)
        acc = acc + gm_ref[...].astype(F32) * pm_ref[...].astype(F32)
        o_ref[...] = acc.astype(BF16)

    return pl.pallas_call(
        body, name="merge", grid=(t // tr,),
        in_specs=[_row_spec(tr, d, 0), _row_spec(tr, d, 1), _row_spec(tr, d, 2),
                  _row_spec(tr, d), _row_spec(tr, d), _row_spec(tr, d)],
        out_specs=_row_spec(tr, d),
        out_shape=jax.ShapeDtypeStruct((t, d), BF16),
        compiler_params=_params(48),
    )(gates, gates, gates, p_a, p_b, p_m)


def _merge_bwd(dm, gates, p_a, p_b, p_m):
    t, d = dm.shape
    tr = min(ROW_BLK, t)

    def body(dm_ref, g_ref, pa_ref, pb_ref, pm_ref, dpa_ref, dpb_ref, dpm_ref, dgl_ref, db_ref):
        dmv = dm_ref[...].astype(F32)

        @pl.when(pl.program_id(0) == 0)
        def _():
            db_ref[...] = jnp.zeros_like(db_ref)

        for n, (p_ref, dp_ref) in enumerate(((pa_ref, dpa_ref), (pb_ref, dpb_ref), (pm_ref, dpm_ref))):
            g = g_ref[:, n * d:(n + 1) * d].astype(F32)
            dp_ref[...] = (dmv * g).astype(BF16)
            dgl = dmv * p_ref[...].astype(F32) * (g * (1.0 - g))
            dgl_ref[:, n * d:(n + 1) * d] = dgl.astype(BF16)
            db_ref[:, n * d:(n + 1) * d] += jnp.sum(dgl, axis=0, keepdims=True)

    act = jax.ShapeDtypeStruct((t, d), BF16)
    return pl.pallas_call(
        body, name="merge_bwd", grid=(t // tr,),
        in_specs=[_row_spec(tr, d), _row_spec(tr, 3 * d), _row_spec(tr, d), _row_spec(tr, d), _row_spec(tr, d)],
        out_specs=[_row_spec(tr, d), _row_spec(tr, d), _row_spec(tr, d), _row_spec(tr, 3 * d), _full_spec((1, 3 * d))],
        out_shape=[act, act, act, jax.ShapeDtypeStruct((t, 3 * d), BF16), jax.ShapeDtypeStruct((1, 3 * d), F32)],
        compiler_params=_params(56),
    )(dm, gates, p_a, p_b, p_m)


def _post_loss(out, x, tgt, g_post):
    t, d = out.shape
    tr = min(ROW_BLK, t)

    def body(o_ref, x_ref, t_ref, g_ref, do_ref, dy_ref, loss_ref, dg_ref):
        ov = o_ref[...]
        g = g_ref[...]
        r = lax.rsqrt(jnp.mean(ov * ov, axis=-1, keepdims=True) + EPS)
        on = ov * r
        err = (x_ref[...] + on * g) - t_ref[...]
        dy = err * (1.0 / d)
        dy_ref[...] = dy
        don = dy * g
        do_ref[...] = (r * (don - on * jnp.mean(don * on, axis=-1, keepdims=True))).astype(BF16)

        @pl.when(pl.program_id(0) == 0)
        def _():
            loss_ref[...] = jnp.zeros_like(loss_ref)
            dg_ref[...] = jnp.zeros_like(dg_ref)

        loss_ref[...] += 0.5 * jnp.sum(jnp.mean(err * err, axis=-1, keepdims=True))
        dg_ref[...] += jnp.sum(dy * on, axis=0, keepdims=True)

    return pl.pallas_call(
        body, name="post_loss", grid=(t // tr,),
        in_specs=[_row_spec(tr, d), _row_spec(tr, d), _row_spec(tr, d), _full_spec((1, d))],
        out_specs=[_row_spec(tr, d), _row_spec(tr, d), _full_spec((8, 128)), _full_spec((1, d))],
        out_shape=[jax.ShapeDtypeStruct((t, d), BF16), jax.ShapeDtypeStruct((t, d), F32),
                   jax.ShapeDtypeStruct((8, 128), F32), jax.ShapeDtypeStruct((1, d), F32)],
        compiler_params=_params(56),
    )(out, x, tgt, g_post)


def _pre_norm_bwd(x, g_pre, dh, dy):
    t, d = x.shape
    tr = min(ROW_BLK, t)

    def body(x_ref, g_ref, dh_ref, dy_ref, dx_ref, dg_ref):
        dx, dg = _rms_bwd(x_ref[...], g_ref[...], dh_ref[...])
        dx_ref[...] = dx + dy_ref[...]

        @pl.when(pl.program_id(0) == 0)
        def _():
            dg_ref[...] = jnp.zeros_like(dg_ref)

        dg_ref[...] += dg

    return pl.pallas_call(
        body, name="pre_norm_bwd", grid=(t // tr,),
        in_specs=[_row_spec(tr, d), _full_spec((1, d)), _row_spec(tr, d), _row_spec(tr, d)],
        out_specs=[_row_spec(tr, d), _full_spec((1, d))],
        out_shape=[jax.ShapeDtypeStruct((t, d), F32), jax.ShapeDtypeStruct((1, d), F32)],
        compiler_params=_params(56),
    )(x, g_pre, dh, dy)


def _causal_mask(n):
    row = lax.broadcasted_iota(jnp.int32, (n, n), 0)
    col = lax.broadcasted_iota(jnp.int32, (n, n), 1)
    return row >= col


def _layernorm_stats(vg):
    mu = jnp.mean(vg, axis=-1, keepdims=True)
    cen = vg - mu
    rstd = lax.rsqrt(jnp.mean(cen * cen, axis=-1, keepdims=True) + EPS)
    return cen * rstd, rstd


def _gmlp_fwd(proj_big, ln_g, ln_b, w_s, b_exp):
    t = proj_big.shape[0]
    rows = min(GMLP_ROWS, t)
    d = D_MODEL

    def body(u_ref, v_ref, z_ref, lg_ref, lb_ref, ws_ref, be_ref, y_ref, vn_scr):
        vhat, _ = _layernorm_stats(_gelu(v_ref[...].astype(F32)))
        vn_scr[...] = (vhat * lg_ref[...] + lb_ref[...]).astype(BF16)
        mask = _causal_mask(CHUNK)
        for g in range(A_GROUPS):
            cols = slice(g * 128, (g + 1) * 128)
            wsm = jnp.where(mask, ws_ref[g], 0.0).astype(BF16)
            for c in range(rows // CHUNK):
                rws = slice(c * CHUNK, (c + 1) * CHUNK)
                sv = jnp.dot(wsm, vn_scr[rws, cols], preferred_element_type=F32) + be_ref[g]
                zs, _ = _silu_and_grad(z_ref[rws, cols].astype(F32))
                y_ref[rws, cols] = (_gelu(u_ref[rws, cols].astype(F32)) * sv * zs).astype(BF16)

    return pl.pallas_call(
        body, name="gmlp_fwd", grid=(t // rows,),
        in_specs=[_row_spec(rows, d, 0), _row_spec(rows, d, 1), _row_spec(rows, d, 2),
                  _full_spec((1, d)), _full_spec((1, d)), _full_spec((A_GROUPS, CHUNK, CHUNK)),
                  _full_spec((A_GROUPS, CHUNK, 128))],
        out_specs=_row_spec(rows, d),
        out_shape=jax.ShapeDtypeStruct((t, d), BF16),
        scratch_shapes=[pltpu.VMEM((rows, d), BF16)],
        compiler_params=_params(40),
    )(proj_big, proj_big, proj_big, ln_g, ln_b, w_s, b_exp)


def _gmlp_bwd(proj_big, dya, ln_g, ln_b, w_s, b_exp):
    t = proj_big.shape[0]
    rows = min(GMLP_ROWS, t)
    d = D_MODEL
    nt = (((1,), (1,)), ((), ()))
    tn = (((0,), (0,)), ((), ()))

    def body(u_ref, v_ref, z_ref, dy_ref, lg_ref, lb_ref, ws_ref, be_ref,
             dp_ref, dws_ref, dbs_ref, dlg_ref, dlb_ref, vn_scr, dvn_scr):
        @pl.when(pl.program_id(0) == 0)
        def _():
            dws_ref[...] = jnp.zeros_like(dws_ref)
            dbs_ref[...] = jnp.zeros_like(dbs_ref)
            dlg_ref[...] = jnp.zeros_like(dlg_ref)
            dlb_ref[...] = jnp.zeros_like(dlb_ref)

        vg, vgrad = _gelu_and_grad(v_ref[...].astype(F32))
        vhat, rstd = _layernorm_stats(vg)
        vn_scr[...] = (vhat * lg_ref[...] + lb_ref[...]).astype(BF16)
        mask = _causal_mask(CHUNK)
        for g in range(A_GROUPS):
            cols = slice(g * 128, (g + 1) * 128)
            wsm = jnp.where(mask, ws_ref[g], 0.0).astype(BF16)
            dws = jnp.zeros((CHUNK, CHUNK), F32)
            dbs = jnp.zeros((CHUNK, 1), F32)
            for c in range(rows // CHUNK):
                rws = slice(c * CHUNK, (c + 1) * CHUNK)
                vn = vn_scr[rws, cols]
                sv = jnp.dot(wsm, vn, preferred_element_type=F32) + be_ref[g]
                ug, ugrad = _gelu_and_grad(u_ref[rws, cols].astype(F32))
                zs, zgrad = _silu_and_grad(z_ref[rws, cols].astype(F32))
                dya = dy_ref[rws, cols].astype(F32)
                dga = dya * zs
                dp_ref[rws, 2 * d + g * 128:2 * d + (g + 1) * 128] = (dya * (ug * sv) * zgrad).astype(BF16)
                dp_ref[rws, cols] = (dga * sv * ugrad).astype(BF16)
                dsv = dga * ug
                dsv16 = dsv.astype(BF16)
                dws = dws + lax.dot_general(dsv16, vn, nt, preferred_element_type=F32)
                dbs = dbs + jnp.sum(dsv, axis=-1, keepdims=True)
                dvn_scr[rws, cols] = lax.dot_general(wsm, dsv16, tn, preferred_element_type=F32)
            dws_ref[g] += jnp.where(mask, dws, 0.0)
            dbs_ref[g] += dbs
        dvn = dvn_scr[...]
        dlg_ref[...] += jnp.sum(dvn * vhat, axis=0, keepdims=True)
        dlb_ref[...] += jnp.sum(dvn, axis=0, keepdims=True)
        dvh = dvn * lg_ref[...]
        dvg = rstd * (dvh - jnp.mean(dvh, axis=-1, keepdims=True) - vhat * jnp.mean(dvh * vhat, axis=-1, keepdims=True))
        dp_ref[:, d:2 * d] = (dvg * vgrad).astype(BF16)

    return pl.pallas_call(
        body, name="gmlp_bwd", grid=(t // rows,),
        in_specs=[_row_spec(rows, d, 0), _row_spec(rows, d, 1), _row_spec(rows, d, 2), _row_spec(rows, d),
                  _full_spec((1, d)), _full_spec((1, d)), _full_spec((A_GROUPS, CHUNK, CHUNK)),
                  _full_spec((A_GROUPS, CHUNK, 128))],
        out_specs=[_row_spec(rows, 3 * d), _full_spec((A_GROUPS, CHUNK, CHUNK)), _full_spec((A_GROUPS, CHUNK, 1)),
                   _full_spec((1, d)), _full_spec((1, d))],
        out_shape=[jax.ShapeDtypeStruct((t, 6 * d), BF16), jax.ShapeDtypeStruct((A_GROUPS, CHUNK, CHUNK), F32),
                   jax.ShapeDtypeStruct((A_GROUPS, CHUNK, 1), F32), jax.ShapeDtypeStruct((1, d), F32),
                   jax.ShapeDtypeStruct((1, d), F32)],
        scratch_shapes=[pltpu.VMEM((rows, d), BF16), pltpu.VMEM((rows, d), F32)],
        compiler_params=_params(48),
    )(proj_big, proj_big, proj_big, dya, ln_g, ln_b, w_s, b_exp)


NT_DIMS = (((1,), (1,)), ((), ()))
TN_DIMS = (((0,), (0,)), ((), ()))


def _mla_fwd(q, kv, kpe, proj_big):
    t = q.shape[0]
    blk = min(ATT_BLK_FWD, t)
    nq = t // blk
    zb_blk0 = (3 * D_MODEL) // 128


    sub = min(ATT_SUB, blk)

    def body(q_ref, kv_ref, kp_ref, zb_ref, o_ref, yb_ref, lse_ref, m_scr, acc_scr):
        i = pl.program_id(1)
        qv = q_ref[...]
        m_scr[...] = jnp.full_like(m_scr, NEG)
        acc_scr[...] = jnp.zeros_like(acc_scr)
        ones = jnp.ones((blk, 128), BF16)

        def step(j, masked):
            ks = pl.ds(pl.multiple_of(j * blk, blk), blk)
            kc = jnp.concatenate([kv_ref[ks, 0:128], kp_ref[ks, :]], axis=1)
            vext = jnp.concatenate([kv_ref[ks, 128:256], ones], axis=1)
            for r in range(blk // sub):
                rows = slice(r * sub, (r + 1) * sub)
                kw = (r + 1) * sub if masked else blk
                tt = lax.dot_general(qv[rows], kc[:kw], NT_DIMS, preferred_element_type=F32)
                if masked:
                    row = lax.broadcasted_iota(jnp.int32, (sub, kw), 0) + r * sub
                    col = lax.broadcasted_iota(jnp.int32, (sub, kw), 1)
                    tt = jnp.where(row >= col, tt, NEG)
                cm = tt[:, 0:128]
                for c in range(1, kw // 128):
                    cm = jnp.maximum(cm, tt[:, c * 128:(c + 1) * 128])
                m_prev = m_scr[rows, :]
                m_new = jnp.maximum(m_prev, jnp.max(cm, axis=-1, keepdims=True))
                alpha = jnp.exp2(m_prev - m_new)
                m_scr[rows, :] = m_new
                p = jnp.concatenate([jnp.exp2(tt[:, c * 128:(c + 1) * 128] - m_new).astype(BF16)
                                     for c in range(kw // 128)], axis=1)
                pv = jnp.dot(p, vext[:kw], preferred_element_type=F32)
                acc_scr[rows, :] = jnp.concatenate([alpha, alpha], axis=1) * acc_scr[rows, :] + pv

        def loop_body(j, carry):
            step(j, False)
            return carry

        lax.fori_loop(0, i, loop_body, 0)
        step(i, True)
        l = acc_scr[:, 128:256]
        o = acc_scr[:, 0:128] / l
        o_ref[...] = o.astype(BF16)
        zs, _ = _silu_and_grad(zb_ref[...].astype(F32))
        yb_ref[...] = (o * zs).astype(BF16)
        lse_ref[0] = m_scr[...] + jnp.log2(l)

    act = jax.ShapeDtypeStruct((t, D_MODEL), BF16)
    return pl.pallas_call(
        body, name="mla_fwd", grid=(MLA_HEADS, nq),
        in_specs=[pl.BlockSpec((blk, HEAD_PAD), lambda h, i: (i, h)),
                  pl.BlockSpec((t, HEAD_PAD), lambda h, i: (0, h)),
                  pl.BlockSpec((t, 128), lambda h, i: (0, 0)),
                  pl.BlockSpec((blk, 128), lambda h, i: (i, zb_blk0 + h))],
        out_specs=[pl.BlockSpec((blk, 128), lambda h, i: (i, h)),
                   pl.BlockSpec((blk, 128), lambda h, i: (i, h)),
                   pl.BlockSpec((1, blk, 128), lambda h, i: (h, i, 0))],
        out_shape=[act, act, jax.ShapeDtypeStruct((MLA_HEADS, t, 128), F32)],
        scratch_shapes=[pltpu.VMEM((blk, 128), F32), pltpu.VMEM((blk, HEAD_PAD), F32)],
        compiler_params=_params(56),
    )(q, kv, kpe, proj_big)


def _mla_gate_bwd(dyb, proj_big, o, dproj):
    t, d = dyb.shape
    tr = min(ROW_BLK, t)

    def body(dy_ref, zb_ref, o_ref, buf_ref, do_ref, dz_ref, dl_ref):
        del buf_ref
        dy = dy_ref[...].astype(F32)
        ov = o_ref[...].astype(F32)
        zs, zgrad = _silu_and_grad(zb_ref[...].astype(F32))
        do16 = (dy * zs).astype(BF16)
        do_ref[...] = do16
        dz_ref[...] = (dy * ov * zgrad).astype(BF16)
        prod = do16.astype(F32) * ov
        for h in range(MLA_HEADS):
            delta = jnp.sum(prod[:, h * 128:(h + 1) * 128], axis=-1, keepdims=True)
            dl_ref[h] = jnp.broadcast_to(delta, (tr, 128))

    act = jax.ShapeDtypeStruct((t, d), BF16)
    head_spec = pl.BlockSpec((MLA_HEADS, tr, 128), lambda i: (0, i, 0))
    return pl.pallas_call(
        body, name="mla_gate_bwd", grid=(t // tr,),
        in_specs=[_row_spec(tr, d), _row_spec(tr, d, 3), _row_spec(tr, d), HBM_SPEC],
        out_specs=[_row_spec(tr, d), _row_spec(tr, d, 3), head_spec],
        out_shape=[act, jax.ShapeDtypeStruct((t, 6 * d), BF16), jax.ShapeDtypeStruct((MLA_HEADS, t, 128), F32)],
        input_output_aliases={3: 1},
        compiler_params=_params(56),
    )(dyb, proj_big, o, dproj)


def _mla_bwd(q, kv, kpe, do, lse, delta, carry):
    t = q.shape[0]
    blk = min(ATT_BLK, t)
    n = t // blk
    nc = blk // 128
    pairs = [(j, i) for j in range(n) for i in range(j, n)]
    j_tab = jnp.asarray([p[0] for p in pairs], jnp.int32)
    i_tab = jnp.asarray([p[1] for p in pairs], jnp.int32)
    n_carry = len(carry.arrays)
    sub = min(ATT_SUB, blk)

    def body(j_ref, i_ref, q_ref, do_ref, lse_ref, dl_ref, kv_ref, kp_ref, *rest):
        c_ins, rest = rest[:n_carry], rest[n_carry:]
        dq_ref, dkv_ref, dkp_ref = rest[:3]
        c_outs, rest = rest[3:3 + n_carry], rest[3 + n_carry:]
        dk_scr, dv_scr = rest[:2]
        c_sems = rest[2:]
        head = pl.program_id(0)
        step = pl.program_id(1)
        j, i = j_ref[step], i_ref[step]

        @pl.when((head == 0) & (step == 0))
        def _():
            carry.start(c_ins, c_outs, c_sems)

        @pl.when(step == 0)
        def _():
            dq_ref[...] = jnp.zeros_like(dq_ref)

        @pl.when(i == j)
        def _():
            dk_scr[...] = jnp.zeros_like(dk_scr)
            dv_scr[...] = jnp.zeros_like(dv_scr)

        kc = jnp.concatenate([kv_ref[:, 0:128], kp_ref[...]], axis=1)
        vv = kv_ref[:, 128:256]

        def tile(diag):
            for r in range(blk // sub):
                rows = slice(r * sub, (r + 1) * sub)
                kw = (r + 1) * sub if diag else blk
                qv, dov = q_ref[rows, :], do_ref[rows, :]
                tt = lax.dot_general(qv, kc[:kw], NT_DIMS, preferred_element_type=F32)
                if diag:
                    row = lax.broadcasted_iota(jnp.int32, (sub, kw), 0) + r * sub
                    col = lax.broadcasted_iota(jnp.int32, (sub, kw), 1)
                    tt = jnp.where(row >= col, tt, NEG)
                dp = lax.dot_general(dov, vv[:kw], NT_DIMS, preferred_element_type=F32)
                lse_v, dl_v = lse_ref[0, rows, :], dl_ref[0, rows, :]
                ps, dss = [], []
                for c in range(kw // 128):
                    cols = slice(c * 128, (c + 1) * 128)
                    p = jnp.exp2(tt[:, cols] - lse_v)
                    ps.append(p.astype(BF16))
                    dss.append((p * (dp[:, cols] - dl_v)).astype(BF16))
                p16 = jnp.concatenate(ps, axis=1)
                ds16 = jnp.concatenate(dss, axis=1)
                dv_scr[0:kw, :] += lax.dot_general(p16, dov, TN_DIMS, preferred_element_type=F32)
                dk_scr[0:kw, :] += lax.dot_general(ds16, qv, TN_DIMS, preferred_element_type=F32)
                qs = pl.ds(pl.multiple_of(i * blk + r * sub, sub), sub)
                dq_ref[qs, :] += jnp.dot(ds16, kc[:kw], preferred_element_type=F32)

        @pl.when(i == j)
        def _():
            tile(True)

        @pl.when(i > j)
        def _():
            tile(False)

        @pl.when(i == n - 1)
        def _():
            dkv_ref[:, 0:128] = (dk_scr[:, 0:128] * (1.0 / LOG2E)).astype(BF16)
            dkv_ref[:, 128:256] = dv_scr[...].astype(BF16)
            ks = pl.ds(pl.multiple_of(j * blk, blk), blk)
            dkp = dk_scr[:, 128:256] * (1.0 / LOG2E)

            @pl.when(head == 0)
            def _():
                dkp_ref[ks, :] = dkp

            @pl.when(head > 0)
            def _():
                dkp_ref[ks, :] += dkp

        @pl.when((head == MLA_HEADS - 1) & (step == len(pairs) - 1))
        def _():
            carry.finish(c_ins, c_outs, c_sems)

    grid_spec = pltpu.PrefetchScalarGridSpec(
        num_scalar_prefetch=2, grid=(MLA_HEADS, len(pairs)),
        in_specs=[pl.BlockSpec((blk, HEAD_PAD), lambda h, s, jt, it: (it[s], h)),
                  pl.BlockSpec((blk, 128), lambda h, s, jt, it: (it[s], h)),
                  pl.BlockSpec((1, blk, 128), lambda h, s, jt, it: (h, it[s], 0)),
                  pl.BlockSpec((1, blk, 128), lambda h, s, jt, it: (h, it[s], 0)),
                  pl.BlockSpec((blk, HEAD_PAD), lambda h, s, jt, it: (jt[s], h)),
                  pl.BlockSpec((blk, 128), lambda h, s, jt, it: (jt[s], 0))] + [HBM_SPEC] * n_carry,
        out_specs=[pl.BlockSpec((t, HEAD_PAD), lambda h, s, jt, it: (0, h)),
                   pl.BlockSpec((blk, HEAD_PAD), lambda h, s, jt, it: (jt[s], h)),
                   pl.BlockSpec((t, 128), lambda h, s, jt, it: (0, 0))] + [HBM_SPEC] * n_carry,
        scratch_shapes=[pltpu.VMEM((blk, HEAD_PAD), F32), pltpu.VMEM((blk, 128), F32)] + carry.sem_shapes,
    )
    return pl.pallas_call(
        body, name="mla_bwd", grid_spec=grid_spec,
        out_shape=[jax.ShapeDtypeStruct((t, MLA_HEADS * HEAD_PAD), F32),
                   jax.ShapeDtypeStruct((t, 2 * D_MODEL), BF16),
                   jax.ShapeDtypeStruct((t, 128), F32)] + carry.out_shapes,
        compiler_params=_params(58),
    )(j_tab, i_tab, q, do, lse, delta, kv, kpe, *carry.arrays)


def _mem_attn_probs(qv, k_ref):
    s = lax.dot_general(qv, k_ref[...], NT_DIMS, preferred_element_type=F32) * MEM_SCALE
    e = jnp.exp(s - jnp.max(s, axis=-1, keepdims=True))
    return e / jnp.sum(e, axis=-1, keepdims=True)


def _mem_fwd(proj_big, kv_m):
    t = proj_big.shape[0]
    tq = min(MEM_Q_BLK, t)
    hd = MEM_HEAD_DIM
    q0, z0 = (4 * D_MODEL) // hd, (5 * D_MODEL) // hd

    def body(q_ref, z_ref, k_ref, v_ref, y_ref):
        p = _mem_attn_probs(q_ref[...], k_ref)
        o = jnp.dot(p.astype(BF16), v_ref[...], preferred_element_type=F32)
        zs, _ = _silu_and_grad(z_ref[...].astype(F32))
        y_ref[...] = (o * zs).astype(BF16)

    return pl.pallas_call(
        body, name="mem_fwd", grid=(t // tq, MEM_HEADS),
        in_specs=[pl.BlockSpec((tq, hd), lambda i, h: (i, q0 + h)), pl.BlockSpec((tq, hd), lambda i, h: (i, z0 + h)),
                  pl.BlockSpec((kv_m.shape[0], hd), lambda i, h: (0, h)),
                  pl.BlockSpec((kv_m.shape[0], hd), lambda i, h: (0, MEM_HEADS + h))],
        out_specs=pl.BlockSpec((tq, hd), lambda i, h: (i, h)),
        out_shape=jax.ShapeDtypeStruct((t, D_MODEL), BF16),
    )(proj_big, proj_big, kv_m, kv_m)


def _mem_bwd(proj_big, kv_m, dym, dproj):
    t = proj_big.shape[0]
    tq = min(ROW_BLK, t)
    hd = MEM_HEAD_DIM
    d = D_MODEL
    mlen = kv_m.shape[0]

    def body(q_ref, z_ref, kv_ref, dy_ref, buf_ref, dqz_ref, dkv_ref):
        del buf_ref

        @pl.when(pl.program_id(0) == 0)
        def _():
            dkv_ref[...] = jnp.zeros_like(dkv_ref)

        for h in range(MEM_HEADS):
            cols = slice(h * hd, (h + 1) * hd)
            k_ref, v_ref = kv_ref.at[:, cols], kv_ref.at[:, d + h * hd:d + (h + 1) * hd]
            qv = q_ref[:, cols]
            p = _mem_attn_probs(qv, k_ref)
            p16 = p.astype(BF16)
            o = jnp.dot(p16, v_ref[...], preferred_element_type=F32)
            zs, zgrad = _silu_and_grad(z_ref[:, cols].astype(F32))
            dy = dy_ref[:, cols].astype(F32)
            dqz_ref[:, d + h * hd:d + (h + 1) * hd] = (dy * o * zgrad).astype(BF16)
            do16 = (dy * zs).astype(BF16)
            dkv_ref[:, d + h * hd:d + (h + 1) * hd] += lax.dot_general(p16, do16, TN_DIMS, preferred_element_type=F32)
            dp = lax.dot_general(do16, v_ref[...], NT_DIMS, preferred_element_type=F32)
            ds = (p * (dp - jnp.sum(dp * p, axis=-1, keepdims=True)) * MEM_SCALE).astype(BF16)
            dqz_ref[:, cols] = jnp.dot(ds, k_ref[...], preferred_element_type=F32).astype(BF16)
            dkv_ref[:, cols] += lax.dot_general(ds, qv, TN_DIMS, preferred_element_type=F32)

    return pl.pallas_call(
        body, name="mem_bwd", grid=(t // tq,),
        in_specs=[_row_spec(tq, d, 4), _row_spec(tq, d, 5), _full_spec((mlen, 2 * d)), _row_spec(tq, d), HBM_SPEC],
        out_specs=[_row_spec(tq, 2 * d, 2), _full_spec((mlen, 2 * d))],
        out_shape=[jax.ShapeDtypeStruct((t, 6 * d), BF16), jax.ShapeDtypeStruct((mlen, 2 * d), F32)],
        input_output_aliases={4: 0},
        compiler_params=_params(40),
    )(proj_big, proj_big, kv_m, dym, dproj)


HBM_SPEC = pl.BlockSpec(memory_space=pl.ANY)
N_PEERS = N_DEV - 1


def _dev_index(px, py, pc):
    return 4 * px + 2 * py + pc


class _Gather:
    def __init__(self, arrays):
        self.arrays = list(arrays)
        n = len(self.arrays)
        self.out_shapes = [jax.ShapeDtypeStruct((N_DEV,) + a.shape, a.dtype) for a in self.arrays]
        self.sem_shapes = [pltpu.SemaphoreType.DMA((n * N_PEERS,)), pltpu.SemaphoreType.DMA((n * N_PEERS,)),
                           pltpu.SemaphoreType.DMA((n,))]

    def _parts(self, ins, outs, sems):
        n = len(self.arrays)
        send_sems, recv_sems, local_sems = sems
        x, y, c = lax.axis_index("x"), lax.axis_index("y"), lax.axis_index("c")
        me, sibling = (x, y, c), (x, y, 1 - c)
        chips = [(1 - x, y), (x, 1 - y), (1 - x, 1 - y)]

        def copy(a, k, block, to, src=None):
            dst = outs[a].at[_dev_index(*block)]
            return pltpu.make_async_remote_copy(
                src_ref=dst if src is None else src, dst_ref=dst,
                send_sem=send_sems.at[a * N_PEERS + k], recv_sem=recv_sems.at[a * N_PEERS + k],
                device_id=to, device_id_type=MESH)

        mine = [pltpu.make_async_copy(ins[a], outs[a].at[_dev_index(*me)], local_sems.at[a]) for a in range(n)]
        first = []
        for a in range(n):
            first.append(copy(a, 0, me, sibling, src=ins[a]))
            first += [copy(a, 1 + j, me, (*chip, c), src=ins[a]) for j, chip in enumerate(chips)]
        return n, c, me, sibling, chips, copy, mine, first

    def start(self, ins, outs, sems):
        _, _, _, _, _, _, mine, first = self._parts(ins, outs, sems)
        for cp in mine + first:
            cp.start()

    def finish(self, ins, outs, sems):
        n, c, me, sibling, chips, copy, mine, first = self._parts(ins, outs, sems)
        passed = []
        for j, chip in enumerate(chips):
            for a in range(n):
                copy(a, 1 + j, (*chip, c), me).wait_recv()
                fwd = copy(a, 4 + j, (*chip, c), sibling)
                fwd.start()
                passed.append(fwd)
        for a in range(n):
            copy(a, 0, sibling, me).wait_recv()
            for j, chip in enumerate(chips):
                copy(a, 4 + j, (*chip, 1 - c), me).wait_recv()
        for cp in first + passed:
            cp.wait_send()
        for cp in mine:
            cp.wait()


class _AllToAll:
    def __init__(self, arrays):
        self.arrays = list(arrays)
        n = len(self.arrays)
        self.out_shapes = [jax.ShapeDtypeStruct(a.shape, a.dtype) for a in self.arrays]
        self.sem_shapes = [pltpu.SemaphoreType.DMA((n * N_PEERS,)), pltpu.SemaphoreType.DMA((n * N_PEERS,)),
                           pltpu.SemaphoreType.DMA((n,))]

    def _parts(self, ins, outs, sems):
        n = len(self.arrays)
        send_sems, recv_sems, local_sems = sems
        x, y, c = lax.axis_index("x"), lax.axis_index("y"), lax.axis_index("c")
        my_idx = _dev_index(x, y, c)
        peers = []
        for k in range(1, N_DEV):
            dx, dy, dc = (k >> 2) & 1, (k >> 1) & 1, k & 1
            peers.append((1 - x if dx else x, 1 - y if dy else y, 1 - c if dc else c))

        def copy(a, k, peer):
            return pltpu.make_async_remote_copy(
                src_ref=ins[a].at[_dev_index(*peer)], dst_ref=outs[a].at[my_idx],
                send_sem=send_sems.at[a * N_PEERS + k], recv_sem=recv_sems.at[a * N_PEERS + k],
                device_id=peer, device_id_type=MESH)

        def landed(a, k, peer):
            slot = outs[a].at[_dev_index(*peer)]
            return pltpu.make_async_remote_copy(
                src_ref=slot, dst_ref=slot,
                send_sem=send_sems.at[a * N_PEERS + k], recv_sem=recv_sems.at[a * N_PEERS + k],
                device_id=peer, device_id_type=MESH)

        mine = [pltpu.make_async_copy(ins[a].at[my_idx], outs[a].at[my_idx], local_sems.at[a]) for a in range(n)]
        sends = [copy(a, k, peer) for a in range(n) for k, peer in enumerate(peers)]
        return n, peers, landed, mine, sends

    def start(self, ins, outs, sems):
        _, _, _, mine, sends = self._parts(ins, outs, sems)
        for cp in mine + sends:
            cp.start()

    def finish(self, ins, outs, sems):
        n, peers, landed, mine, sends = self._parts(ins, outs, sems)
        for a in range(n):
            for k, peer in enumerate(peers):
                landed(a, k, peer).wait_recv()
        for cp in sends:
            cp.wait_send()
        for cp in mine:
            cp.wait()


def _exchange(plan, name):
    n = len(plan.arrays)

    def body(*refs):
        ins, outs, sems = refs[:n], refs[n:2 * n], refs[2 * n:]
        plan.start(ins, outs, sems)
        plan.finish(ins, outs, sems)

    return pl.pallas_call(
        body, name=name, in_specs=[HBM_SPEC] * n, out_specs=[HBM_SPEC] * n,
        out_shape=plan.out_shapes, scratch_shapes=plan.sem_shapes,
    )(*plan.arrays)


def _adamw(w, g, m, v):
    m = ADAM_B1 * m + (1.0 - ADAM_B1) * g
    v = ADAM_B2 * v + (1.0 - ADAM_B2) * jnp.square(g)
    m_hat = m / (1.0 - ADAM_B1 ** ADAM_STEP)
    v_hat = v / (1.0 - ADAM_B2 ** ADAM_STEP)
    delta = -ADAM_LR * (m_hat / (jnp.sqrt(v_hat) + ADAM_EPS) + ADAM_WD * w)
    return delta, m, v


def _adam_sharded(parts_list, w, m, v, name):
    shape = w.shape
    cols = shape[-1]
    rows = int(np.prod(shape[:-1]))
    tr = min(128, rows)
    parts_list = [p.reshape(N_DEV, -1, cols) for p in parts_list]
    bounds = np.cumsum([0] + [p.shape[1] // tr for p in parts_list])
    assert rows % tr == 0 and all(p.shape[1] % tr == 0 for p in parts_list) and bounds[-1] == rows // tr
    n_parts = len(parts_list)

    def body(*refs):
        p_refs = refs[:n_parts]
        w_ref, m_ref, v_ref, g_ref, d_ref, nm_ref, nv_ref = refs[n_parts:]
        i = pl.program_id(0)
        for k, p_ref in enumerate(p_refs):
            @pl.when((i >= bounds[k]) & (i < bounds[k + 1]))
            def _():
                g = p_ref[0].astype(F32)
                for e in range(1, N_DEV):
                    g = g + p_ref[e].astype(F32)
                g_ref[...] = g
                d_ref[...], nm_ref[...], nv_ref[...] = _adamw(w_ref[...], g, m_ref[...], v_ref[...])

    def part_spec(k):
        lo, hi = int(bounds[k]), int(bounds[k + 1])
        return pl.BlockSpec((N_DEV, tr, cols), lambda i: (0, jnp.clip(i, lo, hi - 1) - lo, 0))

    spec = pl.BlockSpec((tr, cols), lambda i: (i, 0))
    flat = jax.ShapeDtypeStruct((rows, cols), F32)
    outs = pl.pallas_call(
        body, name=name, grid=(rows // tr,),
        in_specs=[part_spec(k) for k in range(n_parts)] + [spec, spec, spec],
        out_specs=[spec] * 4, out_shape=[flat] * 4,
        compiler_params=_params(40),
    )(*parts_list, w.reshape(rows, cols), m.reshape(rows, cols), v.reshape(rows, cols))
    return [o.reshape(shape) for o in outs]


def _adam_replicated(parts, w, m, v):
    r = w.shape[0]

    def body(p_ref, w_ref, m_ref, v_ref, g_ref, d_ref, nm_ref, nv_ref):
        g = p_ref[0]
        for e in range(1, N_DEV):
            g = g + p_ref[e]
        g_ref[...] = g
        d_ref[...], nm_ref[...], nv_ref[...] = _adamw(w_ref[...], g, m_ref[...], v_ref[...])

    spec = _full_spec((r, 128))
    flat = jax.ShapeDtypeStruct((r, 128), F32)
    return pl.pallas_call(
        body, name="adam_replicated", grid=(1,),
        in_specs=[_full_spec((N_DEV, r, 128)), spec, spec, spec],
        out_specs=[spec] * 4, out_shape=[flat] * 4,
        compiler_params=_params(48),
    )(parts, w, m, v)


def _pack(arrays):
    parts = []
    for a in arrays:
        f = a.reshape(-1, 128)
        pad = -f.shape[0] % 8
        parts.append(jnp.pad(f, ((0, pad), (0, 0))) if pad else f)
    return jnp.concatenate(parts, axis=0)


def _unpack(packed, shapes):
    out, row = [], 0
    for shape in shapes:
        r = int(np.prod(shape)) // 128
        out.append(packed[row:row + r].reshape(shape))
        row += r + (-r % 8)
    return out


SHARDED = ("w_in", "w_uq", "w_ukv", "w_mem_kv", "w_gate", "w_branch", "w_out")
REPLICATED = ("g_pre", "a_ln_g", "a_ln_b", "a_w_s", "a_b_s", "q_norm_g", "kv_norm_g", "mem_norm_g", "b_gate", "g_post")
WEIGHT_ORDER = ("g_pre", "w_in", "a_ln_g", "a_ln_b", "a_w_s", "a_b_s", "q_norm_g", "w_uq", "kv_norm_g", "w_ukv",
                "mem_norm_g", "w_mem_kv", "w_gate", "b_gate", "w_branch", "w_out", "g_post")


def _unshard_cols(g):
    return g.transpose(1, 0, 2).reshape(g.shape[1], N_DEV * g.shape[2])


def _shard_cols(full):
    rows, n = full.shape
    return full.reshape(rows, N_DEV, n // N_DEV).transpose(1, 0, 2).astype(BF16)


def kernel(x, mem, positions, g_pre, w_in, a_ln_g, a_ln_b, a_w_s, a_b_s, q_norm_g, w_uq, kv_norm_g, w_ukv, mem_norm_g, w_mem_kv, w_gate, b_gate, w_branch, w_out, g_post, loss_target, m_g_pre, m_w_in, m_a_ln_g, m_a_ln_b, m_a_w_s, m_a_b_s, m_q_norm_g, m_w_uq, m_kv_norm_g, m_w_ukv, m_mem_norm_g, m_w_mem_kv, m_w_gate, m_b_gate, m_w_branch, m_w_out, m_g_post, v_g_pre, v_w_in, v_a_ln_g, v_a_ln_b, v_a_w_s, v_a_b_s, v_q_norm_g, v_w_uq, v_kv_norm_g, v_w_ukv, v_mem_norm_g, v_w_mem_kv, v_w_gate, v_b_gate, v_w_branch, v_w_out, v_g_post):
    weights = dict(g_pre=g_pre, w_in=w_in, a_ln_g=a_ln_g, a_ln_b=a_ln_b, a_w_s=a_w_s, a_b_s=a_b_s, q_norm_g=q_norm_g,
                   w_uq=w_uq, kv_norm_g=kv_norm_g, w_ukv=w_ukv, mem_norm_g=mem_norm_g, w_mem_kv=w_mem_kv,
                   w_gate=w_gate, b_gate=b_gate, w_branch=w_branch, w_out=w_out, g_post=g_post)
    mom1 = dict(g_pre=m_g_pre, w_in=m_w_in, a_ln_g=m_a_ln_g, a_ln_b=m_a_ln_b, a_w_s=m_a_w_s, a_b_s=m_a_b_s,
                q_norm_g=m_q_norm_g, w_uq=m_w_uq, kv_norm_g=m_kv_norm_g, w_ukv=m_w_ukv, mem_norm_g=m_mem_norm_g,
                w_mem_kv=m_w_mem_kv, w_gate=m_w_gate, b_gate=m_b_gate, w_branch=m_w_branch, w_out=m_w_out, g_post=m_g_post)
    mom2 = dict(g_pre=v_g_pre, w_in=v_w_in, a_ln_g=v_a_ln_g, a_ln_b=v_a_ln_b, a_w_s=v_a_w_s, a_b_s=v_a_b_s,
                q_norm_g=v_q_norm_g, w_uq=v_w_uq, kv_norm_g=v_kv_norm_g, w_ukv=v_w_ukv, mem_norm_g=v_mem_norm_g,
                w_mem_kv=v_w_mem_kv, w_gate=v_w_gate, b_gate=v_b_gate, w_branch=v_w_branch, w_out=v_w_out, g_post=v_g_post)
    d = D_MODEL
    t = x.shape[1]
    xs, tgt, mems = x[0], loss_target[0], mem[0]
    pos_col = positions.reshape(t, 1)

    shard16 = {n: weights[n][0].astype(BF16) for n in SHARDED}
    h, g_in = _pre_norm(xs, g_pre, _Gather([shard16["w_in"]]))
    w_in_full = _unshard_cols(g_in)
    lat0, lat1 = 3 * d, 3 * d + Q_LORA + KV_LORA + QK_ROPE
    w_big = jnp.concatenate([w_in_full[:, :lat0], w_in_full[:, lat1:]], axis=1)
    w_lat = jnp.concatenate([w_in_full[:, lat0:lat1], jnp.zeros((d, LAT_W - (lat1 - lat0)), BF16)], axis=1)

    inv_freq = 1.0 / (ROPE_THETA ** (jnp.arange(0, QK_ROPE, 2, dtype=F32) / QK_ROPE))
    inv_freq_lanes = jnp.concatenate([inv_freq, inv_freq, jnp.zeros((128 - QK_ROPE,), F32)]).reshape(1, 128)
    ws = a_w_s[0]
    b_exp = jnp.broadcast_to(a_b_s[0][:, :, None], (A_GROUPS, CHUNK, 128))

    proj_big, g_uq, g_ukv, g_gate, g_mem = _mm(
        h, w_big, name="proj_big", tm=1024, tn=1024, tk=2048,
        carry=_Gather([shard16[n] for n in ("w_uq", "w_ukv", "w_gate", "w_mem_kv")]))
    w_uq_p = jnp.pad(_unshard_cols(g_uq).reshape(Q_LORA, MLA_HEADS, QK_DIM),
                     ((0, 0), (0, 0), (0, HEAD_PAD - QK_DIM))).reshape(Q_LORA, MLA_HEADS * HEAD_PAD)
    w_ukv_f = _unshard_cols(g_ukv)
    w_mem_f = _unshard_cols(g_mem)
    w_gate_f = _unshard_cols(g_gate)
    proj_lat = _mm(h, w_lat, name="proj_lat", tm=1024, tn=LAT_W, tk=2048)
    gates, g_br, g_out = _mm(h, w_gate_f, name="gates", tm=1024, tn=1024, tk=2048, bias=b_gate, act="sigmoid",
                             carry=_Gather([shard16["w_branch"], shard16["w_out"]]))
    w_br_f = g_br.transpose(1, 0, 2, 3).reshape(3, d, d)
    w_out_f = g_out.reshape(d, d)
    c_tab, sa_tab, sb_tab = _rope_tables(pos_col, inv_freq_lanes)
    cqn, ckvn, kpe = _latent_norms(proj_lat, q_norm_g, kv_norm_g, c_tab, sa_tab, sb_tab)
    q = _mm(cqn, w_uq_p, name="q_up", tm=1024, tn=1024, tk=512, post=_rope_q_tile, post_rows=(c_tab, sa_tab, sb_tab))
    kv = _mm(ckvn, w_ukv_f, name="kv_up", tm=1024, tn=1024, tk=512)
    o_b, y_b, lse = _mla_fwd(q, kv, kpe, proj_big)
    memn = _mem_norm(mems, mem_norm_g)
    kv_m = _mm(memn, w_mem_f, name="mem_kv", tm=256, tn=1024, tk=2048)
    y_m = _mem_fwd(proj_big, kv_m)
    y_a = _gmlp_fwd(proj_big, a_ln_g, a_ln_b, ws, b_exp)
    ys = (y_a, y_b, y_m)
    ps = [_mm(ys[n], w_br_f[n], name=f"branch{n}", tm=1024, tn=1024, tk=2048) for n in range(3)]
    merged = _merge(gates, *ps)
    out = _mm(merged, w_out_f, name="out_proj", tm=1024, tn=1024, tk=2048, out_dtype=F32)
    d_out, dy, loss_blk, dg_post = _post_loss(out, xs, tgt, g_post)

    dmerged = _mm(d_out, w_out_f, name="d_merged", tb=True, tm=1024, tn=1024, tk=2048)
    dw_out = _mm(merged, d_out, name="dw_out", ta=True, tm=1024, tn=1024, tk=2048)
    dp_a, dp_b, dp_m, dgl, db_gate = _merge_bwd(dmerged, gates, *ps)
    dps = (dp_a, dp_b, dp_m)
    dys = [_mm(dps[n], w_br_f[n], name=f"d_y{n}", tb=True, tm=1024, tn=1024, tk=2048) for n in range(3)]
    dw_br = [_mm(ys[n], dps[n], name=f"dw_branch{n}", ta=True, tm=1024, tn=1024, tk=2048) for n in range(3)]
    dw_gate = _mm(h, dgl, name="dw_gate", ta=True, tm=1024, tn=1024, tk=2048)

    dproj_big, dws, dbs, dlng, dlnb = _gmlp_bwd(proj_big, dys[0], a_ln_g, a_ln_b, ws, b_exp)

    recv = {}
    send = [_shard_cols(dw_gate), jnp.stack(dw_br).reshape(3, N_DEV, d // N_DEV, d).transpose(1, 0, 2, 3),
            dw_out.reshape(N_DEV, d // N_DEV, d)]
    do_b, dproj_big, delta = _mla_gate_bwd(dys[1], proj_big, o_b, dproj_big)
    dq, dkv, dkpe, r_gate, r_br, r_out = _mla_bwd(q, kv, kpe, do_b, lse, delta, _AllToAll(send))
    recv["w_gate"], recv["w_branch"], recv["w_out"] = [r_gate], [r_br], [r_out]
    dq_raw, dkr = _rope_q_bwd(dq, dkpe, c_tab, sa_tab, sb_tab)
    dcqn = _mm(dq_raw, w_uq_p, name="d_cq", tb=True, tm=1024, tn=Q_LORA, tk=2048, out_dtype=F32)
    dw_uq_p = _mm(cqn, dq_raw, name="dw_uq", ta=True, tm=Q_LORA, tn=1024, tk=2048)
    dckvn = _mm(dkv, w_ukv_f, name="d_ckv", tb=True, tm=1024, tn=KV_LORA, tk=2048, out_dtype=F32)
    dw_ukv = _mm(ckvn, dkv, name="dw_ukv", ta=True, tm=KV_LORA, tn=1024, tk=2048)
    dproj_lat, dqg, dkg = _latent_norms_bwd(proj_lat, q_norm_g, kv_norm_g, dcqn, dckvn, dkr)

    dproj_big, dkv_m32 = _mem_bwd(proj_big, kv_m, dys[2], dproj_big)
    dkv_m = dkv_m32.astype(BF16)
    dw_mem = _mm(memn, dkv_m, name="dw_mem", ta=True, tm=1024, tn=1024, tk=256)

    dw_uq_full = dw_uq_p.reshape(Q_LORA, MLA_HEADS, HEAD_PAD)[:, :, :QK_DIM].reshape(Q_LORA, MLA_HEADS * QK_DIM)
    dw_big, r_uq, r_ukv, r_mem = _mm(
        h, dproj_big, name="dw_big", ta=True, tm=1024, tn=1024, tk=2048,
        carry=_AllToAll([_shard_cols(dw_uq_full), _shard_cols(dw_ukv), _shard_cols(dw_mem)]))
    recv["w_uq"], recv["w_ukv"], recv["w_mem_kv"] = [r_uq], [r_ukv], [r_mem]
    dw_lat = _mm(h, dproj_lat, name="dw_lat", ta=True, tm=1024, tn=LAT_W, tk=2048)
    dw_in_full = jnp.concatenate([dw_big[:, :lat0], dw_lat[:, :lat1 - lat0], dw_big[:, lat0:]], axis=1)
    send_in = _shard_cols(dw_in_full)
    dmemn = _mm(dkv_m, w_mem_f, name="d_memn", tb=True, tm=256, tn=1024, tk=2048, out_dtype=F32)
    dg_mem = _mem_norm_bwd(mems, mem_norm_g, dmemn)
    dh, r_in0 = _mm(dgl, w_gate_f, name="dh_gate", tb=True, tm=1024, tn=1024, tk=2048, out_dtype=F32,
                    carry=_AllToAll([send_in[:, :d // 2]]))
    dh = _mm(dproj_lat, w_lat, name="dh_lat", tb=True, tm=1024, tn=1024, tk=LAT_W, out_dtype=F32, add=dh)
    dh, r_in1 = _mm(dproj_big, w_big, name="dh_big", tb=True, tm=1024, tn=1024, tk=2048, out_dtype=F32, add=dh,
                    carry=_AllToAll([send_in[:, d // 2:]]))
    recv["w_in"] = [r_in0, r_in1]
    grad_x, dg_pre = _pre_norm_bwd(xs, g_pre, dh, dy)

    results = {}
    for n in SHARDED:
        results[n] = [r[None] for r in _adam_sharded(recv[n], weights[n][0], mom1[n][0], mom2[n][0], "adam_" + n)]

    small = dict(g_pre=dg_pre, a_ln_g=dlng, a_ln_b=dlnb, a_w_s=dws, a_b_s=dbs, q_norm_g=dqg, kv_norm_g=dkg,
                 mem_norm_g=dg_mem, b_gate=db_gate, g_post=dg_post)
    (parts,) = _exchange(_Gather([_pack([small[n] for n in REPLICATED])]), "gather_small_grads")
    packed = _adam_replicated(parts, _pack([weights[n] for n in REPLICATED]), _pack([mom1[n] for n in REPLICATED]),
                              _pack([mom2[n] for n in REPLICATED]))
    shapes = [weights[n].shape for n in REPLICATED]
    unpacked = [_unpack(p, shapes) for p in packed]
    for i, n in enumerate(REPLICATED):
        results[n] = [u[i] for u in unpacked]

    loss = lax.psum(loss_blk[0, 0], AXES)
    outs = [loss, grad_x[None]]
    for kind in range(4):
        outs += [results[n][kind] for n in WEIGHT_ORDER]
    return tuple(outs)
```

```python
import functools
import math

import jax
import jax.numpy as jnp
import numpy as np
from jax import lax
from jax.experimental import pallas as pl
from jax.experimental.pallas import tpu as pltpu

F32 = jnp.float32
BF16 = jnp.bfloat16
MESH = pl.DeviceIdType.MESH
AXES = ("x", "y", "c")
N_DEV = 8

D_MODEL = 2048
EPS = 1e-6
CHUNK = 128
A_GROUPS = 16
MLA_HEADS = 16
QK_NOPE = 128
QK_ROPE = 64
QK_DIM = QK_NOPE + QK_ROPE
HEAD_PAD = 256
Q_LORA = 512
KV_LORA = 512
MEM_HEADS = 4
MEM_HEAD_DIM = 512
ROPE_THETA = 10000.0
MLA_SCALE = QK_DIM ** -0.5
MEM_SCALE = MEM_HEAD_DIM ** -0.5
NEG = -1e30
LOG2E = 1.4426950408889634

ADAM_LR = 0.001
ADAM_B1 = 0.9
ADAM_B2 = 0.999
ADAM_EPS = 1e-08
ADAM_WD = 0.01
ADAM_STEP = 10

BIG_W = 6 * D_MODEL
LAT_W = Q_LORA + KV_LORA + 128

VMEM_MIB = 1024 * 1024

ROW_BLK = 256
ATT_BLK = 2048
ATT_BLK_FWD = 2048
ATT_SUB = 256
GMLP_ROWS = 256
MEM_Q_BLK = 512


def _params(vmem_mib, **kw):
    return pltpu.CompilerParams(vmem_limit_bytes=int(vmem_mib * VMEM_MIB), **kw)


def _gelu(x):
    k = math.sqrt(2.0 / math.pi)
    t = jnp.tanh(k * (x + 0.044715 * (x * x * x)))
    return 0.5 * x * (1.0 + t)


def _gelu_and_grad(x):
    k = math.sqrt(2.0 / math.pi)
    x2 = x * x
    t = jnp.tanh(k * (x + 0.044715 * (x2 * x)))
    val = 0.5 * x * (1.0 + t)
    grad = 0.5 * (1.0 + t) + 0.5 * x * (1.0 - t * t) * (k * (1.0 + 3.0 * 0.044715 * x2))
    return val, grad


def _silu_and_grad(z):
    s = jax.nn.sigmoid(z)
    return z * s, s * (1.0 + z * (1.0 - s))


def _mm(a, b, *, name, tm, tn, tk, ta=False, tb=False, out_dtype=BF16, bias=None, act=None, add=None, carry=None,
        post=None, post_rows=(), col_shards=False):
    m = a.shape[1] if ta else a.shape[0]
    k = a.shape[0] if ta else a.shape[1]
    n = b.shape[0] if tb else b.shape[1]
    assert k == (b.shape[1] if tb else b.shape[0])
    tm, tn, tk = min(tm, m), min(tn, n), min(tk, k)
    assert m % tm == 0 and n % tn == 0 and k % tk == 0, (name, m, n, k, tm, tn, tk)
    nk = k // tk
    a_spec = pl.BlockSpec((tk, tm), lambda i, j, kk: (kk, i)) if ta else pl.BlockSpec((tm, tk), lambda i, j, kk: (i, kk))
    b_spec = pl.BlockSpec((tn, tk), lambda i, j, kk: (j, kk)) if tb else pl.BlockSpec((tk, tn), lambda i, j, kk: (kk, j))
    dn = (((0 if ta else 1,), (1 if tb else 0,)), ((), ()))
    operands, in_specs = [a, b], [a_spec, b_spec]
    if bias is not None:
        operands.append(bias)
        in_specs.append(pl.BlockSpec((1, tn), lambda i, j, kk: (0, j)))
    if add is not None:
        operands.append(add)
        in_specs.append(pl.BlockSpec((tm, tn), lambda i, j, kk: (i, j)))
    n_fixed = len(operands)
    for r in post_rows:
        operands.append(r)
        in_specs.append(pl.BlockSpec((tm, r.shape[1]), lambda i, j, kk: (i, 0)))

    n_in = len(operands)
    n_carry = len(carry.arrays) if carry is not None else 0
    n_acc = 1 if nk > 1 else 0
    grid = (m // tm, n // tn, nk)

    def body(*refs):
        a_ref, b_ref = refs[0], refs[1]
        pos = 2
        bias_ref = add_ref = None
        if bias is not None:
            bias_ref = refs[pos]
            pos += 1
        if add is not None:
            add_ref = refs[pos]
            pos += 1
        o_ref = refs[n_in + n_carry]
        pos = n_in + n_carry
        if carry is not None:
            c_ins = refs[n_in:n_in + n_carry]
            c_outs = refs[n_in + n_carry + 1:n_in + 2 * n_carry + 1]
            c_sems = refs[n_in + 2 * n_carry + 1 + n_acc:]
            ids = [pl.program_id(ax) for ax in range(3)]

            @pl.when((ids[0] == 0) & (ids[1] == 0) & (ids[2] == 0))
            def _():
                carry.start(c_ins, c_outs, c_sems)

        part = lax.dot_general(a_ref[...], b_ref[...], dn, preferred_element_type=F32)

        def finish(acc):
            if bias_ref is not None:
                acc = acc + bias_ref[...]
            if act == "sigmoid":
                acc = jax.nn.sigmoid(acc)
            if add_ref is not None:
                acc = acc + add_ref[...]
            if post is not None:
                acc = post(acc, *[r[...] for r in refs[n_fixed:n_in]])
            o_ref[...] = acc.astype(o_ref.dtype)

        if nk == 1:
            finish(part)
        else:
            acc_ref = refs[n_in + 2 * n_carry + 1]
            kk = pl.program_id(2)

            @pl.when(kk == 0)
            def _():
                acc_ref[...] = part

            @pl.when(kk > 0)
            def _():
                acc_ref[...] += part

            @pl.when(kk == nk - 1)
            def _():
                finish(acc_ref[...])

        if carry is not None:
            @pl.when((ids[0] == grid[0] - 1) & (ids[1] == grid[1] - 1) & (ids[2] == grid[2] - 1))
            def _():
                carry.finish(c_ins, c_outs, c_sems)

    osz = jnp.dtype(out_dtype).itemsize
    est = 2 * 2 * (tm * tk + tk * tn) + 2 * osz * tm * tn + 8 * tm * tn + (2 * 4 * tm * tn if add is not None else 0)
    if col_shards:
        assert n // tn == N_DEV
        main_spec = pl.BlockSpec((None, tm, tn), lambda i, j, kk: (j, i, 0))
        main_shape = jax.ShapeDtypeStruct((N_DEV, m, tn), out_dtype)
    else:
        main_spec = pl.BlockSpec((tm, tn), lambda i, j, kk: (i, j))
        main_shape = jax.ShapeDtypeStruct((m, n), out_dtype)
    scratch = [pltpu.VMEM((tm, tn), F32)] if nk > 1 else []
    if carry is None:
        return pl.pallas_call(
            body, name=name, grid=grid, in_specs=in_specs, out_specs=main_spec, out_shape=main_shape,
            scratch_shapes=scratch, compiler_params=_params(min(56, est / VMEM_MIB + 12)),
        )(*operands)
    return pl.pallas_call(
        body, name=name, grid=grid,
        in_specs=in_specs + [HBM_SPEC] * n_carry,
        out_specs=[main_spec] + [HBM_SPEC] * n_carry,
        out_shape=[main_shape] + carry.out_shapes,
        scratch_shapes=scratch + carry.sem_shapes,
        compiler_params=_params(min(56, est / VMEM_MIB + 12)),
    )(*operands, *carry.arrays)


def _row_spec(tr, cols, col_blk=0):
    return pl.BlockSpec((tr, cols), lambda i: (i, col_blk))


def _full_spec(shape):
    nd = len(shape)
    return pl.BlockSpec(shape, lambda i: (0,) * nd)


def _pre_norm(x, g_pre, carry):
    t, d = x.shape
    tr = min(ROW_BLK, t)
    n_carry = len(carry.arrays)
    steps = t // tr

    def body(x_ref, g_ref, *rest):
        c_ins, h_ref = rest[:n_carry], rest[n_carry]
        c_outs, c_sems = rest[n_carry + 1:2 * n_carry + 1], rest[2 * n_carry + 1:]

        @pl.when(pl.program_id(0) == 0)
        def _():
            carry.start(c_ins, c_outs, c_sems)

        xv = x_ref[...]
        r = lax.rsqrt(jnp.mean(xv * xv, axis=-1, keepdims=True) + EPS)
        h_ref[...] = ((xv * r) * g_ref[...]).astype(BF16)

        @pl.when(pl.program_id(0) == steps - 1)
        def _():
            carry.finish(c_ins, c_outs, c_sems)

    return pl.pallas_call(
        body, name="pre_norm", grid=(steps,),
        in_specs=[_row_spec(tr, d), _full_spec((1, d))] + [HBM_SPEC] * n_carry,
        out_specs=[_row_spec(tr, d)] + [HBM_SPEC] * n_carry,
        out_shape=[jax.ShapeDtypeStruct((t, d), BF16)] + carry.out_shapes,
        scratch_shapes=carry.sem_shapes,
        compiler_params=_params(32),
    )(x, g_pre, *carry.arrays)


def _rope_tables(pos_col, inv_freq_lanes):
    t = pos_col.shape[0]
    tr = min(ROW_BLK, t)

    def body(p_ref, f_ref, c_ref, sa_ref, sb_ref):
        ang = p_ref[...].astype(F32) * f_ref[...]
        lane = lax.broadcasted_iota(jnp.int32, ang.shape, 1)
        cos, sin = jnp.cos(ang), jnp.sin(ang)
        c_ref[...] = jnp.where(lane < QK_ROPE, cos, 0.0)
        sa_ref[...] = jnp.where(lane < QK_ROPE // 2, sin, 0.0)
        sb_ref[...] = jnp.where((lane >= QK_ROPE // 2) & (lane < QK_ROPE), sin, 0.0)

    tab = jax.ShapeDtypeStruct((t, 128), F32)
    return pl.pallas_call(
        body, name="rope_tables", grid=(t // tr,),
        in_specs=[_row_spec(tr, 1), _full_spec((1, 128))],
        out_specs=[_row_spec(tr, 128)] * 3,
        out_shape=[tab, tab, tab],
    )(pos_col, inv_freq_lanes)


def _rope_fwd(p, c, sa, sb):
    return p * c - pltpu.roll(p, 96, 1) * sa + pltpu.roll(p, 32, 1) * sb


def _rope_bwd(g, c, sa, sb):
    return g * c + pltpu.roll(g, 96, 1) * sa - pltpu.roll(g, 32, 1) * sb


def _rms(xv, g):
    r = lax.rsqrt(jnp.mean(xv * xv, axis=-1, keepdims=True) + EPS)
    return (xv * r) * g


def _rms_bwd(xv, g, dout):
    r = lax.rsqrt(jnp.mean(xv * xv, axis=-1, keepdims=True) + EPS)
    xn = xv * r
    dg = jnp.sum(dout * xn, axis=0, keepdims=True)
    dxn = dout * g
    dx = r * (dxn - xn * jnp.mean(dxn * xn, axis=-1, keepdims=True))
    return dx, dg


def _latent_norms(proj_lat, q_norm_g, kv_norm_g, c_tab, sa_tab, sb_tab):
    t = proj_lat.shape[0]
    tr = min(ROW_BLK, t)

    def body(cq_ref, ckv_ref, kr_ref, qg_ref, kg_ref, c_ref, sa_ref, sb_ref, cqn_ref, ckvn_ref, kpe_ref):
        cqn_ref[...] = _rms(cq_ref[...].astype(F32), qg_ref[...]).astype(BF16)
        ckvn_ref[...] = _rms(ckv_ref[...].astype(F32), kg_ref[...]).astype(BF16)
        kpe_ref[...] = _rope_fwd(kr_ref[...].astype(F32), c_ref[...], sa_ref[...], sb_ref[...]).astype(BF16)

    return pl.pallas_call(
        body, name="latent_norms", grid=(t // tr,),
        in_specs=[_row_spec(tr, Q_LORA, 0), _row_spec(tr, KV_LORA, 1), _row_spec(tr, 128, (Q_LORA + KV_LORA) // 128),
                  _full_spec((1, Q_LORA)), _full_spec((1, KV_LORA)),
                  _row_spec(tr, 128), _row_spec(tr, 128), _row_spec(tr, 128)],
        out_specs=[_row_spec(tr, Q_LORA), _row_spec(tr, KV_LORA), _row_spec(tr, 128)],
        out_shape=[jax.ShapeDtypeStruct((t, Q_LORA), BF16), jax.ShapeDtypeStruct((t, KV_LORA), BF16),
                   jax.ShapeDtypeStruct((t, 128), BF16)],
    )(proj_lat, proj_lat, proj_lat, q_norm_g, kv_norm_g, c_tab, sa_tab, sb_tab)


def _rope_q_tile(acc, c, sa, sb):
    qs = MLA_SCALE * LOG2E
    parts = []
    for h in range(acc.shape[1] // HEAD_PAD):
        parts.append(acc[:, h * HEAD_PAD:h * HEAD_PAD + 128] * qs)
        parts.append(_rope_fwd(acc[:, h * HEAD_PAD + 128:(h + 1) * HEAD_PAD], c, sa, sb) * qs)
    return jnp.concatenate(parts, axis=1)


def _rope_q_bwd(dq, dkpe, c_tab, sa_tab, sb_tab):
    t = dq.shape[0]
    tr = min(ROW_BLK, t)

    def body(dq_ref, dkp_ref, c_ref, sa_ref, sb_ref, o_ref, dkr_ref):
        c, sa, sb = c_ref[...], sa_ref[...], sb_ref[...]
        for h in range(MLA_HEADS):
            o_ref[:, h * HEAD_PAD:h * HEAD_PAD + 128] = (dq_ref[:, h * HEAD_PAD:h * HEAD_PAD + 128] * MLA_SCALE).astype(BF16)
            g = dq_ref[:, h * HEAD_PAD + 128:(h + 1) * HEAD_PAD] * MLA_SCALE
            o_ref[:, h * HEAD_PAD + 128:(h + 1) * HEAD_PAD] = _rope_bwd(g, c, sa, sb).astype(BF16)
        dkr_ref[...] = _rope_bwd(dkp_ref[...], c, sa, sb)

    w = MLA_HEADS * HEAD_PAD
    return pl.pallas_call(
        body, name="rope_q_bwd", grid=(t // tr,),
        in_specs=[_row_spec(tr, w), _row_spec(tr, 128),
                  _row_spec(tr, 128), _row_spec(tr, 128), _row_spec(tr, 128)],
        out_specs=[_row_spec(tr, w), _row_spec(tr, 128)],
        out_shape=[jax.ShapeDtypeStruct((t, w), BF16), jax.ShapeDtypeStruct((t, 128), F32)],
        compiler_params=_params(48),
    )(dq, dkpe, c_tab, sa_tab, sb_tab)


def _latent_norms_bwd(proj_lat, q_norm_g, kv_norm_g, dcqn, dckvn, dkr):
    t = proj_lat.shape[0]
    tr = min(ROW_BLK, t)

    def body(cq_ref, ckv_ref, qg_ref, kg_ref, dcqn_ref, dckvn_ref, dkr_ref, dl_ref, dqg_ref, dkg_ref):
        dcq, dqg = _rms_bwd(cq_ref[...].astype(F32), qg_ref[...], dcqn_ref[...])
        dckv, dkg = _rms_bwd(ckv_ref[...].astype(F32), kg_ref[...], dckvn_ref[...])
        dl_ref[:, 0:Q_LORA] = dcq.astype(BF16)
        dl_ref[:, Q_LORA:Q_LORA + KV_LORA] = dckv.astype(BF16)
        dl_ref[:, Q_LORA + KV_LORA:LAT_W] = dkr_ref[...].astype(BF16)

        @pl.when(pl.program_id(0) == 0)
        def _():
            dqg_ref[...] = jnp.zeros_like(dqg_ref)
            dkg_ref[...] = jnp.zeros_like(dkg_ref)

        dqg_ref[...] += dqg
        dkg_ref[...] += dkg

    return pl.pallas_call(
        body, name="latent_norms_bwd", grid=(t // tr,),
        in_specs=[_row_spec(tr, Q_LORA, 0), _row_spec(tr, KV_LORA, 1), _full_spec((1, Q_LORA)), _full_spec((1, KV_LORA)),
                  _row_spec(tr, Q_LORA), _row_spec(tr, KV_LORA), _row_spec(tr, 128)],
        out_specs=[_row_spec(tr, LAT_W), _full_spec((1, Q_LORA)), _full_spec((1, KV_LORA))],
        out_shape=[jax.ShapeDtypeStruct((t, LAT_W), BF16), jax.ShapeDtypeStruct((1, Q_LORA), F32),
                   jax.ShapeDtypeStruct((1, KV_LORA), F32)],
    )(proj_lat, proj_lat, q_norm_g, kv_norm_g, dcqn, dckvn, dkr)


def _mem_norm(mem, g):
    m, d = mem.shape

    def body(x_ref, g_ref, o_ref):
        o_ref[...] = _rms(x_ref[...], g_ref[...]).astype(BF16)

    return pl.pallas_call(
        body, name="mem_norm", grid=(1,),
        in_specs=[_full_spec((m, d)), _full_spec((1, d))],
        out_specs=_full_spec((m, d)),
        out_shape=jax.ShapeDtypeStruct((m, d), BF16),
    )(mem, g)


def _mem_norm_bwd(mem, g, dmemn):
    m, d = mem.shape

    def body(x_ref, g_ref, d_ref, dg_ref):
        _, dg = _rms_bwd(x_ref[...], g_ref[...], d_ref[...])
        dg_ref[...] = dg

    return pl.pallas_call(
        body, name="mem_norm_bwd", grid=(1,),
        in_specs=[_full_spec((m, d)), _full_spec((1, d)), _full_spec((m, d))],
        out_specs=_full_spec((1, d)),
        out_shape=jax.ShapeDtypeStruct((1, d), F32),
    )(mem, g, dmemn)


def _merge(gates, p_a, p_b, p_m):
    t, d = p_a.shape
    tr = min(ROW_BLK, t)

    def body(ga_ref, gb_ref, gm_ref, pa_ref, pb_ref, pm_ref, o_ref):
        acc = ga_ref[...].astype(F32) * pa_ref[...].astype(F32)
        acc = acc + gb_ref[...].astype(F32) * pb_ref[...].astype(F32)
        acc = acc + gm_ref[...].astype(F32) * pm_ref[...].astype(F32)
        o_ref[...] = acc.astype(BF16)

    return pl.pallas_call(
        body, name="merge", grid=(t // tr,),
        in_specs=[_row_spec(tr, d, 0), _row_spec(tr, d, 1), _row_spec(tr, d, 2),
                  _row_spec(tr, d), _row_spec(tr, d), _row_spec(tr, d)],
        out_specs=_row_spec(tr, d),
        out_shape=jax.ShapeDtypeStruct((t, d), BF16),
        compiler_params=_params(48),
    )(gates, gates, gates, p_a, p_b, p_m)


def _merge_bwd(dm, gates, p_a, p_b, p_m):
    t, d = dm.shape
    tr = min(ROW_BLK, t)

    def body(dm_ref, g_ref, pa_ref, pb_ref, pm_ref, dpa_ref, dpb_ref, dpm_ref, dgl_ref, db_ref):
        dmv = dm_ref[...].astype(F32)

        @pl.when(pl.program_id(0) == 0)
        def _():
            db_ref[...] = jnp.zeros_like(db_ref)

        for n, (p_ref, dp_ref) in enumerate(((pa_ref, dpa_ref), (pb_ref, dpb_ref), (pm_ref, dpm_ref))):
            g = g_ref[:, n * d:(n + 1) * d].astype(F32)
            dp_ref[...] = (dmv * g).astype(BF16)
            dgl = dmv * p_ref[...].astype(F32) * (g * (1.0 - g))
            dgl_ref[:, n * d:(n + 1) * d] = dgl.astype(BF16)
            db_ref[:, n * d:(n + 1) * d] += jnp.sum(dgl, axis=0, keepdims=True)

    act = jax.ShapeDtypeStruct((t, d), BF16)
    return pl.pallas_call(
        body, name="merge_bwd", grid=(t // tr,),
        in_specs=[_row_spec(tr, d), _row_spec(tr, 3 * d), _row_spec(tr, d), _row_spec(tr, d), _row_spec(tr, d)],
        out_specs=[_row_spec(tr, d), _row_spec(tr, d), _row_spec(tr, d), _row_spec(tr, 3 * d), _full_spec((1, 3 * d))],
        out_shape=[act, act, act, jax.ShapeDtypeStruct((t, 3 * d), BF16), jax.ShapeDtypeStruct((1, 3 * d), F32)],
        compiler_params=_params(56),
    )(dm, gates, p_a, p_b, p_m)


def _post_loss(out, x, tgt, g_post):
    t, d = out.shape
    tr = min(ROW_BLK, t)

    def body(o_ref, x_ref, t_ref, g_ref, do_ref, dy_ref, loss_ref, dg_ref):
        ov = o_ref[...]
        g = g_ref[...]
        r = lax.rsqrt(jnp.mean(ov * ov, axis=-1, keepdims=True) + EPS)
        on = ov * r
        err = (x_ref[...] + on * g) - t_ref[...]
        dy = err * (1.0 / d)
        dy_ref[...] = dy
        don = dy * g
        do_ref[...] = (r * (don - on * jnp.mean(don * on, axis=-1, keepdims=True))).astype(BF16)

        @pl.when(pl.program_id(0) == 0)
        def _():
            loss_ref[...] = jnp.zeros_like(loss_ref)
            dg_ref[...] = jnp.zeros_like(dg_ref)

        loss_ref[...] += 0.5 * jnp.sum(jnp.mean(err * err, axis=-1, keepdims=True))
        dg_ref[...] += jnp.sum(dy * on, axis=0, keepdims=True)

    return pl.pallas_call(
        body, name="post_loss", grid=(t // tr,),
        in_specs=[_row_spec(tr, d), _row_spec(tr, d), _row_spec(tr, d), _full_spec((1, d))],
        out_specs=[_row_spec(tr, d), _row_spec(tr, d), _full_spec((8, 128)), _full_spec((1, d))],
        out_shape=[jax.ShapeDtypeStruct((t, d), BF16), jax.ShapeDtypeStruct((t, d), F32),
                   jax.ShapeDtypeStruct((8, 128), F32), jax.ShapeDtypeStruct((1, d), F32)],
        compiler_params=_params(56),
    )(out, x, tgt, g_post)


def _pre_norm_bwd(x, g_pre, dh, dy):
    t, d = x.shape
    tr = min(ROW_BLK, t)

    def body(x_ref, g_ref, dh_ref, dy_ref, dx_ref, dg_ref):
        dx, dg = _rms_bwd(x_ref[...], g_ref[...], dh_ref[...])
        dx_ref[...] = dx + dy_ref[...]

        @pl.when(pl.program_id(0) == 0)
        def _():
            dg_ref[...] = jnp.zeros_like(dg_ref)

        dg_ref[...] += dg

    return pl.pallas_call(
        body, name="pre_norm_bwd", grid=(t // tr,),
        in_specs=[_row_spec(tr, d), _full_spec((1, d)), _row_spec(tr, d), _row_spec(tr, d)],
        out_specs=[_row_spec(tr, d), _full_spec((1, d))],
        out_shape=[jax.ShapeDtypeStruct((t, d), F32), jax.ShapeDtypeStruct((1, d), F32)],
        compiler_params=_params(56),
    )(x, g_pre, dh, dy)


def _causal_mask(n):
    row = lax.broadcasted_iota(jnp.int32, (n, n), 0)
    col = lax.broadcasted_iota(jnp.int32, (n, n), 1)
    return row >= col


def _layernorm_stats(vg):
    mu = jnp.mean(vg, axis=-1, keepdims=True)
    cen = vg - mu
    rstd = lax.rsqrt(jnp.mean(cen * cen, axis=-1, keepdims=True) + EPS)
    return cen * rstd, rstd


def _gmlp_fwd(proj_big, ln_g, ln_b, w_s, b_exp):
    t = proj_big.shape[0]
    rows = min(GMLP_ROWS, t)
    d = D_MODEL

    def body(u_ref, v_ref, z_ref, lg_ref, lb_ref, ws_ref, be_ref, y_ref, vn_scr):
        vhat, _ = _layernorm_stats(_gelu(v_ref[...].astype(F32)))
        vn_scr[...] = (vhat * lg_ref[...] + lb_ref[...]).astype(BF16)
        mask = _causal_mask(CHUNK)
        for g in range(A_GROUPS):
            cols = slice(g * 128, (g + 1) * 128)
            wsm = jnp.where(mask, ws_ref[g], 0.0).astype(BF16)
            for c in range(rows // CHUNK):
                rws = slice(c * CHUNK, (c + 1) * CHUNK)
                sv = jnp.dot(wsm, vn_scr[rws, cols], preferred_element_type=F32) + be_ref[g]
                zs, _ = _silu_and_grad(z_ref[rws, cols].astype(F32))
                y_ref[rws, cols] = (_gelu(u_ref[rws, cols].astype(F32)) * sv * zs).astype(BF16)

    return pl.pallas_call(
        body, name="gmlp_fwd", grid=(t // rows,),
        in_specs=[_row_spec(rows, d, 0), _row_spec(rows, d, 1), _row_spec(rows, d, 2),
                  _full_spec((1, d)), _full_spec((1, d)), _full_spec((A_GROUPS, CHUNK, CHUNK)),
                  _full_spec((A_GROUPS, CHUNK, 128))],
        out_specs=_row_spec(rows, d),
        out_shape=jax.ShapeDtypeStruct((t, d), BF16),
        scratch_shapes=[pltpu.VMEM((rows, d), BF16)],
        compiler_params=_params(40),
    )(proj_big, proj_big, proj_big, ln_g, ln_b, w_s, b_exp)


def _gmlp_bwd(proj_big, dya, ln_g, ln_b, w_s, b_exp):
    t = proj_big.shape[0]
    rows = min(GMLP_ROWS, t)
    d = D_MODEL
    nt = (((1,), (1,)), ((), ()))
    tn = (((0,), (0,)), ((), ()))

    def body(u_ref, v_ref, z_ref, dy_ref, lg_ref, lb_ref, ws_ref, be_ref,
             dp_ref, dws_ref, dbs_ref, dlg_ref, dlb_ref, vn_scr, dvn_scr):
        @pl.when(pl.program_id(0) == 0)
        def _():
            dws_ref[...] = jnp.zeros_like(dws_ref)
            dbs_ref[...] = jnp.zeros_like(dbs_ref)
            dlg_ref[...] = jnp.zeros_like(dlg_ref)
            dlb_ref[...] = jnp.zeros_like(dlb_ref)

        vg, vgrad = _gelu_and_grad(v_ref[...].astype(F32))
        vhat, rstd = _layernorm_stats(vg)
        vn_scr[...] = (vhat * lg_ref[...] + lb_ref[...]).astype(BF16)
        mask = _causal_mask(CHUNK)
        for g in range(A_GROUPS):
            cols = slice(g * 128, (g + 1) * 128)
            wsm = jnp.where(mask, ws_ref[g], 0.0).astype(BF16)
            dws = jnp.zeros((CHUNK, CHUNK), F32)
            dbs = jnp.zeros((CHUNK, 1), F32)
            for c in range(rows // CHUNK):
                rws = slice(c * CHUNK, (c + 1) * CHUNK)
                vn = vn_scr[rws, cols]
                sv = jnp.dot(wsm, vn, preferred_element_type=F32) + be_ref[g]
                ug, ugrad = _gelu_and_grad(u_ref[rws, cols].astype(F32))
                zs, zgrad = _silu_and_grad(z_ref[rws, cols].astype(F32))
                dya = dy_ref[rws, cols].astype(F32)
                dga = dya * zs
                dp_ref[rws, 2 * d + g * 128:2 * d + (g + 1) * 128] = (dya * (ug * sv) * zgrad).astype(BF16)
                dp_ref[rws, cols] = (dga * sv * ugrad).astype(BF16)
                dsv = dga * ug
                dsv16 = dsv.astype(BF16)
                dws = dws + lax.dot_general(dsv16, vn, nt, preferred_element_type=F32)
                dbs = dbs + jnp.sum(dsv, axis=-1, keepdims=True)
                dvn_scr[rws, cols] = lax.dot_general(wsm, dsv16, tn, preferred_element_type=F32)
            dws_ref[g] += jnp.where(mask, dws, 0.0)
            dbs_ref[g] += dbs
        dvn = dvn_scr[...]
        dlg_ref[...] += jnp.sum(dvn * vhat, axis=0, keepdims=True)
        dlb_ref[...] += jnp.sum(dvn, axis=0, keepdims=True)
        dvh = dvn * lg_ref[...]
        dvg = rstd * (dvh - jnp.mean(dvh, axis=-1, keepdims=True) - vhat * jnp.mean(dvh * vhat, axis=-1, keepdims=True))
        dp_ref[:, d:2 * d] = (dvg * vgrad).astype(BF16)

    return pl.pallas_call(
        body, name="gmlp_bwd", grid=(t // rows,),
        in_specs=[_row_spec(rows, d, 0), _row_spec(rows, d, 1), _row_spec(rows, d, 2), _row_spec(rows, d),
                  _full_spec((1, d)), _full_spec((1, d)), _full_spec((A_GROUPS, CHUNK, CHUNK)),
                  _full_spec((A_GROUPS, CHUNK, 128))],
        out_specs=[_row_spec(rows, 3 * d), _full_spec((A_GROUPS, CHUNK, CHUNK)), _full_spec((A_GROUPS, CHUNK, 1)),
                   _full_spec((1, d)), _full_spec((1, d))],
        out_shape=[jax.ShapeDtypeStruct((t, 6 * d), BF16), jax.ShapeDtypeStruct((A_GROUPS, CHUNK, CHUNK), F32),
                   jax.ShapeDtypeStruct((A_GROUPS, CHUNK, 1), F32), jax.ShapeDtypeStruct((1, d), F32),
                   jax.ShapeDtypeStruct((1, d), F32)],
        scratch_shapes=[pltpu.VMEM((rows, d), BF16), pltpu.VMEM((rows, d), F32)],
        compiler_params=_params(48),
    )(proj_big, proj_big, proj_big, dya, ln_g, ln_b, w_s, b_exp)


NT_DIMS = (((1,), (1,)), ((), ()))
TN_DIMS = (((0,), (0,)), ((), ()))


def _mla_fwd(q, kv, kpe, proj_big):
    t = q.shape[0]
    blk = min(ATT_BLK_FWD, t)
    nq = t // blk
    zb_blk0 = (3 * D_MODEL) // 128


    sub = min(ATT_SUB, blk)

    def body(q_ref, kv_ref, kp_ref, zb_ref, o_ref, yb_ref, lse_ref, m_scr, acc_scr):
        i = pl.program_id(1)
        qv = q_ref[...]
        m_scr[...] = jnp.full_like(m_scr, NEG)
        acc_scr[...] = jnp.zeros_like(acc_scr)
        ones = jnp.ones((blk, 128), BF16)

        def step(j, masked):
            ks = pl.ds(pl.multiple_of(j * blk, blk), blk)
            kc = jnp.concatenate([kv_ref[ks, 0:128], kp_ref[ks, :]], axis=1)
            vext = jnp.concatenate([kv_ref[ks, 128:256], ones], axis=1)
            for r in range(blk // sub):
                rows = slice(r * sub, (r + 1) * sub)
                kw = (r + 1) * sub if masked else blk
                tt = lax.dot_general(qv[rows], kc[:kw], NT_DIMS, preferred_element_type=F32)
                if masked:
                    row = lax.broadcasted_iota(jnp.int32, (sub, kw), 0) + r * sub
                    col = lax.broadcasted_iota(jnp.int32, (sub, kw), 1)
                    tt = jnp.where(row >= col, tt, NEG)
                cm = tt[:, 0:128]
                for c in range(1, kw // 128):
                    cm = jnp.maximum(cm, tt[:, c * 128:(c + 1) * 128])
                m_prev = m_scr[rows, :]
                m_new = jnp.maximum(m_prev, jnp.max(cm, axis=-1, keepdims=True))
                alpha = jnp.exp2(m_prev - m_new)
                m_scr[rows, :] = m_new
                p = jnp.concatenate([jnp.exp2(tt[:, c * 128:(c + 1) * 128] - m_new).astype(BF16)
                                     for c in range(kw // 128)], axis=1)
                pv = jnp.dot(p, vext[:kw], preferred_element_type=F32)
                acc_scr[rows, :] = jnp.concatenate([alpha, alpha], axis=1) * acc_scr[rows, :] + pv

        def loop_body(j, carry):
            step(j, False)
            return carry

        lax.fori_loop(0, i, loop_body, 0)
        step(i, True)
        l = acc_scr[:, 128:256]
        o = acc_scr[:, 0:128] / l
        o_ref[...] = o.astype(BF16)
        zs, _ = _silu_and_grad(zb_ref[...].astype(F32))
        yb_ref[...] = (o * zs).astype(BF16)
        lse_ref[0] = m_scr[...] + jnp.log2(l)

    act = jax.ShapeDtypeStruct((t, D_MODEL), BF16)
    return pl.pallas_call(
        body, name="mla_fwd", grid=(MLA_HEADS, nq),
        in_specs=[pl.BlockSpec((blk, HEAD_PAD), lambda h, i: (i, h)),
                  pl.BlockSpec((t, HEAD_PAD), lambda h, i: (0, h)),
                  pl.BlockSpec((t, 128), lambda h, i: (0, 0)),
                  pl.BlockSpec((blk, 128), lambda h, i: (i, zb_blk0 + h))],
        out_specs=[pl.BlockSpec((blk, 128), lambda h, i: (i, h)),
                   pl.BlockSpec((blk, 128), lambda h, i: (i, h)),
                   pl.BlockSpec((1, blk, 128), lambda h, i: (h, i, 0))],
        out_shape=[act, act, jax.ShapeDtypeStruct((MLA_HEADS, t, 128), F32)],
        scratch_shapes=[pltpu.VMEM((blk, 128), F32), pltpu.VMEM((blk, HEAD_PAD), F32)],
        compiler_params=_params(56),
    )(q, kv, kpe, proj_big)


def _mla_gate_bwd(dyb, proj_big, o, dproj):
    t, d = dyb.shape
    tr = min(ROW_BLK, t)

    def body(dy_ref, zb_ref, o_ref, buf_ref, do_ref, dz_ref, dl_ref):
        del buf_ref
        dy = dy_ref[...].astype(F32)
        ov = o_ref[...].astype(F32)
        zs, zgrad = _silu_and_grad(zb_ref[...].astype(F32))
        do16 = (dy * zs).astype(BF16)
        do_ref[...] = do16
        dz_ref[...] = (dy * ov * zgrad).astype(BF16)
        prod = do16.astype(F32) * ov
        for h in range(MLA_HEADS):
            delta = jnp.sum(prod[:, h * 128:(h + 1) * 128], axis=-1, keepdims=True)
            dl_ref[h] = jnp.broadcast_to(delta, (tr, 128))

    act = jax.ShapeDtypeStruct((t, d), BF16)
    head_spec = pl.BlockSpec((MLA_HEADS, tr, 128), lambda i: (0, i, 0))
    return pl.pallas_call(
        body, name="mla_gate_bwd", grid=(t // tr,),
        in_specs=[_row_spec(tr, d), _row_spec(tr, d, 3), _row_spec(tr, d), HBM_SPEC],
        out_specs=[_row_spec(tr, d), _row_spec(tr, d, 3), head_spec],
        out_shape=[act, jax.ShapeDtypeStruct((t, 6 * d), BF16), jax.ShapeDtypeStruct((MLA_HEADS, t, 128), F32)],
        input_output_aliases={3: 1},
        compiler_params=_params(56),
    )(dyb, proj_big, o, dproj)


def _mla_bwd(q, kv, kpe, do, lse, delta, carry):
    t = q.shape[0]
    blk = min(ATT_BLK, t)
    n = t // blk
    nc = blk // 128
    pairs = [(j, i) for j in range(n) for i in range(j, n)]
    j_tab = jnp.asarray([p[0] for p in pairs], jnp.int32)
    i_tab = jnp.asarray([p[1] for p in pairs], jnp.int32)
    n_carry = len(carry.arrays)
    sub = min(ATT_SUB, blk)

    def body(j_ref, i_ref, q_ref, do_ref, lse_ref, dl_ref, kv_ref, kp_ref, *rest):
        c_ins, rest = rest[:n_carry], rest[n_carry:]
        dq_ref, dkv_ref, dkp_ref = rest[:3]
        c_outs, rest = rest[3:3 + n_carry], rest[3 + n_carry:]
        dk_scr, dv_scr = rest[:2]
        c_sems = rest[2:]
        head = pl.program_id(0)
        step = pl.program_id(1)
        j, i = j_ref[step], i_ref[step]

        @pl.when((head == 0) & (step == 0))
        def _():
            carry.start(c_ins, c_outs, c_sems)

        @pl.when(step == 0)
        def _():
            dq_ref[...] = jnp.zeros_like(dq_ref)

        @pl.when(i == j)
        def _():
            dk_scr[...] = jnp.zeros_like(dk_scr)
            dv_scr[...] = jnp.zeros_like(dv_scr)

        kc = jnp.concatenate([kv_ref[:, 0:128], kp_ref[...]], axis=1)
        vv = kv_ref[:, 128:256]

        def tile(diag):
            for r in range(blk // sub):
                rows = slice(r * sub, (r + 1) * sub)
                kw = (r + 1) * sub if diag else blk
                qv, dov = q_ref[rows, :], do_ref[rows, :]
                tt = lax.dot_general(qv, kc[:kw], NT_DIMS, preferred_element_type=F32)
                if diag:
                    row = lax.broadcasted_iota(jnp.int32, (sub, kw), 0) + r * sub
                    col = lax.broadcasted_iota(jnp.int32, (sub, kw), 1)
                    tt = jnp.where(row >= col, tt, NEG)
                dp = lax.dot_general(dov, vv[:kw], NT_DIMS, preferred_element_type=F32)
                lse_v, dl_v = lse_ref[0, rows, :], dl_ref[0, rows, :]
                ps, dss = [], []
                for c in range(kw // 128):
                    cols = slice(c * 128, (c + 1) * 128)
                    p = jnp.exp2(tt[:, cols] - lse_v)
                    ps.append(p.astype(BF16))
                    dss.append((p * (dp[:, cols] - dl_v)).astype(BF16))
                p16 = jnp.concatenate(ps, axis=1)
                ds16 = jnp.concatenate(dss, axis=1)
                dv_scr[0:kw, :] += lax.dot_general(p16, dov, TN_DIMS, preferred_element_type=F32)
                dk_scr[0:kw, :] += lax.dot_general(ds16, qv, TN_DIMS, preferred_element_type=F32)
                qs = pl.ds(pl.multiple_of(i * blk + r * sub, sub), sub)
                dq_ref[qs, :] += jnp.dot(ds16, kc[:kw], preferred_element_type=F32)

        @pl.when(i == j)
        def _():
            tile(True)

        @pl.when(i > j)
        def _():
            tile(False)

        @pl.when(i == n - 1)
        def _():
            dkv_ref[:, 0:128] = (dk_scr[:, 0:128] * (1.0 / LOG2E)).astype(BF16)
            dkv_ref[:, 128:256] = dv_scr[...].astype(BF16)
            ks = pl.ds(pl.multiple_of(j * blk, blk), blk)
            dkp = dk_scr[:, 128:256] * (1.0 / LOG2E)

            @pl.when(head == 0)
            def _():
                dkp_ref[ks, :] = dkp

            @pl.when(head > 0)
            def _():
                dkp_ref[ks, :] += dkp

        @pl.when((head == MLA_HEADS - 1) & (step == len(pairs) - 1))
        def _():
            carry.finish(c_ins, c_outs, c_sems)

    grid_spec = pltpu.PrefetchScalarGridSpec(
        num_scalar_prefetch=2, grid=(MLA_HEADS, len(pairs)),
        in_specs=[pl.BlockSpec((blk, HEAD_PAD), lambda h, s, jt, it: (it[s], h)),
                  pl.BlockSpec((blk, 128), lambda h, s, jt, it: (it[s], h)),
                  pl.BlockSpec((1, blk, 128), lambda h, s, jt, it: (h, it[s], 0)),
                  pl.BlockSpec((1, blk, 128), lambda h, s, jt, it: (h, it[s], 0)),
                  pl.BlockSpec((blk, HEAD_PAD), lambda h, s, jt, it: (jt[s], h)),
                  pl.BlockSpec((blk, 128), lambda h, s, jt, it: (jt[s], 0))] + [HBM_SPEC] * n_carry,
        out_specs=[pl.BlockSpec((t, HEAD_PAD), lambda h, s, jt, it: (0, h)),
                   pl.BlockSpec((blk, HEAD_PAD), lambda h, s, jt, it: (jt[s], h)),
                   pl.BlockSpec((t, 128), lambda h, s, jt, it: (0, 0))] + [HBM_SPEC] * n_carry,
        scratch_shapes=[pltpu.VMEM((blk, HEAD_PAD), F32), pltpu.VMEM((blk, 128), F32)] + carry.sem_shapes,
    )
    return pl.pallas_call(
        body, name="mla_bwd", grid_spec=grid_spec,
        out_shape=[jax.ShapeDtypeStruct((t, MLA_HEADS * HEAD_PAD), F32),
                   jax.ShapeDtypeStruct((t, 2 * D_MODEL), BF16),
                   jax.ShapeDtypeStruct((t, 128), F32)] + carry.out_shapes,
        compiler_params=_params(58),
    )(j_tab, i_tab, q, do, lse, delta, kv, kpe, *carry.arrays)


def _mem_attn_probs(qv, k_ref):
    s = lax.dot_general(qv, k_ref[...], NT_DIMS, preferred_element_type=F32) * MEM_SCALE
    e = jnp.exp(s - jnp.max(s, axis=-1, keepdims=True))
    return e / jnp.sum(e, axis=-1, keepdims=True)


def _mem_fwd(proj_big, kv_m):
    t = proj_big.shape[0]
    tq = min(MEM_Q_BLK, t)
    hd = MEM_HEAD_DIM
    d = D_MODEL
    mlen = kv_m.shape[0]

    def body(q_ref, z_ref, kv_ref, y_ref):
        for h in range(MEM_HEADS):
            cols = slice(h * hd, (h + 1) * hd)
            p = _mem_attn_probs(q_ref[:, cols], kv_ref.at[:, cols])
            o = jnp.dot(p.astype(BF16), kv_ref[:, d + h * hd:d + (h + 1) * hd], preferred_element_type=F32)
            zs, _ = _silu_and_grad(z_ref[:, cols].astype(F32))
            y_ref[:, cols] = (o * zs).astype(BF16)

    return pl.pallas_call(
        body, name="mem_fwd", grid=(t // tq,),
        in_specs=[_row_spec(tq, d, 4), _row_spec(tq, d, 5), _full_spec((mlen, 2 * d))],
        out_specs=_row_spec(tq, d),
        out_shape=jax.ShapeDtypeStruct((t, d), BF16),
        compiler_params=_params(40),
    )(proj_big, proj_big, kv_m)


def _mem_bwd(proj_big, kv_m, dym, dproj):
    t = proj_big.shape[0]
    tq = min(ROW_BLK, t)
    hd = MEM_HEAD_DIM
    d = D_MODEL
    mlen = kv_m.shape[0]

    def body(q_ref, z_ref, kv_ref, dy_ref, buf_ref, dqz_ref, dkv_ref):
        del buf_ref

        @pl.when(pl.program_id(0) == 0)
        def _():
            dkv_ref[...] = jnp.zeros_like(dkv_ref)

        for h in range(MEM_HEADS):
            cols = slice(h * hd, (h + 1) * hd)
            k_ref, v_ref = kv_ref.at[:, cols], kv_ref.at[:, d + h * hd:d + (h + 1) * hd]
            qv = q_ref[:, cols]
            p = _mem_attn_probs(qv, k_ref)
            p16 = p.astype(BF16)
            o = jnp.dot(p16, v_ref[...], preferred_element_type=F32)
            zs, zgrad = _silu_and_grad(z_ref[:, cols].astype(F32))
            dy = dy_ref[:, cols].astype(F32)
            dqz_ref[:, d + h * hd:d + (h + 1) * hd] = (dy * o * zgrad).astype(BF16)
            do16 = (dy * zs).astype(BF16)
            dkv_ref[:, d + h * hd:d + (h + 1) * hd] += lax.dot_general(p16, do16, TN_DIMS, preferred_element_type=F32)
            dp = lax.dot_general(do16, v_ref[...], NT_DIMS, preferred_element_type=F32)
            ds = (p * (dp - jnp.sum(dp * p, axis=-1, keepdims=True)) * MEM_SCALE).astype(BF16)
            dqz_ref[:, cols] = jnp.dot(ds, k_ref[...], preferred_element_type=F32).astype(BF16)
            dkv_ref[:, cols] += lax.dot_general(ds, qv, TN_DIMS, preferred_element_type=F32)

    return pl.pallas_call(
        body, name="mem_bwd", grid=(t // tq,),
        in_specs=[_row_spec(tq, d, 4), _row_spec(tq, d, 5), _full_spec((mlen, 2 * d)), _row_spec(tq, d), HBM_SPEC],
        out_specs=[_row_spec(tq, 2 * d, 2), _full_spec((mlen, 2 * d))],
        out_shape=[jax.ShapeDtypeStruct((t, 6 * d), BF16), jax.ShapeDtypeStruct((mlen, 2 * d), F32)],
        input_output_aliases={4: 0},
        compiler_params=_params(40),
    )(proj_big, proj_big, kv_m, dym, dproj)


HBM_SPEC = pl.BlockSpec(memory_space=pl.ANY)
N_PEERS = N_DEV - 1


def _dev_index(px, py, pc):
    return 4 * px + 2 * py + pc


class _Gather:
    def __init__(self, arrays):
        self.arrays = list(arrays)
        n = len(self.arrays)
        self.out_shapes = [jax.ShapeDtypeStruct((N_DEV,) + a.shape, a.dtype) for a in self.arrays]
        self.sem_shapes = [pltpu.SemaphoreType.DMA((n * N_PEERS,)), pltpu.SemaphoreType.DMA((n * N_PEERS,)),
                           pltpu.SemaphoreType.DMA((n,))]

    def _parts(self, ins, outs, sems):
        n = len(self.arrays)
        send_sems, recv_sems, local_sems = sems
        x, y, c = lax.axis_index("x"), lax.axis_index("y"), lax.axis_index("c")
        me, sibling = (x, y, c), (x, y, 1 - c)
        chips = [(1 - x, y), (x, 1 - y), (1 - x, 1 - y)]

        def copy(a, k, block, to, src=None):
            dst = outs[a].at[_dev_index(*block)]
            return pltpu.make_async_remote_copy(
                src_ref=dst if src is None else src, dst_ref=dst,
                send_sem=send_sems.at[a * N_PEERS + k], recv_sem=recv_sems.at[a * N_PEERS + k],
                device_id=to, device_id_type=MESH)

        mine = [pltpu.make_async_copy(ins[a], outs[a].at[_dev_index(*me)], local_sems.at[a]) for a in range(n)]
        first = []
        for a in range(n):
            first.append(copy(a, 0, me, sibling, src=ins[a]))
            first += [copy(a, 1 + j, me, (*chip, c), src=ins[a]) for j, chip in enumerate(chips)]
        return n, c, me, sibling, chips, copy, mine, first

    def start(self, ins, outs, sems):
        _, _, _, _, _, _, mine, first = self._parts(ins, outs, sems)
        for cp in mine + first:
            cp.start()

    def finish(self, ins, outs, sems):
        n, c, me, sibling, chips, copy, mine, first = self._parts(ins, outs, sems)
        passed = []
        for j, chip in enumerate(chips):
            for a in range(n):
                copy(a, 1 + j, (*chip, c), me).wait_recv()
                fwd = copy(a, 4 + j, (*chip, c), sibling)
                fwd.start()
                passed.append(fwd)
        for a in range(n):
            copy(a, 0, sibling, me).wait_recv()
            for j, chip in enumerate(chips):
                copy(a, 4 + j, (*chip, 1 - c), me).wait_recv()
        for cp in first + passed:
            cp.wait_send()
        for cp in mine:
            cp.wait()


class _AllToAll:
    def __init__(self, arrays):
        self.arrays = list(arrays)
        n = len(self.arrays)
        self.out_shapes = [jax.ShapeDtypeStruct(a.shape, a.dtype) for a in self.arrays]
        self.sem_shapes = [pltpu.SemaphoreType.DMA((n * N_PEERS,)), pltpu.SemaphoreType.DMA((n * N_PEERS,)),
                           pltpu.SemaphoreType.DMA((n,))]

    def _parts(self, ins, outs, sems):
        n = len(self.arrays)
        send_sems, recv_sems, local_sems = sems
        x, y, c = lax.axis_index("x"), lax.axis_index("y"), lax.axis_index("c")
        my_idx = _dev_index(x, y, c)
        peers = []
        for k in range(1, N_DEV):
            dx, dy, dc = (k >> 2) & 1, (k >> 1) & 1, k & 1
            peers.append((1 - x if dx else x, 1 - y if dy else y, 1 - c if dc else c))

        def copy(a, k, peer):
            return pltpu.make_async_remote_copy(
                src_ref=ins[a].at[_dev_index(*peer)], dst_ref=outs[a].at[my_idx],
                send_sem=send_sems.at[a * N_PEERS + k], recv_sem=recv_sems.at[a * N_PEERS + k],
                device_id=peer, device_id_type=MESH)

        def landed(a, k, peer):
            slot = outs[a].at[_dev_index(*peer)]
            return pltpu.make_async_remote_copy(
                src_ref=slot, dst_ref=slot,
                send_sem=send_sems.at[a * N_PEERS + k], recv_sem=recv_sems.at[a * N_PEERS + k],
                device_id=peer, device_id_type=MESH)

        mine = [pltpu.make_async_copy(ins[a].at[my_idx], outs[a].at[my_idx], local_sems.at[a]) for a in range(n)]
        sends = [copy(a, k, peer) for a in range(n) for k, peer in enumerate(peers)]
        return n, peers, landed, mine, sends

    def start(self, ins, outs, sems):
        _, _, _, mine, sends = self._parts(ins, outs, sems)
        for cp in mine + sends:
            cp.start()

    def finish(self, ins, outs, sems):
        n, peers, landed, mine, sends = self._parts(ins, outs, sems)
        for a in range(n):
            for k, peer in enumerate(peers):
                landed(a, k, peer).wait_recv()
        for cp in sends:
            cp.wait_send()
        for cp in mine:
            cp.wait()


def _exchange(plan, name):
    n = len(plan.arrays)

    def body(*refs):
        ins, outs, sems = refs[:n], refs[n:2 * n], refs[2 * n:]
        plan.start(ins, outs, sems)
        plan.finish(ins, outs, sems)

    return pl.pallas_call(
        body, name=name, in_specs=[HBM_SPEC] * n, out_specs=[HBM_SPEC] * n,
        out_shape=plan.out_shapes, scratch_shapes=plan.sem_shapes,
    )(*plan.arrays)


def _adamw(w, g, m, v):
    m = ADAM_B1 * m + (1.0 - ADAM_B1) * g
    v = ADAM_B2 * v + (1.0 - ADAM_B2) * jnp.square(g)
    m_hat = m / (1.0 - ADAM_B1 ** ADAM_STEP)
    v_hat = v / (1.0 - ADAM_B2 ** ADAM_STEP)
    delta = -ADAM_LR * (m_hat / (jnp.sqrt(v_hat) + ADAM_EPS) + ADAM_WD * w)
    return delta, m, v


def _adam_sharded(parts_list, w, m, v, name):
    shape = w.shape
    cols = shape[-1]
    rows = int(np.prod(shape[:-1]))
    tr = min(128, rows)
    parts_list = [p.reshape(N_DEV, -1, cols) for p in parts_list]
    bounds = np.cumsum([0] + [p.shape[1] // tr for p in parts_list])
    assert rows % tr == 0 and all(p.shape[1] % tr == 0 for p in parts_list) and bounds[-1] == rows // tr
    n_parts = len(parts_list)

    def body(*refs):
        p_refs = refs[:n_parts]
        w_ref, m_ref, v_ref, g_ref, d_ref, nm_ref, nv_ref = refs[n_parts:]
        i = pl.program_id(0)
        for k, p_ref in enumerate(p_refs):
            @pl.when((i >= bounds[k]) & (i < bounds[k + 1]))
            def _():
                g = p_ref[0].astype(F32)
                for e in range(1, N_DEV):
                    g = g + p_ref[e].astype(F32)
                g_ref[...] = g
                d_ref[...], nm_ref[...], nv_ref[...] = _adamw(w_ref[...], g, m_ref[...], v_ref[...])

    def part_spec(k):
        lo, hi = int(bounds[k]), int(bounds[k + 1])
        return pl.BlockSpec((N_DEV, tr, cols), lambda i: (0, jnp.clip(i, lo, hi - 1) - lo, 0))

    spec = pl.BlockSpec((tr, cols), lambda i: (i, 0))
    flat = jax.ShapeDtypeStruct((rows, cols), F32)
    outs = pl.pallas_call(
        body, name=name, grid=(rows // tr,),
        in_specs=[part_spec(k) for k in range(n_parts)] + [spec, spec, spec],
        out_specs=[spec] * 4, out_shape=[flat] * 4,
        compiler_params=_params(40),
    )(*parts_list, w.reshape(rows, cols), m.reshape(rows, cols), v.reshape(rows, cols))
    return [o.reshape(shape) for o in outs]


def _adam_replicated(parts, w, m, v):
    r = w.shape[0]

    def body(p_ref, w_ref, m_ref, v_ref, g_ref, d_ref, nm_ref, nv_ref):
        g = p_ref[0]
        for e in range(1, N_DEV):
            g = g + p_ref[e]
        g_ref[...] = g
        d_ref[...], nm_ref[...], nv_ref[...] = _adamw(w_ref[...], g, m_ref[...], v_ref[...])

    spec = _full_spec((r, 128))
    flat = jax.ShapeDtypeStruct((r, 128), F32)
    return pl.pallas_call(
        body, name="adam_replicated", grid=(1,),
        in_specs=[_full_spec((N_DEV, r, 128)), spec, spec, spec],
        out_specs=[spec] * 4, out_shape=[flat] * 4,
        compiler_params=_params(48),
    )(parts, w, m, v)


def _pack(arrays):
    parts = []
    for a in arrays:
        f = a.reshape(-1, 128)
        pad = -f.shape[0] % 8
        parts.append(jnp.pad(f, ((0, pad), (0, 0))) if pad else f)
    return jnp.concatenate(parts, axis=0)


def _unpack(packed, shapes):
    out, row = [], 0
    for shape in shapes:
        r = int(np.prod(shape)) // 128
        out.append(packed[row:row + r].reshape(shape))
        row += r + (-r % 8)
    return out


SHARDED = ("w_in", "w_uq", "w_ukv", "w_mem_kv", "w_gate", "w_branch", "w_out")
REPLICATED = ("g_pre", "a_ln_g", "a_ln_b", "a_w_s", "a_b_s", "q_norm_g", "kv_norm_g", "mem_norm_g", "b_gate", "g_post")
WEIGHT_ORDER = ("g_pre", "w_in", "a_ln_g", "a_ln_b", "a_w_s", "a_b_s", "q_norm_g", "w_uq", "kv_norm_g", "w_ukv",
                "mem_norm_g", "w_mem_kv", "w_gate", "b_gate", "w_branch", "w_out", "g_post")


def _unshard_cols(g):
    return g.transpose(1, 0, 2).reshape(g.shape[1], N_DEV * g.shape[2])


def _shard_cols(full):
    rows, n = full.shape
    return full.reshape(rows, N_DEV, n // N_DEV).transpose(1, 0, 2).astype(BF16)


def kernel(x, mem, positions, g_pre, w_in, a_ln_g, a_ln_b, a_w_s, a_b_s, q_norm_g, w_uq, kv_norm_g, w_ukv, mem_norm_g, w_mem_kv, w_gate, b_gate, w_branch, w_out, g_post, loss_target, m_g_pre, m_w_in, m_a_ln_g, m_a_ln_b, m_a_w_s, m_a_b_s, m_q_norm_g, m_w_uq, m_kv_norm_g, m_w_ukv, m_mem_norm_g, m_w_mem_kv, m_w_gate, m_b_gate, m_w_branch, m_w_out, m_g_post, v_g_pre, v_w_in, v_a_ln_g, v_a_ln_b, v_a_w_s, v_a_b_s, v_q_norm_g, v_w_uq, v_kv_norm_g, v_w_ukv, v_mem_norm_g, v_w_mem_kv, v_w_gate, v_b_gate, v_w_branch, v_w_out, v_g_post):
    weights = dict(g_pre=g_pre, w_in=w_in, a_ln_g=a_ln_g, a_ln_b=a_ln_b, a_w_s=a_w_s, a_b_s=a_b_s, q_norm_g=q_norm_g,
                   w_uq=w_uq, kv_norm_g=kv_norm_g, w_ukv=w_ukv, mem_norm_g=mem_norm_g, w_mem_kv=w_mem_kv,
                   w_gate=w_gate, b_gate=b_gate, w_branch=w_branch, w_out=w_out, g_post=g_post)
    mom1 = dict(g_pre=m_g_pre, w_in=m_w_in, a_ln_g=m_a_ln_g, a_ln_b=m_a_ln_b, a_w_s=m_a_w_s, a_b_s=m_a_b_s,
                q_norm_g=m_q_norm_g, w_uq=m_w_uq, kv_norm_g=m_kv_norm_g, w_ukv=m_w_ukv, mem_norm_g=m_mem_norm_g,
                w_mem_kv=m_w_mem_kv, w_gate=m_w_gate, b_gate=m_b_gate, w_branch=m_w_branch, w_out=m_w_out, g_post=m_g_post)
    mom2 = dict(g_pre=v_g_pre, w_in=v_w_in, a_ln_g=v_a_ln_g, a_ln_b=v_a_ln_b, a_w_s=v_a_w_s, a_b_s=v_a_b_s,
                q_norm_g=v_q_norm_g, w_uq=v_w_uq, kv_norm_g=v_kv_norm_g, w_ukv=v_w_ukv, mem_norm_g=v_mem_norm_g,
                w_mem_kv=v_w_mem_kv, w_gate=v_w_gate, b_gate=v_b_gate, w_branch=v_w_branch, w_out=v_w_out, g_post=v_g_post)
    d = D_MODEL
    t = x.shape[1]
    xs, tgt, mems = x[0], loss_target[0], mem[0]
    pos_col = positions.reshape(t, 1)

    shard16 = {n: weights[n][0].astype(BF16) for n in SHARDED}
    h, g_gate = _pre_norm(xs, g_pre, _Gather([shard16["w_gate"]]))
    w_gate_f = _unshard_cols(g_gate)
    gates, g_in = _mm(h, w_gate_f, name="gates", tm=1024, tn=1024, tk=2048, bias=b_gate, act="sigmoid",
                      carry=_Gather([shard16["w_in"]]))
    w_in_full = _unshard_cols(g_in)
    lat0, lat1 = 3 * d, 3 * d + Q_LORA + KV_LORA + QK_ROPE
    w_big = jnp.concatenate([w_in_full[:, :lat0], w_in_full[:, lat1:]], axis=1)
    w_lat = jnp.concatenate([w_in_full[:, lat0:lat1], jnp.zeros((d, LAT_W - (lat1 - lat0)), BF16)], axis=1)

    inv_freq = 1.0 / (ROPE_THETA ** (jnp.arange(0, QK_ROPE, 2, dtype=F32) / QK_ROPE))
    inv_freq_lanes = jnp.concatenate([inv_freq, inv_freq, jnp.zeros((128 - QK_ROPE,), F32)]).reshape(1, 128)
    ws = a_w_s[0]
    b_exp = jnp.broadcast_to(a_b_s[0][:, :, None], (A_GROUPS, CHUNK, 128))

    proj_big, g_uq, g_ukv, g_mem, g_br, g_out = _mm(
        h, w_big, name="proj_big", tm=1024, tn=1024, tk=2048,
        carry=_Gather([shard16[n] for n in ("w_uq", "w_ukv", "w_mem_kv", "w_branch", "w_out")]))
    w_uq_p = jnp.pad(_unshard_cols(g_uq).reshape(Q_LORA, MLA_HEADS, QK_DIM),
                     ((0, 0), (0, 0), (0, HEAD_PAD - QK_DIM))).reshape(Q_LORA, MLA_HEADS * HEAD_PAD)
    w_ukv_f = _unshard_cols(g_ukv)
    w_mem_f = _unshard_cols(g_mem)
    proj_lat = _mm(h, w_lat, name="proj_lat", tm=1024, tn=LAT_W, tk=2048)
    w_br_f = g_br.transpose(1, 0, 2, 3).reshape(3, d, d)
    w_out_f = g_out.reshape(d, d)
    c_tab, sa_tab, sb_tab = _rope_tables(pos_col, inv_freq_lanes)
    cqn, ckvn, kpe = _latent_norms(proj_lat, q_norm_g, kv_norm_g, c_tab, sa_tab, sb_tab)
    q = _mm(cqn, w_uq_p, name="q_up", tm=1024, tn=1024, tk=512, post=_rope_q_tile, post_rows=(c_tab, sa_tab, sb_tab))
    kv = _mm(ckvn, w_ukv_f, name="kv_up", tm=1024, tn=1024, tk=512)
    o_b, y_b, lse = _mla_fwd(q, kv, kpe, proj_big)
    memn = _mem_norm(mems, mem_norm_g)
    kv_m = _mm(memn, w_mem_f, name="mem_kv", tm=256, tn=1024, tk=2048)
    y_m = _mem_fwd(proj_big, kv_m)
    y_a = _gmlp_fwd(proj_big, a_ln_g, a_ln_b, ws, b_exp)
    ys = (y_a, y_b, y_m)
    ps = [_mm(ys[n], w_br_f[n], name=f"branch{n}", tm=1024, tn=1024, tk=2048) for n in range(3)]
    merged = _merge(gates, *ps)
    out = _mm(merged, w_out_f, name="out_proj", tm=1024, tn=1024, tk=2048, out_dtype=F32)
    d_out, dy, loss_blk, dg_post = _post_loss(out, xs, tgt, g_post)

    dmerged = _mm(d_out, w_out_f, name="d_merged", tb=True, tm=1024, tn=1024, tk=2048)
    dw_out = _mm(merged, d_out, name="dw_out", ta=True, tm=1024, tn=1024, tk=2048)
    dp_a, dp_b, dp_m, dgl, db_gate = _merge_bwd(dmerged, gates, *ps)
    dps = (dp_a, dp_b, dp_m)
    dys = [_mm(dps[n], w_br_f[n], name=f"d_y{n}", tb=True, tm=1024, tn=1024, tk=2048) for n in range(3)]
    dw_br = [_mm(ys[n], dps[n], name=f"dw_branch{n}", ta=True, tm=1024, tn=1024, tk=2048) for n in range(3)]
    dw_gate = _mm(h, dgl, name="dw_gate", ta=True, tm=1024, tn=3 * d // N_DEV, tk=2048, col_shards=True)

    dproj_big, dws, dbs, dlng, dlnb = _gmlp_bwd(proj_big, dys[0], a_ln_g, a_ln_b, ws, b_exp)

    recv = {}
    send = [dw_gate, jnp.stack(dw_br).reshape(3, N_DEV, d // N_DEV, d).transpose(1, 0, 2, 3),
            dw_out.reshape(N_DEV, d // N_DEV, d)]
    do_b, dproj_big, delta = _mla_gate_bwd(dys[1], proj_big, o_b, dproj_big)
    dq, dkv, dkpe, r_gate, r_br, r_out = _mla_bwd(q, kv, kpe, do_b, lse, delta, _AllToAll(send))
    recv["w_gate"], recv["w_branch"], recv["w_out"] = [r_gate], [r_br], [r_out]
    dq_raw, dkr = _rope_q_bwd(dq, dkpe, c_tab, sa_tab, sb_tab)
    dcqn = _mm(dq_raw, w_uq_p, name="d_cq", tb=True, tm=1024, tn=Q_LORA, tk=2048, out_dtype=F32)
    dw_uq_p = _mm(cqn, dq_raw, name="dw_uq", ta=True, tm=Q_LORA, tn=1024, tk=2048)
    dckvn = _mm(dkv, w_ukv_f, name="d_ckv", tb=True, tm=1024, tn=KV_LORA, tk=2048, out_dtype=F32)
    dw_ukv = _mm(ckvn, dkv, name="dw_ukv", ta=True, tm=KV_LORA, tn=2 * d // N_DEV, tk=2048, col_shards=True)
    dproj_lat, dqg, dkg = _latent_norms_bwd(proj_lat, q_norm_g, kv_norm_g, dcqn, dckvn, dkr)

    dproj_big, dkv_m32 = _mem_bwd(proj_big, kv_m, dys[2], dproj_big)
    dkv_m = dkv_m32.astype(BF16)
    dw_mem = _mm(memn, dkv_m, name="dw_mem", ta=True, tm=1024, tn=2 * d // N_DEV, tk=256, col_shards=True)

    dw_uq_full = dw_uq_p.reshape(Q_LORA, MLA_HEADS, HEAD_PAD)[:, :, :QK_DIM].reshape(Q_LORA, MLA_HEADS * QK_DIM)
    dw_big, r_uq, r_ukv, r_mem = _mm(
        h, dproj_big, name="dw_big", ta=True, tm=1024, tn=1024, tk=2048,
        carry=_AllToAll([_shard_cols(dw_uq_full), dw_ukv, dw_mem]))
    recv["w_uq"], recv["w_ukv"], recv["w_mem_kv"] = [r_uq], [r_ukv], [r_mem]
    dw_lat = _mm(h, dproj_lat, name="dw_lat", ta=True, tm=1024, tn=LAT_W, tk=2048)

    def w_in_shards(rows):
        n_cols = w_in.shape[-1]
        pieces = []
        for j in range(N_DEV):
            lo, hi = n_cols * j, n_cols * (j + 1)
            parts = []
            for a, b, src, shift in ((0, lat0, dw_big, 0), (lat0, lat1, dw_lat, lat0), (lat1, N_DEV * n_cols, dw_big, lat1 - lat0)):
                s, e = max(lo, a), min(hi, b)
                if s < e:
                    parts.append(src[rows, s - shift:e - shift])
            pieces.append(parts[0] if len(parts) == 1 else jnp.concatenate(parts, axis=1))
        return jnp.stack(pieces)

    dmemn = _mm(dkv_m, w_mem_f, name="d_memn", tb=True, tm=256, tn=1024, tk=2048, out_dtype=F32)
    dg_mem = _mem_norm_bwd(mems, mem_norm_g, dmemn)
    dh, r_in0 = _mm(dgl, w_gate_f, name="dh_gate", tb=True, tm=1024, tn=1024, tk=2048, out_dtype=F32,
                    carry=_AllToAll([w_in_shards(slice(0, d // 2))]))
    dh = _mm(dproj_lat, w_lat, name="dh_lat", tb=True, tm=1024, tn=1024, tk=LAT_W, out_dtype=F32, add=dh)
    dh, r_in1 = _mm(dproj_big, w_big, name="dh_big", tb=True, tm=1024, tn=1024, tk=2048, out_dtype=F32, add=dh,
                    carry=_AllToAll([w_in_shards(slice(d // 2, d))]))
    recv["w_in"] = [r_in0, r_in1]
    grad_x, dg_pre = _pre_norm_bwd(xs, g_pre, dh, dy)

    results = {}
    for n in SHARDED:
        results[n] = [r[None] for r in _adam_sharded(recv[n], weights[n][0], mom1[n][0], mom2[n][0], "adam_" + n)]

    small = dict(g_pre=dg_pre, a_ln_g=dlng, a_ln_b=dlnb, a_w_s=dws, a_b_s=dbs, q_norm_g=dqg, kv_norm_g=dkg,
                 mem_norm_g=dg_mem, b_gate=db_gate, g_post=dg_post)
    (parts,) = _exchange(_Gather([_pack([small[n] for n in REPLICATED])]), "gather_small_grads")
    packed = _adam_replicated(parts, _pack([weights[n] for n in REPLICATED]), _pack([mom1[n] for n in REPLICATED]),
                              _pack([mom2[n] for n in REPLICATED]))
    shapes = [weights[n].shape for n in REPLICATED]
    unpacked = [_unpack(p, shapes) for p in packed]
    for i, n in enumerate(REPLICATED):
        results[n] = [u[i] for u in unpacked]

    loss = lax.psum(loss_blk[0, 0], AXES)
    outs = [loss, grad_x[None]]
    for kind in range(4):
        outs += [results[n][kind] for n in WEIGHT_ORDER]
    return tuple(outs)
```

```python
import functools
import math

import jax
import jax.numpy as jnp
import numpy as np
from jax import lax
from jax.experimental import pallas as pl
from jax.experimental.pallas import tpu as pltpu

F32 = jnp.float32
BF16 = jnp.bfloat16
MESH = pl.DeviceIdType.MESH
AXES = ("x", "y", "c")
N_DEV = 8

D_MODEL = 2048
EPS = 1e-6
CHUNK = 128
A_GROUPS = 16
MLA_HEADS = 16
QK_NOPE = 128
QK_ROPE = 64
QK_DIM = QK_NOPE + QK_ROPE
HEAD_PAD = 256
Q_LORA = 512
KV_LORA = 512
MEM_HEADS = 4
MEM_HEAD_DIM = 512
ROPE_THETA = 10000.0
MLA_SCALE = QK_DIM ** -0.5
MEM_SCALE = MEM_HEAD_DIM ** -0.5
NEG = -1e30
LOG2E = 1.4426950408889634

ADAM_LR = 0.001
ADAM_B1 = 0.9
ADAM_B2 = 0.999
ADAM_EPS = 1e-08
ADAM_WD = 0.01
ADAM_STEP = 10

BIG_W = 6 * D_MODEL
LAT_W = Q_LORA + KV_LORA + 128

VMEM_MIB = 1024 * 1024

ROW_BLK = 256
ATT_BLK = 2048
ATT_BLK_FWD = 2048
ATT_SUB = 256
GMLP_ROWS = 256
MEM_Q_BLK = 512


def _params(vmem_mib, **kw):
    return pltpu.CompilerParams(vmem_limit_bytes=int(vmem_mib * VMEM_MIB), **kw)


def _gelu(x):
    k = math.sqrt(2.0 / math.pi)
    t = jnp.tanh(k * (x + 0.044715 * (x * x * x)))
    return 0.5 * x * (1.0 + t)


def _gelu_and_grad(x):
    k = math.sqrt(2.0 / math.pi)
    x2 = x * x
    t = jnp.tanh(k * (x + 0.044715 * (x2 * x)))
    val = 0.5 * x * (1.0 + t)
    grad = 0.5 * (1.0 + t) + 0.5 * x * (1.0 - t * t) * (k * (1.0 + 3.0 * 0.044715 * x2))
    return val, grad


def _silu_and_grad(z):
    s = jax.nn.sigmoid(z)
    return z * s, s * (1.0 + z * (1.0 - s))


def _mm(a, b, *, name, tm, tn, tk, ta=False, tb=False, out_dtype=BF16, bias=None, act=None, add=None, carry=None,
        post=None, post_rows=(), col_shards=False):
    m = a.shape[1] if ta else a.shape[0]
    k = a.shape[0] if ta else a.shape[1]
    n = b.shape[0] if tb else b.shape[1]
    assert k == (b.shape[1] if tb else b.shape[0])
    tm, tn, tk = min(tm, m), min(tn, n), min(tk, k)
    assert m % tm == 0 and n % tn == 0 and k % tk == 0, (name, m, n, k, tm, tn, tk)
    nk = k // tk
    a_spec = pl.BlockSpec((tk, tm), lambda i, j, kk: (kk, i)) if ta else pl.BlockSpec((tm, tk), lambda i, j, kk: (i, kk))
    b_spec = pl.BlockSpec((tn, tk), lambda i, j, kk: (j, kk)) if tb else pl.BlockSpec((tk, tn), lambda i, j, kk: (kk, j))
    dn = (((0 if ta else 1,), (1 if tb else 0,)), ((), ()))
    operands, in_specs = [a, b], [a_spec, b_spec]
    if bias is not None:
        operands.append(bias)
        in_specs.append(pl.BlockSpec((1, tn), lambda i, j, kk: (0, j)))
    if add is not None:
        operands.append(add)
        in_specs.append(pl.BlockSpec((tm, tn), lambda i, j, kk: (i, j)))
    n_fixed = len(operands)
    for r in post_rows:
        operands.append(r)
        in_specs.append(pl.BlockSpec((tm, r.shape[1]), lambda i, j, kk: (i, 0)))

    n_in = len(operands)
    n_carry = len(carry.arrays) if carry is not None else 0
    n_acc = 1 if nk > 1 else 0
    grid = (m // tm, n // tn, nk)

    def body(*refs):
        a_ref, b_ref = refs[0], refs[1]
        pos = 2
        bias_ref = add_ref = None
        if bias is not None:
            bias_ref = refs[pos]
            pos += 1
        if add is not None:
            add_ref = refs[pos]
            pos += 1
        o_ref = refs[n_in + n_carry]
        pos = n_in + n_carry
        if carry is not None:
            c_ins = refs[n_in:n_in + n_carry]
            c_outs = refs[n_in + n_carry + 1:n_in + 2 * n_carry + 1]
            c_sems = refs[n_in + 2 * n_carry + 1 + n_acc:]
            ids = [pl.program_id(ax) for ax in range(3)]

            @pl.when((ids[0] == 0) & (ids[1] == 0) & (ids[2] == 0))
            def _():
                carry.start(c_ins, c_outs, c_sems)

        part = lax.dot_general(a_ref[...], b_ref[...], dn, preferred_element_type=F32)

        def finish(acc):
            if bias_ref is not None:
                acc = acc + bias_ref[...]
            if act == "sigmoid":
                acc = jax.nn.sigmoid(acc)
            if add_ref is not None:
                acc = acc + add_ref[...]
            if post is not None:
                acc = post(acc, *[r[...] for r in refs[n_fixed:n_in]])
            o_ref[...] = acc.astype(o_ref.dtype)

        if nk == 1:
            finish(part)
        else:
            acc_ref = refs[n_in + 2 * n_carry + 1]
            kk = pl.program_id(2)

            @pl.when(kk == 0)
            def _():
                acc_ref[...] = part

            @pl.when(kk > 0)
            def _():
                acc_ref[...] += part

            @pl.when(kk == nk - 1)
            def _():
                finish(acc_ref[...])

        if carry is not None:
            total = grid[0] * grid[1] * grid[2]
            early = hasattr(carry, "forward") and total >= 16
            if early:
                @pl.when((ids[0] * grid[1] + ids[1]) * grid[2] + ids[2] == (4 * total) // 5)
                def _():
                    carry.forward(c_ins, c_outs, c_sems)

            @pl.when((ids[0] == grid[0] - 1) & (ids[1] == grid[1] - 1) & (ids[2] == grid[2] - 1))
            def _():
                if early:
                    carry.finish(c_ins, c_outs, c_sems, forwarded=True)
                else:
                    carry.finish(c_ins, c_outs, c_sems)

    osz = jnp.dtype(out_dtype).itemsize
    est = 2 * 2 * (tm * tk + tk * tn) + 2 * osz * tm * tn + 8 * tm * tn + (2 * 4 * tm * tn if add is not None else 0)
    if col_shards:
        assert n // tn == N_DEV
        main_spec = pl.BlockSpec((None, tm, tn), lambda i, j, kk: (j, i, 0))
        main_shape = jax.ShapeDtypeStruct((N_DEV, m, tn), out_dtype)
    else:
        main_spec = pl.BlockSpec((tm, tn), lambda i, j, kk: (i, j))
        main_shape = jax.ShapeDtypeStruct((m, n), out_dtype)
    scratch = [pltpu.VMEM((tm, tn), F32)] if nk > 1 else []
    if carry is None:
        return pl.pallas_call(
            body, name=name, grid=grid, in_specs=in_specs, out_specs=main_spec, out_shape=main_shape,
            scratch_shapes=scratch, compiler_params=_params(min(56, est / VMEM_MIB + 12)),
        )(*operands)
    return pl.pallas_call(
        body, name=name, grid=grid,
        in_specs=in_specs + [HBM_SPEC] * n_carry,
        out_specs=[main_spec] + [HBM_SPEC] * n_carry,
        out_shape=[main_shape] + carry.out_shapes,
        scratch_shapes=scratch + carry.sem_shapes,
        compiler_params=_params(min(56, est / VMEM_MIB + 12)),
    )(*operands, *carry.arrays)


def _row_spec(tr, cols, col_blk=0):
    return pl.BlockSpec((tr, cols), lambda i: (i, col_blk))


def _full_spec(shape):
    nd = len(shape)
    return pl.BlockSpec(shape, lambda i: (0,) * nd)


def _pre_norm(x, g_pre, carry):
    t, d = x.shape
    tr = min(ROW_BLK, t)
    n_carry = len(carry.arrays)
    steps = t // tr

    def body(x_ref, g_ref, *rest):
        c_ins, h_ref = rest[:n_carry], rest[n_carry]
        c_outs, c_sems = rest[n_carry + 1:2 * n_carry + 1], rest[2 * n_carry + 1:]

        @pl.when(pl.program_id(0) == 0)
        def _():
            carry.start(c_ins, c_outs, c_sems)

        xv = x_ref[...]
        r = lax.rsqrt(jnp.mean(xv * xv, axis=-1, keepdims=True) + EPS)
        h_ref[...] = ((xv * r) * g_ref[...]).astype(BF16)

        @pl.when(pl.program_id(0) == steps - 1)
        def _():
            carry.finish(c_ins, c_outs, c_sems)

    return pl.pallas_call(
        body, name="pre_norm", grid=(steps,),
        in_specs=[_row_spec(tr, d), _full_spec((1, d))] + [HBM_SPEC] * n_carry,
        out_specs=[_row_spec(tr, d)] + [HBM_SPEC] * n_carry,
        out_shape=[jax.ShapeDtypeStruct((t, d), BF16)] + carry.out_shapes,
        scratch_shapes=carry.sem_shapes,
        compiler_params=_params(32),
    )(x, g_pre, *carry.arrays)


def _rope_tables(pos_col, inv_freq_lanes):
    t = pos_col.shape[0]
    tr = min(ROW_BLK, t)

    def body(p_ref, f_ref, c_ref, sa_ref, sb_ref):
        ang = p_ref[...].astype(F32) * f_ref[...]
        lane = lax.broadcasted_iota(jnp.int32, ang.shape, 1)
        cos, sin = jnp.cos(ang), jnp.sin(ang)
        c_ref[...] = jnp.where(lane < QK_ROPE, cos, 0.0)
        sa_ref[...] = jnp.where(lane < QK_ROPE // 2, sin, 0.0)
        sb_ref[...] = jnp.where((lane >= QK_ROPE // 2) & (lane < QK_ROPE), sin, 0.0)

    tab = jax.ShapeDtypeStruct((t, 128), F32)
    return pl.pallas_call(
        body, name="rope_tables", grid=(t // tr,),
        in_specs=[_row_spec(tr, 1), _full_spec((1, 128))],
        out_specs=[_row_spec(tr, 128)] * 3,
        out_shape=[tab, tab, tab],
    )(pos_col, inv_freq_lanes)


def _rope_fwd(p, c, sa, sb):
    return p * c - pltpu.roll(p, 96, 1) * sa + pltpu.roll(p, 32, 1) * sb


def _rope_bwd(g, c, sa, sb):
    return g * c + pltpu.roll(g, 96, 1) * sa - pltpu.roll(g, 32, 1) * sb


def _rms(xv, g):
    r = lax.rsqrt(jnp.mean(xv * xv, axis=-1, keepdims=True) + EPS)
    return (xv * r) * g


def _rms_bwd(xv, g, dout):
    r = lax.rsqrt(jnp.mean(xv * xv, axis=-1, keepdims=True) + EPS)
    xn = xv * r
    dg = jnp.sum(dout * xn, axis=0, keepdims=True)
    dxn = dout * g
    dx = r * (dxn - xn * jnp.mean(dxn * xn, axis=-1, keepdims=True))
    return dx, dg


def _latent_norms(proj_lat, q_norm_g, kv_norm_g, c_tab, sa_tab, sb_tab):
    t = proj_lat.shape[0]
    tr = min(ROW_BLK, t)

    def body(cq_ref, ckv_ref, kr_ref, qg_ref, kg_ref, c_ref, sa_ref, sb_ref, cqn_ref, ckvn_ref, kpe_ref):
        cqn_ref[...] = _rms(cq_ref[...].astype(F32), qg_ref[...]).astype(BF16)
        ckvn_ref[...] = _rms(ckv_ref[...].astype(F32), kg_ref[...]).astype(BF16)
        kpe_ref[...] = _rope_fwd(kr_ref[...].astype(F32), c_ref[...], sa_ref[...], sb_ref[...]).astype(BF16)

    return pl.pallas_call(
        body, name="latent_norms", grid=(t // tr,),
        in_specs=[_row_spec(tr, Q_LORA, 0), _row_spec(tr, KV_LORA, 1), _row_spec(tr, 128, (Q_LORA + KV_LORA) // 128),
                  _full_spec((1, Q_LORA)), _full_spec((1, KV_LORA)),
                  _row_spec(tr, 128), _row_spec(tr, 128), _row_spec(tr, 128)],
        out_specs=[_row_spec(tr, Q_LORA), _row_spec(tr, KV_LORA), _row_spec(tr, 128)],
        out_shape=[jax.ShapeDtypeStruct((t, Q_LORA), BF16), jax.ShapeDtypeStruct((t, KV_LORA), BF16),
                   jax.ShapeDtypeStruct((t, 128), BF16)],
    )(proj_lat, proj_lat, proj_lat, q_norm_g, kv_norm_g, c_tab, sa_tab, sb_tab)


def _rope_q_tile(acc, c, sa, sb):
    qs = MLA_SCALE * LOG2E
    parts = []
    for h in range(acc.shape[1] // HEAD_PAD):
        parts.append(acc[:, h * HEAD_PAD:h * HEAD_PAD + 128] * qs)
        parts.append(_rope_fwd(acc[:, h * HEAD_PAD + 128:(h + 1) * HEAD_PAD], c, sa, sb) * qs)
    return jnp.concatenate(parts, axis=1)


def _rope_q_bwd(dq, dkpe, c_tab, sa_tab, sb_tab):
    t = dq.shape[0]
    tr = min(ROW_BLK, t)

    def body(dq_ref, dkp_ref, c_ref, sa_ref, sb_ref, o_ref, dkr_ref):
        c, sa, sb = c_ref[...], sa_ref[...], sb_ref[...]
        for h in range(MLA_HEADS):
            o_ref[:, h * HEAD_PAD:h * HEAD_PAD + 128] = (dq_ref[:, h * HEAD_PAD:h * HEAD_PAD + 128] * MLA_SCALE).astype(BF16)
            g = dq_ref[:, h * HEAD_PAD + 128:(h + 1) * HEAD_PAD] * MLA_SCALE
            o_ref[:, h * HEAD_PAD + 128:(h + 1) * HEAD_PAD] = _rope_bwd(g, c, sa, sb).astype(BF16)
        dkr_ref[...] = _rope_bwd(dkp_ref[...], c, sa, sb)

    w = MLA_HEADS * HEAD_PAD
    return pl.pallas_call(
        body, name="rope_q_bwd", grid=(t // tr,),
        in_specs=[_row_spec(tr, w), _row_spec(tr, 128),
                  _row_spec(tr, 128), _row_spec(tr, 128), _row_spec(tr, 128)],
        out_specs=[_row_spec(tr, w), _row_spec(tr, 128)],
        out_shape=[jax.ShapeDtypeStruct((t, w), BF16), jax.ShapeDtypeStruct((t, 128), F32)],
        compiler_params=_params(48),
    )(dq, dkpe, c_tab, sa_tab, sb_tab)


def _latent_norms_bwd(proj_lat, q_norm_g, kv_norm_g, dcqn, dckvn, dkr):
    t = proj_lat.shape[0]
    tr = min(ROW_BLK, t)

    def body(cq_ref, ckv_ref, qg_ref, kg_ref, dcqn_ref, dckvn_ref, dkr_ref, dl_ref, dqg_ref, dkg_ref):
        dcq, dqg = _rms_bwd(cq_ref[...].astype(F32), qg_ref[...], dcqn_ref[...])
        dckv, dkg = _rms_bwd(ckv_ref[...].astype(F32), kg_ref[...], dckvn_ref[...])
        dl_ref[:, 0:Q_LORA] = dcq.astype(BF16)
        dl_ref[:, Q_LORA:Q_LORA + KV_LORA] = dckv.astype(BF16)
        dl_ref[:, Q_LORA + KV_LORA:LAT_W] = dkr_ref[...].astype(BF16)

        @pl.when(pl.program_id(0) == 0)
        def _():
            dqg_ref[...] = jnp.zeros_like(dqg_ref)
            dkg_ref[...] = jnp.zeros_like(dkg_ref)

        dqg_ref[...] += dqg
        dkg_ref[...] += dkg

    return pl.pallas_call(
        body, name="latent_norms_bwd", grid=(t // tr,),
        in_specs=[_row_spec(tr, Q_LORA, 0), _row_spec(tr, KV_LORA, 1), _full_spec((1, Q_LORA)), _full_spec((1, KV_LORA)),
                  _row_spec(tr, Q_LORA), _row_spec(tr, KV_LORA), _row_spec(tr, 128)],
        out_specs=[_row_spec(tr, LAT_W), _full_spec((1, Q_LORA)), _full_spec((1, KV_LORA))],
        out_shape=[jax.ShapeDtypeStruct((t, LAT_W), BF16), jax.ShapeDtypeStruct((1, Q_LORA), F32),
                   jax.ShapeDtypeStruct((1, KV_LORA), F32)],
    )(proj_lat, proj_lat, q_norm_g, kv_norm_g, dcqn, dckvn, dkr)


def _mem_norm(mem, g):
    m, d = mem.shape

    def body(x_ref, g_ref, o_ref):
        o_ref[...] = _rms(x_ref[...], g_ref[...]).astype(BF16)

    return pl.pallas_call(
        body, name="mem_norm", grid=(1,),
        in_specs=[_full_spec((m, d)), _full_spec((1, d))],
        out_specs=_full_spec((m, d)),
        out_shape=jax.ShapeDtypeStruct((m, d), BF16),
    )(mem, g)


def _mem_norm_bwd(mem, g, dmemn):
    m, d = mem.shape

    def body(x_ref, g_ref, d_ref, dg_ref):
        _, dg = _rms_bwd(x_ref[...], g_ref[...], d_ref[...])
        dg_ref[...] = dg

    return pl.pallas_call(
        body, name="mem_norm_bwd", grid=(1,),
        in_specs=[_full_spec((m, d)), _full_spec((1, d)), _full_spec((m, d))],
        out_specs=_full_spec((1, d)),
        out_shape=jax.ShapeDtypeStruct((1, d), F32),
    )(mem, g, dmemn)


def _merge(gates, p_a, p_b, p_m):
    t, d = p_a.shape
    tr = min(ROW_BLK, t)

    def body(ga_ref, gb_ref, gm_ref, pa_ref, pb_ref, pm_ref, o_ref):
        acc = ga_ref[...].astype(F32) * pa_ref[...].astype(F32)
        acc = acc + gb_ref[...].astype(F32) * pb_ref[...].astype(F32)
        acc = acc + gm_ref[...].astype(F32) * pm_ref[...].astype(F32)
        o_ref[...] = acc.astype(BF16)

    return pl.pallas_call(
        body, name="merge", grid=(t // tr,),
        in_specs=[_row_spec(tr, d, 0), _row_spec(tr, d, 1), _row_spec(tr, d, 2),
                  _row_spec(tr, d), _row_spec(tr, d), _row_spec(tr, d)],
        out_specs=_row_spec(tr, d),
        out_shape=jax.ShapeDtypeStruct((t, d), BF16),
        compiler_params=_params(48),
    )(gates, gates, gates, p_a, p_b, p_m)


def _merge_bwd(dm, gates, p_a, p_b, p_m):
    t, d = dm.shape
    tr = min(ROW_BLK, t)

    def body(dm_ref, g_ref, pa_ref, pb_ref, pm_ref, dpa_ref, dpb_ref, dpm_ref, dgl_ref, db_ref):
        dmv = dm_ref[...].astype(F32)

        @pl.when(pl.program_id(0) == 0)
        def _():
            db_ref[...] = jnp.zeros_like(db_ref)

        for n, (p_ref, dp_ref) in enumerate(((pa_ref, dpa_ref), (pb_ref, dpb_ref), (pm_ref, dpm_ref))):
            g = g_ref[:, n * d:(n + 1) * d].astype(F32)
            dp_ref[...] = (dmv * g).astype(BF16)
            dgl = dmv * p_ref[...].astype(F32) * (g * (1.0 - g))
            dgl_ref[:, n * d:(n + 1) * d] = dgl.astype(BF16)
            db_ref[:, n * d:(n + 1) * d] += jnp.sum(dgl, axis=0, keepdims=True)

    act = jax.ShapeDtypeStruct((t, d), BF16)
    return pl.pallas_call(
        body, name="merge_bwd", grid=(t // tr,),
        in_specs=[_row_spec(tr, d), _row_spec(tr, 3 * d), _row_spec(tr, d), _row_spec(tr, d), _row_spec(tr, d)],
        out_specs=[_row_spec(tr, d), _row_spec(tr, d), _row_spec(tr, d), _row_spec(tr, 3 * d), _full_spec((1, 3 * d))],
        out_shape=[act, act, act, jax.ShapeDtypeStruct((t, 3 * d), BF16), jax.ShapeDtypeStruct((1, 3 * d), F32)],
        compiler_params=_params(56),
    )(dm, gates, p_a, p_b, p_m)


def _post_loss(out, x, tgt, g_post):
    t, d = out.shape
    tr = min(ROW_BLK, t)

    def body(o_ref, x_ref, t_ref, g_ref, do_ref, dy_ref, loss_ref, dg_ref):
        ov = o_ref[...]
        g = g_ref[...]
        r = lax.rsqrt(jnp.mean(ov * ov, axis=-1, keepdims=True) + EPS)
        on = ov * r
        err = (x_ref[...] + on * g) - t_ref[...]
        dy = err * (1.0 / d)
        dy_ref[...] = dy
        don = dy * g
        do_ref[...] = (r * (don - on * jnp.mean(don * on, axis=-1, keepdims=True))).astype(BF16)

        @pl.when(pl.program_id(0) == 0)
        def _():
            loss_ref[...] = jnp.zeros_like(loss_ref)
            dg_ref[...] = jnp.zeros_like(dg_ref)

        loss_ref[...] += 0.5 * jnp.sum(jnp.mean(err * err, axis=-1, keepdims=True))
        dg_ref[...] += jnp.sum(dy * on, axis=0, keepdims=True)

    return pl.pallas_call(
        body, name="post_loss", grid=(t // tr,),
        in_specs=[_row_spec(tr, d), _row_spec(tr, d), _row_spec(tr, d), _full_spec((1, d))],
        out_specs=[_row_spec(tr, d), _row_spec(tr, d), _full_spec((8, 128)), _full_spec((1, d))],
        out_shape=[jax.ShapeDtypeStruct((t, d), BF16), jax.ShapeDtypeStruct((t, d), F32),
                   jax.ShapeDtypeStruct((8, 128), F32), jax.ShapeDtypeStruct((1, d), F32)],
        compiler_params=_params(56),
    )(out, x, tgt, g_post)


def _pre_norm_bwd(x, g_pre, dh, dy):
    t, d = x.shape
    tr = min(ROW_BLK, t)

    def body(x_ref, g_ref, dh_ref, dy_ref, dx_ref, dg_ref):
        dx, dg = _rms_bwd(x_ref[...], g_ref[...], dh_ref[...])
        dx_ref[...] = dx + dy_ref[...]

        @pl.when(pl.program_id(0) == 0)
        def _():
            dg_ref[...] = jnp.zeros_like(dg_ref)

        dg_ref[...] += dg

    return pl.pallas_call(
        body, name="pre_norm_bwd", grid=(t // tr,),
        in_specs=[_row_spec(tr, d), _full_spec((1, d)), _row_spec(tr, d), _row_spec(tr, d)],
        out_specs=[_row_spec(tr, d), _full_spec((1, d))],
        out_shape=[jax.ShapeDtypeStruct((t, d), F32), jax.ShapeDtypeStruct((1, d), F32)],
        compiler_params=_params(56),
    )(x, g_pre, dh, dy)


def _causal_mask(n):
    row = lax.broadcasted_iota(jnp.int32, (n, n), 0)
    col = lax.broadcasted_iota(jnp.int32, (n, n), 1)
    return row >= col


def _layernorm_stats(vg):
    mu = jnp.mean(vg, axis=-1, keepdims=True)
    cen = vg - mu
    rstd = lax.rsqrt(jnp.mean(cen * cen, axis=-1, keepdims=True) + EPS)
    return cen * rstd, rstd


def _gmlp_fwd(proj_big, ln_g, ln_b, w_s, b_exp):
    t = proj_big.shape[0]
    rows = min(GMLP_ROWS, t)
    d = D_MODEL

    def body(u_ref, v_ref, z_ref, lg_ref, lb_ref, ws_ref, be_ref, y_ref, vn_scr):
        vhat, _ = _layernorm_stats(_gelu(v_ref[...].astype(F32)))
        vn_scr[...] = (vhat * lg_ref[...] + lb_ref[...]).astype(BF16)
        mask = _causal_mask(CHUNK)
        for g in range(A_GROUPS):
            cols = slice(g * 128, (g + 1) * 128)
            wsm = jnp.where(mask, ws_ref[g], 0.0).astype(BF16)
            for c in range(rows // CHUNK):
                rws = slice(c * CHUNK, (c + 1) * CHUNK)
                sv = jnp.dot(wsm, vn_scr[rws, cols], preferred_element_type=F32) + be_ref[g]
                zs, _ = _silu_and_grad(z_ref[rws, cols].astype(F32))
                y_ref[rws, cols] = (_gelu(u_ref[rws, cols].astype(F32)) * sv * zs).astype(BF16)

    return pl.pallas_call(
        body, name="gmlp_fwd", grid=(t // rows,),
        in_specs=[_row_spec(rows, d, 0), _row_spec(rows, d, 1), _row_spec(rows, d, 2),
                  _full_spec((1, d)), _full_spec((1, d)), _full_spec((A_GROUPS, CHUNK, CHUNK)),
                  _full_spec((A_GROUPS, CHUNK, 128))],
        out_specs=_row_spec(rows, d),
        out_shape=jax.ShapeDtypeStruct((t, d), BF16),
        scratch_shapes=[pltpu.VMEM((rows, d), BF16)],
        compiler_params=_params(40),
    )(proj_big, proj_big, proj_big, ln_g, ln_b, w_s, b_exp)


def _gmlp_bwd(proj_big, dya, ln_g, ln_b, w_s, b_exp):
    t = proj_big.shape[0]
    rows = min(GMLP_ROWS, t)
    d = D_MODEL
    nt = (((1,), (1,)), ((), ()))
    tn = (((0,), (0,)), ((), ()))

    def body(u_ref, v_ref, z_ref, dy_ref, lg_ref, lb_ref, ws_ref, be_ref,
             dp_ref, dws_ref, dbs_ref, dlg_ref, dlb_ref, vn_scr, dvn_scr):
        @pl.when(pl.program_id(0) == 0)
        def _():
            dws_ref[...] = jnp.zeros_like(dws_ref)
            dbs_ref[...] = jnp.zeros_like(dbs_ref)
            dlg_ref[...] = jnp.zeros_like(dlg_ref)
            dlb_ref[...] = jnp.zeros_like(dlb_ref)

        vg, vgrad = _gelu_and_grad(v_ref[...].astype(F32))
        vhat, rstd = _layernorm_stats(vg)
        vn_scr[...] = (vhat * lg_ref[...] + lb_ref[...]).astype(BF16)
        mask = _causal_mask(CHUNK)
        for g in range(A_GROUPS):
            cols = slice(g * 128, (g + 1) * 128)
            wsm = jnp.where(mask, ws_ref[g], 0.0).astype(BF16)
            dws = jnp.zeros((CHUNK, CHUNK), F32)
            dbs = jnp.zeros((CHUNK, 1), F32)
            for c in range(rows // CHUNK):
                rws = slice(c * CHUNK, (c + 1) * CHUNK)
                vn = vn_scr[rws, cols]
                sv = jnp.dot(wsm, vn, preferred_element_type=F32) + be_ref[g]
                ug, ugrad = _gelu_and_grad(u_ref[rws, cols].astype(F32))
                zs, zgrad = _silu_and_grad(z_ref[rws, cols].astype(F32))
                dya = dy_ref[rws, cols].astype(F32)
                dga = dya * zs
                dp_ref[rws, 2 * d + g * 128:2 * d + (g + 1) * 128] = (dya * (ug * sv) * zgrad).astype(BF16)
                dp_ref[rws, cols] = (dga * sv * ugrad).astype(BF16)
                dsv = dga * ug
                dsv16 = dsv.astype(BF16)
                dws = dws + lax.dot_general(dsv16, vn, nt, preferred_element_type=F32)
                dbs = dbs + jnp.sum(dsv, axis=-1, keepdims=True)
                dvn_scr[rws, cols] = lax.dot_general(wsm, dsv16, tn, preferred_element_type=F32)
            dws_ref[g] += jnp.where(mask, dws, 0.0)
            dbs_ref[g] += dbs
        dvn = dvn_scr[...]
        dlg_ref[...] += jnp.sum(dvn * vhat, axis=0, keepdims=True)
        dlb_ref[...] += jnp.sum(dvn, axis=0, keepdims=True)
        dvh = dvn * lg_ref[...]
        dvg = rstd * (dvh - jnp.mean(dvh, axis=-1, keepdims=True) - vhat * jnp.mean(dvh * vhat, axis=-1, keepdims=True))
        dp_ref[:, d:2 * d] = (dvg * vgrad).astype(BF16)

    return pl.pallas_call(
        body, name="gmlp_bwd", grid=(t // rows,),
        in_specs=[_row_spec(rows, d, 0), _row_spec(rows, d, 1), _row_spec(rows, d, 2), _row_spec(rows, d),
                  _full_spec((1, d)), _full_spec((1, d)), _full_spec((A_GROUPS, CHUNK, CHUNK)),
                  _full_spec((A_GROUPS, CHUNK, 128))],
        out_specs=[_row_spec(rows, 3 * d), _full_spec((A_GROUPS, CHUNK, CHUNK)), _full_spec((A_GROUPS, CHUNK, 1)),
                   _full_spec((1, d)), _full_spec((1, d))],
        out_shape=[jax.ShapeDtypeStruct((t, 6 * d), BF16), jax.ShapeDtypeStruct((A_GROUPS, CHUNK, CHUNK), F32),
                   jax.ShapeDtypeStruct((A_GROUPS, CHUNK, 1), F32), jax.ShapeDtypeStruct((1, d), F32),
                   jax.ShapeDtypeStruct((1, d), F32)],
        scratch_shapes=[pltpu.VMEM((rows, d), BF16), pltpu.VMEM((rows, d), F32)],
        compiler_params=_params(48),
    )(proj_big, proj_big, proj_big, dya, ln_g, ln_b, w_s, b_exp)


NT_DIMS = (((1,), (1,)), ((), ()))
TN_DIMS = (((0,), (0,)), ((), ()))


def _mla_fwd(q, kv, kpe, proj_big):
    t = q.shape[0]
    blk = min(ATT_BLK_FWD, t)
    nq = t // blk
    zb_blk0 = (3 * D_MODEL) // 128


    sub = min(ATT_SUB, blk)

    def body(q_ref, kv_ref, kp_ref, zb_ref, o_ref, yb_ref, lse_ref, m_scr, acc_scr):
        i = pl.program_id(1)
        qv = q_ref[...]
        m_scr[...] = jnp.full_like(m_scr, NEG)
        acc_scr[...] = jnp.zeros_like(acc_scr)
        ones = jnp.ones((blk, 128), BF16)

        def step(j, masked):
            ks = pl.ds(pl.multiple_of(j * blk, blk), blk)
            kc = jnp.concatenate([kv_ref[ks, 0:128], kp_ref[ks, :]], axis=1)
            vext = jnp.concatenate([kv_ref[ks, 128:256], ones], axis=1)
            for r in range(blk // sub):
                rows = slice(r * sub, (r + 1) * sub)
                kw = (r + 1) * sub if masked else blk
                tt = lax.dot_general(qv[rows], kc[:kw], NT_DIMS, preferred_element_type=F32)
                if masked:
                    row = lax.broadcasted_iota(jnp.int32, (sub, kw), 0) + r * sub
                    col = lax.broadcasted_iota(jnp.int32, (sub, kw), 1)
                    tt = jnp.where(row >= col, tt, NEG)
                cm = tt[:, 0:128]
                for c in range(1, kw // 128):
                    cm = jnp.maximum(cm, tt[:, c * 128:(c + 1) * 128])
                m_prev = m_scr[rows, :]
                m_new = jnp.maximum(m_prev, jnp.max(cm, axis=-1, keepdims=True))
                alpha = jnp.exp2(m_prev - m_new)
                m_scr[rows, :] = m_new
                p = jnp.concatenate([jnp.exp2(tt[:, c * 128:(c + 1) * 128] - m_new).astype(BF16)
                                     for c in range(kw // 128)], axis=1)
                pv = jnp.dot(p, vext[:kw], preferred_element_type=F32)
                acc_scr[rows, :] = jnp.concatenate([alpha, alpha], axis=1) * acc_scr[rows, :] + pv

        def loop_body(j, carry):
            step(j, False)
            return carry

        lax.fori_loop(0, i, loop_body, 0)
        step(i, True)
        l = acc_scr[:, 128:256]
        o = acc_scr[:, 0:128] / l
        o_ref[...] = o.astype(BF16)
        zs, _ = _silu_and_grad(zb_ref[...].astype(F32))
        yb_ref[...] = (o * zs).astype(BF16)
        lse_ref[0] = m_scr[...] + jnp.log2(l)

    act = jax.ShapeDtypeStruct((t, D_MODEL), BF16)
    return pl.pallas_call(
        body, name="mla_fwd", grid=(MLA_HEADS, nq),
        in_specs=[pl.BlockSpec((blk, HEAD_PAD), lambda h, i: (i, h)),
                  pl.BlockSpec((t, HEAD_PAD), lambda h, i: (0, h)),
                  pl.BlockSpec((t, 128), lambda h, i: (0, 0)),
                  pl.BlockSpec((blk, 128), lambda h, i: (i, zb_blk0 + h))],
        out_specs=[pl.BlockSpec((blk, 128), lambda h, i: (i, h)),
                   pl.BlockSpec((blk, 128), lambda h, i: (i, h)),
                   pl.BlockSpec((1, blk, 128), lambda h, i: (h, i, 0))],
        out_shape=[act, act, jax.ShapeDtypeStruct((MLA_HEADS, t, 128), F32)],
        scratch_shapes=[pltpu.VMEM((blk, 128), F32), pltpu.VMEM((blk, HEAD_PAD), F32)],
        compiler_params=_params(56),
    )(q, kv, kpe, proj_big)


def _mla_gate_bwd(dyb, proj_big, o, dproj):
    t, d = dyb.shape
    tr = min(ROW_BLK, t)

    def body(dy_ref, zb_ref, o_ref, buf_ref, do_ref, dz_ref, dl_ref):
        del buf_ref
        dy = dy_ref[...].astype(F32)
        ov = o_ref[...].astype(F32)
        zs, zgrad = _silu_and_grad(zb_ref[...].astype(F32))
        do16 = (dy * zs).astype(BF16)
        do_ref[...] = do16
        dz_ref[...] = (dy * ov * zgrad).astype(BF16)
        prod = do16.astype(F32) * ov
        for h in range(MLA_HEADS):
            delta = jnp.sum(prod[:, h * 128:(h + 1) * 128], axis=-1, keepdims=True)
            dl_ref[h] = jnp.broadcast_to(delta, (tr, 128))

    act = jax.ShapeDtypeStruct((t, d), BF16)
    head_spec = pl.BlockSpec((MLA_HEADS, tr, 128), lambda i: (0, i, 0))
    return pl.pallas_call(
        body, name="mla_gate_bwd", grid=(t // tr,),
        in_specs=[_row_spec(tr, d), _row_spec(tr, d, 3), _row_spec(tr, d), HBM_SPEC],
        out_specs=[_row_spec(tr, d), _row_spec(tr, d, 3), head_spec],
        out_shape=[act, jax.ShapeDtypeStruct((t, 6 * d), BF16), jax.ShapeDtypeStruct((MLA_HEADS, t, 128), F32)],
        input_output_aliases={3: 1},
        compiler_params=_params(56),
    )(dyb, proj_big, o, dproj)


def _mla_bwd(q, kv, kpe, do, lse, delta, carry):
    t = q.shape[0]
    blk = min(ATT_BLK, t)
    n = t // blk
    nc = blk // 128
    pairs = [(j, i) for j in range(n) for i in range(j, n)]
    j_tab = jnp.asarray([p[0] for p in pairs], jnp.int32)
    i_tab = jnp.asarray([p[1] for p in pairs], jnp.int32)
    n_carry = len(carry.arrays)
    sub = min(ATT_SUB, blk)

    def body(j_ref, i_ref, q_ref, do_ref, lse_ref, dl_ref, kv_ref, kp_ref, *rest):
        c_ins, rest = rest[:n_carry], rest[n_carry:]
        dq_ref, dkv_ref, dkp_ref = rest[:3]
        c_outs, rest = rest[3:3 + n_carry], rest[3 + n_carry:]
        dk_scr, dv_scr = rest[:2]
        c_sems = rest[2:]
        head = pl.program_id(0)
        step = pl.program_id(1)
        j, i = j_ref[step], i_ref[step]

        @pl.when((head == 0) & (step == 0))
        def _():
            carry.start(c_ins, c_outs, c_sems)

        @pl.when(step == 0)
        def _():
            dq_ref[...] = jnp.zeros_like(dq_ref)

        @pl.when(i == j)
        def _():
            dk_scr[...] = jnp.zeros_like(dk_scr)
            dv_scr[...] = jnp.zeros_like(dv_scr)

        kc = jnp.concatenate([kv_ref[:, 0:128], kp_ref[...]], axis=1)
        vv = kv_ref[:, 128:256]

        def tile(diag):
            for r in range(blk // sub):
                rows = slice(r * sub, (r + 1) * sub)
                kw = (r + 1) * sub if diag else blk
                qv, dov = q_ref[rows, :], do_ref[rows, :]
                tt = lax.dot_general(qv, kc[:kw], NT_DIMS, preferred_element_type=F32)
                if diag:
                    row = lax.broadcasted_iota(jnp.int32, (sub, kw), 0) + r * sub
                    col = lax.broadcasted_iota(jnp.int32, (sub, kw), 1)
                    tt = jnp.where(row >= col, tt, NEG)
                dp = lax.dot_general(dov, vv[:kw], NT_DIMS, preferred_element_type=F32)
                lse_v, dl_v = lse_ref[0, rows, :], dl_ref[0, rows, :]
                ps, dss = [], []
                for c in range(kw // 128):
                    cols = slice(c * 128, (c + 1) * 128)
                    p = jnp.exp2(tt[:, cols] - lse_v)
                    ps.append(p.astype(BF16))
                    dss.append((p * (dp[:, cols] - dl_v)).astype(BF16))
                p16 = jnp.concatenate(ps, axis=1)
                ds16 = jnp.concatenate(dss, axis=1)
                dv_scr[0:kw, :] += lax.dot_general(p16, dov, TN_DIMS, preferred_element_type=F32)
                dk_scr[0:kw, :] += lax.dot_general(ds16, qv, TN_DIMS, preferred_element_type=F32)
                qs = pl.ds(pl.multiple_of(i * blk + r * sub, sub), sub)
                dq_ref[qs, :] += jnp.dot(ds16, kc[:kw], preferred_element_type=F32)

        @pl.when(i == j)
        def _():
            tile(True)

        @pl.when(i > j)
        def _():
            tile(False)

        @pl.when(i == n - 1)
        def _():
            dkv_ref[:, 0:128] = (dk_scr[:, 0:128] * (1.0 / LOG2E)).astype(BF16)
            dkv_ref[:, 128:256] = dv_scr[...].astype(BF16)
            ks = pl.ds(pl.multiple_of(j * blk, blk), blk)
            dkp = dk_scr[:, 128:256] * (1.0 / LOG2E)

            @pl.when(head == 0)
            def _():
                dkp_ref[ks, :] = dkp

            @pl.when(head > 0)
            def _():
                dkp_ref[ks, :] += dkp

        @pl.when((head == MLA_HEADS - 1) & (step == len(pairs) - 1))
        def _():
            carry.finish(c_ins, c_outs, c_sems)

    grid_spec = pltpu.PrefetchScalarGridSpec(
        num_scalar_prefetch=2, grid=(MLA_HEADS, len(pairs)),
        in_specs=[pl.BlockSpec((blk, HEAD_PAD), lambda h, s, jt, it: (it[s], h)),
                  pl.BlockSpec((blk, 128), lambda h, s, jt, it: (it[s], h)),
                  pl.BlockSpec((1, blk, 128), lambda h, s, jt, it: (h, it[s], 0)),
                  pl.BlockSpec((1, blk, 128), lambda h, s, jt, it: (h, it[s], 0)),
                  pl.BlockSpec((blk, HEAD_PAD), lambda h, s, jt, it: (jt[s], h)),
                  pl.BlockSpec((blk, 128), lambda h, s, jt, it: (jt[s], 0))] + [HBM_SPEC] * n_carry,
        out_specs=[pl.BlockSpec((t, HEAD_PAD), lambda h, s, jt, it: (0, h)),
                   pl.BlockSpec((blk, HEAD_PAD), lambda h, s, jt, it: (jt[s], h)),
                   pl.BlockSpec((t, 128), lambda h, s, jt, it: (0, 0))] + [HBM_SPEC] * n_carry,
        scratch_shapes=[pltpu.VMEM((blk, HEAD_PAD), F32), pltpu.VMEM((blk, 128), F32)] + carry.sem_shapes,
    )
    return pl.pallas_call(
        body, name="mla_bwd", grid_spec=grid_spec,
        out_shape=[jax.ShapeDtypeStruct((t, MLA_HEADS * HEAD_PAD), F32),
                   jax.ShapeDtypeStruct((t, 2 * D_MODEL), BF16),
                   jax.ShapeDtypeStruct((t, 128), F32)] + carry.out_shapes,
        compiler_params=_params(58),
    )(j_tab, i_tab, q, do, lse, delta, kv, kpe, *carry.arrays)


def _mem_attn_probs(qv, k_ref):
    s = lax.dot_general(qv, k_ref[...], NT_DIMS, preferred_element_type=F32) * MEM_SCALE
    e = jnp.exp(s - jnp.max(s, axis=-1, keepdims=True))
    return e / jnp.sum(e, axis=-1, keepdims=True)


def _mem_fwd(proj_big, kv_m):
    t = proj_big.shape[0]
    tq = min(MEM_Q_BLK, t)
    hd = MEM_HEAD_DIM
    d = D_MODEL
    mlen = kv_m.shape[0]

    def body(q_ref, z_ref, kv_ref, y_ref):
        for h in range(MEM_HEADS):
            cols = slice(h * hd, (h + 1) * hd)
            p = _mem_attn_probs(q_ref[:, cols], kv_ref.at[:, cols])
            o = jnp.dot(p.astype(BF16), kv_ref[:, d + h * hd:d + (h + 1) * hd], preferred_element_type=F32)
            zs, _ = _silu_and_grad(z_ref[:, cols].astype(F32))
            y_ref[:, cols] = (o * zs).astype(BF16)

    return pl.pallas_call(
        body, name="mem_fwd", grid=(t // tq,),
        in_specs=[_row_spec(tq, d, 4), _row_spec(tq, d, 5), _full_spec((mlen, 2 * d))],
        out_specs=_row_spec(tq, d),
        out_shape=jax.ShapeDtypeStruct((t, d), BF16),
        compiler_params=_params(40),
    )(proj_big, proj_big, kv_m)


def _mem_bwd(proj_big, kv_m, dym, dproj):
    t = proj_big.shape[0]
    tq = min(MEM_Q_BLK, t)
    hd = MEM_HEAD_DIM
    d = D_MODEL
    mlen = kv_m.shape[0]

    def body(q_ref, z_ref, kv_ref, dy_ref, buf_ref, dqz_ref, dkv_ref):
        del buf_ref

        @pl.when(pl.program_id(0) == 0)
        def _():
            dkv_ref[...] = jnp.zeros_like(dkv_ref)

        for h in range(MEM_HEADS):
            cols = slice(h * hd, (h + 1) * hd)
            k_ref, v_ref = kv_ref.at[:, cols], kv_ref.at[:, d + h * hd:d + (h + 1) * hd]
            qv = q_ref[:, cols]
            p = _mem_attn_probs(qv, k_ref)
            p16 = p.astype(BF16)
            o = jnp.dot(p16, v_ref[...], preferred_element_type=F32)
            zs, zgrad = _silu_and_grad(z_ref[:, cols].astype(F32))
            dy = dy_ref[:, cols].astype(F32)
            dqz_ref[:, d + h * hd:d + (h + 1) * hd] = (dy * o * zgrad).astype(BF16)
            do16 = (dy * zs).astype(BF16)
            dkv_ref[:, d + h * hd:d + (h + 1) * hd] += lax.dot_general(p16, do16, TN_DIMS, preferred_element_type=F32)
            dp = lax.dot_general(do16, v_ref[...], NT_DIMS, preferred_element_type=F32)
            ds = (p * (dp - jnp.sum(dp * p, axis=-1, keepdims=True)) * MEM_SCALE).astype(BF16)
            dqz_ref[:, cols] = jnp.dot(ds, k_ref[...], preferred_element_type=F32).astype(BF16)
            dkv_ref[:, cols] += lax.dot_general(ds, qv, TN_DIMS, preferred_element_type=F32)

    return pl.pallas_call(
        body, name="mem_bwd", grid=(t // tq,),
        in_specs=[_row_spec(tq, d, 4), _row_spec(tq, d, 5), _full_spec((mlen, 2 * d)), _row_spec(tq, d), HBM_SPEC],
        out_specs=[_row_spec(tq, 2 * d, 2), _full_spec((mlen, 2 * d))],
        out_shape=[jax.ShapeDtypeStruct((t, 6 * d), BF16), jax.ShapeDtypeStruct((mlen, 2 * d), F32)],
        input_output_aliases={4: 0},
        compiler_params=_params(56),
    )(proj_big, proj_big, kv_m, dym, dproj)


HBM_SPEC = pl.BlockSpec(memory_space=pl.ANY)
N_PEERS = N_DEV - 1


def _dev_index(px, py, pc):
    return 4 * px + 2 * py + pc


class _Gather:
    def __init__(self, arrays):
        self.arrays = list(arrays)
        n = len(self.arrays)
        self.out_shapes = [jax.ShapeDtypeStruct((N_DEV,) + a.shape, a.dtype) for a in self.arrays]
        self.sem_shapes = [pltpu.SemaphoreType.DMA((n * N_PEERS,)), pltpu.SemaphoreType.DMA((n * N_PEERS,)),
                           pltpu.SemaphoreType.DMA((n,))]

    def _parts(self, ins, outs, sems):
        n = len(self.arrays)
        send_sems, recv_sems, local_sems = sems
        x, y, c = lax.axis_index("x"), lax.axis_index("y"), lax.axis_index("c")
        me, sibling = (x, y, c), (x, y, 1 - c)
        chips = [(1 - x, y), (x, 1 - y), (1 - x, 1 - y)]

        def copy(a, k, block, to, src=None):
            dst = outs[a].at[_dev_index(*block)]
            return pltpu.make_async_remote_copy(
                src_ref=dst if src is None else src, dst_ref=dst,
                send_sem=send_sems.at[a * N_PEERS + k], recv_sem=recv_sems.at[a * N_PEERS + k],
                device_id=to, device_id_type=MESH)

        mine = [pltpu.make_async_copy(ins[a], outs[a].at[_dev_index(*me)], local_sems.at[a]) for a in range(n)]
        first = []
        for a in range(n):
            first.append(copy(a, 0, me, sibling, src=ins[a]))
            first += [copy(a, 1 + j, me, (*chip, c), src=ins[a]) for j, chip in enumerate(chips)]
        return n, c, me, sibling, chips, copy, mine, first

    def start(self, ins, outs, sems):
        _, _, _, _, _, _, mine, first = self._parts(ins, outs, sems)
        for cp in mine + first:
            cp.start()

    def forward(self, ins, outs, sems):
        n, c, me, sibling, chips, copy, _, _ = self._parts(ins, outs, sems)
        for j, chip in enumerate(chips):
            for a in range(n):
                copy(a, 1 + j, (*chip, c), me).wait_recv()
                copy(a, 4 + j, (*chip, c), sibling).start()

    def finish(self, ins, outs, sems, forwarded=False):
        if not forwarded:
            self.forward(ins, outs, sems)
        n, c, me, sibling, chips, copy, mine, first = self._parts(ins, outs, sems)
        passed = [copy(a, 4 + j, (*chip, c), sibling) for j, chip in enumerate(chips) for a in range(n)]
        for a in range(n):
            copy(a, 0, sibling, me).wait_recv()
            for j, chip in enumerate(chips):
                copy(a, 4 + j, (*chip, 1 - c), me).wait_recv()
        for cp in first + passed:
            cp.wait_send()
        for cp in mine:
            cp.wait()


class _AllToAll:
    def __init__(self, arrays, broadcast=False):
        self.arrays = list(arrays)
        self.broadcast = broadcast
        n = len(self.arrays)
        self.out_shapes = [jax.ShapeDtypeStruct(((N_DEV,) + a.shape) if broadcast else a.shape, a.dtype)
                           for a in self.arrays]
        self.sem_shapes = [pltpu.SemaphoreType.DMA((n * N_PEERS,)), pltpu.SemaphoreType.DMA((n * N_PEERS,)),
                           pltpu.SemaphoreType.DMA((n,))]

    def _parts(self, ins, outs, sems):
        n = len(self.arrays)
        send_sems, recv_sems, local_sems = sems
        x, y, c = lax.axis_index("x"), lax.axis_index("y"), lax.axis_index("c")
        my_idx = _dev_index(x, y, c)
        peers = []
        for k in range(1, N_DEV):
            dx, dy, dc = (k >> 2) & 1, (k >> 1) & 1, k & 1
            peers.append((1 - x if dx else x, 1 - y if dy else y, 1 - c if dc else c))

        def mine_of(a, idx):
            return ins[a] if self.broadcast else ins[a].at[idx]

        def copy(a, k, peer):
            return pltpu.make_async_remote_copy(
                src_ref=mine_of(a, _dev_index(*peer)), dst_ref=outs[a].at[my_idx],
                send_sem=send_sems.at[a * N_PEERS + k], recv_sem=recv_sems.at[a * N_PEERS + k],
                device_id=peer, device_id_type=MESH)

        def landed(a, k, peer):
            slot = outs[a].at[_dev_index(*peer)]
            return pltpu.make_async_remote_copy(
                src_ref=slot, dst_ref=slot,
                send_sem=send_sems.at[a * N_PEERS + k], recv_sem=recv_sems.at[a * N_PEERS + k],
                device_id=peer, device_id_type=MESH)

        mine = [pltpu.make_async_copy(mine_of(a, my_idx), outs[a].at[my_idx], local_sems.at[a]) for a in range(n)]
        sends = [copy(a, k, peer) for a in range(n) for k, peer in enumerate(peers)]
        return n, peers, landed, mine, sends

    def start(self, ins, outs, sems):
        _, _, _, mine, sends = self._parts(ins, outs, sems)
        for cp in mine + sends:
            cp.start()

    def finish(self, ins, outs, sems):
        n, peers, landed, mine, sends = self._parts(ins, outs, sems)
        for a in range(n):
            for k, peer in enumerate(peers):
                landed(a, k, peer).wait_recv()
        for cp in sends:
            cp.wait_send()
        for cp in mine:
            cp.wait()


def _exchange(plan, name):
    n = len(plan.arrays)

    def body(*refs):
        ins, outs, sems = refs[:n], refs[n:2 * n], refs[2 * n:]
        plan.start(ins, outs, sems)
        plan.finish(ins, outs, sems)

    return pl.pallas_call(
        body, name=name, in_specs=[HBM_SPEC] * n, out_specs=[HBM_SPEC] * n,
        out_shape=plan.out_shapes, scratch_shapes=plan.sem_shapes,
    )(*plan.arrays)


def _adamw(w, g, m, v):
    m = ADAM_B1 * m + (1.0 - ADAM_B1) * g
    v = ADAM_B2 * v + (1.0 - ADAM_B2) * jnp.square(g)
    m_hat = m / (1.0 - ADAM_B1 ** ADAM_STEP)
    v_hat = v / (1.0 - ADAM_B2 ** ADAM_STEP)
    delta = -ADAM_LR * (m_hat / (jnp.sqrt(v_hat) + ADAM_EPS) + ADAM_WD * w)
    return delta, m, v


def _adam_sharded(parts_list, w, m, v, name):
    shape = w.shape
    cols = shape[-1]
    rows = int(np.prod(shape[:-1]))
    tr = min(128, rows)
    parts_list = [p.reshape(N_DEV, -1, cols) for p in parts_list]
    bounds = np.cumsum([0] + [p.shape[1] // tr for p in parts_list])
    assert rows % tr == 0 and all(p.shape[1] % tr == 0 for p in parts_list) and bounds[-1] == rows // tr
    n_parts = len(parts_list)

    def body(*refs):
        p_refs = refs[:n_parts]
        w_ref, m_ref, v_ref, g_ref, d_ref, nm_ref, nv_ref = refs[n_parts:]
        i = pl.program_id(0)
        for k, p_ref in enumerate(p_refs):
            @pl.when((i >= bounds[k]) & (i < bounds[k + 1]))
            def _():
                g = p_ref[0].astype(F32)
                for e in range(1, N_DEV):
                    g = g + p_ref[e].astype(F32)
                g_ref[...] = g
                d_ref[...], nm_ref[...], nv_ref[...] = _adamw(w_ref[...], g, m_ref[...], v_ref[...])

    def part_spec(k):
        lo, hi = int(bounds[k]), int(bounds[k + 1])
        return pl.BlockSpec((N_DEV, tr, cols), lambda i: (0, jnp.clip(i, lo, hi - 1) - lo, 0))

    spec = pl.BlockSpec((tr, cols), lambda i: (i, 0))
    flat = jax.ShapeDtypeStruct((rows, cols), F32)
    outs = pl.pallas_call(
        body, name=name, grid=(rows // tr,),
        in_specs=[part_spec(k) for k in range(n_parts)] + [spec, spec, spec],
        out_specs=[spec] * 4, out_shape=[flat] * 4,
        compiler_params=_params(40),
    )(*parts_list, w.reshape(rows, cols), m.reshape(rows, cols), v.reshape(rows, cols))
    return [o.reshape(shape) for o in outs]


def _adam_replicated(parts, w, m, v):
    r = w.shape[0]

    def body(p_ref, w_ref, m_ref, v_ref, g_ref, d_ref, nm_ref, nv_ref):
        g = p_ref[0]
        for e in range(1, N_DEV):
            g = g + p_ref[e]
        g_ref[...] = g
        d_ref[...], nm_ref[...], nv_ref[...] = _adamw(w_ref[...], g, m_ref[...], v_ref[...])

    spec = _full_spec((r, 128))
    flat = jax.ShapeDtypeStruct((r, 128), F32)
    return pl.pallas_call(
        body, name="adam_replicated", grid=(1,),
        in_specs=[_full_spec((N_DEV, r, 128)), spec, spec, spec],
        out_specs=[spec] * 4, out_shape=[flat] * 4,
        compiler_params=_params(48),
    )(parts, w, m, v)


def _pack(arrays):
    parts = []
    for a in arrays:
        f = a.reshape(-1, 128)
        pad = -f.shape[0] % 8
        parts.append(jnp.pad(f, ((0, pad), (0, 0))) if pad else f)
    return jnp.concatenate(parts, axis=0)


def _unpack(packed, shapes):
    out, row = [], 0
    for shape in shapes:
        r = int(np.prod(shape)) // 128
        out.append(packed[row:row + r].reshape(shape))
        row += r + (-r % 8)
    return out


SHARDED = ("w_in", "w_uq", "w_ukv", "w_mem_kv", "w_gate", "w_branch", "w_out")
REPLICATED = ("g_pre", "a_ln_g", "a_ln_b", "a_w_s", "a_b_s", "q_norm_g", "kv_norm_g", "mem_norm_g", "b_gate", "g_post")
WEIGHT_ORDER = ("g_pre", "w_in", "a_ln_g", "a_ln_b", "a_w_s", "a_b_s", "q_norm_g", "w_uq", "kv_norm_g", "w_ukv",
                "mem_norm_g", "w_mem_kv", "w_gate", "b_gate", "w_branch", "w_out", "g_post")


def _unshard_cols(g):
    return g.transpose(1, 0, 2).reshape(g.shape[1], N_DEV * g.shape[2])


def _shard_cols(full):
    rows, n = full.shape
    return full.reshape(rows, N_DEV, n // N_DEV).transpose(1, 0, 2).astype(BF16)


def kernel(x, mem, positions, g_pre, w_in, a_ln_g, a_ln_b, a_w_s, a_b_s, q_norm_g, w_uq, kv_norm_g, w_ukv, mem_norm_g, w_mem_kv, w_gate, b_gate, w_branch, w_out, g_post, loss_target, m_g_pre, m_w_in, m_a_ln_g, m_a_ln_b, m_a_w_s, m_a_b_s, m_q_norm_g, m_w_uq, m_kv_norm_g, m_w_ukv, m_mem_norm_g, m_w_mem_kv, m_w_gate, m_b_gate, m_w_branch, m_w_out, m_g_post, v_g_pre, v_w_in, v_a_ln_g, v_a_ln_b, v_a_w_s, v_a_b_s, v_q_norm_g, v_w_uq, v_kv_norm_g, v_w_ukv, v_mem_norm_g, v_w_mem_kv, v_w_gate, v_b_gate, v_w_branch, v_w_out, v_g_post):
    weights = dict(g_pre=g_pre, w_in=w_in, a_ln_g=a_ln_g, a_ln_b=a_ln_b, a_w_s=a_w_s, a_b_s=a_b_s, q_norm_g=q_norm_g,
                   w_uq=w_uq, kv_norm_g=kv_norm_g, w_ukv=w_ukv, mem_norm_g=mem_norm_g, w_mem_kv=w_mem_kv,
                   w_gate=w_gate, b_gate=b_gate, w_branch=w_branch, w_out=w_out, g_post=g_post)
    mom1 = dict(g_pre=m_g_pre, w_in=m_w_in, a_ln_g=m_a_ln_g, a_ln_b=m_a_ln_b, a_w_s=m_a_w_s, a_b_s=m_a_b_s,
                q_norm_g=m_q_norm_g, w_uq=m_w_uq, kv_norm_g=m_kv_norm_g, w_ukv=m_w_ukv, mem_norm_g=m_mem_norm_g,
                w_mem_kv=m_w_mem_kv, w_gate=m_w_gate, b_gate=m_b_gate, w_branch=m_w_branch, w_out=m_w_out, g_post=m_g_post)
    mom2 = dict(g_pre=v_g_pre, w_in=v_w_in, a_ln_g=v_a_ln_g, a_ln_b=v_a_ln_b, a_w_s=v_a_w_s, a_b_s=v_a_b_s,
                q_norm_g=v_q_norm_g, w_uq=v_w_uq, kv_norm_g=v_kv_norm_g, w_ukv=v_w_ukv, mem_norm_g=v_mem_norm_g,
                w_mem_kv=v_w_mem_kv, w_gate=v_w_gate, b_gate=v_b_gate, w_branch=v_w_branch, w_out=v_w_out, g_post=v_g_post)
    d = D_MODEL
    t = x.shape[1]
    xs, tgt, mems = x[0], loss_target[0], mem[0]
    pos_col = positions.reshape(t, 1)

    shard16 = {n: weights[n][0].astype(BF16) for n in SHARDED}
    h, g_gate = _pre_norm(xs, g_pre, _Gather([shard16["w_gate"]]))
    w_gate_f = _unshard_cols(g_gate)
    gates, g_in = _mm(h, w_gate_f, name="gates", tm=1024, tn=1024, tk=2048, bias=b_gate, act="sigmoid",
                      carry=_Gather([shard16["w_in"]]))
    w_in_full = _unshard_cols(g_in)
    lat0, lat1 = 3 * d, 3 * d + Q_LORA + KV_LORA + QK_ROPE
    w_big = jnp.concatenate([w_in_full[:, :lat0], w_in_full[:, lat1:]], axis=1)
    w_lat = jnp.concatenate([w_in_full[:, lat0:lat1], jnp.zeros((d, LAT_W - (lat1 - lat0)), BF16)], axis=1)

    inv_freq = 1.0 / (ROPE_THETA ** (jnp.arange(0, QK_ROPE, 2, dtype=F32) / QK_ROPE))
    inv_freq_lanes = jnp.concatenate([inv_freq, inv_freq, jnp.zeros((128 - QK_ROPE,), F32)]).reshape(1, 128)
    ws = a_w_s[0]
    b_exp = jnp.broadcast_to(a_b_s[0][:, :, None], (A_GROUPS, CHUNK, 128))

    proj_big, g_uq, g_ukv, g_mem, g_br, g_out = _mm(
        h, w_big, name="proj_big", tm=1024, tn=1024, tk=2048,
        carry=_Gather([shard16[n] for n in ("w_uq", "w_ukv", "w_mem_kv", "w_branch", "w_out")]))
    w_uq_p = jnp.pad(_unshard_cols(g_uq).reshape(Q_LORA, MLA_HEADS, QK_DIM),
                     ((0, 0), (0, 0), (0, HEAD_PAD - QK_DIM))).reshape(Q_LORA, MLA_HEADS * HEAD_PAD)
    w_ukv_f = _unshard_cols(g_ukv)
    w_mem_f = _unshard_cols(g_mem)
    proj_lat = _mm(h, w_lat, name="proj_lat", tm=1024, tn=LAT_W, tk=2048)
    w_br_f = g_br.transpose(1, 0, 2, 3).reshape(3, d, d)
    w_out_f = g_out.reshape(d, d)
    c_tab, sa_tab, sb_tab = _rope_tables(pos_col, inv_freq_lanes)
    cqn, ckvn, kpe = _latent_norms(proj_lat, q_norm_g, kv_norm_g, c_tab, sa_tab, sb_tab)
    q = _mm(cqn, w_uq_p, name="q_up", tm=1024, tn=1024, tk=512, post=_rope_q_tile, post_rows=(c_tab, sa_tab, sb_tab))
    kv = _mm(ckvn, w_ukv_f, name="kv_up", tm=1024, tn=1024, tk=512)
    o_b, y_b, lse = _mla_fwd(q, kv, kpe, proj_big)
    memn = _mem_norm(mems, mem_norm_g)
    kv_m = _mm(memn, w_mem_f, name="mem_kv", tm=256, tn=1024, tk=2048)
    y_m = _mem_fwd(proj_big, kv_m)
    y_a = _gmlp_fwd(proj_big, a_ln_g, a_ln_b, ws, b_exp)
    ys = (y_a, y_b, y_m)
    ps = [_mm(ys[n], w_br_f[n], name=f"branch{n}", tm=1024, tn=1024, tk=2048) for n in range(3)]
    merged = _merge(gates, *ps)
    out = _mm(merged, w_out_f, name="out_proj", tm=1024, tn=1024, tk=2048, out_dtype=F32)
    d_out, dy, loss_blk, dg_post = _post_loss(out, xs, tgt, g_post)

    dmerged = _mm(d_out, w_out_f, name="d_merged", tb=True, tm=1024, tn=1024, tk=2048)
    dw_out = _mm(merged, d_out, name="dw_out", ta=True, tm=1024, tn=1024, tk=2048)
    dp_a, dp_b, dp_m, dgl, db_gate = _merge_bwd(dmerged, gates, *ps)
    dps = (dp_a, dp_b, dp_m)
    dys = [_mm(dps[n], w_br_f[n], name=f"d_y{n}", tb=True, tm=1024, tn=1024, tk=2048) for n in range(3)]
    dw_br = [_mm(ys[n], dps[n], name=f"dw_branch{n}", ta=True, tm=1024, tn=1024, tk=2048) for n in range(3)]
    h_t = h.T
    dw_gate = _mm(h_t, dgl, name="dw_gate", tm=1024, tn=3 * d // N_DEV, tk=2048, col_shards=True)

    dproj_big, dws, dbs, dlng, dlnb = _gmlp_bwd(proj_big, dys[0], a_ln_g, a_ln_b, ws, b_exp)

    recv = {}
    send = [dw_gate, jnp.stack(dw_br).reshape(3, N_DEV, d // N_DEV, d).transpose(1, 0, 2, 3),
            dw_out.reshape(N_DEV, d // N_DEV, d)]
    do_b, dproj_big, delta = _mla_gate_bwd(dys[1], proj_big, o_b, dproj_big)
    dq, dkv, dkpe, r_gate, r_br, r_out = _mla_bwd(q, kv, kpe, do_b, lse, delta, _AllToAll(send))
    recv["w_gate"], recv["w_branch"], recv["w_out"] = [r_gate], [r_br], [r_out]
    dq_raw, dkr = _rope_q_bwd(dq, dkpe, c_tab, sa_tab, sb_tab)
    dcqn = _mm(dq_raw, w_uq_p, name="d_cq", tb=True, tm=1024, tn=Q_LORA, tk=2048, out_dtype=F32)
    dw_uq_p = _mm(cqn, dq_raw, name="dw_uq", ta=True, tm=Q_LORA, tn=1024, tk=2048)
    dckvn = _mm(dkv, w_ukv_f, name="d_ckv", tb=True, tm=1024, tn=KV_LORA, tk=2048, out_dtype=F32)
    dw_ukv = _mm(ckvn, dkv, name="dw_ukv", ta=True, tm=KV_LORA, tn=2 * d // N_DEV, tk=2048, col_shards=True)
    dproj_lat, dqg, dkg = _latent_norms_bwd(proj_lat, q_norm_g, kv_norm_g, dcqn, dckvn, dkr)

    dproj_big, dkv_m32 = _mem_bwd(proj_big, kv_m, dys[2], dproj_big)
    dkv_m = dkv_m32.astype(BF16)
    dw_mem = _mm(memn, dkv_m, name="dw_mem", ta=True, tm=1024, tn=2 * d // N_DEV, tk=256, col_shards=True)

    dw_uq_full = dw_uq_p.reshape(Q_LORA, MLA_HEADS, HEAD_PAD)[:, :, :QK_DIM].reshape(Q_LORA, MLA_HEADS * QK_DIM)
    dw_big, r_uq, r_ukv, r_mem = _mm(
        h_t, dproj_big, name="dw_big", tm=1024, tn=1024, tk=2048,
        carry=_AllToAll([_shard_cols(dw_uq_full), dw_ukv, dw_mem]))
    recv["w_uq"], recv["w_ukv"], recv["w_mem_kv"] = [r_uq], [r_ukv], [r_mem]
    dw_lat = _mm(h_t, dproj_lat, name="dw_lat", tm=1024, tn=LAT_W, tk=2048)

    def w_in_shards(rows):
        n_cols = w_in.shape[-1]
        pieces = []
        for j in range(N_DEV):
            lo, hi = n_cols * j, n_cols * (j + 1)
            parts = []
            for a, b, src, shift in ((0, lat0, dw_big, 0), (lat0, lat1, dw_lat, lat0), (lat1, N_DEV * n_cols, dw_big, lat1 - lat0)):
                s, e = max(lo, a), min(hi, b)
                if s < e:
                    parts.append(src[rows, s - shift:e - shift])
            pieces.append(parts[0] if len(parts) == 1 else jnp.concatenate(parts, axis=1))
        return jnp.stack(pieces)

    dmemn = _mm(dkv_m, w_mem_f, name="d_memn", tb=True, tm=256, tn=1024, tk=2048, out_dtype=F32)
    dg_mem = _mem_norm_bwd(mems, mem_norm_g, dmemn)
    dh, r_in0 = _mm(dgl, w_gate_f, name="dh_gate", tb=True, tm=1024, tn=1024, tk=2048, out_dtype=F32,
                    carry=_AllToAll([w_in_shards(slice(0, d // 2))]))
    dh = _mm(dproj_lat, w_lat, name="dh_lat", tb=True, tm=1024, tn=1024, tk=LAT_W, out_dtype=F32, add=dh)
    dh, r_in1 = _mm(dproj_big, w_big, name="dh_big", tb=True, tm=1024, tn=1024, tk=2048, out_dtype=F32, add=dh,
                    carry=_AllToAll([w_in_shards(slice(d // 2, d))]))
    recv["w_in"] = [r_in0, r_in1]
    grad_x, dg_pre = _pre_norm_bwd(xs, g_pre, dh, dy)

    results = {}
    for n in SHARDED:
        results[n] = [r[None] for r in _adam_sharded(recv[n], weights[n][0], mom1[n][0], mom2[n][0], "adam_" + n)]

    small = dict(g_pre=dg_pre, a_ln_g=dlng, a_ln_b=dlnb, a_w_s=dws, a_b_s=dbs, q_norm_g=dqg, kv_norm_g=dkg,
                 mem_norm_g=dg_mem, b_gate=db_gate, g_post=dg_post)
    (parts,) = _exchange(_AllToAll([_pack([small[n] for n in REPLICATED])], broadcast=True), "gather_small_grads")
    packed = _adam_replicated(parts, _pack([weights[n] for n in REPLICATED]), _pack([mom1[n] for n in REPLICATED]),
                              _pack([mom2[n] for n in REPLICATED]))
    shapes = [weights[n].shape for n in REPLICATED]
    unpacked = [_unpack(p, shapes) for p in packed]
    for i, n in enumerate(REPLICATED):
        results[n] = [u[i] for u in unpacked]

    loss = lax.psum(loss_blk[0, 0], AXES)
    outs = [loss, grad_x[None]]
    for kind in range(4):
        outs += [results[n][kind] for n in WEIGHT_ORDER]
    return tuple(outs)
```

```python
import functools
import math

import jax
import jax.numpy as jnp
import numpy as np
from jax import lax
from jax.experimental import pallas as pl
from jax.experimental.pallas import tpu as pltpu

F32 = jnp.float32
BF16 = jnp.bfloat16
MESH = pl.DeviceIdType.MESH
AXES = ("x", "y", "c")
N_DEV = 8

D_MODEL = 2048
EPS = 1e-6
CHUNK = 128
A_GROUPS = 16
MLA_HEADS = 16
QK_NOPE = 128
QK_ROPE = 64
QK_DIM = QK_NOPE + QK_ROPE
HEAD_PAD = 256
Q_LORA = 512
KV_LORA = 512
MEM_HEADS = 4
MEM_HEAD_DIM = 512
ROPE_THETA = 10000.0
MLA_SCALE = QK_DIM ** -0.5
MEM_SCALE = MEM_HEAD_DIM ** -0.5
NEG = -1e30
LOG2E = 1.4426950408889634

ADAM_LR = 0.001
ADAM_B1 = 0.9
ADAM_B2 = 0.999
ADAM_EPS = 1e-08
ADAM_WD = 0.01
ADAM_STEP = 10

BIG_W = 6 * D_MODEL
LAT_W = Q_LORA + KV_LORA + 128

VMEM_MIB = 1024 * 1024

ROW_BLK = 256
ATT_BLK = 2048
ATT_BLK_FWD = 2048
ATT_SUB = 256
GMLP_ROWS = 256
MEM_Q_BLK = 512


def _params(vmem_mib, **kw):
    return pltpu.CompilerParams(vmem_limit_bytes=int(vmem_mib * VMEM_MIB), **kw)


def _gelu(x):
    k = math.sqrt(2.0 / math.pi)
    t = jnp.tanh(k * (x + 0.044715 * (x * x * x)))
    return 0.5 * x * (1.0 + t)


def _gelu_and_grad(x):
    k = math.sqrt(2.0 / math.pi)
    x2 = x * x
    t = jnp.tanh(k * (x + 0.044715 * (x2 * x)))
    val = 0.5 * x * (1.0 + t)
    grad = 0.5 * (1.0 + t) + 0.5 * x * (1.0 - t * t) * (k * (1.0 + 3.0 * 0.044715 * x2))
    return val, grad


def _silu_and_grad(z):
    s = jax.nn.sigmoid(z)
    return z * s, s * (1.0 + z * (1.0 - s))


def _mm(a, b, *, name, tm, tn, tk, ta=False, tb=False, out_dtype=BF16, bias=None, act=None, add=None, carry=None,
        post=None, post_rows=(), col_shards=False, early_forward=False):
    m = a.shape[1] if ta else a.shape[0]
    k = a.shape[0] if ta else a.shape[1]
    n = b.shape[0] if tb else b.shape[1]
    assert k == (b.shape[1] if tb else b.shape[0])
    tm, tn, tk = min(tm, m), min(tn, n), min(tk, k)
    assert m % tm == 0 and n % tn == 0 and k % tk == 0, (name, m, n, k, tm, tn, tk)
    nk = k // tk
    a_spec = pl.BlockSpec((tk, tm), lambda i, j, kk: (kk, i)) if ta else pl.BlockSpec((tm, tk), lambda i, j, kk: (i, kk))
    b_spec = pl.BlockSpec((tn, tk), lambda i, j, kk: (j, kk)) if tb else pl.BlockSpec((tk, tn), lambda i, j, kk: (kk, j))
    dn = (((0 if ta else 1,), (1 if tb else 0,)), ((), ()))
    operands, in_specs = [a, b], [a_spec, b_spec]
    if bias is not None:
        operands.append(bias)
        in_specs.append(pl.BlockSpec((1, tn), lambda i, j, kk: (0, j)))
    if add is not None:
        operands.append(add)
        in_specs.append(pl.BlockSpec((tm, tn), lambda i, j, kk: (i, j)))
    n_fixed = len(operands)
    for r in post_rows:
        operands.append(r)
        in_specs.append(pl.BlockSpec((tm, r.shape[1]), lambda i, j, kk: (i, 0)))

    n_in = len(operands)
    n_carry = len(carry.arrays) if carry is not None else 0
    n_acc = 1 if nk > 1 else 0
    grid = (m // tm, n // tn, nk)

    def body(*refs):
        a_ref, b_ref = refs[0], refs[1]
        pos = 2
        bias_ref = add_ref = None
        if bias is not None:
            bias_ref = refs[pos]
            pos += 1
        if add is not None:
            add_ref = refs[pos]
            pos += 1
        o_ref = refs[n_in + n_carry]
        pos = n_in + n_carry
        if carry is not None:
            c_ins = refs[n_in:n_in + n_carry]
            c_outs = refs[n_in + n_carry + 1:n_in + 2 * n_carry + 1]
            c_sems = refs[n_in + 2 * n_carry + 1 + n_acc:]
            ids = [pl.program_id(ax) for ax in range(3)]

            @pl.when((ids[0] == 0) & (ids[1] == 0) & (ids[2] == 0))
            def _():
                carry.start(c_ins, c_outs, c_sems)

        part = lax.dot_general(a_ref[...], b_ref[...], dn, preferred_element_type=F32)

        def finish(acc):
            if bias_ref is not None:
                acc = acc + bias_ref[...]
            if act == "sigmoid":
                acc = jax.nn.sigmoid(acc)
            if add_ref is not None:
                acc = acc + add_ref[...]
            if post is not None:
                acc = post(acc, *[r[...] for r in refs[n_fixed:n_in]])
            o_ref[...] = acc.astype(o_ref.dtype)

        if nk == 1:
            finish(part)
        else:
            acc_ref = refs[n_in + 2 * n_carry + 1]
            kk = pl.program_id(2)

            @pl.when(kk == 0)
            def _():
                acc_ref[...] = part

            @pl.when(kk > 0)
            def _():
                acc_ref[...] += part

            @pl.when(kk == nk - 1)
            def _():
                finish(acc_ref[...])

        if carry is not None:
            total = grid[0] * grid[1] * grid[2]
            early = early_forward and total >= 16
            if early:
                @pl.when((ids[0] * grid[1] + ids[1]) * grid[2] + ids[2] == (4 * total) // 5)
                def _():
                    carry.forward(c_ins, c_outs, c_sems)

            @pl.when((ids[0] == grid[0] - 1) & (ids[1] == grid[1] - 1) & (ids[2] == grid[2] - 1))
            def _():
                if early:
                    carry.finish(c_ins, c_outs, c_sems, forwarded=True)
                else:
                    carry.finish(c_ins, c_outs, c_sems)

    osz = jnp.dtype(out_dtype).itemsize
    est = 2 * 2 * (tm * tk + tk * tn) + 2 * osz * tm * tn + 8 * tm * tn + (2 * 4 * tm * tn if add is not None else 0)
    if col_shards:
        assert n // tn == N_DEV
        main_spec = pl.BlockSpec((None, tm, tn), lambda i, j, kk: (j, i, 0))
        main_shape = jax.ShapeDtypeStruct((N_DEV, m, tn), out_dtype)
    else:
        main_spec = pl.BlockSpec((tm, tn), lambda i, j, kk: (i, j))
        main_shape = jax.ShapeDtypeStruct((m, n), out_dtype)
    scratch = [pltpu.VMEM((tm, tn), F32)] if nk > 1 else []
    if carry is None:
        return pl.pallas_call(
            body, name=name, grid=grid, in_specs=in_specs, out_specs=main_spec, out_shape=main_shape,
            scratch_shapes=scratch, compiler_params=_params(min(56, est / VMEM_MIB + 12)),
        )(*operands)
    return pl.pallas_call(
        body, name=name, grid=grid,
        in_specs=in_specs + [HBM_SPEC] * n_carry,
        out_specs=[main_spec] + [HBM_SPEC] * n_carry,
        out_shape=[main_shape] + carry.out_shapes,
        scratch_shapes=scratch + carry.sem_shapes,
        compiler_params=_params(min(56, est / VMEM_MIB + 12)),
    )(*operands, *carry.arrays)


def _row_spec(tr, cols, col_blk=0):
    return pl.BlockSpec((tr, cols), lambda i: (i, col_blk))


def _full_spec(shape):
    nd = len(shape)
    return pl.BlockSpec(shape, lambda i: (0,) * nd)


def _pre_norm(x, g_pre, carry):
    t, d = x.shape
    tr = min(ROW_BLK, t)
    n_carry = len(carry.arrays)
    steps = t // tr

    def body(x_ref, g_ref, *rest):
        c_ins, h_ref = rest[:n_carry], rest[n_carry]
        c_outs, c_sems = rest[n_carry + 1:2 * n_carry + 1], rest[2 * n_carry + 1:]

        @pl.when(pl.program_id(0) == 0)
        def _():
            carry.start(c_ins, c_outs, c_sems)

        xv = x_ref[...]
        r = lax.rsqrt(jnp.mean(xv * xv, axis=-1, keepdims=True) + EPS)
        h_ref[...] = ((xv * r) * g_ref[...]).astype(BF16)

        @pl.when(pl.program_id(0) == steps - 1)
        def _():
            carry.finish(c_ins, c_outs, c_sems)

    return pl.pallas_call(
        body, name="pre_norm", grid=(steps,),
        in_specs=[_row_spec(tr, d), _full_spec((1, d))] + [HBM_SPEC] * n_carry,
        out_specs=[_row_spec(tr, d)] + [HBM_SPEC] * n_carry,
        out_shape=[jax.ShapeDtypeStruct((t, d), BF16)] + carry.out_shapes,
        scratch_shapes=carry.sem_shapes,
        compiler_params=_params(32),
    )(x, g_pre, *carry.arrays)


def _rope_tables(pos_col, inv_freq_lanes):
    t = pos_col.shape[0]
    tr = min(ROW_BLK, t)

    def body(p_ref, f_ref, c_ref, sa_ref, sb_ref):
        ang = p_ref[...].astype(F32) * f_ref[...]
        lane = lax.broadcasted_iota(jnp.int32, ang.shape, 1)
        cos, sin = jnp.cos(ang), jnp.sin(ang)
        c_ref[...] = jnp.where(lane < QK_ROPE, cos, 0.0)
        sa_ref[...] = jnp.where(lane < QK_ROPE // 2, sin, 0.0)
        sb_ref[...] = jnp.where((lane >= QK_ROPE // 2) & (lane < QK_ROPE), sin, 0.0)

    tab = jax.ShapeDtypeStruct((t, 128), F32)
    return pl.pallas_call(
        body, name="rope_tables", grid=(t // tr,),
        in_specs=[_row_spec(tr, 1), _full_spec((1, 128))],
        out_specs=[_row_spec(tr, 128)] * 3,
        out_shape=[tab, tab, tab],
    )(pos_col, inv_freq_lanes)


def _rope_fwd(p, c, sa, sb):
    return p * c - pltpu.roll(p, 96, 1) * sa + pltpu.roll(p, 32, 1) * sb


def _rope_bwd(g, c, sa, sb):
    return g * c + pltpu.roll(g, 96, 1) * sa - pltpu.roll(g, 32, 1) * sb


def _rms(xv, g):
    r = lax.rsqrt(jnp.mean(xv * xv, axis=-1, keepdims=True) + EPS)
    return (xv * r) * g


def _rms_bwd(xv, g, dout):
    r = lax.rsqrt(jnp.mean(xv * xv, axis=-1, keepdims=True) + EPS)
    xn = xv * r
    dg = jnp.sum(dout * xn, axis=0, keepdims=True)
    dxn = dout * g
    dx = r * (dxn - xn * jnp.mean(dxn * xn, axis=-1, keepdims=True))
    return dx, dg


def _latent_norms(proj_lat, q_norm_g, kv_norm_g, c_tab, sa_tab, sb_tab):
    t = proj_lat.shape[0]
    tr = min(ROW_BLK, t)

    def body(cq_ref, ckv_ref, kr_ref, qg_ref, kg_ref, c_ref, sa_ref, sb_ref, cqn_ref, ckvn_ref, kpe_ref):
        cqn_ref[...] = _rms(cq_ref[...].astype(F32), qg_ref[...]).astype(BF16)
        ckvn_ref[...] = _rms(ckv_ref[...].astype(F32), kg_ref[...]).astype(BF16)
        kpe_ref[...] = _rope_fwd(kr_ref[...].astype(F32), c_ref[...], sa_ref[...], sb_ref[...]).astype(BF16)

    return pl.pallas_call(
        body, name="latent_norms", grid=(t // tr,),
        in_specs=[_row_spec(tr, Q_LORA, 0), _row_spec(tr, KV_LORA, 1), _row_spec(tr, 128, (Q_LORA + KV_LORA) // 128),
                  _full_spec((1, Q_LORA)), _full_spec((1, KV_LORA)),
                  _row_spec(tr, 128), _row_spec(tr, 128), _row_spec(tr, 128)],
        out_specs=[_row_spec(tr, Q_LORA), _row_spec(tr, KV_LORA), _row_spec(tr, 128)],
        out_shape=[jax.ShapeDtypeStruct((t, Q_LORA), BF16), jax.ShapeDtypeStruct((t, KV_LORA), BF16),
                   jax.ShapeDtypeStruct((t, 128), BF16)],
    )(proj_lat, proj_lat, proj_lat, q_norm_g, kv_norm_g, c_tab, sa_tab, sb_tab)


def _rope_q_tile(acc, c, sa, sb):
    qs = MLA_SCALE * LOG2E
    parts = []
    for h in range(acc.shape[1] // HEAD_PAD):
        parts.append(acc[:, h * HEAD_PAD:h * HEAD_PAD + 128] * qs)
        parts.append(_rope_fwd(acc[:, h * HEAD_PAD + 128:(h + 1) * HEAD_PAD], c, sa, sb) * qs)
    return jnp.concatenate(parts, axis=1)


def _rope_q_bwd(dq, dkpe, c_tab, sa_tab, sb_tab):
    t = dq.shape[0]
    tr = min(ROW_BLK, t)

    def body(dq_ref, dkp_ref, c_ref, sa_ref, sb_ref, o_ref, dkr_ref):
        c, sa, sb = c_ref[...], sa_ref[...], sb_ref[...]
        for h in range(MLA_HEADS):
            o_ref[:, h * HEAD_PAD:h * HEAD_PAD + 128] = (dq_ref[:, h * HEAD_PAD:h * HEAD_PAD + 128] * MLA_SCALE).astype(BF16)
            g = dq_ref[:, h * HEAD_PAD + 128:(h + 1) * HEAD_PAD] * MLA_SCALE
            o_ref[:, h * HEAD_PAD + 128:(h + 1) * HEAD_PAD] = _rope_bwd(g, c, sa, sb).astype(BF16)
        dkr_ref[...] = _rope_bwd(dkp_ref[...], c, sa, sb)

    w = MLA_HEADS * HEAD_PAD
    return pl.pallas_call(
        body, name="rope_q_bwd", grid=(t // tr,),
        in_specs=[_row_spec(tr, w), _row_spec(tr, 128),
                  _row_spec(tr, 128), _row_spec(tr, 128), _row_spec(tr, 128)],
        out_specs=[_row_spec(tr, w), _row_spec(tr, 128)],
        out_shape=[jax.ShapeDtypeStruct((t, w), BF16), jax.ShapeDtypeStruct((t, 128), F32)],
        compiler_params=_params(48),
    )(dq, dkpe, c_tab, sa_tab, sb_tab)


def _latent_norms_bwd(proj_lat, q_norm_g, kv_norm_g, dcqn, dckvn, dkr):
    t = proj_lat.shape[0]
    tr = min(ROW_BLK, t)

    def body(cq_ref, ckv_ref, qg_ref, kg_ref, dcqn_ref, dckvn_ref, dkr_ref, dl_ref, dqg_ref, dkg_ref):
        dcq, dqg = _rms_bwd(cq_ref[...].astype(F32), qg_ref[...], dcqn_ref[...])
        dckv, dkg = _rms_bwd(ckv_ref[...].astype(F32), kg_ref[...], dckvn_ref[...])
        dl_ref[:, 0:Q_LORA] = dcq.astype(BF16)
        dl_ref[:, Q_LORA:Q_LORA + KV_LORA] = dckv.astype(BF16)
        dl_ref[:, Q_LORA + KV_LORA:LAT_W] = dkr_ref[...].astype(BF16)

        @pl.when(pl.program_id(0) == 0)
        def _():
            dqg_ref[...] = jnp.zeros_like(dqg_ref)
            dkg_ref[...] = jnp.zeros_like(dkg_ref)

        dqg_ref[...] += dqg
        dkg_ref[...] += dkg

    return pl.pallas_call(
        body, name="latent_norms_bwd", grid=(t // tr,),
        in_specs=[_row_spec(tr, Q_LORA, 0), _row_spec(tr, KV_LORA, 1), _full_spec((1, Q_LORA)), _full_spec((1, KV_LORA)),
                  _row_spec(tr, Q_LORA), _row_spec(tr, KV_LORA), _row_spec(tr, 128)],
        out_specs=[_row_spec(tr, LAT_W), _full_spec((1, Q_LORA)), _full_spec((1, KV_LORA))],
        out_shape=[jax.ShapeDtypeStruct((t, LAT_W), BF16), jax.ShapeDtypeStruct((1, Q_LORA), F32),
                   jax.ShapeDtypeStruct((1, KV_LORA), F32)],
    )(proj_lat, proj_lat, q_norm_g, kv_norm_g, dcqn, dckvn, dkr)


def _mem_norm(mem, g):
    m, d = mem.shape

    def body(x_ref, g_ref, o_ref):
        o_ref[...] = _rms(x_ref[...], g_ref[...]).astype(BF16)

    return pl.pallas_call(
        body, name="mem_norm", grid=(1,),
        in_specs=[_full_spec((m, d)), _full_spec((1, d))],
        out_specs=_full_spec((m, d)),
        out_shape=jax.ShapeDtypeStruct((m, d), BF16),
    )(mem, g)


def _mem_norm_bwd(mem, g, dmemn):
    m, d = mem.shape

    def body(x_ref, g_ref, d_ref, dg_ref):
        _, dg = _rms_bwd(x_ref[...], g_ref[...], d_ref[...])
        dg_ref[...] = dg

    return pl.pallas_call(
        body, name="mem_norm_bwd", grid=(1,),
        in_specs=[_full_spec((m, d)), _full_spec((1, d)), _full_spec((m, d))],
        out_specs=_full_spec((1, d)),
        out_shape=jax.ShapeDtypeStruct((1, d), F32),
    )(mem, g, dmemn)


def _merge(gates, p_a, p_b, p_m):
    t, d = p_a.shape
    tr = min(ROW_BLK, t)

    def body(ga_ref, gb_ref, gm_ref, pa_ref, pb_ref, pm_ref, o_ref):
        acc = ga_ref[...].astype(F32) * pa_ref[...].astype(F32)
        acc = acc + gb_ref[...].astype(F32) * pb_ref[...].astype(F32)
        acc = acc + gm_ref[...].astype(F32) * pm_ref[...].astype(F32)
        o_ref[...] = acc.astype(BF16)

    return pl.pallas_call(
        body, name="merge", grid=(t // tr,),
        in_specs=[_row_spec(tr, d, 0), _row_spec(tr, d, 1), _row_spec(tr, d, 2),
                  _row_spec(tr, d), _row_spec(tr, d), _row_spec(tr, d)],
        out_specs=_row_spec(tr, d),
        out_shape=jax.ShapeDtypeStruct((t, d), BF16),
        compiler_params=_params(48),
    )(gates, gates, gates, p_a, p_b, p_m)


def _merge_bwd(dm, gates, p_a, p_b, p_m):
    t, d = dm.shape
    tr = min(ROW_BLK, t)

    def body(dm_ref, g_ref, pa_ref, pb_ref, pm_ref, dpa_ref, dpb_ref, dpm_ref, dgl_ref, db_ref):
        dmv = dm_ref[...].astype(F32)

        @pl.when(pl.program_id(0) == 0)
        def _():
            db_ref[...] = jnp.zeros_like(db_ref)

        for n, (p_ref, dp_ref) in enumerate(((pa_ref, dpa_ref), (pb_ref, dpb_ref), (pm_ref, dpm_ref))):
            g = g_ref[:, n * d:(n + 1) * d].astype(F32)
            dp_ref[...] = (dmv * g).astype(BF16)
            dgl = dmv * p_ref[...].astype(F32) * (g * (1.0 - g))
            dgl_ref[:, n * d:(n + 1) * d] = dgl.astype(BF16)
            db_ref[:, n * d:(n + 1) * d] += jnp.sum(dgl, axis=0, keepdims=True)

    act = jax.ShapeDtypeStruct((t, d), BF16)
    return pl.pallas_call(
        body, name="merge_bwd", grid=(t // tr,),
        in_specs=[_row_spec(tr, d), _row_spec(tr, 3 * d), _row_spec(tr, d), _row_spec(tr, d), _row_spec(tr, d)],
        out_specs=[_row_spec(tr, d), _row_spec(tr, d), _row_spec(tr, d), _row_spec(tr, 3 * d), _full_spec((1, 3 * d))],
        out_shape=[act, act, act, jax.ShapeDtypeStruct((t, 3 * d), BF16), jax.ShapeDtypeStruct((1, 3 * d), F32)],
        compiler_params=_params(56),
    )(dm, gates, p_a, p_b, p_m)


def _post_loss(out, x, tgt, g_post):
    t, d = out.shape
    tr = min(ROW_BLK, t)

    def body(o_ref, x_ref, t_ref, g_ref, do_ref, dy_ref, loss_ref, dg_ref):
        ov = o_ref[...]
        g = g_ref[...]
        r = lax.rsqrt(jnp.mean(ov * ov, axis=-1, keepdims=True) + EPS)
        on = ov * r
        err = (x_ref[...] + on * g) - t_ref[...]
        dy = err * (1.0 / d)
        dy_ref[...] = dy
        don = dy * g
        do_ref[...] = (r * (don - on * jnp.mean(don * on, axis=-1, keepdims=True))).astype(BF16)

        @pl.when(pl.program_id(0) == 0)
        def _():
            loss_ref[...] = jnp.zeros_like(loss_ref)
            dg_ref[...] = jnp.zeros_like(dg_ref)

        loss_ref[...] += 0.5 * jnp.sum(jnp.mean(err * err, axis=-1, keepdims=True))
        dg_ref[...] += jnp.sum(dy * on, axis=0, keepdims=True)

    return pl.pallas_call(
        body, name="post_loss", grid=(t // tr,),
        in_specs=[_row_spec(tr, d), _row_spec(tr, d), _row_spec(tr, d), _full_spec((1, d))],
        out_specs=[_row_spec(tr, d), _row_spec(tr, d), _full_spec((8, 128)), _full_spec((1, d))],
        out_shape=[jax.ShapeDtypeStruct((t, d), BF16), jax.ShapeDtypeStruct((t, d), F32),
                   jax.ShapeDtypeStruct((8, 128), F32), jax.ShapeDtypeStruct((1, d), F32)],
        compiler_params=_params(56),
    )(out, x, tgt, g_post)


def _pre_norm_bwd(x, g_pre, dh, dy):
    t, d = x.shape
    tr = min(ROW_BLK, t)

    def body(x_ref, g_ref, dh_ref, dy_ref, dx_ref, dg_ref):
        dx, dg = _rms_bwd(x_ref[...], g_ref[...], dh_ref[...])
        dx_ref[...] = dx + dy_ref[...]

        @pl.when(pl.program_id(0) == 0)
        def _():
            dg_ref[...] = jnp.zeros_like(dg_ref)

        dg_ref[...] += dg

    return pl.pallas_call(
        body, name="pre_norm_bwd", grid=(t // tr,),
        in_specs=[_row_spec(tr, d), _full_spec((1, d)), _row_spec(tr, d), _row_spec(tr, d)],
        out_specs=[_row_spec(tr, d), _full_spec((1, d))],
        out_shape=[jax.ShapeDtypeStruct((t, d), F32), jax.ShapeDtypeStruct((1, d), F32)],
        compiler_params=_params(56),
    )(x, g_pre, dh, dy)


def _causal_mask(n):
    row = lax.broadcasted_iota(jnp.int32, (n, n), 0)
    col = lax.broadcasted_iota(jnp.int32, (n, n), 1)
    return row >= col


def _layernorm_stats(vg):
    mu = jnp.mean(vg, axis=-1, keepdims=True)
    cen = vg - mu
    rstd = lax.rsqrt(jnp.mean(cen * cen, axis=-1, keepdims=True) + EPS)
    return cen * rstd, rstd


def _gmlp_fwd(proj_big, ln_g, ln_b, w_s, b_exp):
    t = proj_big.shape[0]
    rows = min(GMLP_ROWS, t)
    d = D_MODEL

    def body(u_ref, v_ref, z_ref, lg_ref, lb_ref, ws_ref, be_ref, y_ref, vn_scr):
        vhat, _ = _layernorm_stats(_gelu(v_ref[...].astype(F32)))
        vn_scr[...] = (vhat * lg_ref[...] + lb_ref[...]).astype(BF16)
        mask = _causal_mask(CHUNK)
        for g in range(A_GROUPS):
            cols = slice(g * 128, (g + 1) * 128)
            wsm = jnp.where(mask, ws_ref[g], 0.0).astype(BF16)
            for c in range(rows // CHUNK):
                rws = slice(c * CHUNK, (c + 1) * CHUNK)
                sv = jnp.dot(wsm, vn_scr[rws, cols], preferred_element_type=F32) + be_ref[g]
                zs, _ = _silu_and_grad(z_ref[rws, cols].astype(F32))
                y_ref[rws, cols] = (_gelu(u_ref[rws, cols].astype(F32)) * sv * zs).astype(BF16)

    return pl.pallas_call(
        body, name="gmlp_fwd", grid=(t // rows,),
        in_specs=[_row_spec(rows, d, 0), _row_spec(rows, d, 1), _row_spec(rows, d, 2),
                  _full_spec((1, d)), _full_spec((1, d)), _full_spec((A_GROUPS, CHUNK, CHUNK)),
                  _full_spec((A_GROUPS, CHUNK, 128))],
        out_specs=_row_spec(rows, d),
        out_shape=jax.ShapeDtypeStruct((t, d), BF16),
        scratch_shapes=[pltpu.VMEM((rows, d), BF16)],
        compiler_params=_params(40),
    )(proj_big, proj_big, proj_big, ln_g, ln_b, w_s, b_exp)


def _gmlp_bwd(proj_big, dya, ln_g, ln_b, w_s, b_exp):
    t = proj_big.shape[0]
    rows = min(GMLP_ROWS, t)
    d = D_MODEL
    nt = (((1,), (1,)), ((), ()))
    tn = (((0,), (0,)), ((), ()))

    def body(u_ref, v_ref, z_ref, dy_ref, lg_ref, lb_ref, ws_ref, be_ref,
             dp_ref, dws_ref, dbs_ref, dlg_ref, dlb_ref, vn_scr, dvn_scr):
        @pl.when(pl.program_id(0) == 0)
        def _():
            dws_ref[...] = jnp.zeros_like(dws_ref)
            dbs_ref[...] = jnp.zeros_like(dbs_ref)
            dlg_ref[...] = jnp.zeros_like(dlg_ref)
            dlb_ref[...] = jnp.zeros_like(dlb_ref)

        vg, vgrad = _gelu_and_grad(v_ref[...].astype(F32))
        vhat, rstd = _layernorm_stats(vg)
        vn_scr[...] = (vhat * lg_ref[...] + lb_ref[...]).astype(BF16)
        mask = _causal_mask(CHUNK)
        for g in range(A_GROUPS):
            cols = slice(g * 128, (g + 1) * 128)
            wsm = jnp.where(mask, ws_ref[g], 0.0).astype(BF16)
            dws = jnp.zeros((CHUNK, CHUNK), F32)
            dbs = jnp.zeros((CHUNK, 1), F32)
            for c in range(rows // CHUNK):
                rws = slice(c * CHUNK, (c + 1) * CHUNK)
                vn = vn_scr[rws, cols]
                sv = jnp.dot(wsm, vn, preferred_element_type=F32) + be_ref[g]
                ug, ugrad = _gelu_and_grad(u_ref[rws, cols].astype(F32))
                zs, zgrad = _silu_and_grad(z_ref[rws, cols].astype(F32))
                dya = dy_ref[rws, cols].astype(F32)
                dga = dya * zs
                dp_ref[rws, 2 * d + g * 128:2 * d + (g + 1) * 128] = (dya * (ug * sv) * zgrad).astype(BF16)
                dp_ref[rws, cols] = (dga * sv * ugrad).astype(BF16)
                dsv = dga * ug
                dsv16 = dsv.astype(BF16)
                dws = dws + lax.dot_general(dsv16, vn, nt, preferred_element_type=F32)
                dbs = dbs + jnp.sum(dsv, axis=-1, keepdims=True)
                dvn_scr[rws, cols] = lax.dot_general(wsm, dsv16, tn, preferred_element_type=F32)
            dws_ref[g] += jnp.where(mask, dws, 0.0)
            dbs_ref[g] += dbs
        dvn = dvn_scr[...]
        dlg_ref[...] += jnp.sum(dvn * vhat, axis=0, keepdims=True)
        dlb_ref[...] += jnp.sum(dvn, axis=0, keepdims=True)
        dvh = dvn * lg_ref[...]
        dvg = rstd * (dvh - jnp.mean(dvh, axis=-1, keepdims=True) - vhat * jnp.mean(dvh * vhat, axis=-1, keepdims=True))
        dp_ref[:, d:2 * d] = (dvg * vgrad).astype(BF16)

    return pl.pallas_call(
        body, name="gmlp_bwd", grid=(t // rows,),
        in_specs=[_row_spec(rows, d, 0), _row_spec(rows, d, 1), _row_spec(rows, d, 2), _row_spec(rows, d),
                  _full_spec((1, d)), _full_spec((1, d)), _full_spec((A_GROUPS, CHUNK, CHUNK)),
                  _full_spec((A_GROUPS, CHUNK, 128))],
        out_specs=[_row_spec(rows, 3 * d), _full_spec((A_GROUPS, CHUNK, CHUNK)), _full_spec((A_GROUPS, CHUNK, 1)),
                   _full_spec((1, d)), _full_spec((1, d))],
        out_shape=[jax.ShapeDtypeStruct((t, 6 * d), BF16), jax.ShapeDtypeStruct((A_GROUPS, CHUNK, CHUNK), F32),
                   jax.ShapeDtypeStruct((A_GROUPS, CHUNK, 1), F32), jax.ShapeDtypeStruct((1, d), F32),
                   jax.ShapeDtypeStruct((1, d), F32)],
        scratch_shapes=[pltpu.VMEM((rows, d), BF16), pltpu.VMEM((rows, d), F32)],
        compiler_params=_params(48),
    )(proj_big, proj_big, proj_big, dya, ln_g, ln_b, w_s, b_exp)


NT_DIMS = (((1,), (1,)), ((), ()))
TN_DIMS = (((0,), (0,)), ((), ()))


def _mla_fwd(q, kv, kpe, proj_big):
    t = q.shape[0]
    blk = min(ATT_BLK_FWD, t)
    nq = t // blk
    zb_blk0 = (3 * D_MODEL) // 128


    sub = min(ATT_SUB, blk)

    def body(q_ref, kv_ref, kp_ref, zb_ref, o_ref, yb_ref, lse_ref, m_scr, acc_scr):
        i = pl.program_id(1)
        qv = q_ref[...]
        m_scr[...] = jnp.full_like(m_scr, NEG)
        acc_scr[...] = jnp.zeros_like(acc_scr)
        ones = jnp.ones((blk, 128), BF16)

        def step(j, masked):
            ks = pl.ds(pl.multiple_of(j * blk, blk), blk)
            kc = jnp.concatenate([kv_ref[ks, 0:128], kp_ref[ks, :]], axis=1)
            vext = jnp.concatenate([kv_ref[ks, 128:256], ones], axis=1)
            for r in range(blk // sub):
                rows = slice(r * sub, (r + 1) * sub)
                kw = (r + 1) * sub if masked else blk
                tt = lax.dot_general(qv[rows], kc[:kw], NT_DIMS, preferred_element_type=F32)
                if masked:
                    row = lax.broadcasted_iota(jnp.int32, (sub, kw), 0) + r * sub
                    col = lax.broadcasted_iota(jnp.int32, (sub, kw), 1)
                    tt = jnp.where(row >= col, tt, NEG)
                cm = tt[:, 0:128]
                for c in range(1, kw // 128):
                    cm = jnp.maximum(cm, tt[:, c * 128:(c + 1) * 128])
                m_prev = m_scr[rows, :]
                m_new = jnp.maximum(m_prev, jnp.max(cm, axis=-1, keepdims=True))
                alpha = jnp.exp2(m_prev - m_new)
                m_scr[rows, :] = m_new
                p = jnp.concatenate([jnp.exp2(tt[:, c * 128:(c + 1) * 128] - m_new).astype(BF16)
                                     for c in range(kw // 128)], axis=1)
                pv = jnp.dot(p, vext[:kw], preferred_element_type=F32)
                acc_scr[rows, :] = jnp.concatenate([alpha, alpha], axis=1) * acc_scr[rows, :] + pv

        def loop_body(j, carry):
            step(j, False)
            return carry

        lax.fori_loop(0, i, loop_body, 0)
        step(i, True)
        l = acc_scr[:, 128:256]
        o = acc_scr[:, 0:128] / l
        o_ref[...] = o.astype(BF16)
        zs, _ = _silu_and_grad(zb_ref[...].astype(F32))
        yb_ref[...] = (o * zs).astype(BF16)
        lse_ref[0] = m_scr[...] + jnp.log2(l)

    act = jax.ShapeDtypeStruct((t, D_MODEL), BF16)
    return pl.pallas_call(
        body, name="mla_fwd", grid=(MLA_HEADS, nq),
        in_specs=[pl.BlockSpec((blk, HEAD_PAD), lambda h, i: (i, h)),
                  pl.BlockSpec((t, HEAD_PAD), lambda h, i: (0, h)),
                  pl.BlockSpec((t, 128), lambda h, i: (0, 0)),
                  pl.BlockSpec((blk, 128), lambda h, i: (i, zb_blk0 + h))],
        out_specs=[pl.BlockSpec((blk, 128), lambda h, i: (i, h)),
                   pl.BlockSpec((blk, 128), lambda h, i: (i, h)),
                   pl.BlockSpec((1, blk, 128), lambda h, i: (h, i, 0))],
        out_shape=[act, act, jax.ShapeDtypeStruct((MLA_HEADS, t, 128), F32)],
        scratch_shapes=[pltpu.VMEM((blk, 128), F32), pltpu.VMEM((blk, HEAD_PAD), F32)],
        compiler_params=_params(56),
    )(q, kv, kpe, proj_big)


def _mla_gate_bwd(dyb, proj_big, o, dproj):
    t, d = dyb.shape
    tr = min(ROW_BLK, t)

    def body(dy_ref, zb_ref, o_ref, buf_ref, do_ref, dz_ref, dl_ref):
        del buf_ref
        dy = dy_ref[...].astype(F32)
        ov = o_ref[...].astype(F32)
        zs, zgrad = _silu_and_grad(zb_ref[...].astype(F32))
        do16 = (dy * zs).astype(BF16)
        do_ref[...] = do16
        dz_ref[...] = (dy * ov * zgrad).astype(BF16)
        prod = do16.astype(F32) * ov
        for h in range(MLA_HEADS):
            delta = jnp.sum(prod[:, h * 128:(h + 1) * 128], axis=-1, keepdims=True)
            dl_ref[h] = jnp.broadcast_to(delta, (tr, 128))

    act = jax.ShapeDtypeStruct((t, d), BF16)
    head_spec = pl.BlockSpec((MLA_HEADS, tr, 128), lambda i: (0, i, 0))
    return pl.pallas_call(
        body, name="mla_gate_bwd", grid=(t // tr,),
        in_specs=[_row_spec(tr, d), _row_spec(tr, d, 3), _row_spec(tr, d), HBM_SPEC],
        out_specs=[_row_spec(tr, d), _row_spec(tr, d, 3), head_spec],
        out_shape=[act, jax.ShapeDtypeStruct((t, 6 * d), BF16), jax.ShapeDtypeStruct((MLA_HEADS, t, 128), F32)],
        input_output_aliases={3: 1},
        compiler_params=_params(56),
    )(dyb, proj_big, o, dproj)


def _mla_bwd(q, kv, kpe, do, lse, delta, carry):
    t = q.shape[0]
    blk = min(ATT_BLK, t)
    n = t // blk
    nc = blk // 128
    pairs = [(j, i) for j in range(n) for i in range(j, n)]
    j_tab = jnp.asarray([p[0] for p in pairs], jnp.int32)
    i_tab = jnp.asarray([p[1] for p in pairs], jnp.int32)
    n_carry = len(carry.arrays)
    sub = min(ATT_SUB, blk)

    def body(j_ref, i_ref, q_ref, do_ref, lse_ref, dl_ref, kv_ref, kp_ref, *rest):
        c_ins, rest = rest[:n_carry], rest[n_carry:]
        dq_ref, dkv_ref, dkp_ref = rest[:3]
        c_outs, rest = rest[3:3 + n_carry], rest[3 + n_carry:]
        dk_scr, dv_scr = rest[:2]
        c_sems = rest[2:]
        head = pl.program_id(0)
        step = pl.program_id(1)
        j, i = j_ref[step], i_ref[step]

        @pl.when((head == 0) & (step == 0))
        def _():
            carry.start(c_ins, c_outs, c_sems)

        @pl.when(step == 0)
        def _():
            dq_ref[...] = jnp.zeros_like(dq_ref)

        @pl.when(i == j)
        def _():
            dk_scr[...] = jnp.zeros_like(dk_scr)
            dv_scr[...] = jnp.zeros_like(dv_scr)

        kc = jnp.concatenate([kv_ref[:, 0:128], kp_ref[...]], axis=1)
        vv = kv_ref[:, 128:256]

        def tile(diag):
            for r in range(blk // sub):
                rows = slice(r * sub, (r + 1) * sub)
                kw = (r + 1) * sub if diag else blk
                qv, dov = q_ref[rows, :], do_ref[rows, :]
                tt = lax.dot_general(qv, kc[:kw], NT_DIMS, preferred_element_type=F32)
                if diag:
                    row = lax.broadcasted_iota(jnp.int32, (sub, kw), 0) + r * sub
                    col = lax.broadcasted_iota(jnp.int32, (sub, kw), 1)
                    tt = jnp.where(row >= col, tt, NEG)
                dp = lax.dot_general(dov, vv[:kw], NT_DIMS, preferred_element_type=F32)
                lse_v, dl_v = lse_ref[0, rows, :], dl_ref[0, rows, :]
                ps, dss = [], []
                for c in range(kw // 128):
                    cols = slice(c * 128, (c + 1) * 128)
                    p = jnp.exp2(tt[:, cols] - lse_v)
                    ps.append(p.astype(BF16))
                    dss.append((p * (dp[:, cols] - dl_v)).astype(BF16))
                p16 = jnp.concatenate(ps, axis=1)
                ds16 = jnp.concatenate(dss, axis=1)
                dv_scr[0:kw, :] += lax.dot_general(p16, dov, TN_DIMS, preferred_element_type=F32)
                dk_scr[0:kw, :] += lax.dot_general(ds16, qv, TN_DIMS, preferred_element_type=F32)
                qs = pl.ds(pl.multiple_of(i * blk + r * sub, sub), sub)
                dq_ref[qs, :] += jnp.dot(ds16, kc[:kw], preferred_element_type=F32)

        @pl.when(i == j)
        def _():
            tile(True)

        @pl.when(i > j)
        def _():
            tile(False)

        @pl.when(i == n - 1)
        def _():
            dkv_ref[:, 0:128] = (dk_scr[:, 0:128] * (1.0 / LOG2E)).astype(BF16)
            dkv_ref[:, 128:256] = dv_scr[...].astype(BF16)
            ks = pl.ds(pl.multiple_of(j * blk, blk), blk)
            dkp = dk_scr[:, 128:256] * (1.0 / LOG2E)

            @pl.when(head == 0)
            def _():
                dkp_ref[ks, :] = dkp

            @pl.when(head > 0)
            def _():
                dkp_ref[ks, :] += dkp

        @pl.when((head == MLA_HEADS - 1) & (step == len(pairs) - 1))
        def _():
            carry.finish(c_ins, c_outs, c_sems)

    grid_spec = pltpu.PrefetchScalarGridSpec(
        num_scalar_prefetch=2, grid=(MLA_HEADS, len(pairs)),
        in_specs=[pl.BlockSpec((blk, HEAD_PAD), lambda h, s, jt, it: (it[s], h)),
                  pl.BlockSpec((blk, 128), lambda h, s, jt, it: (it[s], h)),
                  pl.BlockSpec((1, blk, 128), lambda h, s, jt, it: (h, it[s], 0)),
                  pl.BlockSpec((1, blk, 128), lambda h, s, jt, it: (h, it[s], 0)),
                  pl.BlockSpec((blk, HEAD_PAD), lambda h, s, jt, it: (jt[s], h)),
                  pl.BlockSpec((blk, 128), lambda h, s, jt, it: (jt[s], 0))] + [HBM_SPEC] * n_carry,
        out_specs=[pl.BlockSpec((t, HEAD_PAD), lambda h, s, jt, it: (0, h)),
                   pl.BlockSpec((blk, HEAD_PAD), lambda h, s, jt, it: (jt[s], h)),
                   pl.BlockSpec((t, 128), lambda h, s, jt, it: (0, 0))] + [HBM_SPEC] * n_carry,
        scratch_shapes=[pltpu.VMEM((blk, HEAD_PAD), F32), pltpu.VMEM((blk, 128), F32)] + carry.sem_shapes,
    )
    return pl.pallas_call(
        body, name="mla_bwd", grid_spec=grid_spec,
        out_shape=[jax.ShapeDtypeStruct((t, MLA_HEADS * HEAD_PAD), F32),
                   jax.ShapeDtypeStruct((t, 2 * D_MODEL), BF16),
                   jax.ShapeDtypeStruct((t, 128), F32)] + carry.out_shapes,
        compiler_params=_params(58),
    )(j_tab, i_tab, q, do, lse, delta, kv, kpe, *carry.arrays)


def _mem_attn_probs(qv, k_ref):
    s = lax.dot_general(qv, k_ref[...], NT_DIMS, preferred_element_type=F32) * MEM_SCALE
    e = jnp.exp(s - jnp.max(s, axis=-1, keepdims=True))
    return e / jnp.sum(e, axis=-1, keepdims=True)


def _mem_fwd(proj_big, kv_m):
    t = proj_big.shape[0]
    tq = min(MEM_Q_BLK, t)
    hd = MEM_HEAD_DIM
    d = D_MODEL
    mlen = kv_m.shape[0]

    def body(q_ref, z_ref, kv_ref, y_ref):
        for h in range(MEM_HEADS):
            cols = slice(h * hd, (h + 1) * hd)
            p = _mem_attn_probs(q_ref[:, cols], kv_ref.at[:, cols])
            o = jnp.dot(p.astype(BF16), kv_ref[:, d + h * hd:d + (h + 1) * hd], preferred_element_type=F32)
            zs, _ = _silu_and_grad(z_ref[:, cols].astype(F32))
            y_ref[:, cols] = (o * zs).astype(BF16)

    return pl.pallas_call(
        body, name="mem_fwd", grid=(t // tq,),
        in_specs=[_row_spec(tq, d, 4), _row_spec(tq, d, 5), _full_spec((mlen, 2 * d))],
        out_specs=_row_spec(tq, d),
        out_shape=jax.ShapeDtypeStruct((t, d), BF16),
        compiler_params=_params(40),
    )(proj_big, proj_big, kv_m)


def _mem_bwd(proj_big, kv_m, dym, dproj):
    t = proj_big.shape[0]
    tq = min(MEM_Q_BLK, t)
    hd = MEM_HEAD_DIM
    d = D_MODEL
    mlen = kv_m.shape[0]

    def body(q_ref, z_ref, kv_ref, dy_ref, buf_ref, dqz_ref, dkv_ref):
        del buf_ref

        @pl.when(pl.program_id(0) == 0)
        def _():
            dkv_ref[...] = jnp.zeros_like(dkv_ref)

        for h in range(MEM_HEADS):
            cols = slice(h * hd, (h + 1) * hd)
            k_ref, v_ref = kv_ref.at[:, cols], kv_ref.at[:, d + h * hd:d + (h + 1) * hd]
            qv = q_ref[:, cols]
            p = _mem_attn_probs(qv, k_ref)
            p16 = p.astype(BF16)
            o = jnp.dot(p16, v_ref[...], preferred_element_type=F32)
            zs, zgrad = _silu_and_grad(z_ref[:, cols].astype(F32))
            dy = dy_ref[:, cols].astype(F32)
            dqz_ref[:, d + h * hd:d + (h + 1) * hd] = (dy * o * zgrad).astype(BF16)
            do16 = (dy * zs).astype(BF16)
            dkv_ref[:, d + h * hd:d + (h + 1) * hd] += lax.dot_general(p16, do16, TN_DIMS, preferred_element_type=F32)
            dp = lax.dot_general(do16, v_ref[...], NT_DIMS, preferred_element_type=F32)
            ds = (p * (dp - jnp.sum(dp * p, axis=-1, keepdims=True)) * MEM_SCALE).astype(BF16)
            dqz_ref[:, cols] = jnp.dot(ds, k_ref[...], preferred_element_type=F32).astype(BF16)
            dkv_ref[:, cols] += lax.dot_general(ds, qv, TN_DIMS, preferred_element_type=F32)

    return pl.pallas_call(
        body, name="mem_bwd", grid=(t // tq,),
        in_specs=[_row_spec(tq, d, 4), _row_spec(tq, d, 5), _full_spec((mlen, 2 * d)), _row_spec(tq, d), HBM_SPEC],
        out_specs=[_row_spec(tq, 2 * d, 2), _full_spec((mlen, 2 * d))],
        out_shape=[jax.ShapeDtypeStruct((t, 6 * d), BF16), jax.ShapeDtypeStruct((mlen, 2 * d), F32)],
        input_output_aliases={4: 0},
        compiler_params=_params(56),
    )(proj_big, proj_big, kv_m, dym, dproj)


HBM_SPEC = pl.BlockSpec(memory_space=pl.ANY)
N_PEERS = N_DEV - 1


def _dev_index(px, py, pc):
    return 4 * px + 2 * py + pc


class _Gather:
    def __init__(self, arrays):
        self.arrays = list(arrays)
        n = len(self.arrays)
        self.out_shapes = [jax.ShapeDtypeStruct((N_DEV,) + a.shape, a.dtype) for a in self.arrays]
        self.sem_shapes = [pltpu.SemaphoreType.DMA((n * N_PEERS,)), pltpu.SemaphoreType.DMA((n * N_PEERS,)),
                           pltpu.SemaphoreType.DMA((n,))]

    def _parts(self, ins, outs, sems):
        n = len(self.arrays)
        send_sems, recv_sems, local_sems = sems
        x, y, c = lax.axis_index("x"), lax.axis_index("y"), lax.axis_index("c")
        me, sibling = (x, y, c), (x, y, 1 - c)
        chips = [(1 - x, y), (x, 1 - y), (1 - x, 1 - y)]

        def copy(a, k, block, to, src=None):
            dst = outs[a].at[_dev_index(*block)]
            return pltpu.make_async_remote_copy(
                src_ref=dst if src is None else src, dst_ref=dst,
                send_sem=send_sems.at[a * N_PEERS + k], recv_sem=recv_sems.at[a * N_PEERS + k],
                device_id=to, device_id_type=MESH)

        mine = [pltpu.make_async_copy(ins[a], outs[a].at[_dev_index(*me)], local_sems.at[a]) for a in range(n)]
        first = []
        for a in range(n):
            first.append(copy(a, 0, me, sibling, src=ins[a]))
            first += [copy(a, 1 + j, me, (*chip, c), src=ins[a]) for j, chip in enumerate(chips)]
        return n, c, me, sibling, chips, copy, mine, first

    def start(self, ins, outs, sems):
        _, _, _, _, _, _, mine, first = self._parts(ins, outs, sems)
        for cp in mine + first:
            cp.start()

    def forward(self, ins, outs, sems):
        n, c, me, sibling, chips, copy, _, _ = self._parts(ins, outs, sems)
        for j, chip in enumerate(chips):
            for a in range(n):
                copy(a, 1 + j, (*chip, c), me).wait_recv()
                copy(a, 4 + j, (*chip, c), sibling).start()

    def finish(self, ins, outs, sems, forwarded=False):
        if not forwarded:
            self.forward(ins, outs, sems)
        n, c, me, sibling, chips, copy, mine, first = self._parts(ins, outs, sems)
        passed = [copy(a, 4 + j, (*chip, c), sibling) for j, chip in enumerate(chips) for a in range(n)]
        for a in range(n):
            copy(a, 0, sibling, me).wait_recv()
            for j, chip in enumerate(chips):
                copy(a, 4 + j, (*chip, 1 - c), me).wait_recv()
        for cp in first + passed:
            cp.wait_send()
        for cp in mine:
            cp.wait()


class _AllToAll:
    def __init__(self, arrays):
        self.arrays = list(arrays)
        n = len(self.arrays)
        self.out_shapes = [jax.ShapeDtypeStruct(a.shape, a.dtype) for a in self.arrays]
        self.sem_shapes = [pltpu.SemaphoreType.DMA((n * N_PEERS,)), pltpu.SemaphoreType.DMA((n * N_PEERS,)),
                           pltpu.SemaphoreType.DMA((n,))]

    def _parts(self, ins, outs, sems):
        n = len(self.arrays)
        send_sems, recv_sems, local_sems = sems
        x, y, c = lax.axis_index("x"), lax.axis_index("y"), lax.axis_index("c")
        my_idx = _dev_index(x, y, c)
        peers = []
        for k in range(1, N_DEV):
            dx, dy, dc = (k >> 2) & 1, (k >> 1) & 1, k & 1
            peers.append((1 - x if dx else x, 1 - y if dy else y, 1 - c if dc else c))

        def copy(a, k, peer):
            return pltpu.make_async_remote_copy(
                src_ref=ins[a].at[_dev_index(*peer)], dst_ref=outs[a].at[my_idx],
                send_sem=send_sems.at[a * N_PEERS + k], recv_sem=recv_sems.at[a * N_PEERS + k],
                device_id=peer, device_id_type=MESH)

        def landed(a, k, peer):
            slot = outs[a].at[_dev_index(*peer)]
            return pltpu.make_async_remote_copy(
                src_ref=slot, dst_ref=slot,
                send_sem=send_sems.at[a * N_PEERS + k], recv_sem=recv_sems.at[a * N_PEERS + k],
                device_id=peer, device_id_type=MESH)

        mine = [pltpu.make_async_copy(ins[a].at[my_idx], outs[a].at[my_idx], local_sems.at[a]) for a in range(n)]
        sends = [copy(a, k, peer) for a in range(n) for k, peer in enumerate(peers)]
        return n, peers, landed, mine, sends

    def start(self, ins, outs, sems):
        _, _, _, mine, sends = self._parts(ins, outs, sems)
        for cp in mine + sends:
            cp.start()

    def finish(self, ins, outs, sems):
        n, peers, landed, mine, sends = self._parts(ins, outs, sems)
        for a in range(n):
            for k, peer in enumerate(peers):
                landed(a, k, peer).wait_recv()
        for cp in sends:
            cp.wait_send()
        for cp in mine:
            cp.wait()


def _exchange(plan, name):
    n = len(plan.arrays)

    def body(*refs):
        ins, outs, sems = refs[:n], refs[n:2 * n], refs[2 * n:]
        plan.start(ins, outs, sems)
        plan.finish(ins, outs, sems)

    return pl.pallas_call(
        body, name=name, in_specs=[HBM_SPEC] * n, out_specs=[HBM_SPEC] * n,
        out_shape=plan.out_shapes, scratch_shapes=plan.sem_shapes,
    )(*plan.arrays)


def _adamw(w, g, m, v):
    m = ADAM_B1 * m + (1.0 - ADAM_B1) * g
    v = ADAM_B2 * v + (1.0 - ADAM_B2) * jnp.square(g)
    m_hat = m / (1.0 - ADAM_B1 ** ADAM_STEP)
    v_hat = v / (1.0 - ADAM_B2 ** ADAM_STEP)
    delta = -ADAM_LR * (m_hat / (jnp.sqrt(v_hat) + ADAM_EPS) + ADAM_WD * w)
    return delta, m, v


def _adam_sharded(parts_list, w, m, v, name):
    shape = w.shape
    cols = shape[-1]
    rows = int(np.prod(shape[:-1]))
    tr = min(128, rows)
    parts_list = [p.reshape(N_DEV, -1, cols) for p in parts_list]
    bounds = np.cumsum([0] + [p.shape[1] // tr for p in parts_list])
    assert rows % tr == 0 and all(p.shape[1] % tr == 0 for p in parts_list) and bounds[-1] == rows // tr
    n_parts = len(parts_list)

    def body(*refs):
        p_refs = refs[:n_parts]
        w_ref, m_ref, v_ref, g_ref, d_ref, nm_ref, nv_ref = refs[n_parts:]
        i = pl.program_id(0)
        for k, p_ref in enumerate(p_refs):
            @pl.when((i >= bounds[k]) & (i < bounds[k + 1]))
            def _():
                g = p_ref[0].astype(F32)
                for e in range(1, N_DEV):
                    g = g + p_ref[e].astype(F32)
                g_ref[...] = g
                d_ref[...], nm_ref[...], nv_ref[...] = _adamw(w_ref[...], g, m_ref[...], v_ref[...])

    def part_spec(k):
        lo, hi = int(bounds[k]), int(bounds[k + 1])
        return pl.BlockSpec((N_DEV, tr, cols), lambda i: (0, jnp.clip(i, lo, hi - 1) - lo, 0))

    spec = pl.BlockSpec((tr, cols), lambda i: (i, 0))
    flat = jax.ShapeDtypeStruct((rows, cols), F32)
    outs = pl.pallas_call(
        body, name=name, grid=(rows // tr,),
        in_specs=[part_spec(k) for k in range(n_parts)] + [spec, spec, spec],
        out_specs=[spec] * 4, out_shape=[flat] * 4,
        compiler_params=_params(40),
    )(*parts_list, w.reshape(rows, cols), m.reshape(rows, cols), v.reshape(rows, cols))
    return [o.reshape(shape) for o in outs]


def _adam_replicated(parts, w, m, v):
    r = w.shape[0]

    def body(p_ref, w_ref, m_ref, v_ref, g_ref, d_ref, nm_ref, nv_ref):
        g = p_ref[0]
        for e in range(1, N_DEV):
            g = g + p_ref[e]
        g_ref[...] = g
        d_ref[...], nm_ref[...], nv_ref[...] = _adamw(w_ref[...], g, m_ref[...], v_ref[...])

    spec = _full_spec((r, 128))
    flat = jax.ShapeDtypeStruct((r, 128), F32)
    return pl.pallas_call(
        body, name="adam_replicated", grid=(1,),
        in_specs=[_full_spec((N_DEV, r, 128)), spec, spec, spec],
        out_specs=[spec] * 4, out_shape=[flat] * 4,
        compiler_params=_params(48),
    )(parts, w, m, v)


def _pack(arrays):
    parts = []
    for a in arrays:
        f = a.reshape(-1, 128)
        pad = -f.shape[0] % 8
        parts.append(jnp.pad(f, ((0, pad), (0, 0))) if pad else f)
    return jnp.concatenate(parts, axis=0)


def _unpack(packed, shapes):
    out, row = [], 0
    for shape in shapes:
        r = int(np.prod(shape)) // 128
        out.append(packed[row:row + r].reshape(shape))
        row += r + (-r % 8)
    return out


SHARDED = ("w_in", "w_uq", "w_ukv", "w_mem_kv", "w_gate", "w_branch", "w_out")
REPLICATED = ("g_pre", "a_ln_g", "a_ln_b", "a_w_s", "a_b_s", "q_norm_g", "kv_norm_g", "mem_norm_g", "b_gate", "g_post")
WEIGHT_ORDER = ("g_pre", "w_in", "a_ln_g", "a_ln_b", "a_w_s", "a_b_s", "q_norm_g", "w_uq", "kv_norm_g", "w_ukv",
                "mem_norm_g", "w_mem_kv", "w_gate", "b_gate", "w_branch", "w_out", "g_post")


def _unshard_cols(g):
    return g.transpose(1, 0, 2).reshape(g.shape[1], N_DEV * g.shape[2])


def _shard_cols(full):
    rows, n = full.shape
    return full.reshape(rows, N_DEV, n // N_DEV).transpose(1, 0, 2).astype(BF16)


def kernel(x, mem, positions, g_pre, w_in, a_ln_g, a_ln_b, a_w_s, a_b_s, q_norm_g, w_uq, kv_norm_g, w_ukv, mem_norm_g, w_mem_kv, w_gate, b_gate, w_branch, w_out, g_post, loss_target, m_g_pre, m_w_in, m_a_ln_g, m_a_ln_b, m_a_w_s, m_a_b_s, m_q_norm_g, m_w_uq, m_kv_norm_g, m_w_ukv, m_mem_norm_g, m_w_mem_kv, m_w_gate, m_b_gate, m_w_branch, m_w_out, m_g_post, v_g_pre, v_w_in, v_a_ln_g, v_a_ln_b, v_a_w_s, v_a_b_s, v_q_norm_g, v_w_uq, v_kv_norm_g, v_w_ukv, v_mem_norm_g, v_w_mem_kv, v_w_gate, v_b_gate, v_w_branch, v_w_out, v_g_post):
    weights = dict(g_pre=g_pre, w_in=w_in, a_ln_g=a_ln_g, a_ln_b=a_ln_b, a_w_s=a_w_s, a_b_s=a_b_s, q_norm_g=q_norm_g,
                   w_uq=w_uq, kv_norm_g=kv_norm_g, w_ukv=w_ukv, mem_norm_g=mem_norm_g, w_mem_kv=w_mem_kv,
                   w_gate=w_gate, b_gate=b_gate, w_branch=w_branch, w_out=w_out, g_post=g_post)
    mom1 = dict(g_pre=m_g_pre, w_in=m_w_in, a_ln_g=m_a_ln_g, a_ln_b=m_a_ln_b, a_w_s=m_a_w_s, a_b_s=m_a_b_s,
                q_norm_g=m_q_norm_g, w_uq=m_w_uq, kv_norm_g=m_kv_norm_g, w_ukv=m_w_ukv, mem_norm_g=m_mem_norm_g,
                w_mem_kv=m_w_mem_kv, w_gate=m_w_gate, b_gate=m_b_gate, w_branch=m_w_branch, w_out=m_w_out, g_post=m_g_post)
    mom2 = dict(g_pre=v_g_pre, w_in=v_w_in, a_ln_g=v_a_ln_g, a_ln_b=v_a_ln_b, a_w_s=v_a_w_s, a_b_s=v_a_b_s,
                q_norm_g=v_q_norm_g, w_uq=v_w_uq, kv_norm_g=v_kv_norm_g, w_ukv=v_w_ukv, mem_norm_g=v_mem_norm_g,
                w_mem_kv=v_w_mem_kv, w_gate=v_w_gate, b_gate=v_b_gate, w_branch=v_w_branch, w_out=v_w_out, g_post=v_g_post)
    d = D_MODEL
    t = x.shape[1]
    xs, tgt, mems = x[0], loss_target[0], mem[0]
    pos_col = positions.reshape(t, 1)

    shard16 = {n: weights[n][0].astype(BF16) for n in SHARDED}
    h, g_gate = _pre_norm(xs, g_pre, _Gather([shard16["w_gate"]]))
    w_gate_f = _unshard_cols(g_gate)
    gates, g_in = _mm(h, w_gate_f, name="gates", tm=1024, tn=1024, tk=2048, bias=b_gate, act="sigmoid",
                      carry=_Gather([shard16["w_in"]]))
    w_in_full = _unshard_cols(g_in)
    lat0, lat1 = 3 * d, 3 * d + Q_LORA + KV_LORA + QK_ROPE
    w_big = jnp.concatenate([w_in_full[:, :lat0], w_in_full[:, lat1:]], axis=1)
    w_lat = jnp.concatenate([w_in_full[:, lat0:lat1], jnp.zeros((d, LAT_W - (lat1 - lat0)), BF16)], axis=1)

    inv_freq = 1.0 / (ROPE_THETA ** (jnp.arange(0, QK_ROPE, 2, dtype=F32) / QK_ROPE))
    inv_freq_lanes = jnp.concatenate([inv_freq, inv_freq, jnp.zeros((128 - QK_ROPE,), F32)]).reshape(1, 128)
    ws = a_w_s[0]
    b_exp = jnp.broadcast_to(a_b_s[0][:, :, None], (A_GROUPS, CHUNK, 128))

    proj_big, g_uq, g_ukv, g_mem, g_br, g_out = _mm(
        h, w_big, name="proj_big", tm=1024, tn=1024, tk=2048, early_forward=True,
        carry=_Gather([shard16[n] for n in ("w_uq", "w_ukv", "w_mem_kv", "w_branch", "w_out")]))
    w_uq_p = jnp.pad(_unshard_cols(g_uq).reshape(Q_LORA, MLA_HEADS, QK_DIM),
                     ((0, 0), (0, 0), (0, HEAD_PAD - QK_DIM))).reshape(Q_LORA, MLA_HEADS * HEAD_PAD)
    w_ukv_f = _unshard_cols(g_ukv)
    w_mem_f = _unshard_cols(g_mem)
    proj_lat = _mm(h, w_lat, name="proj_lat", tm=1024, tn=LAT_W, tk=2048)
    w_br_f = g_br.transpose(1, 0, 2, 3).reshape(3, d, d)
    w_out_f = g_out.reshape(d, d)
    c_tab, sa_tab, sb_tab = _rope_tables(pos_col, inv_freq_lanes)
    cqn, ckvn, kpe = _latent_norms(proj_lat, q_norm_g, kv_norm_g, c_tab, sa_tab, sb_tab)
    q = _mm(cqn, w_uq_p, name="q_up", tm=1024, tn=1024, tk=512, post=_rope_q_tile, post_rows=(c_tab, sa_tab, sb_tab))
    kv = _mm(ckvn, w_ukv_f, name="kv_up", tm=1024, tn=1024, tk=512)
    o_b, y_b, lse = _mla_fwd(q, kv, kpe, proj_big)
    memn = _mem_norm(mems, mem_norm_g)
    kv_m = _mm(memn, w_mem_f, name="mem_kv", tm=256, tn=1024, tk=2048)
    y_m = _mem_fwd(proj_big, kv_m)
    y_a = _gmlp_fwd(proj_big, a_ln_g, a_ln_b, ws, b_exp)
    ys = (y_a, y_b, y_m)
    ps = [_mm(ys[n], w_br_f[n], name=f"branch{n}", tm=1024, tn=1024, tk=2048) for n in range(3)]
    merged = _merge(gates, *ps)
    out = _mm(merged, w_out_f, name="out_proj", tm=1024, tn=1024, tk=2048, out_dtype=F32)
    d_out, dy, loss_blk, dg_post = _post_loss(out, xs, tgt, g_post)

    dmerged = _mm(d_out, w_out_f, name="d_merged", tb=True, tm=1024, tn=1024, tk=2048)
    dw_out = _mm(merged, d_out, name="dw_out", ta=True, tm=1024, tn=1024, tk=2048)
    dp_a, dp_b, dp_m, dgl, db_gate = _merge_bwd(dmerged, gates, *ps)
    dps = (dp_a, dp_b, dp_m)
    dys = [_mm(dps[n], w_br_f[n], name=f"d_y{n}", tb=True, tm=1024, tn=1024, tk=2048) for n in range(3)]
    dw_br = [_mm(ys[n], dps[n], name=f"dw_branch{n}", ta=True, tm=1024, tn=1024, tk=2048) for n in range(3)]
    dw_gate = _mm(h, dgl, name="dw_gate", ta=True, tm=1024, tn=3 * d // N_DEV, tk=2048, col_shards=True)

    dproj_big, dws, dbs, dlng, dlnb = _gmlp_bwd(proj_big, dys[0], a_ln_g, a_ln_b, ws, b_exp)

    recv = {}
    send = [dw_gate, jnp.stack(dw_br).reshape(3, N_DEV, d // N_DEV, d).transpose(1, 0, 2, 3),
            dw_out.reshape(N_DEV, d // N_DEV, d)]
    do_b, dproj_big, delta = _mla_gate_bwd(dys[1], proj_big, o_b, dproj_big)
    dq, dkv, dkpe, r_gate, r_br, r_out = _mla_bwd(q, kv, kpe, do_b, lse, delta, _AllToAll(send))
    recv["w_gate"], recv["w_branch"], recv["w_out"] = [r_gate], [r_br], [r_out]
    dq_raw, dkr = _rope_q_bwd(dq, dkpe, c_tab, sa_tab, sb_tab)
    dcqn = _mm(dq_raw, w_uq_p, name="d_cq", tb=True, tm=1024, tn=Q_LORA, tk=2048, out_dtype=F32)
    dw_uq_p = _mm(cqn, dq_raw, name="dw_uq", ta=True, tm=Q_LORA, tn=1024, tk=2048)
    dckvn = _mm(dkv, w_ukv_f, name="d_ckv", tb=True, tm=1024, tn=KV_LORA, tk=2048, out_dtype=F32)
    dw_ukv = _mm(ckvn, dkv, name="dw_ukv", ta=True, tm=KV_LORA, tn=2 * d // N_DEV, tk=2048, col_shards=True)
    dproj_lat, dqg, dkg = _latent_norms_bwd(proj_lat, q_norm_g, kv_norm_g, dcqn, dckvn, dkr)

    dproj_big, dkv_m32 = _mem_bwd(proj_big, kv_m, dys[2], dproj_big)
    dkv_m = dkv_m32.astype(BF16)
    dw_mem = _mm(memn, dkv_m, name="dw_mem", ta=True, tm=1024, tn=2 * d // N_DEV, tk=256, col_shards=True)

    dw_uq_full = dw_uq_p.reshape(Q_LORA, MLA_HEADS, HEAD_PAD)[:, :, :QK_DIM].reshape(Q_LORA, MLA_HEADS * QK_DIM)
    dw_big, r_uq, r_ukv, r_mem = _mm(
        h, dproj_big, name="dw_big", ta=True, tm=1024, tn=1024, tk=2048,
        carry=_AllToAll([_shard_cols(dw_uq_full), dw_ukv, dw_mem]))
    recv["w_uq"], recv["w_ukv"], recv["w_mem_kv"] = [r_uq], [r_ukv], [r_mem]
    dw_lat = _mm(h, dproj_lat, name="dw_lat", ta=True, tm=1024, tn=LAT_W, tk=2048)

    def w_in_shards(rows):
        n_cols = w_in.shape[-1]
        pieces = []
        for j in range(N_DEV):
            lo, hi = n_cols * j, n_cols * (j + 1)
            parts = []
            for a, b, src, shift in ((0, lat0, dw_big, 0), (lat0, lat1, dw_lat, lat0), (lat1, N_DEV * n_cols, dw_big, lat1 - lat0)):
                s, e = max(lo, a), min(hi, b)
                if s < e:
                    parts.append(src[rows, s - shift:e - shift])
            pieces.append(parts[0] if len(parts) == 1 else jnp.concatenate(parts, axis=1))
        return jnp.stack(pieces)

    dmemn = _mm(dkv_m, w_mem_f, name="d_memn", tb=True, tm=256, tn=1024, tk=2048, out_dtype=F32)
    dg_mem = _mem_norm_bwd(mems, mem_norm_g, dmemn)
    dh, r_in0 = _mm(dgl, w_gate_f, name="dh_gate", tb=True, tm=1024, tn=1024, tk=2048, out_dtype=F32,
                    carry=_AllToAll([w_in_shards(slice(0, d // 2))]))
    dh = _mm(dproj_lat, w_lat, name="dh_lat", tb=True, tm=1024, tn=1024, tk=LAT_W, out_dtype=F32, add=dh)
    dh, r_in1 = _mm(dproj_big, w_big, name="dh_big", tb=True, tm=1024, tn=1024, tk=2048, out_dtype=F32, add=dh,
                    carry=_AllToAll([w_in_shards(slice(d // 2, d))]))
    recv["w_in"] = [r_in0, r_in1]
    grad_x, dg_pre = _pre_norm_bwd(xs, g_pre, dh, dy)

    results = {}
    for n in SHARDED:
        results[n] = [r[None] for r in _adam_sharded(recv[n], weights[n][0], mom1[n][0], mom2[n][0], "adam_" + n)]

    small = dict(g_pre=dg_pre, a_ln_g=dlng, a_ln_b=dlnb, a_w_s=dws, a_b_s=dbs, q_norm_g=dqg, kv_norm_g=dkg,
                 mem_norm_g=dg_mem, b_gate=db_gate, g_post=dg_post)
    (parts,) = _exchange(_Gather([_pack([small[n] for n in REPLICATED])]), "gather_small_grads")
    packed = _adam_replicated(parts, _pack([weights[n] for n in REPLICATED]), _pack([mom1[n] for n in REPLICATED]),
                              _pack([mom2[n] for n in REPLICATED]))
    shapes = [weights[n].shape for n in REPLICATED]
    unpacked = [_unpack(p, shapes) for p in packed]
    for i, n in enumerate(REPLICATED):
        results[n] = [u[i] for u in unpacked]

    loss = lax.psum(loss_blk[0, 0], AXES)
    outs = [loss, grad_x[None]]
    for kind in range(4):
        outs += [results[n][kind] for n in WEIGHT_ORDER]
    return tuple(outs)
```

```python
import math

import jax
import jax.numpy as jnp
import numpy as np
from jax import lax
from jax.experimental import pallas as pl
from jax.experimental.pallas import tpu as pltpu

F32 = jnp.float32
BF16 = jnp.bfloat16
MESH = pl.DeviceIdType.MESH
AXES = ("x", "y", "c")
N_DEV = 8

D_MODEL = 2048
EPS = 1e-6
CHUNK = 128
A_GROUPS = 16
MLA_HEADS = 16
QK_NOPE = 128
QK_ROPE = 64
QK_DIM = QK_NOPE + QK_ROPE
HEAD_PAD = 256
Q_LORA = 512
KV_LORA = 512
MEM_HEADS = 4
MEM_HEAD_DIM = 512
ROPE_THETA = 10000.0
MLA_SCALE = QK_DIM ** -0.5
MEM_SCALE = MEM_HEAD_DIM ** -0.5
NEG = -1e30
LOG2E = 1.4426950408889634

ADAM_LR = 0.001
ADAM_B1 = 0.9
ADAM_B2 = 0.999
ADAM_EPS = 1e-08
ADAM_WD = 0.01
ADAM_STEP = 10

BIG_W = 6 * D_MODEL
LAT_W = Q_LORA + KV_LORA + 128

VMEM_MIB = 1024 * 1024

ROW_BLK = 256
ATT_BLK = 2048
ATT_BLK_FWD = 2048
ATT_SUB = 256
GMLP_ROWS = 256
MEM_Q_BLK = 512


def _params(vmem_mib, **kw):
    return pltpu.CompilerParams(vmem_limit_bytes=int(vmem_mib * VMEM_MIB), **kw)


def _gelu(x):
    k = math.sqrt(2.0 / math.pi)
    t = jnp.tanh(k * (x + 0.044715 * (x * x * x)))
    return 0.5 * x * (1.0 + t)


def _gelu_and_grad(x):
    k = math.sqrt(2.0 / math.pi)
    x2 = x * x
    t = jnp.tanh(k * (x + 0.044715 * (x2 * x)))
    val = 0.5 * x * (1.0 + t)
    grad = 0.5 * (1.0 + t) + 0.5 * x * (1.0 - t * t) * (k * (1.0 + 3.0 * 0.044715 * x2))
    return val, grad


def _silu_and_grad(z):
    s = jax.nn.sigmoid(z)
    return z * s, s * (1.0 + z * (1.0 - s))


def _mm(a, b, *, name, tm, tn, tk, ta=False, tb=False, out_dtype=BF16, bias=None, act=None, add=None, carry=None,
        post=None, post_rows=(), col_shards=False, early_forward=False):
    m = a.shape[1] if ta else a.shape[0]
    k = a.shape[0] if ta else a.shape[1]
    n = b.shape[0] if tb else b.shape[1]
    assert k == (b.shape[1] if tb else b.shape[0])
    tm, tn, tk = min(tm, m), min(tn, n), min(tk, k)
    assert m % tm == 0 and n % tn == 0 and k % tk == 0, (name, m, n, k, tm, tn, tk)
    nk = k // tk
    a_spec = pl.BlockSpec((tk, tm), lambda i, j, kk: (kk, i)) if ta else pl.BlockSpec((tm, tk), lambda i, j, kk: (i, kk))
    b_spec = pl.BlockSpec((tn, tk), lambda i, j, kk: (j, kk)) if tb else pl.BlockSpec((tk, tn), lambda i, j, kk: (kk, j))
    dn = (((0 if ta else 1,), (1 if tb else 0,)), ((), ()))
    operands, in_specs = [a, b], [a_spec, b_spec]
    if bias is not None:
        operands.append(bias)
        in_specs.append(pl.BlockSpec((1, tn), lambda i, j, kk: (0, j)))
    if add is not None:
        operands.append(add)
        in_specs.append(pl.BlockSpec((tm, tn), lambda i, j, kk: (i, j)))
    n_fixed = len(operands)
    for r in post_rows:
        operands.append(r)
        in_specs.append(pl.BlockSpec((tm, r.shape[1]), lambda i, j, kk: (i, 0)))

    n_in = len(operands)
    n_carry = len(carry.arrays) if carry is not None else 0
    n_acc = 1 if nk > 1 else 0
    grid = (m // tm, n // tn, nk)

    def body(*refs):
        a_ref, b_ref = refs[0], refs[1]
        pos = 2
        bias_ref = add_ref = None
        if bias is not None:
            bias_ref = refs[pos]
            pos += 1
        if add is not None:
            add_ref = refs[pos]
            pos += 1
        o_ref = refs[n_in + n_carry]
        pos = n_in + n_carry
        if carry is not None:
            c_ins = refs[n_in:n_in + n_carry]
            c_outs = refs[n_in + n_carry + 1:n_in + 2 * n_carry + 1]
            c_sems = refs[n_in + 2 * n_carry + 1 + n_acc:]
            ids = [pl.program_id(ax) for ax in range(3)]

            @pl.when((ids[0] == 0) & (ids[1] == 0) & (ids[2] == 0))
            def _():
                carry.start(c_ins, c_outs, c_sems)

        part = lax.dot_general(a_ref[...], b_ref[...], dn, preferred_element_type=F32)

        def finish(acc):
            if bias_ref is not None:
                acc = acc + bias_ref[...]
            if act == "sigmoid":
                acc = jax.nn.sigmoid(acc)
            if add_ref is not None:
                acc = acc + add_ref[...]
            if post is not None:
                acc = post(acc, *[r[...] for r in refs[n_fixed:n_in]])
            o_ref[...] = acc.astype(o_ref.dtype)

        if nk == 1:
            finish(part)
        else:
            acc_ref = refs[n_in + 2 * n_carry + 1]
            kk = pl.program_id(2)

            @pl.when(kk == 0)
            def _():
                acc_ref[...] = part

            @pl.when(kk > 0)
            def _():
                acc_ref[...] += part

            @pl.when(kk == nk - 1)
            def _():
                finish(acc_ref[...])

        if carry is not None:
            total = grid[0] * grid[1] * grid[2]
            early = early_forward and total >= 16
            if early:
                @pl.when((ids[0] * grid[1] + ids[1]) * grid[2] + ids[2] == (4 * total) // 5)
                def _():
                    carry.forward(c_ins, c_outs, c_sems)

            @pl.when((ids[0] == grid[0] - 1) & (ids[1] == grid[1] - 1) & (ids[2] == grid[2] - 1))
            def _():
                if early:
                    carry.finish(c_ins, c_outs, c_sems, forwarded=True)
                else:
                    carry.finish(c_ins, c_outs, c_sems)

    osz = jnp.dtype(out_dtype).itemsize
    est = 2 * 2 * (tm * tk + tk * tn) + 2 * osz * tm * tn + 8 * tm * tn + (2 * 4 * tm * tn if add is not None else 0)
    if col_shards:
        assert n // tn == N_DEV
        main_spec = pl.BlockSpec((None, tm, tn), lambda i, j, kk: (j, i, 0))
        main_shape = jax.ShapeDtypeStruct((N_DEV, m, tn), out_dtype)
    else:
        main_spec = pl.BlockSpec((tm, tn), lambda i, j, kk: (i, j))
        main_shape = jax.ShapeDtypeStruct((m, n), out_dtype)
    scratch = [pltpu.VMEM((tm, tn), F32)] if nk > 1 else []
    if carry is None:
        return pl.pallas_call(
            body, name=name, grid=grid, in_specs=in_specs, out_specs=main_spec, out_shape=main_shape,
            scratch_shapes=scratch, compiler_params=_params(min(56, est / VMEM_MIB + 12)),
        )(*operands)
    return pl.pallas_call(
        body, name=name, grid=grid,
        in_specs=in_specs + [HBM_SPEC] * n_carry,
        out_specs=[main_spec] + [HBM_SPEC] * n_carry,
        out_shape=[main_shape] + carry.out_shapes,
        scratch_shapes=scratch + carry.sem_shapes,
        compiler_params=_params(min(56, est / VMEM_MIB + 12)),
    )(*operands, *carry.arrays)


def _row_spec(tr, cols, col_blk=0):
    return pl.BlockSpec((tr, cols), lambda i: (i, col_blk))


def _full_spec(shape):
    nd = len(shape)
    return pl.BlockSpec(shape, lambda i: (0,) * nd)


def _pre_norm(x, g_pre, carry):
    t, d = x.shape
    tr = min(ROW_BLK, t)
    n_carry = len(carry.arrays)
    steps = t // tr

    def body(x_ref, g_ref, *rest):
        c_ins, h_ref = rest[:n_carry], rest[n_carry]
        c_outs, c_sems = rest[n_carry + 1:2 * n_carry + 1], rest[2 * n_carry + 1:]

        @pl.when(pl.program_id(0) == 0)
        def _():
            carry.start(c_ins, c_outs, c_sems)

        xv = x_ref[...]
        r = lax.rsqrt(jnp.mean(xv * xv, axis=-1, keepdims=True) + EPS)
        h_ref[...] = ((xv * r) * g_ref[...]).astype(BF16)

        @pl.when(pl.program_id(0) == steps - 1)
        def _():
            carry.finish(c_ins, c_outs, c_sems)

    return pl.pallas_call(
        body, name="pre_norm", grid=(steps,),
        in_specs=[_row_spec(tr, d), _full_spec((1, d))] + [HBM_SPEC] * n_carry,
        out_specs=[_row_spec(tr, d)] + [HBM_SPEC] * n_carry,
        out_shape=[jax.ShapeDtypeStruct((t, d), BF16)] + carry.out_shapes,
        scratch_shapes=carry.sem_shapes,
        compiler_params=_params(32),
    )(x, g_pre, *carry.arrays)


def _rope_tables(pos_col, inv_freq_lanes):
    t = pos_col.shape[0]
    tr = min(ROW_BLK, t)

    def body(p_ref, f_ref, c_ref, sa_ref, sb_ref):
        ang = p_ref[...].astype(F32) * f_ref[...]
        lane = lax.broadcasted_iota(jnp.int32, ang.shape, 1)
        cos, sin = jnp.cos(ang), jnp.sin(ang)
        c_ref[...] = jnp.where(lane < QK_ROPE, cos, 0.0)
        sa_ref[...] = jnp.where(lane < QK_ROPE // 2, sin, 0.0)
        sb_ref[...] = jnp.where((lane >= QK_ROPE // 2) & (lane < QK_ROPE), sin, 0.0)

    tab = jax.ShapeDtypeStruct((t, 128), F32)
    return pl.pallas_call(
        body, name="rope_tables", grid=(t // tr,),
        in_specs=[_row_spec(tr, 1), _full_spec((1, 128))],
        out_specs=[_row_spec(tr, 128)] * 3,
        out_shape=[tab, tab, tab],
    )(pos_col, inv_freq_lanes)


def _rope_fwd(p, c, sa, sb):
    return p * c - pltpu.roll(p, 96, 1) * sa + pltpu.roll(p, 32, 1) * sb


def _rope_bwd(g, c, sa, sb):
    return g * c + pltpu.roll(g, 96, 1) * sa - pltpu.roll(g, 32, 1) * sb


def _rms(xv, g):
    r = lax.rsqrt(jnp.mean(xv * xv, axis=-1, keepdims=True) + EPS)
    return (xv * r) * g


def _rms_bwd(xv, g, dout):
    r = lax.rsqrt(jnp.mean(xv * xv, axis=-1, keepdims=True) + EPS)
    xn = xv * r
    dg = jnp.sum(dout * xn, axis=0, keepdims=True)
    dxn = dout * g
    dx = r * (dxn - xn * jnp.mean(dxn * xn, axis=-1, keepdims=True))
    return dx, dg


def _latent_norms(proj_lat, q_norm_g, kv_norm_g, c_tab, sa_tab, sb_tab):
    t = proj_lat.shape[0]
    tr = min(ROW_BLK, t)

    def body(cq_ref, ckv_ref, kr_ref, qg_ref, kg_ref, c_ref, sa_ref, sb_ref, cqn_ref, ckvn_ref, kpe_ref):
        cqn_ref[...] = _rms(cq_ref[...].astype(F32), qg_ref[...]).astype(BF16)
        ckvn_ref[...] = _rms(ckv_ref[...].astype(F32), kg_ref[...]).astype(BF16)
        kpe_ref[...] = _rope_fwd(kr_ref[...].astype(F32), c_ref[...], sa_ref[...], sb_ref[...]).astype(BF16)

    return pl.pallas_call(
        body, name="latent_norms", grid=(t // tr,),
        in_specs=[_row_spec(tr, Q_LORA, 0), _row_spec(tr, KV_LORA, 1), _row_spec(tr, 128, (Q_LORA + KV_LORA) // 128),
                  _full_spec((1, Q_LORA)), _full_spec((1, KV_LORA)),
                  _row_spec(tr, 128), _row_spec(tr, 128), _row_spec(tr, 128)],
        out_specs=[_row_spec(tr, Q_LORA), _row_spec(tr, KV_LORA), _row_spec(tr, 128)],
        out_shape=[jax.ShapeDtypeStruct((t, Q_LORA), BF16), jax.ShapeDtypeStruct((t, KV_LORA), BF16),
                   jax.ShapeDtypeStruct((t, 128), BF16)],
    )(proj_lat, proj_lat, proj_lat, q_norm_g, kv_norm_g, c_tab, sa_tab, sb_tab)


def _rope_q_tile(acc, c, sa, sb):
    qs = MLA_SCALE * LOG2E
    parts = []
    for h in range(acc.shape[1] // HEAD_PAD):
        parts.append(acc[:, h * HEAD_PAD:h * HEAD_PAD + 128] * qs)
        parts.append(_rope_fwd(acc[:, h * HEAD_PAD + 128:(h + 1) * HEAD_PAD], c, sa, sb) * qs)
    return jnp.concatenate(parts, axis=1)


def _rope_q_bwd(dq, dkpe, c_tab, sa_tab, sb_tab):
    t = dq.shape[0]
    tr = min(ROW_BLK, t)

    def body(dq_ref, dkp_ref, c_ref, sa_ref, sb_ref, o_ref, dkr_ref):
        c, sa, sb = c_ref[...], sa_ref[...], sb_ref[...]
        for h in range(MLA_HEADS):
            o_ref[:, h * HEAD_PAD:h * HEAD_PAD + 128] = (dq_ref[:, h * HEAD_PAD:h * HEAD_PAD + 128] * MLA_SCALE).astype(BF16)
            g = dq_ref[:, h * HEAD_PAD + 128:(h + 1) * HEAD_PAD] * MLA_SCALE
            o_ref[:, h * HEAD_PAD + 128:(h + 1) * HEAD_PAD] = _rope_bwd(g, c, sa, sb).astype(BF16)
        dkr_ref[...] = _rope_bwd(dkp_ref[...], c, sa, sb)

    w = MLA_HEADS * HEAD_PAD
    return pl.pallas_call(
        body, name="rope_q_bwd", grid=(t // tr,),
        in_specs=[_row_spec(tr, w), _row_spec(tr, 128),
                  _row_spec(tr, 128), _row_spec(tr, 128), _row_spec(tr, 128)],
        out_specs=[_row_spec(tr, w), _row_spec(tr, 128)],
        out_shape=[jax.ShapeDtypeStruct((t, w), BF16), jax.ShapeDtypeStruct((t, 128), F32)],
        compiler_params=_params(48),
    )(dq, dkpe, c_tab, sa_tab, sb_tab)


def _latent_norms_bwd(proj_lat, q_norm_g, kv_norm_g, dcqn, dckvn, dkr):
    t = proj_lat.shape[0]
    tr = min(ROW_BLK, t)

    def body(cq_ref, ckv_ref, qg_ref, kg_ref, dcqn_ref, dckvn_ref, dkr_ref, dl_ref, dqg_ref, dkg_ref):
        dcq, dqg = _rms_bwd(cq_ref[...].astype(F32), qg_ref[...], dcqn_ref[...])
        dckv, dkg = _rms_bwd(ckv_ref[...].astype(F32), kg_ref[...], dckvn_ref[...])
        dl_ref[:, 0:Q_LORA] = dcq.astype(BF16)
        dl_ref[:, Q_LORA:Q_LORA + KV_LORA] = dckv.astype(BF16)
        dl_ref[:, Q_LORA + KV_LORA:LAT_W] = dkr_ref[...].astype(BF16)

        @pl.when(pl.program_id(0) == 0)
        def _():
            dqg_ref[...] = jnp.zeros_like(dqg_ref)
            dkg_ref[...] = jnp.zeros_like(dkg_ref)

        dqg_ref[...] += dqg
        dkg_ref[...] += dkg

    return pl.pallas_call(
        body, name="latent_norms_bwd", grid=(t // tr,),
        in_specs=[_row_spec(tr, Q_LORA, 0), _row_spec(tr, KV_LORA, 1), _full_spec((1, Q_LORA)), _full_spec((1, KV_LORA)),
                  _row_spec(tr, Q_LORA), _row_spec(tr, KV_LORA), _row_spec(tr, 128)],
        out_specs=[_row_spec(tr, LAT_W), _full_spec((1, Q_LORA)), _full_spec((1, KV_LORA))],
        out_shape=[jax.ShapeDtypeStruct((t, LAT_W), BF16), jax.ShapeDtypeStruct((1, Q_LORA), F32),
                   jax.ShapeDtypeStruct((1, KV_LORA), F32)],
    )(proj_lat, proj_lat, q_norm_g, kv_norm_g, dcqn, dckvn, dkr)


def _mem_norm(mem, g):
    m, d = mem.shape

    def body(x_ref, g_ref, o_ref):
        o_ref[...] = _rms(x_ref[...], g_ref[...]).astype(BF16)

    return pl.pallas_call(
        body, name="mem_norm", grid=(1,),
        in_specs=[_full_spec((m, d)), _full_spec((1, d))],
        out_specs=_full_spec((m, d)),
        out_shape=jax.ShapeDtypeStruct((m, d), BF16),
    )(mem, g)


def _mem_norm_bwd(mem, g, dmemn):
    m, d = mem.shape

    def body(x_ref, g_ref, d_ref, dg_ref):
        _, dg = _rms_bwd(x_ref[...], g_ref[...], d_ref[...])
        dg_ref[...] = dg

    return pl.pallas_call(
        body, name="mem_norm_bwd", grid=(1,),
        in_specs=[_full_spec((m, d)), _full_spec((1, d)), _full_spec((m, d))],
        out_specs=_full_spec((1, d)),
        out_shape=jax.ShapeDtypeStruct((1, d), F32),
    )(mem, g, dmemn)


def _merge(gates, p_a, p_b, p_m):
    t, d = p_a.shape
    tr = min(ROW_BLK, t)

    def body(ga_ref, gb_ref, gm_ref, pa_ref, pb_ref, pm_ref, o_ref):
        acc = ga_ref[...].astype(F32) * pa_ref[...].astype(F32)
        acc = acc + gb_ref[...].astype(F32) * pb_ref[...].astype(F32)
        acc = acc + gm_ref[...].astype(F32) * pm_ref[...].astype(F32)
        o_ref[...] = acc.astype(BF16)

    return pl.pallas_call(
        body, name="merge", grid=(t // tr,),
        in_specs=[_row_spec(tr, d, 0), _row_spec(tr, d, 1), _row_spec(tr, d, 2),
                  _row_spec(tr, d), _row_spec(tr, d), _row_spec(tr, d)],
        out_specs=_row_spec(tr, d),
        out_shape=jax.ShapeDtypeStruct((t, d), BF16),
        compiler_params=_params(48),
    )(gates, gates, gates, p_a, p_b, p_m)


def _merge_bwd(dm, gates, p_a, p_b, p_m):
    t, d = dm.shape
    tr = min(ROW_BLK, t)

    def body(dm_ref, g_ref, pa_ref, pb_ref, pm_ref, dpa_ref, dpb_ref, dpm_ref, dgl_ref, db_ref):
        dmv = dm_ref[...].astype(F32)

        @pl.when(pl.program_id(0) == 0)
        def _():
            db_ref[...] = jnp.zeros_like(db_ref)

        for n, (p_ref, dp_ref) in enumerate(((pa_ref, dpa_ref), (pb_ref, dpb_ref), (pm_ref, dpm_ref))):
            g = g_ref[:, n * d:(n + 1) * d].astype(F32)
            dp_ref[...] = (dmv * g).astype(BF16)
            dgl = dmv * p_ref[...].astype(F32) * (g * (1.0 - g))
            dgl_ref[:, n * d:(n + 1) * d] = dgl.astype(BF16)
            db_ref[:, n * d:(n + 1) * d] += jnp.sum(dgl, axis=0, keepdims=True)

    act = jax.ShapeDtypeStruct((t, d), BF16)
    return pl.pallas_call(
        body, name="merge_bwd", grid=(t // tr,),
        in_specs=[_row_spec(tr, d), _row_spec(tr, 3 * d), _row_spec(tr, d), _row_spec(tr, d), _row_spec(tr, d)],
        out_specs=[_row_spec(tr, d), _row_spec(tr, d), _row_spec(tr, d), _row_spec(tr, 3 * d), _full_spec((1, 3 * d))],
        out_shape=[act, act, act, jax.ShapeDtypeStruct((t, 3 * d), BF16), jax.ShapeDtypeStruct((1, 3 * d), F32)],
        compiler_params=_params(56),
    )(dm, gates, p_a, p_b, p_m)


def _post_loss(out, x, tgt, g_post):
    t, d = out.shape
    tr = min(ROW_BLK, t)

    def body(o_ref, x_ref, t_ref, g_ref, do_ref, dy_ref, loss_ref, dg_ref):
        ov = o_ref[...]
        g = g_ref[...]
        r = lax.rsqrt(jnp.mean(ov * ov, axis=-1, keepdims=True) + EPS)
        on = ov * r
        err = (x_ref[...] + on * g) - t_ref[...]
        dy = err * (1.0 / d)
        dy_ref[...] = dy.astype(BF16)
        don = dy * g
        do_ref[...] = (r * (don - on * jnp.mean(don * on, axis=-1, keepdims=True))).astype(BF16)

        @pl.when(pl.program_id(0) == 0)
        def _():
            loss_ref[...] = jnp.zeros_like(loss_ref)
            dg_ref[...] = jnp.zeros_like(dg_ref)

        loss_ref[...] += 0.5 * jnp.sum(jnp.mean(err * err, axis=-1, keepdims=True))
        dg_ref[...] += jnp.sum(dy * on, axis=0, keepdims=True)

    return pl.pallas_call(
        body, name="post_loss", grid=(t // tr,),
        in_specs=[_row_spec(tr, d), _row_spec(tr, d), _row_spec(tr, d), _full_spec((1, d))],
        out_specs=[_row_spec(tr, d), _row_spec(tr, d), _full_spec((8, 128)), _full_spec((1, d))],
        out_shape=[jax.ShapeDtypeStruct((t, d), BF16), jax.ShapeDtypeStruct((t, d), BF16),
                   jax.ShapeDtypeStruct((8, 128), F32), jax.ShapeDtypeStruct((1, d), F32)],
        compiler_params=_params(56),
    )(out, x, tgt, g_post)


def _pre_norm_bwd(x, g_pre, dh, dy):
    t, d = x.shape
    tr = min(ROW_BLK, t)

    def body(x_ref, g_ref, dh_ref, dy_ref, dx_ref, dg_ref):
        dx, dg = _rms_bwd(x_ref[...], g_ref[...], dh_ref[...])
        dx_ref[...] = dx + dy_ref[...].astype(F32)

        @pl.when(pl.program_id(0) == 0)
        def _():
            dg_ref[...] = jnp.zeros_like(dg_ref)

        dg_ref[...] += dg

    return pl.pallas_call(
        body, name="pre_norm_bwd", grid=(t // tr,),
        in_specs=[_row_spec(tr, d), _full_spec((1, d)), _row_spec(tr, d), _row_spec(tr, d)],
        out_specs=[_row_spec(tr, d), _full_spec((1, d))],
        out_shape=[jax.ShapeDtypeStruct((t, d), F32), jax.ShapeDtypeStruct((1, d), F32)],
        compiler_params=_params(56),
    )(x, g_pre, dh, dy)


def _causal_mask(n):
    row = lax.broadcasted_iota(jnp.int32, (n, n), 0)
    col = lax.broadcasted_iota(jnp.int32, (n, n), 1)
    return row >= col


def _layernorm_stats(vg):
    mu = jnp.mean(vg, axis=-1, keepdims=True)
    cen = vg - mu
    rstd = lax.rsqrt(jnp.mean(cen * cen, axis=-1, keepdims=True) + EPS)
    return cen * rstd, rstd


def _gmlp_fwd(proj_big, ln_g, ln_b, w_s, b_exp):
    t = proj_big.shape[0]
    rows = min(GMLP_ROWS, t)
    d = D_MODEL

    def body(u_ref, v_ref, z_ref, lg_ref, lb_ref, ws_ref, be_ref, y_ref, vn_scr):
        vhat, _ = _layernorm_stats(_gelu(v_ref[...].astype(F32)))
        vn_scr[...] = (vhat * lg_ref[...] + lb_ref[...]).astype(BF16)
        mask = _causal_mask(CHUNK)
        for g in range(A_GROUPS):
            cols = slice(g * 128, (g + 1) * 128)
            wsm = jnp.where(mask, ws_ref[g], 0.0).astype(BF16)
            for c in range(rows // CHUNK):
                rws = slice(c * CHUNK, (c + 1) * CHUNK)
                sv = jnp.dot(wsm, vn_scr[rws, cols], preferred_element_type=F32) + be_ref[g]
                zs, _ = _silu_and_grad(z_ref[rws, cols].astype(F32))
                y_ref[rws, cols] = (_gelu(u_ref[rws, cols].astype(F32)) * sv * zs).astype(BF16)

    return pl.pallas_call(
        body, name="gmlp_fwd", grid=(t // rows,),
        in_specs=[_row_spec(rows, d, 0), _row_spec(rows, d, 1), _row_spec(rows, d, 2),
                  _full_spec((1, d)), _full_spec((1, d)), _full_spec((A_GROUPS, CHUNK, CHUNK)),
                  _full_spec((A_GROUPS, CHUNK, 128))],
        out_specs=_row_spec(rows, d),
        out_shape=jax.ShapeDtypeStruct((t, d), BF16),
        scratch_shapes=[pltpu.VMEM((rows, d), BF16)],
        compiler_params=_params(40),
    )(proj_big, proj_big, proj_big, ln_g, ln_b, w_s, b_exp)


def _gmlp_bwd(proj_big, dya, ln_g, ln_b, w_s, b_exp):
    t = proj_big.shape[0]
    rows = min(GMLP_ROWS, t)
    d = D_MODEL
    nt = (((1,), (1,)), ((), ()))
    tn = (((0,), (0,)), ((), ()))

    def body(u_ref, v_ref, z_ref, dy_ref, lg_ref, lb_ref, ws_ref, be_ref,
             dp_ref, dws_ref, dbs_ref, dlg_ref, dlb_ref, vn_scr, dvn_scr):
        @pl.when(pl.program_id(0) == 0)
        def _():
            dws_ref[...] = jnp.zeros_like(dws_ref)
            dbs_ref[...] = jnp.zeros_like(dbs_ref)
            dlg_ref[...] = jnp.zeros_like(dlg_ref)
            dlb_ref[...] = jnp.zeros_like(dlb_ref)

        vg, vgrad = _gelu_and_grad(v_ref[...].astype(F32))
        vhat, rstd = _layernorm_stats(vg)
        vn_scr[...] = (vhat * lg_ref[...] + lb_ref[...]).astype(BF16)
        mask = _causal_mask(CHUNK)
        for g in range(A_GROUPS):
            cols = slice(g * 128, (g + 1) * 128)
            wsm = jnp.where(mask, ws_ref[g], 0.0).astype(BF16)
            dws = jnp.zeros((CHUNK, CHUNK), F32)
            dbs = jnp.zeros((CHUNK, 1), F32)
            for c in range(rows // CHUNK):
                rws = slice(c * CHUNK, (c + 1) * CHUNK)
                vn = vn_scr[rws, cols]
                sv = jnp.dot(wsm, vn, preferred_element_type=F32) + be_ref[g]
                ug, ugrad = _gelu_and_grad(u_ref[rws, cols].astype(F32))
                zs, zgrad = _silu_and_grad(z_ref[rws, cols].astype(F32))
                dya = dy_ref[rws, cols].astype(F32)
                dga = dya * zs
                dp_ref[rws, 2 * d + g * 128:2 * d + (g + 1) * 128] = (dya * (ug * sv) * zgrad).astype(BF16)
                dp_ref[rws, cols] = (dga * sv * ugrad).astype(BF16)
                dsv = dga * ug
                dsv16 = dsv.astype(BF16)
                dws = dws + lax.dot_general(dsv16, vn, nt, preferred_element_type=F32)
                dbs = dbs + jnp.sum(dsv, axis=-1, keepdims=True)
                dvn_scr[rws, cols] = lax.dot_general(wsm, dsv16, tn, preferred_element_type=F32)
            dws_ref[g] += jnp.where(mask, dws, 0.0)
            dbs_ref[g] += dbs
        dvn = dvn_scr[...]
        dlg_ref[...] += jnp.sum(dvn * vhat, axis=0, keepdims=True)
        dlb_ref[...] += jnp.sum(dvn, axis=0, keepdims=True)
        dvh = dvn * lg_ref[...]
        dvg = rstd * (dvh - jnp.mean(dvh, axis=-1, keepdims=True) - vhat * jnp.mean(dvh * vhat, axis=-1, keepdims=True))
        dp_ref[:, d:2 * d] = (dvg * vgrad).astype(BF16)

    return pl.pallas_call(
        body, name="gmlp_bwd", grid=(t // rows,),
        in_specs=[_row_spec(rows, d, 0), _row_spec(rows, d, 1), _row_spec(rows, d, 2), _row_spec(rows, d),
                  _full_spec((1, d)), _full_spec((1, d)), _full_spec((A_GROUPS, CHUNK, CHUNK)),
                  _full_spec((A_GROUPS, CHUNK, 128))],
        out_specs=[_row_spec(rows, 3 * d), _full_spec((A_GROUPS, CHUNK, CHUNK)), _full_spec((A_GROUPS, CHUNK, 1)),
                   _full_spec((1, d)), _full_spec((1, d))],
        out_shape=[jax.ShapeDtypeStruct((t, 6 * d), BF16), jax.ShapeDtypeStruct((A_GROUPS, CHUNK, CHUNK), F32),
                   jax.ShapeDtypeStruct((A_GROUPS, CHUNK, 1), F32), jax.ShapeDtypeStruct((1, d), F32),
                   jax.ShapeDtypeStruct((1, d), F32)],
        scratch_shapes=[pltpu.VMEM((rows, d), BF16), pltpu.VMEM((rows, d), F32)],
        compiler_params=_params(48),
    )(proj_big, proj_big, proj_big, dya, ln_g, ln_b, w_s, b_exp)


NT_DIMS = (((1,), (1,)), ((), ()))
TN_DIMS = (((0,), (0,)), ((), ()))


def _mla_fwd(q, kv, kpe, proj_big):
    t = q.shape[0]
    blk = min(ATT_BLK_FWD, t)
    nq = t // blk
    zb_blk0 = (3 * D_MODEL) // 128


    sub = min(ATT_SUB, blk)

    def body(q_ref, kv_ref, kp_ref, zb_ref, o_ref, yb_ref, lse_ref, m_scr, acc_scr):
        i = pl.program_id(1)
        qv = q_ref[...]
        m_scr[...] = jnp.full_like(m_scr, NEG)
        acc_scr[...] = jnp.zeros_like(acc_scr)
        ones = jnp.ones((blk, 128), BF16)

        def step(j, masked):
            ks = pl.ds(pl.multiple_of(j * blk, blk), blk)
            kc = jnp.concatenate([kv_ref[ks, 0:128], kp_ref[ks, :]], axis=1)
            vext = jnp.concatenate([kv_ref[ks, 128:256], ones], axis=1)
            for r in range(blk // sub):
                rows = slice(r * sub, (r + 1) * sub)
                kw = (r + 1) * sub if masked else blk
                tt = lax.dot_general(qv[rows], kc[:kw], NT_DIMS, preferred_element_type=F32)
                if masked:
                    row = lax.broadcasted_iota(jnp.int32, (sub, kw), 0) + r * sub
                    col = lax.broadcasted_iota(jnp.int32, (sub, kw), 1)
                    tt = jnp.where(row >= col, tt, NEG)
                cm = tt[:, 0:128]
                for c in range(1, kw // 128):
                    cm = jnp.maximum(cm, tt[:, c * 128:(c + 1) * 128])
                m_prev = m_scr[rows, :]
                m_new = jnp.maximum(m_prev, jnp.max(cm, axis=-1, keepdims=True))
                alpha = jnp.exp2(m_prev - m_new)
                m_scr[rows, :] = m_new
                p = jnp.concatenate([jnp.exp2(tt[:, c * 128:(c + 1) * 128] - m_new).astype(BF16)
                                     for c in range(kw // 128)], axis=1)
                pv = jnp.dot(p, vext[:kw], preferred_element_type=F32)
                acc_scr[rows, :] = jnp.concatenate([alpha, alpha], axis=1) * acc_scr[rows, :] + pv

        def loop_body(j, carry):
            step(j, False)
            return carry

        lax.fori_loop(0, i, loop_body, 0)
        step(i, True)
        l = acc_scr[:, 128:256]
        o = acc_scr[:, 0:128] / l
        o_ref[...] = o.astype(BF16)
        zs, _ = _silu_and_grad(zb_ref[...].astype(F32))
        yb_ref[...] = (o * zs).astype(BF16)
        lse_ref[0] = m_scr[...] + jnp.log2(l)

    act = jax.ShapeDtypeStruct((t, D_MODEL), BF16)
    return pl.pallas_call(
        body, name="mla_fwd", grid=(MLA_HEADS, nq),
        in_specs=[pl.BlockSpec((blk, HEAD_PAD), lambda h, i: (i, h)),
                  pl.BlockSpec((t, HEAD_PAD), lambda h, i: (0, h)),
                  pl.BlockSpec((t, 128), lambda h, i: (0, 0)),
                  pl.BlockSpec((blk, 128), lambda h, i: (i, zb_blk0 + h))],
        out_specs=[pl.BlockSpec((blk, 128), lambda h, i: (i, h)),
                   pl.BlockSpec((blk, 128), lambda h, i: (i, h)),
                   pl.BlockSpec((1, blk, 128), lambda h, i: (h, i, 0))],
        out_shape=[act, act, jax.ShapeDtypeStruct((MLA_HEADS, t, 128), F32)],
        scratch_shapes=[pltpu.VMEM((blk, 128), F32), pltpu.VMEM((blk, HEAD_PAD), F32)],
        compiler_params=_params(56),
    )(q, kv, kpe, proj_big)


def _mla_gate_bwd(dyb, proj_big, o, dproj):
    t, d = dyb.shape
    tr = min(ROW_BLK, t)

    def body(dy_ref, zb_ref, o_ref, buf_ref, do_ref, dz_ref, dl_ref):
        del buf_ref
        dy = dy_ref[...].astype(F32)
        ov = o_ref[...].astype(F32)
        zs, zgrad = _silu_and_grad(zb_ref[...].astype(F32))
        do16 = (dy * zs).astype(BF16)
        do_ref[...] = do16
        dz_ref[...] = (dy * ov * zgrad).astype(BF16)
        prod = do16.astype(F32) * ov
        for h in range(MLA_HEADS):
            delta = jnp.sum(prod[:, h * 128:(h + 1) * 128], axis=-1, keepdims=True)
            dl_ref[h] = jnp.broadcast_to(delta, (tr, 128))

    act = jax.ShapeDtypeStruct((t, d), BF16)
    head_spec = pl.BlockSpec((MLA_HEADS, tr, 128), lambda i: (0, i, 0))
    return pl.pallas_call(
        body, name="mla_gate_bwd", grid=(t // tr,),
        in_specs=[_row_spec(tr, d), _row_spec(tr, d, 3), _row_spec(tr, d), HBM_SPEC],
        out_specs=[_row_spec(tr, d), _row_spec(tr, d, 3), head_spec],
        out_shape=[act, jax.ShapeDtypeStruct((t, 6 * d), BF16), jax.ShapeDtypeStruct((MLA_HEADS, t, 128), F32)],
        input_output_aliases={3: 1},
        compiler_params=_params(56),
    )(dyb, proj_big, o, dproj)


def _mla_bwd(q, kv, kpe, do, lse, delta, carry):
    t = q.shape[0]
    blk = min(ATT_BLK, t)
    n = t // blk
    pairs = [(j, i) for j in range(n) for i in range(j, n)]
    j_tab = jnp.asarray([p[0] for p in pairs], jnp.int32)
    i_tab = jnp.asarray([p[1] for p in pairs], jnp.int32)
    n_carry = len(carry.arrays)
    sub = min(ATT_SUB, blk)

    def body(j_ref, i_ref, q_ref, do_ref, lse_ref, dl_ref, kv_ref, kp_ref, *rest):
        c_ins, rest = rest[:n_carry], rest[n_carry:]
        dq_ref, dkv_ref, dkp_ref = rest[:3]
        c_outs, rest = rest[3:3 + n_carry], rest[3 + n_carry:]
        dk_scr, dv_scr = rest[:2]
        c_sems = rest[2:]
        head = pl.program_id(0)
        step = pl.program_id(1)
        j, i = j_ref[step], i_ref[step]

        @pl.when((head == 0) & (step == 0))
        def _():
            carry.start(c_ins, c_outs, c_sems)

        @pl.when(step == 0)
        def _():
            dq_ref[...] = jnp.zeros_like(dq_ref)

        @pl.when(i == j)
        def _():
            dk_scr[...] = jnp.zeros_like(dk_scr)
            dv_scr[...] = jnp.zeros_like(dv_scr)

        kc = jnp.concatenate([kv_ref[:, 0:128], kp_ref[...]], axis=1)
        vv = kv_ref[:, 128:256]

        def tile(diag):
            for r in range(blk // sub):
                rows = slice(r * sub, (r + 1) * sub)
                kw = (r + 1) * sub if diag else blk
                qv, dov = q_ref[rows, :], do_ref[rows, :]
                tt = lax.dot_general(qv, kc[:kw], NT_DIMS, preferred_element_type=F32)
                if diag:
                    row = lax.broadcasted_iota(jnp.int32, (sub, kw), 0) + r * sub
                    col = lax.broadcasted_iota(jnp.int32, (sub, kw), 1)
                    tt = jnp.where(row >= col, tt, NEG)
                dp = lax.dot_general(dov, vv[:kw], NT_DIMS, preferred_element_type=F32)
                lse_v, dl_v = lse_ref[0, rows, :], dl_ref[0, rows, :]
                ps, dss = [], []
                for c in range(kw // 128):
                    cols = slice(c * 128, (c + 1) * 128)
                    p = jnp.exp2(tt[:, cols] - lse_v)
                    ps.append(p.astype(BF16))
                    dss.append((p * (dp[:, cols] - dl_v)).astype(BF16))
                p16 = jnp.concatenate(ps, axis=1)
                ds16 = jnp.concatenate(dss, axis=1)
                dv_scr[0:kw, :] += lax.dot_general(p16, dov, TN_DIMS, preferred_element_type=F32)
                dk_scr[0:kw, :] += lax.dot_general(ds16, qv, TN_DIMS, preferred_element_type=F32)
                qs = pl.ds(pl.multiple_of(i * blk + r * sub, sub), sub)
                dq_ref[qs, :] += jnp.dot(ds16, kc[:kw], preferred_element_type=F32)

        @pl.when(i == j)
        def _():
            tile(True)

        @pl.when(i > j)
        def _():
            tile(False)

        @pl.when(i == n - 1)
        def _():
            dkv_ref[:, 0:128] = (dk_scr[:, 0:128] * (1.0 / LOG2E)).astype(BF16)
            dkv_ref[:, 128:256] = dv_scr[...].astype(BF16)
            ks = pl.ds(pl.multiple_of(j * blk, blk), blk)
            dkp = dk_scr[:, 128:256] * (1.0 / LOG2E)

            @pl.when(head == 0)
            def _():
                dkp_ref[ks, :] = dkp

            @pl.when(head > 0)
            def _():
                dkp_ref[ks, :] += dkp

        @pl.when((head == MLA_HEADS - 1) & (step == len(pairs) - 1))
        def _():
            carry.finish(c_ins, c_outs, c_sems)

    grid_spec = pltpu.PrefetchScalarGridSpec(
        num_scalar_prefetch=2, grid=(MLA_HEADS, len(pairs)),
        in_specs=[pl.BlockSpec((blk, HEAD_PAD), lambda h, s, jt, it: (it[s], h)),
                  pl.BlockSpec((blk, 128), lambda h, s, jt, it: (it[s], h)),
                  pl.BlockSpec((1, blk, 128), lambda h, s, jt, it: (h, it[s], 0)),
                  pl.BlockSpec((1, blk, 128), lambda h, s, jt, it: (h, it[s], 0)),
                  pl.BlockSpec((blk, HEAD_PAD), lambda h, s, jt, it: (jt[s], h)),
                  pl.BlockSpec((blk, 128), lambda h, s, jt, it: (jt[s], 0))] + [HBM_SPEC] * n_carry,
        out_specs=[pl.BlockSpec((t, HEAD_PAD), lambda h, s, jt, it: (0, h)),
                   pl.BlockSpec((blk, HEAD_PAD), lambda h, s, jt, it: (jt[s], h)),
                   pl.BlockSpec((t, 128), lambda h, s, jt, it: (0, 0))] + [HBM_SPEC] * n_carry,
        scratch_shapes=[pltpu.VMEM((blk, HEAD_PAD), F32), pltpu.VMEM((blk, 128), F32)] + carry.sem_shapes,
    )
    return pl.pallas_call(
        body, name="mla_bwd", grid_spec=grid_spec,
        out_shape=[jax.ShapeDtypeStruct((t, MLA_HEADS * HEAD_PAD), F32),
                   jax.ShapeDtypeStruct((t, 2 * D_MODEL), BF16),
                   jax.ShapeDtypeStruct((t, 128), F32)] + carry.out_shapes,
        compiler_params=_params(58),
    )(j_tab, i_tab, q, do, lse, delta, kv, kpe, *carry.arrays)


def _mem_attn_probs(qv, k_ref):
    s = lax.dot_general(qv, k_ref[...], NT_DIMS, preferred_element_type=F32) * MEM_SCALE
    e = jnp.exp(s - jnp.max(s, axis=-1, keepdims=True))
    return e / jnp.sum(e, axis=-1, keepdims=True)


def _mem_fwd(proj_big, kv_m):
    t = proj_big.shape[0]
    tq = min(MEM_Q_BLK, t)
    hd = MEM_HEAD_DIM
    d = D_MODEL
    mlen = kv_m.shape[0]

    def body(q_ref, z_ref, kv_ref, y_ref):
        for h in range(MEM_HEADS):
            cols = slice(h * hd, (h + 1) * hd)
            p = _mem_attn_probs(q_ref[:, cols], kv_ref.at[:, cols])
            o = jnp.dot(p.astype(BF16), kv_ref[:, d + h * hd:d + (h + 1) * hd], preferred_element_type=F32)
            zs, _ = _silu_and_grad(z_ref[:, cols].astype(F32))
            y_ref[:, cols] = (o * zs).astype(BF16)

    return pl.pallas_call(
        body, name="mem_fwd", grid=(t // tq,),
        in_specs=[_row_spec(tq, d, 4), _row_spec(tq, d, 5), _full_spec((mlen, 2 * d))],
        out_specs=_row_spec(tq, d),
        out_shape=jax.ShapeDtypeStruct((t, d), BF16),
        compiler_params=_params(40),
    )(proj_big, proj_big, kv_m)


def _mem_bwd(proj_big, kv_m, dym, dproj):
    t = proj_big.shape[0]
    tq = min(MEM_Q_BLK, t)
    hd = MEM_HEAD_DIM
    d = D_MODEL
    mlen = kv_m.shape[0]

    def body(q_ref, z_ref, kv_ref, dy_ref, buf_ref, dqz_ref, dkv_ref):
        del buf_ref

        @pl.when(pl.program_id(0) == 0)
        def _():
            dkv_ref[...] = jnp.zeros_like(dkv_ref)

        for h in range(MEM_HEADS):
            cols = slice(h * hd, (h + 1) * hd)
            k_ref, v_ref = kv_ref.at[:, cols], kv_ref.at[:, d + h * hd:d + (h + 1) * hd]
            qv = q_ref[:, cols]
            p = _mem_attn_probs(qv, k_ref)
            p16 = p.astype(BF16)
            o = jnp.dot(p16, v_ref[...], preferred_element_type=F32)
            zs, zgrad = _silu_and_grad(z_ref[:, cols].astype(F32))
            dy = dy_ref[:, cols].astype(F32)
            dqz_ref[:, d + h * hd:d + (h + 1) * hd] = (dy * o * zgrad).astype(BF16)
            do16 = (dy * zs).astype(BF16)
            dkv_ref[:, d + h * hd:d + (h + 1) * hd] += lax.dot_general(p16, do16, TN_DIMS, preferred_element_type=F32)
            dp = lax.dot_general(do16, v_ref[...], NT_DIMS, preferred_element_type=F32)
            ds = (p * (dp - jnp.sum(dp * p, axis=-1, keepdims=True)) * MEM_SCALE).astype(BF16)
            dqz_ref[:, cols] = jnp.dot(ds, k_ref[...], preferred_element_type=F32).astype(BF16)
            dkv_ref[:, cols] += lax.dot_general(ds, qv, TN_DIMS, preferred_element_type=F32)

    return pl.pallas_call(
        body, name="mem_bwd", grid=(t // tq,),
        in_specs=[_row_spec(tq, d, 4), _row_spec(tq, d, 5), _full_spec((mlen, 2 * d)), _row_spec(tq, d), HBM_SPEC],
        out_specs=[_row_spec(tq, 2 * d, 2), _full_spec((mlen, 2 * d))],
        out_shape=[jax.ShapeDtypeStruct((t, 6 * d), BF16), jax.ShapeDtypeStruct((mlen, 2 * d), F32)],
        input_output_aliases={4: 0},
        compiler_params=_params(56),
    )(proj_big, proj_big, kv_m, dym, dproj)


HBM_SPEC = pl.BlockSpec(memory_space=pl.ANY)
N_PEERS = N_DEV - 1


def _dev_index(px, py, pc):
    return 4 * px + 2 * py + pc


class _Gather:
    def __init__(self, arrays):
        self.arrays = list(arrays)
        n = len(self.arrays)
        self.out_shapes = [jax.ShapeDtypeStruct((N_DEV,) + a.shape, a.dtype) for a in self.arrays]
        self.sem_shapes = [pltpu.SemaphoreType.DMA((n * N_PEERS,)), pltpu.SemaphoreType.DMA((n * N_PEERS,)),
                           pltpu.SemaphoreType.DMA((n,))]

    def _parts(self, ins, outs, sems, first_leg=True):
        n = len(self.arrays)
        send_sems, recv_sems, local_sems = sems
        x, y, c = lax.axis_index("x"), lax.axis_index("y"), lax.axis_index("c")
        me, sibling = (x, y, c), (x, y, 1 - c)
        chips = [(1 - x, y), (x, 1 - y), (1 - x, 1 - y)]

        def copy(a, k, block, to, src=None):
            dst = outs[a].at[_dev_index(*block)]
            return pltpu.make_async_remote_copy(
                src_ref=dst if src is None else src, dst_ref=dst,
                send_sem=send_sems.at[a * N_PEERS + k], recv_sem=recv_sems.at[a * N_PEERS + k],
                device_id=to, device_id_type=MESH)

        mine, first = [], []
        if first_leg:
            mine = [pltpu.make_async_copy(ins[a], outs[a].at[_dev_index(*me)], local_sems.at[a]) for a in range(n)]
            for a in range(n):
                first.append(copy(a, 0, me, sibling, src=ins[a]))
                first += [copy(a, 1 + j, me, (*chip, c), src=ins[a]) for j, chip in enumerate(chips)]
        return n, c, me, sibling, chips, copy, mine, first

    def start(self, ins, outs, sems):
        _, _, _, _, _, _, mine, first = self._parts(ins, outs, sems)
        for cp in mine + first:
            cp.start()

    def forward(self, ins, outs, sems):
        n, c, me, sibling, chips, copy, _, _ = self._parts(ins, outs, sems, first_leg=False)
        for j, chip in enumerate(chips):
            for a in range(n):
                copy(a, 1 + j, (*chip, c), me).wait_recv()
                copy(a, 4 + j, (*chip, c), sibling).start()

    def finish(self, ins, outs, sems, forwarded=False):
        if not forwarded:
            self.forward(ins, outs, sems)
        n, c, me, sibling, chips, copy, mine, first = self._parts(ins, outs, sems)
        passed = [copy(a, 4 + j, (*chip, c), sibling) for j, chip in enumerate(chips) for a in range(n)]
        for a in range(n):
            copy(a, 0, sibling, me).wait_recv()
            for j, chip in enumerate(chips):
                copy(a, 4 + j, (*chip, 1 - c), me).wait_recv()
        for cp in first + passed:
            cp.wait_send()
        for cp in mine:
            cp.wait()


class _AllToAll:
    def __init__(self, arrays):
        self.arrays = list(arrays)
        n = len(self.arrays)
        self.out_shapes = [jax.ShapeDtypeStruct(a.shape, a.dtype) for a in self.arrays]
        self.sem_shapes = [pltpu.SemaphoreType.DMA((n * N_PEERS,)), pltpu.SemaphoreType.DMA((n * N_PEERS,)),
                           pltpu.SemaphoreType.DMA((n,))]

    def _parts(self, ins, outs, sems):
        n = len(self.arrays)
        send_sems, recv_sems, local_sems = sems
        x, y, c = lax.axis_index("x"), lax.axis_index("y"), lax.axis_index("c")
        my_idx = _dev_index(x, y, c)
        peers = []
        for k in range(1, N_DEV):
            dx, dy, dc = (k >> 2) & 1, (k >> 1) & 1, k & 1
            peers.append((1 - x if dx else x, 1 - y if dy else y, 1 - c if dc else c))

        def copy(a, k, peer):
            return pltpu.make_async_remote_copy(
                src_ref=ins[a].at[_dev_index(*peer)], dst_ref=outs[a].at[my_idx],
                send_sem=send_sems.at[a * N_PEERS + k], recv_sem=recv_sems.at[a * N_PEERS + k],
                device_id=peer, device_id_type=MESH)

        def landed(a, k, peer):
            slot = outs[a].at[_dev_index(*peer)]
            return pltpu.make_async_remote_copy(
                src_ref=slot, dst_ref=slot,
                send_sem=send_sems.at[a * N_PEERS + k], recv_sem=recv_sems.at[a * N_PEERS + k],
                device_id=peer, device_id_type=MESH)

        mine = [pltpu.make_async_copy(ins[a].at[my_idx], outs[a].at[my_idx], local_sems.at[a]) for a in range(n)]
        sends = [copy(a, k, peer) for a in range(n) for k, peer in enumerate(peers)]
        return n, peers, landed, mine, sends

    def start(self, ins, outs, sems):
        _, _, _, mine, sends = self._parts(ins, outs, sems)
        for cp in mine + sends:
            cp.start()

    def finish(self, ins, outs, sems):
        n, peers, landed, mine, sends = self._parts(ins, outs, sems)
        for a in range(n):
            for k, peer in enumerate(peers):
                landed(a, k, peer).wait_recv()
        for cp in sends:
            cp.wait_send()
        for cp in mine:
            cp.wait()


def _exchange(plan, name):
    n = len(plan.arrays)

    def body(*refs):
        ins, outs, sems = refs[:n], refs[n:2 * n], refs[2 * n:]
        plan.start(ins, outs, sems)
        plan.finish(ins, outs, sems)

    return pl.pallas_call(
        body, name=name, in_specs=[HBM_SPEC] * n, out_specs=[HBM_SPEC] * n,
        out_shape=plan.out_shapes, scratch_shapes=plan.sem_shapes,
    )(*plan.arrays)


def _adamw(w, g, m, v):
    m = ADAM_B1 * m + (1.0 - ADAM_B1) * g
    v = ADAM_B2 * v + (1.0 - ADAM_B2) * jnp.square(g)
    m_hat = m / (1.0 - ADAM_B1 ** ADAM_STEP)
    v_hat = v / (1.0 - ADAM_B2 ** ADAM_STEP)
    delta = -ADAM_LR * (m_hat / (jnp.sqrt(v_hat) + ADAM_EPS) + ADAM_WD * w)
    return delta, m, v


def _adam_sharded(parts_list, w, m, v, name):
    shape = w.shape
    cols = shape[-1]
    rows = int(np.prod(shape[:-1]))
    tr = min(128, rows)
    parts_list = [p.reshape(N_DEV, -1, cols) for p in parts_list]
    bounds = np.cumsum([0] + [p.shape[1] // tr for p in parts_list])
    assert rows % tr == 0 and all(p.shape[1] % tr == 0 for p in parts_list) and bounds[-1] == rows // tr
    n_parts = len(parts_list)

    def body(*refs):
        p_refs = refs[:n_parts]
        w_ref, m_ref, v_ref, g_ref, d_ref, nm_ref, nv_ref = refs[n_parts:]
        i = pl.program_id(0)
        for k, p_ref in enumerate(p_refs):
            @pl.when((i >= bounds[k]) & (i < bounds[k + 1]))
            def _():
                g = p_ref[0].astype(F32)
                for e in range(1, N_DEV):
                    g = g + p_ref[e].astype(F32)
                g_ref[...] = g
                d_ref[...], nm_ref[...], nv_ref[...] = _adamw(w_ref[...], g, m_ref[...], v_ref[...])

    def part_spec(k):
        lo, hi = int(bounds[k]), int(bounds[k + 1])
        return pl.BlockSpec((N_DEV, tr, cols), lambda i: (0, jnp.clip(i, lo, hi - 1) - lo, 0))

    spec = pl.BlockSpec((tr, cols), lambda i: (i, 0))
    flat = jax.ShapeDtypeStruct((rows, cols), F32)
    outs = pl.pallas_call(
        body, name=name, grid=(rows // tr,),
        in_specs=[part_spec(k) for k in range(n_parts)] + [spec, spec, spec],
        out_specs=[spec] * 4, out_shape=[flat] * 4,
        compiler_params=_params(40),
    )(*parts_list, w.reshape(rows, cols), m.reshape(rows, cols), v.reshape(rows, cols))
    return [o.reshape(shape) for o in outs]


def _adam_replicated(parts, w, m, v):
    r = w.shape[0]

    def body(p_ref, w_ref, m_ref, v_ref, g_ref, d_ref, nm_ref, nv_ref):
        g = p_ref[0]
        for e in range(1, N_DEV):
            g = g + p_ref[e]
        g_ref[...] = g
        d_ref[...], nm_ref[...], nv_ref[...] = _adamw(w_ref[...], g, m_ref[...], v_ref[...])

    spec = _full_spec((r, 128))
    flat = jax.ShapeDtypeStruct((r, 128), F32)
    return pl.pallas_call(
        body, name="adam_replicated", grid=(1,),
        in_specs=[_full_spec((N_DEV, r, 128)), spec, spec, spec],
        out_specs=[spec] * 4, out_shape=[flat] * 4,
        compiler_params=_params(48),
    )(parts, w, m, v)


def _pack(arrays):
    parts = []
    for a in arrays:
        f = a.reshape(-1, 128)
        pad = -f.shape[0] % 8
        parts.append(jnp.pad(f, ((0, pad), (0, 0))) if pad else f)
    return jnp.concatenate(parts, axis=0)


def _unpack(packed, shapes):
    out, row = [], 0
    for shape in shapes:
        r = int(np.prod(shape)) // 128
        out.append(packed[row:row + r].reshape(shape))
        row += r + (-r % 8)
    return out


SHARDED = ("w_in", "w_uq", "w_ukv", "w_mem_kv", "w_gate", "w_branch", "w_out")
REPLICATED = ("g_pre", "a_ln_g", "a_ln_b", "a_w_s", "a_b_s", "q_norm_g", "kv_norm_g", "mem_norm_g", "b_gate", "g_post")
WEIGHT_ORDER = ("g_pre", "w_in", "a_ln_g", "a_ln_b", "a_w_s", "a_b_s", "q_norm_g", "w_uq", "kv_norm_g", "w_ukv",
                "mem_norm_g", "w_mem_kv", "w_gate", "b_gate", "w_branch", "w_out", "g_post")


def _unshard_cols(g):
    return g.transpose(1, 0, 2).reshape(g.shape[1], N_DEV * g.shape[2])


def _shard_cols(full):
    rows, n = full.shape
    return full.reshape(rows, N_DEV, n // N_DEV).transpose(1, 0, 2).astype(BF16)


def kernel(x, mem, positions, g_pre, w_in, a_ln_g, a_ln_b, a_w_s, a_b_s, q_norm_g, w_uq, kv_norm_g, w_ukv, mem_norm_g, w_mem_kv, w_gate, b_gate, w_branch, w_out, g_post, loss_target, m_g_pre, m_w_in, m_a_ln_g, m_a_ln_b, m_a_w_s, m_a_b_s, m_q_norm_g, m_w_uq, m_kv_norm_g, m_w_ukv, m_mem_norm_g, m_w_mem_kv, m_w_gate, m_b_gate, m_w_branch, m_w_out, m_g_post, v_g_pre, v_w_in, v_a_ln_g, v_a_ln_b, v_a_w_s, v_a_b_s, v_q_norm_g, v_w_uq, v_kv_norm_g, v_w_ukv, v_mem_norm_g, v_w_mem_kv, v_w_gate, v_b_gate, v_w_branch, v_w_out, v_g_post):
    weights = dict(g_pre=g_pre, w_in=w_in, a_ln_g=a_ln_g, a_ln_b=a_ln_b, a_w_s=a_w_s, a_b_s=a_b_s, q_norm_g=q_norm_g,
                   w_uq=w_uq, kv_norm_g=kv_norm_g, w_ukv=w_ukv, mem_norm_g=mem_norm_g, w_mem_kv=w_mem_kv,
                   w_gate=w_gate, b_gate=b_gate, w_branch=w_branch, w_out=w_out, g_post=g_post)
    mom1 = dict(g_pre=m_g_pre, w_in=m_w_in, a_ln_g=m_a_ln_g, a_ln_b=m_a_ln_b, a_w_s=m_a_w_s, a_b_s=m_a_b_s,
                q_norm_g=m_q_norm_g, w_uq=m_w_uq, kv_norm_g=m_kv_norm_g, w_ukv=m_w_ukv, mem_norm_g=m_mem_norm_g,
                w_mem_kv=m_w_mem_kv, w_gate=m_w_gate, b_gate=m_b_gate, w_branch=m_w_branch, w_out=m_w_out, g_post=m_g_post)
    mom2 = dict(g_pre=v_g_pre, w_in=v_w_in, a_ln_g=v_a_ln_g, a_ln_b=v_a_ln_b, a_w_s=v_a_w_s, a_b_s=v_a_b_s,
                q_norm_g=v_q_norm_g, w_uq=v_w_uq, kv_norm_g=v_kv_norm_g, w_ukv=v_w_ukv, mem_norm_g=v_mem_norm_g,
                w_mem_kv=v_w_mem_kv, w_gate=v_w_gate, b_gate=v_b_gate, w_branch=v_w_branch, w_out=v_w_out, g_post=v_g_post)
    d = D_MODEL
    t = x.shape[1]
    xs, tgt, mems = x[0], loss_target[0], mem[0]
    pos_col = positions.reshape(t, 1)

    shard16 = {n: weights[n][0].astype(BF16) for n in SHARDED}
    h, g_gate = _pre_norm(xs, g_pre, _Gather([shard16["w_gate"]]))
    w_gate_f = _unshard_cols(g_gate)
    gates, g_in = _mm(h, w_gate_f, name="gates", tm=1024, tn=1024, tk=2048, bias=b_gate, act="sigmoid",
                      carry=_Gather([shard16["w_in"]]))
    w_in_full = _unshard_cols(g_in)
    lat0, lat1 = 3 * d, 3 * d + Q_LORA + KV_LORA + QK_ROPE
    w_big = jnp.concatenate([w_in_full[:, :lat0], w_in_full[:, lat1:]], axis=1)
    w_lat = jnp.concatenate([w_in_full[:, lat0:lat1], jnp.zeros((d, LAT_W - (lat1 - lat0)), BF16)], axis=1)

    inv_freq = 1.0 / (ROPE_THETA ** (jnp.arange(0, QK_ROPE, 2, dtype=F32) / QK_ROPE))
    inv_freq_lanes = jnp.concatenate([inv_freq, inv_freq, jnp.zeros((128 - QK_ROPE,), F32)]).reshape(1, 128)
    ws = a_w_s[0]
    b_exp = jnp.broadcast_to(a_b_s[0][:, :, None], (A_GROUPS, CHUNK, 128))

    proj_big, g_uq, g_ukv, g_mem, g_br, g_out = _mm(
        h, w_big, name="proj_big", tm=1024, tn=1024, tk=2048, early_forward=True,
        carry=_Gather([shard16[n] for n in ("w_uq", "w_ukv", "w_mem_kv", "w_branch", "w_out")]))
    w_uq_p = jnp.pad(_unshard_cols(g_uq).reshape(Q_LORA, MLA_HEADS, QK_DIM),
                     ((0, 0), (0, 0), (0, HEAD_PAD - QK_DIM))).reshape(Q_LORA, MLA_HEADS * HEAD_PAD)
    w_ukv_f = _unshard_cols(g_ukv)
    w_mem_f = _unshard_cols(g_mem)
    proj_lat = _mm(h, w_lat, name="proj_lat", tm=1024, tn=LAT_W, tk=2048)
    w_br_f = g_br.transpose(1, 0, 2, 3).reshape(3, d, d)
    w_out_f = g_out.reshape(d, d)
    c_tab, sa_tab, sb_tab = _rope_tables(pos_col, inv_freq_lanes)
    cqn, ckvn, kpe = _latent_norms(proj_lat, q_norm_g, kv_norm_g, c_tab, sa_tab, sb_tab)
    q = _mm(cqn, w_uq_p, name="q_up", tm=1024, tn=1024, tk=512, post=_rope_q_tile, post_rows=(c_tab, sa_tab, sb_tab))
    kv = _mm(ckvn, w_ukv_f, name="kv_up", tm=1024, tn=1024, tk=512)
    o_b, y_b, lse = _mla_fwd(q, kv, kpe, proj_big)
    memn = _mem_norm(mems, mem_norm_g)
    kv_m = _mm(memn, w_mem_f, name="mem_kv", tm=256, tn=1024, tk=2048)
    y_m = _mem_fwd(proj_big, kv_m)
    y_a = _gmlp_fwd(proj_big, a_ln_g, a_ln_b, ws, b_exp)
    ys = (y_a, y_b, y_m)
    ps = [_mm(ys[n], w_br_f[n], name=f"branch{n}", tm=1024, tn=1024, tk=2048) for n in range(3)]
    merged = _merge(gates, *ps)
    out = _mm(merged, w_out_f, name="out_proj", tm=1024, tn=1024, tk=2048, out_dtype=F32)
    d_out, dy, loss_blk, dg_post = _post_loss(out, xs, tgt, g_post)

    dmerged = _mm(d_out, w_out_f, name="d_merged", tb=True, tm=1024, tn=1024, tk=2048)
    dw_out = _mm(merged, d_out, name="dw_out", ta=True, tm=1024, tn=1024, tk=2048)
    dp_a, dp_b, dp_m, dgl, db_gate = _merge_bwd(dmerged, gates, *ps)
    dps = (dp_a, dp_b, dp_m)
    dys = [_mm(dps[n], w_br_f[n], name=f"d_y{n}", tb=True, tm=1024, tn=1024, tk=2048) for n in range(3)]
    dw_br = [_mm(ys[n], dps[n], name=f"dw_branch{n}", ta=True, tm=1024, tn=1024, tk=2048) for n in range(3)]
    dw_gate = _mm(h, dgl, name="dw_gate", ta=True, tm=1024, tn=3 * d // N_DEV, tk=2048, col_shards=True)

    dproj_big, dws, dbs, dlng, dlnb = _gmlp_bwd(proj_big, dys[0], a_ln_g, a_ln_b, ws, b_exp)

    recv = {}
    send = [dw_gate, jnp.stack(dw_br).reshape(3, N_DEV, d // N_DEV, d).transpose(1, 0, 2, 3),
            dw_out.reshape(N_DEV, d // N_DEV, d)]
    do_b, dproj_big, delta = _mla_gate_bwd(dys[1], proj_big, o_b, dproj_big)
    dq, dkv, dkpe, r_gate, r_br, r_out = _mla_bwd(q, kv, kpe, do_b, lse, delta, _AllToAll(send))
    recv["w_gate"], recv["w_branch"], recv["w_out"] = [r_gate], [r_br], [r_out]
    dq_raw, dkr = _rope_q_bwd(dq, dkpe, c_tab, sa_tab, sb_tab)
    dcqn = _mm(dq_raw, w_uq_p, name="d_cq", tb=True, tm=1024, tn=Q_LORA, tk=2048, out_dtype=F32)
    dw_uq_p = _mm(cqn, dq_raw, name="dw_uq", ta=True, tm=Q_LORA, tn=1024, tk=2048)
    dckvn = _mm(dkv, w_ukv_f, name="d_ckv", tb=True, tm=1024, tn=KV_LORA, tk=2048, out_dtype=F32)
    dw_ukv = _mm(ckvn, dkv, name="dw_ukv", ta=True, tm=KV_LORA, tn=2 * d // N_DEV, tk=2048, col_shards=True)
    dproj_lat, dqg, dkg = _latent_norms_bwd(proj_lat, q_norm_g, kv_norm_g, dcqn, dckvn, dkr)

    dproj_big, dkv_m32 = _mem_bwd(proj_big, kv_m, dys[2], dproj_big)
    dkv_m = dkv_m32.astype(BF16)
    dw_mem = _mm(memn, dkv_m, name="dw_mem", ta=True, tm=1024, tn=2 * d // N_DEV, tk=256, col_shards=True)

    dw_uq_full = dw_uq_p.reshape(Q_LORA, MLA_HEADS, HEAD_PAD)[:, :, :QK_DIM].reshape(Q_LORA, MLA_HEADS * QK_DIM)
    dw_big, r_uq, r_ukv, r_mem = _mm(
        h, dproj_big, name="dw_big", ta=True, tm=1024, tn=1024, tk=2048,
        carry=_AllToAll([_shard_cols(dw_uq_full), dw_ukv, dw_mem]))
    recv["w_uq"], recv["w_ukv"], recv["w_mem_kv"] = [r_uq], [r_ukv], [r_mem]
    dw_lat = _mm(h, dproj_lat, name="dw_lat", ta=True, tm=1024, tn=LAT_W, tk=2048)

    def w_in_shards(rows):
        n_cols = w_in.shape[-1]
        pieces = []
        for j in range(N_DEV):
            lo, hi = n_cols * j, n_cols * (j + 1)
            parts = []
            for a, b, src, shift in ((0, lat0, dw_big, 0), (lat0, lat1, dw_lat, lat0), (lat1, N_DEV * n_cols, dw_big, lat1 - lat0)):
                s, e = max(lo, a), min(hi, b)
                if s < e:
                    parts.append(src[rows, s - shift:e - shift])
            pieces.append(parts[0] if len(parts) == 1 else jnp.concatenate(parts, axis=1))
        return jnp.stack(pieces)

    dmemn = _mm(dkv_m, w_mem_f, name="d_memn", tb=True, tm=256, tn=1024, tk=2048, out_dtype=F32)
    dg_mem = _mem_norm_bwd(mems, mem_norm_g, dmemn)
    dh, r_in0 = _mm(dgl, w_gate_f, name="dh_gate", tb=True, tm=1024, tn=1024, tk=2048, out_dtype=F32,
                    carry=_AllToAll([w_in_shards(slice(0, d // 2))]))
    dh = _mm(dproj_lat, w_lat, name="dh_lat", tb=True, tm=1024, tn=1024, tk=LAT_W, out_dtype=F32, add=dh)
    dh, r_in1 = _mm(dproj_big, w_big, name="dh_big", tb=True, tm=1024, tn=1024, tk=2048, out_dtype=F32, add=dh,
                    carry=_AllToAll([w_in_shards(slice(d // 2, d))]))
    recv["w_in"] = [r_in0, r_in1]
    grad_x, dg_pre = _pre_norm_bwd(xs, g_pre, dh, dy)

    results = {}
    for n in SHARDED:
        results[n] = [r[None] for r in _adam_sharded(recv[n], weights[n][0], mom1[n][0], mom2[n][0], "adam_" + n)]

    small = dict(g_pre=dg_pre, a_ln_g=dlng, a_ln_b=dlnb, a_w_s=dws, a_b_s=dbs, q_norm_g=dqg, kv_norm_g=dkg,
                 mem_norm_g=dg_mem, b_gate=db_gate, g_post=dg_post)
    (parts,) = _exchange(_Gather([_pack([small[n] for n in REPLICATED])]), "gather_small_grads")
    packed = _adam_replicated(parts, _pack([weights[n] for n in REPLICATED]), _pack([mom1[n] for n in REPLICATED]),
                              _pack([mom2[n] for n in REPLICATED]))
    shapes = [weights[n].shape for n in REPLICATED]
    unpacked = [_unpack(p, shapes) for p in packed]
    for i, n in enumerate(REPLICATED):
        results[n] = [u[i] for u in unpacked]

    loss = lax.psum(loss_blk[0, 0], AXES)
    outs = [loss, grad_x[None]]
    for kind in range(4):
        outs += [results[n][kind] for n in WEIGHT_ORDER]
    return tuple(outs)
```
